```python
import math
import jax, jax.numpy as jnp
from jax import lax
import numpy as np

D_MODEL = 1024
BATCH = 4
SEQ = 4096
DEPTH = 1
DEC_BATCH = 32
DEC_SEQ = 1
PAST_LEN = 16384
PAGE_SIZE = 128

SSM_WIDTH = D_MODEL
SSM_GROUP = 16
SSM_GROUPS = SSM_WIDTH // SSM_GROUP
SSM_STATE = 64
SSM_CHUNK = 128
DT_MIN = 0.001
DT_MAX = 0.1
HEAD_DIM = 64
HEADS_PER_GROUP = 4
DIL_WINDOWS = (128, 512, 2048)
DIL_RATES = (1, 4, 16)
N_DIL_GROUPS = 3
ATTN_HEADS = N_DIL_GROUPS * HEADS_PER_GROUP
ATTN_WIDTH = ATTN_HEADS * HEAD_DIM
SLOT_WIDTH = HEADS_PER_GROUP * HEAD_DIM
Q_BLOCK = 128
ALIBI_MAX_EXP = 8.0
D_FF = 2816
RMS_EPS = 1e-6
IN_WIDTH = SSM_WIDTH + 3 * ATTN_WIDTH + 2 * D_MODEL

kernel_name = "hybrid_s5_dilated_attn_macaron_step"

F32 = jnp.float32


def _rmsnorm(x, w):
    x32 = x.astype(F32)
    y = x32 * lax.rsqrt(jnp.mean(x32 * x32, axis=-1, keepdims=True) + RMS_EPS)
    return (y * w.astype(F32)).astype(x.dtype)


def _swiglu(h, w_gate, w_up, w_down):
    return (jax.nn.silu(h @ w_gate) * (h @ w_up)) @ w_down


def _cmul(ar, ai, br, bi):
    return ar * br - ai * bi, ar * bi + ai * br


def _scan_op(e1, e2):
    a1r, a1i, b1r, b1i = e1
    a2r, a2i, b2r, b2i = e2
    ar, ai = _cmul(a2r, a2i, a1r, a1i)
    br, bi = _cmul(a2r, a2i, b1r, b1i)
    return ar, ai, br + b2r, bi + b2i


def _s5_discretise(lambda_re, lambda_im, b_re, b_im, log_dt):
    lr = jnp.minimum(lambda_re.astype(F32), -1e-4)
    li = lambda_im.astype(F32)
    dt = jnp.exp(log_dt.astype(F32))[:, None]
    mag = jnp.exp(lr * dt)
    abar_re = mag * jnp.cos(li * dt)
    abar_im = mag * jnp.sin(li * dt)
    nr = abar_re - 1.0
    ni = abar_im
    den = lr * lr + li * li
    fr = (nr * lr + ni * li) / den
    fi = (ni * lr - nr * li) / den
    bbar_re, bbar_im = _cmul(fr[..., None], fi[..., None], b_re.astype(F32), b_im.astype(F32))
    return abar_re, abar_im, bbar_re, bbar_im


def _s5_scan(u, h0_re, h0_im, abar_re, abar_im, bbar_re, bbar_im, c_re, c_im):
    bt, length = u.shape[:2]
    chunk = SSM_CHUNK if length % SSM_CHUNK == 0 else length
    n_chunks = length // chunk
    uc = u.reshape(bt, n_chunks, chunk, SSM_GROUPS, SSM_GROUP).swapaxes(0, 1)
    a_shape = (bt, chunk, SSM_GROUPS, SSM_STATE)
    ar = jnp.broadcast_to(abar_re, a_shape)
    ai = jnp.broadcast_to(abar_im, a_shape)
    c_re32 = c_re.astype(F32)
    c_im32 = c_im.astype(F32)

    def step(carry, u_blk):
        hr, hi = carry
        br = jnp.einsum('btgc,gpc->btgp', u_blk, bbar_re)
        bi = jnp.einsum('btgc,gpc->btgp', u_blk, bbar_im)
        pr, pi, sr, si = lax.associative_scan(_scan_op, (ar, ai, br, bi), axis=1)
        xr = pr * hr[:, None] - pi * hi[:, None] + sr
        xi = pr * hi[:, None] + pi * hr[:, None] + si
        y = jnp.einsum('btgp,gkp->btgk', xr, c_re32) - jnp.einsum('btgp,gkp->btgk', xi, c_im32)
        return (xr[:, -1], xi[:, -1]), y

    (hr, hi), ys = lax.scan(step, (h0_re, h0_im), uc)
    return ys.swapaxes(0, 1).reshape(bt, length, SSM_WIDTH), hr, hi


def _dilated_attention(q, kv, q_pos, window, dilation, slopes):
    dist = jnp.arange(window // dilation + 1, dtype=jnp.int32) * dilation
    penalty = slopes[:, None] * dist.astype(F32)[None, :]

    def attend(q_blk, pos_blk):
        idx = pos_blk[:, None] - dist[None, :]
        valid = idx >= 0
        kv_g = jnp.take(kv, jnp.maximum(idx, 0), axis=1)
        s = jnp.einsum('bqhd,bqkhd->bhqk', q_blk, kv_g[:, :, :, 0], preferred_element_type=F32)
        s = s - penalty[None, :, None, :]
        s = jnp.where(valid[None, None], s, -jnp.inf)
        m = jnp.max(s, axis=-1, keepdims=True)
        p = jnp.exp(s - m)
        den = jnp.sum(p, axis=-1, keepdims=True)
        o = jnp.einsum('bhqk,bqkhd->bqhd', (p / den).astype(kv.dtype), kv_g[:, :, :, 1])
        lse = (m + jnp.log(den))[..., 0].transpose(0, 2, 1)
        return o, lse

    bt, lq = q.shape[:2]
    if lq % Q_BLOCK == 0 and lq > Q_BLOCK:
        nb = lq // Q_BLOCK
        qb = q.reshape(bt, nb, Q_BLOCK, HEADS_PER_GROUP, HEAD_DIM).swapaxes(0, 1)
        pb = q_pos.reshape(nb, Q_BLOCK)
        o, lse = lax.map(lambda a: attend(a[0], a[1]), (qb, pb))
        o = o.swapaxes(0, 1).reshape(bt, lq, HEADS_PER_GROUP, HEAD_DIM)
        lse = lse.swapaxes(0, 1).reshape(bt, lq, HEADS_PER_GROUP)
        return o, lse
    return attend(q, q_pos)


def _layer(x, h0_re, h0_im, kv_past, keep_rows, p, slopes):
    bt, length = x.shape[:2]
    x = x + 0.5 * _swiglu(_rmsnorm(x, p['ffn1_norm']), p['ffn1_w_gate'], p['ffn1_w_up'], p['ffn1_w_down'])
    h = _rmsnorm(x, p['mix_norm'])
    z = h @ p['w_in']
    cuts = [SSM_WIDTH, SSM_WIDTH + ATTN_WIDTH, SSM_WIDTH + 2 * ATTN_WIDTH,
            SSM_WIDTH + 3 * ATTN_WIDTH, SSM_WIDTH + 3 * ATTN_WIDTH + D_MODEL]
    u_a, q, k, v, g_a, g_b = jnp.split(z, cuts, axis=-1)

    abar_re, abar_im, bbar_re, bbar_im = _s5_discretise(
        p['ssm_lambda_re'], p['ssm_lambda_im'], p['ssm_b_re'], p['ssm_b_im'], p['ssm_log_dt'])
    u32 = u_a.astype(F32)
    y_ssm, hr, hi = _s5_scan(u32, h0_re.astype(F32), h0_im.astype(F32),
                             abar_re, abar_im, bbar_re, bbar_im, p['ssm_c_re'], p['ssm_c_im'])
    y_ssm = jax.nn.gelu(y_ssm + p['ssm_d'].astype(F32) * u32).astype(x.dtype)
    y_a = y_ssm * jax.nn.sigmoid(y_ssm @ p['w_glu'])

    hs = (bt, length, N_DIL_GROUPS, HEADS_PER_GROUP, HEAD_DIM)
    q = _rmsnorm(q.reshape(hs), p['q_gain']) * (HEAD_DIM ** -0.5)
    k = _rmsnorm(k.reshape(hs), p['k_gain'])
    kv_new = jnp.stack([k, v.reshape(hs)], axis=3)
    outs, lses, new_kv = [], [], []
    for g in range(N_DIL_GROUPS):
        kv_g = kv_new[:, :, g]
        if kv_past is None:
            kv_all = kv_g
            offset = 0
        else:
            kv_all = jnp.concatenate([kv_past[g].astype(kv_g.dtype), kv_g], axis=1)
            offset = kv_past[g].shape[1]
        pos = offset + jnp.arange(length, dtype=jnp.int32)
        o, lse = _dilated_attention(q[:, :, g], kv_all, pos, DIL_WINDOWS[g], DIL_RATES[g], slopes[g])
        outs.append(o)
        lses.append(lse)
        new_kv.append(kv_all[:, kv_all.shape[1] - keep_rows[g]:])
    mix_w = jax.nn.softmax(jnp.stack(lses, axis=0), axis=0)
    o_b = jnp.sum(mix_w[..., None] * jnp.stack(outs, axis=0).astype(F32), axis=0)
    o_b = o_b.astype(x.dtype).reshape(bt, length, SLOT_WIDTH)

    merged = jax.nn.sigmoid(g_a) * (y_a @ p['w_proj_a']) + jax.nn.sigmoid(g_b) * (o_b @ p['w_proj_b'])
    x = x + merged @ p['w_out']
    x = x + 0.5 * _swiglu(_rmsnorm(x, p['ffn2_norm']), p['ffn2_w_gate'], p['ffn2_w_up'], p['ffn2_w_down'])
    return x, hr.astype(h0_re.dtype), hi.astype(h0_im.dtype), new_kv


def setup_inputs(seed: int = 0) -> dict:
    key = jax.random.key(seed)
    ks = iter(jax.random.split(key, 40))
    nrm = lambda shape, s: jax.random.normal(next(ks), shape, F32) * s
    L = DEPTH
    inp = {}
    inp['x_prompt'] = nrm((BATCH, SEQ, D_MODEL), 1.0)
    inp['x_sample'] = nrm((DEC_BATCH, DEC_SEQ, D_MODEL), 1.0)
    inp['state_ssm_re'] = nrm((L, DEC_BATCH, SSM_GROUPS, SSM_STATE), 0.5)
    inp['state_ssm_im'] = nrm((L, DEC_BATCH, SSM_GROUPS, SSM_STATE), 0.5)
    for w in DIL_WINDOWS:
        inp['cache_kv_w%d' % w] = nrm((L, DEC_BATCH, min(w, PAST_LEN), 2, HEADS_PER_GROUP, HEAD_DIM), 1.0)
    inp['ffn1_norm'] = 1.0 + nrm((L, D_MODEL), 0.02)
    inp['ffn1_w_gate'] = nrm((L, D_MODEL, D_FF), D_MODEL ** -0.5)
    inp['ffn1_w_up'] = nrm((L, D_MODEL, D_FF), D_MODEL ** -0.5)
    inp['ffn1_w_down'] = nrm((L, D_FF, D_MODEL), D_FF ** -0.5)
    inp['mix_norm'] = 1.0 + nrm((L, D_MODEL), 0.02)
    inp['w_in'] = nrm((L, D_MODEL, IN_WIDTH), D_MODEL ** -0.5)
    inp['ssm_lambda_re'] = -0.5 + nrm((L, SSM_GROUPS, SSM_STATE), 0.01)
    inp['ssm_lambda_im'] = jnp.pi * jnp.arange(SSM_STATE, dtype=F32) + nrm((L, SSM_GROUPS, SSM_STATE), 0.01)
    inp['ssm_b_re'] = nrm((L, SSM_GROUPS, SSM_STATE, SSM_GROUP), (2 * SSM_GROUP) ** -0.5)
    inp['ssm_b_im'] = nrm((L, SSM_GROUPS, SSM_STATE, SSM_GROUP), (2 * SSM_GROUP) ** -0.5)
    inp['ssm_c_re'] = nrm((L, SSM_GROUPS, SSM_GROUP, SSM_STATE), SSM_STATE ** -0.5)
    inp['ssm_c_im'] = nrm((L, SSM_GROUPS, SSM_GROUP, SSM_STATE), SSM_STATE ** -0.5)
    inp['ssm_d'] = nrm((L, SSM_WIDTH), 1.0)
    inp['ssm_log_dt'] = jax.random.uniform(next(ks), (L, SSM_GROUPS), F32,
                                           minval=math.log(DT_MIN), maxval=math.log(DT_MAX))
    inp['w_glu'] = nrm((L, SSM_WIDTH, SSM_WIDTH), SSM_WIDTH ** -0.5)
    inp['q_gain'] = 1.0 + nrm((L, HEAD_DIM), 0.02)
    inp['k_gain'] = 1.0 + nrm((L, HEAD_DIM), 0.02)
    inp['w_proj_a'] = nrm((L, SSM_WIDTH, D_MODEL), SSM_WIDTH ** -0.5)
    inp['w_proj_b'] = nrm((L, SLOT_WIDTH, D_MODEL), SLOT_WIDTH ** -0.5)
    inp['w_out'] = nrm((L, D_MODEL, D_MODEL), D_MODEL ** -0.5)
    inp['ffn2_norm'] = 1.0 + nrm((L, D_MODEL), 0.02)
    inp['ffn2_w_gate'] = nrm((L, D_MODEL, D_FF), D_MODEL ** -0.5)
    inp['ffn2_w_up'] = nrm((L, D_MODEL, D_FF), D_MODEL ** -0.5)
    inp['ffn2_w_down'] = nrm((L, D_FF, D_MODEL), D_FF ** -0.5)
    return inp


def reference(x_prompt, x_sample, state_ssm_re, state_ssm_im, cache_kv_w128, cache_kv_w512, cache_kv_w2048,
              ffn1_norm, ffn1_w_gate, ffn1_w_up, ffn1_w_down, mix_norm, w_in,
              ssm_lambda_re, ssm_lambda_im, ssm_b_re, ssm_b_im, ssm_c_re, ssm_c_im, ssm_d, ssm_log_dt,
              w_glu, q_gain, k_gain, w_proj_a, w_proj_b, w_out,
              ffn2_norm, ffn2_w_gate, ffn2_w_up, ffn2_w_down):
    slopes = jnp.exp2(-ALIBI_MAX_EXP * jnp.arange(1, ATTN_HEADS + 1, dtype=F32) / ATTN_HEADS)
    slopes = slopes.reshape(N_DIL_GROUPS, HEADS_PER_GROUP)
    yp, ys = x_prompt, x_sample
    p_re, p_im, s_re, s_im = [], [], [], []
    p_kv = [[], [], []]
    s_kv = [[], [], []]
    for l in range(DEPTH):
        p = dict(ffn1_norm=ffn1_norm[l], ffn1_w_gate=ffn1_w_gate[l], ffn1_w_up=ffn1_w_up[l],
                 ffn1_w_down=ffn1_w_down[l], mix_norm=mix_norm[l], w_in=w_in[l],
                 ssm_lambda_re=ssm_lambda_re[l], ssm_lambda_im=ssm_lambda_im[l],
                 ssm_b_re=ssm_b_re[l], ssm_b_im=ssm_b_im[l], ssm_c_re=ssm_c_re[l], ssm_c_im=ssm_c_im[l],
                 ssm_d=ssm_d[l], ssm_log_dt=ssm_log_dt[l], w_glu=w_glu[l], q_gain=q_gain[l], k_gain=k_gain[l],
                 w_proj_a=w_proj_a[l], w_proj_b=w_proj_b[l], w_out=w_out[l],
                 ffn2_norm=ffn2_norm[l], ffn2_w_gate=ffn2_w_gate[l], ffn2_w_up=ffn2_w_up[l],
                 ffn2_w_down=ffn2_w_down[l])
        h0 = jnp.zeros((yp.shape[0], SSM_GROUPS, SSM_STATE), yp.dtype)
        keep_p = tuple(min(w, yp.shape[1]) for w in DIL_WINDOWS)
        yp, hr, hi, kvp = _layer(yp, h0, h0, None, keep_p, p, slopes)
        past = (cache_kv_w128[l], cache_kv_w512[l], cache_kv_w2048[l])
        keep_s = tuple(c.shape[1] for c in past)
        ys, sr, si, kvs = _layer(ys, state_ssm_re[l], state_ssm_im[l], past, keep_s, p, slopes)
        p_re.append(hr)
        p_im.append(hi)
        s_re.append(sr)
        s_im.append(si)
        for g in range(N_DIL_GROUPS):
            p_kv[g].append(kvp[g])
            s_kv[g].append(kvs[g])
    return (yp, ys,
            jnp.stack(p_re), jnp.stack(p_im),
            jnp.stack(p_kv[0]), jnp.stack(p_kv[1]), jnp.stack(p_kv[2]),
            jnp.stack(s_re), jnp.stack(s_im),
            jnp.stack(s_kv[0]), jnp.stack(s_kv[1]), jnp.stack(s_kv[2]))
```

```python
import functools
import math

import jax
import jax.numpy as jnp
from jax import lax
from jax.experimental import pallas as pl
from jax.experimental.pallas import tpu as pltpu

F32 = jnp.float32
BF16 = jnp.bfloat16

D_MODEL = 1024
SSM_GROUP = 16
SSM_GROUPS = 64
SSM_STATE = 64
SSM_PAIRS = SSM_GROUPS // 2
HEAD_DIM = 64
HEADS = 4
DIL_WINDOWS = (128, 512, 2048)
DIL_RATES = (1, 4, 16)
N_DIL = 3
KEYS_BACK = 128
ATTN_WIDTH = N_DIL * HEADS * HEAD_DIM
SLOT_WIDTH = HEADS * HEAD_DIM
D_FF = 2816
RMS_EPS = 1e-6
ALIBI_MAX_EXP = 8.0
IN_WIDTH = D_MODEL + 3 * ATTN_WIDTH + 2 * D_MODEL

LANES = 128
SUBLANES = 8
MXU_DIM = 256
VMEM_LIMIT_BYTES = 56 * 1024 * 1024

LANE_BLOCKS = D_MODEL // LANES

SSM_CHUNK = SUBLANES
SSM_ROW_PITCH = 40
FF_CHUNKS = ((0, 1024), (1024, 2048), (2048, 2816))


def _slope(group, head):
    return 2.0 ** (-ALIBI_MAX_EXP * (group * HEADS + head + 1) / (N_DIL * HEADS))


def _const_spec(shape):
    zeros = (0,) * len(shape)
    return pl.BlockSpec(shape, lambda *_: zeros, pipeline_mode=pl.Buffered(1))


def _lane_blocked_spec(rows, row_block):
    return pl.BlockSpec((LANE_BLOCKS, rows, LANES), lambda *idx: (0, row_block(*idx), 0))


def _params(*semantics):
    return pltpu.CompilerParams(dimension_semantics=semantics, vmem_limit_bytes=VMEM_LIMIT_BYTES)


def _rms(x, w):
    return x * lax.rsqrt(jnp.mean(x * x, axis=-1, keepdims=True) + RMS_EPS) * w


def _split_bf16(x, terms):
    parts = []
    for _ in range(terms):
        p = x.astype(BF16)
        parts.append(p)
        x = x - p.astype(F32)
    return parts


def _ffn_body(x_ref, nw_ref, wg_ref, wu_ref, wd_ref, o_ref):
    x = x_ref[...]
    h = _rms(x, nw_ref[...]).astype(BF16)
    acc = jnp.zeros_like(x)
    for lo, hi in FF_CHUNKS:
        g = jnp.dot(h, wg_ref[:, lo:hi], preferred_element_type=F32)
        u = jnp.dot(h, wu_ref[:, lo:hi], preferred_element_type=F32)
        a = (jax.nn.silu(g) * u).astype(BF16)
        acc = acc + jnp.dot(a, wd_ref[lo:hi, :], preferred_element_type=F32)
    o_ref[...] = x + 0.5 * acc


def _ffn(x, norm_w, wg, wu, wd, tm):
    t = x.shape[0]
    row = pl.BlockSpec((tm, D_MODEL), lambda i: (i, 0))
    return pl.pallas_call(
        _ffn_body,
        grid=(t // tm,),
        in_specs=[row, _const_spec((1, D_MODEL)), _const_spec((D_MODEL, D_FF)),
                  _const_spec((D_MODEL, D_FF)), _const_spec((D_FF, D_MODEL))],
        out_specs=row,
        out_shape=jax.ShapeDtypeStruct((t, D_MODEL), F32),
        compiler_params=_params("parallel"),
        name="ffn",
    )(x, norm_w, wg, wu, wd)


def _head_norm(x, gain, sred, sexp):
    ss = sum(jnp.dot(p, sred, preferred_element_type=F32) for p in _split_bf16(x * x, 2))
    r = lax.rsqrt(ss * (1.0 / HEAD_DIM) + RMS_EPS)
    rb = sum(jnp.dot(p, sexp, preferred_element_type=F32) for p in _split_bf16(r, 3))
    return x * rb * gain


def _mix_in_body(x_ref, nw_ref, w_ref, qg_ref, kg_ref, sred_ref, sexp_ref,
                 u_ref, q_ref, k_ref, v_ref, ga_ref, gb_ref):
    h = _rms(x_ref[...], nw_ref[...]).astype(BF16)

    def proj(lo, hi):
        return jnp.dot(h, w_ref[:, lo:hi], preferred_element_type=F32)

    c0 = D_MODEL
    c1 = c0 + ATTN_WIDTH
    c2 = c1 + ATTN_WIDTH
    c3 = c2 + ATTN_WIDTH
    c4 = c3 + D_MODEL
    u = proj(0, c0)
    for blk in range(LANE_BLOCKS):
        u_ref[blk] = u[:, blk * LANES:(blk + 1) * LANES]
    sred = sred_ref[...]
    sexp = sexp_ref[...]
    q_ref[...] = _head_norm(proj(c0, c1), qg_ref[...], sred, sexp) * (HEAD_DIM ** -0.5)
    k_ref[...] = _head_norm(proj(c1, c2), kg_ref[...], sred, sexp)
    v_ref[...] = proj(c2, c3)
    ga_ref[...] = jax.nn.sigmoid(proj(c3, c4))
    gb_ref[...] = jax.nn.sigmoid(proj(c4, IN_WIDTH))


def _mix_in(x, norm_w, w_in, q_gain, k_gain, sred, sexp, tm):
    t = x.shape[0]

    def row(width):
        return pl.BlockSpec((tm, width), lambda i: (i, 0))

    widths = (ATTN_WIDTH, ATTN_WIDTH, ATTN_WIDTH, D_MODEL, D_MODEL)
    return pl.pallas_call(
        _mix_in_body,
        grid=(t // tm,),
        in_specs=[row(D_MODEL), _const_spec((1, D_MODEL)), _const_spec((D_MODEL, IN_WIDTH)),
                  _const_spec((1, ATTN_WIDTH)), _const_spec((1, ATTN_WIDTH)),
                  _const_spec((ATTN_WIDTH, LANES)), _const_spec((LANES, ATTN_WIDTH))],
        out_specs=[_lane_blocked_spec(tm, lambda i: i)] + [row(w) for w in widths],
        out_shape=([jax.ShapeDtypeStruct((LANE_BLOCKS, t, LANES), F32)]
                   + [jax.ShapeDtypeStruct((t, w), F32) for w in widths]),
        compiler_params=_params("parallel"),
        name="mix_in",
    )(x, norm_w, w_in, q_gain, k_gain, sred, sexp)


def _cmul(ar, ai, br, bi):
    return ar * br - ai * bi, ar * bi + ai * br


def _ssm_prep_body(lre_ref, lim_ref, ldt_ref, bre_ref, bim_ref, cre_ref, cim_ref,
                   apow_ref, bbar_ref, ab_ref, ca_ref, kern_ref):
    lr = jnp.minimum(lre_ref[...], -1e-4)
    li = lim_ref[...]
    dt = jnp.exp(ldt_ref[...])
    mag = jnp.exp(lr * dt)
    ar = mag * jnp.cos(li * dt)
    ai = mag * jnp.sin(li * dt)
    nr = ar - 1.0
    ni = ai
    den = lr * lr + li * li
    fr = (nr * lr + ni * li) / den
    fi = (ni * lr - nr * li) / den
    bbr, bbi = _cmul(fr, fi, bre_ref[...], bim_ref[...])
    bbar_ref[0] = bbr
    bbar_ref[1] = bbi
    cre = cre_ref[...]
    cim = cim_ref[...]
    pr = jnp.ones_like(ar)
    pi = jnp.zeros_like(ai)
    nt = (((2,), (2,)), ((0,), (0,)))
    for t in range(SSM_CHUNK + 1):
        apow_ref[0, t] = pr
        apow_ref[1, t] = pi
        car, cai = _cmul(cre, cim, pr, pi)
        ca_ref[0, t] = car
        ca_ref[1, t] = cai
        if t < SSM_CHUNK:
            abr, abi = _cmul(pr, pi, bbr, bbi)
            ab_ref[0, t] = abr
            ab_ref[1, t] = abi
            kern_ref[t] = (
                lax.dot_general(car, bbr, nt, precision=lax.Precision.HIGHEST, preferred_element_type=F32)
                - lax.dot_general(cai, bbi, nt, precision=lax.Precision.HIGHEST, preferred_element_type=F32))
        pr, pi = _cmul(pr, pi, ar, ai)


def _ssm_prep(lam_re, lam_im, log_dt, b_re_t, b_im_t, c_re, c_im):
    g, p, c = SSM_GROUPS, SSM_STATE, SSM_GROUP
    n = SSM_CHUNK
    return pl.pallas_call(
        _ssm_prep_body,
        out_shape=[jax.ShapeDtypeStruct((2, n + 1, g, 1, p), F32),
                   jax.ShapeDtypeStruct((2, g, c, p), F32),
                   jax.ShapeDtypeStruct((2, n, g, c, p), F32),
                   jax.ShapeDtypeStruct((2, n + 1, g, c, p), F32),
                   jax.ShapeDtypeStruct((n, g, c, c), F32)],
        compiler_params=pltpu.CompilerParams(vmem_limit_bytes=VMEM_LIMIT_BYTES),
        name="ssm_prep",
    )(lam_re.reshape(g, 1, p), lam_im.reshape(g, 1, p), log_dt.reshape(g, 1, 1), b_re_t, b_im_t, c_re, c_im)


def _pair_block_diag(m):
    g, r, c = m.shape
    m = m.reshape(g // 2, 2, r, c)
    z = jnp.zeros_like(m[:, 0])
    top = jnp.concatenate([m[:, 0], z], axis=2)
    bot = jnp.concatenate([z, m[:, 1]], axis=2)
    return jnp.concatenate([top, bot], axis=1)


def _ssm_chunk_matrices(ab, ca, kern):
    n, g, c, p = SSM_CHUNK, SSM_GROUPS, SSM_GROUP, SSM_STATE
    s_idx = jnp.arange(n)[:, None]
    t_idx = jnp.arange(n)[None, :]
    lag = jnp.clip(t_idx - s_idx, 0, n - 1)
    blocks = kern[lag]
    blocks = jnp.where((t_idx >= s_idx)[:, :, None, None, None], blocks, 0.0)
    toep = blocks.transpose(2, 0, 4, 1, 3).reshape(g, n * c, n * c)
    toep = _pair_block_diag(toep)
    rev = ab[:, ::-1]
    win = rev.transpose(0, 2, 1, 3, 4).reshape(2, g, n * c, p)
    win = jnp.concatenate([_pair_block_diag(win[0]), _pair_block_diag(win[1])], axis=2)
    ca1 = ca[:, 1:]
    wout = ca1.transpose(0, 2, 4, 1, 3).reshape(2, g, p, n * c)
    wout = jnp.concatenate([_pair_block_diag(wout[0]), -_pair_block_diag(wout[1])], axis=1)
    return toep.astype(BF16), win.astype(BF16), wout.astype(BF16)


def _fold_tokens(load, n_src, n_dst, out_of):
    block = lax.broadcasted_iota(jnp.int32, (1, LANES), 1) // SSM_GROUP
    for a in range(n_dst):
        acc = None
        for b in range(n_src):
            x = load(b)
            shift = (SSM_GROUP * (b - a)) % LANES
            if shift:
                x = pltpu.roll(x, shift, 1)
            acc = jnp.where(block == b, x, 0.0 if acc is None else acc)
        out_of(a, acc)


def _ssm_body(u_ref, toep_ref, win_ref, wout_ref, a8_ref, y_ref, hfin_ref,
              lhs_ref, bst_ref, hst_ref, yfl_ref, carry_ref, *, rows):
    i = pl.program_id(1)
    n_groups_per_vreg = LANES // SSM_GROUP

    @pl.when(i == 0)
    def _():
        carry_ref[...] = jnp.zeros_like(carry_ref)

    for gb in range(LANE_BLOCKS):

        def put(q, val, gb=gb):
            pair = gb * (n_groups_per_vreg // 2) + q // 2
            lhs_ref[pair, :, (q % 2) * LANES:(q % 2 + 1) * LANES] = val.astype(BF16)

        _fold_tokens(lambda s, gb=gb: u_ref[gb, pl.ds(s, rows, stride=SSM_CHUNK), :],
                     SSM_CHUNK, n_groups_per_vreg, put)

    def state_in(r, _):
        b = jnp.dot(lhs_ref[r], win_ref[r], preferred_element_type=F32)
        bst_ref[0, pl.ds(r, rows, stride=SSM_ROW_PITCH), :] = b[:, :LANES]
        bst_ref[1, pl.ds(r, rows, stride=SSM_ROW_PITCH), :] = b[:, LANES:]
        return 0

    lax.fori_loop(0, SSM_PAIRS, state_in, 0)

    a_re = a8_ref[0]
    a_im = a8_ref[1]

    def step(j, h):
        h_re, h_im = h
        base = pl.multiple_of(j * SSM_ROW_PITCH, SUBLANES)
        hst_ref[0, pl.ds(base, SSM_PAIRS), :] = h_re
        hst_ref[1, pl.ds(base, SSM_PAIRS), :] = h_im
        n_re = a_re * h_re - a_im * h_im + bst_ref[0, pl.ds(base, SSM_PAIRS), :]
        n_im = a_re * h_im + a_im * h_re + bst_ref[1, pl.ds(base, SSM_PAIRS), :]
        return n_re, n_im

    h_re, h_im = lax.fori_loop(0, rows, step, (carry_ref[0], carry_ref[1]))
    carry_ref[0] = h_re
    carry_ref[1] = h_im
    hfin_ref[0] = h_re
    hfin_ref[1] = h_im

    def chunk_out(r, _):
        hcat = jnp.concatenate([hst_ref[0, pl.ds(r, rows, stride=SSM_ROW_PITCH), :],
                                hst_ref[1, pl.ds(r, rows, stride=SSM_ROW_PITCH), :]], axis=1).astype(BF16)
        yfl_ref[r] = (jnp.dot(lhs_ref[r], toep_ref[r], preferred_element_type=F32)
                      + jnp.dot(hcat, wout_ref[r], preferred_element_type=F32))
        return 0

    lax.fori_loop(0, SSM_PAIRS, chunk_out, 0)

    for gb in range(LANE_BLOCKS):

        def get(q, gb=gb):
            pair = gb * (n_groups_per_vreg // 2) + q // 2
            return yfl_ref[pair, :, (q % 2) * LANES:(q % 2 + 1) * LANES]

        def put(t, val, gb=gb):
            y_ref[gb, pl.ds(t, rows, stride=SSM_CHUNK), :] = val

        _fold_tokens(get, n_groups_per_vreg, SSM_CHUNK, put)


def _ssm_prompt(u, toep, win, wout, a8, batch, seq, tile):
    rows = tile // SSM_CHUNK
    n_tiles = seq // tile
    tok = _lane_blocked_spec(tile, lambda b, i: b * n_tiles + i)
    pair_w = _const_spec((SSM_PAIRS, MXU_DIM, MXU_DIM))
    return pl.pallas_call(
        functools.partial(_ssm_body, rows=rows),
        grid=(batch, n_tiles),
        in_specs=[tok, pair_w, pair_w, pair_w, _const_spec((2, SSM_PAIRS, LANES))],
        out_specs=[tok, pl.BlockSpec((None, 2, SSM_PAIRS, LANES), lambda b, i: (b, 0, 0, 0))],
        out_shape=[jax.ShapeDtypeStruct((LANE_BLOCKS, batch * seq, LANES), F32),
                   jax.ShapeDtypeStruct((batch, 2, SSM_PAIRS, LANES), F32)],
        scratch_shapes=[pltpu.VMEM((SSM_PAIRS, rows, MXU_DIM), BF16),
                        pltpu.VMEM((2, rows * SSM_ROW_PITCH, LANES), F32),
                        pltpu.VMEM((2, rows * SSM_ROW_PITCH, LANES), F32),
                        pltpu.VMEM((SSM_PAIRS, rows, MXU_DIM), F32),
                        pltpu.VMEM((2, SSM_PAIRS, LANES), F32)],
        compiler_params=_params("parallel", "arbitrary"),
        name="ssm_prompt",
    )(u, toep, win, wout, a8)


def _ssm_step_body(u_ref, hre_ref, him_ref, abar_ref, bbar_ref, cre_ref, cim_ref,
                   y_ref, ore_ref, oim_ref):
    hp = lax.Precision.HIGHEST
    nt = (((1,), (1,)), ((), ()))
    groups_per_block = LANES // SSM_GROUP
    for g in range(SSM_GROUPS):
        blk = g // groups_per_block
        ch = slice((g % groups_per_block) * SSM_GROUP, (g % groups_per_block + 1) * SSM_GROUP)
        st = slice(g * SSM_STATE, (g + 1) * SSM_STATE)
        ug = u_ref[blk, :, ch]
        bu_re = jnp.dot(ug, bbar_ref[0, g], precision=hp, preferred_element_type=F32)
        bu_im = jnp.dot(ug, bbar_ref[1, g], precision=hp, preferred_element_type=F32)
        a_re = abar_ref[0, g]
        a_im = abar_ref[1, g]
        h_re = hre_ref[:, st]
        h_im = him_ref[:, st]
        n_re = a_re * h_re - a_im * h_im + bu_re
        n_im = a_re * h_im + a_im * h_re + bu_im
        ore_ref[:, st] = n_re
        oim_ref[:, st] = n_im
        y_ref[blk, :, ch] = (lax.dot_general(n_re, cre_ref[g], nt, precision=hp, preferred_element_type=F32)
                        - lax.dot_general(n_im, cim_ref[g], nt, precision=hp, preferred_element_type=F32))


def _ssm_step(u, h_re, h_im, abar, bbar, c_re, c_im):
    b = u.shape[1]
    width = SSM_GROUPS * SSM_STATE
    return pl.pallas_call(
        _ssm_step_body,
        out_shape=[jax.ShapeDtypeStruct((LANE_BLOCKS, b, LANES), F32),
                   jax.ShapeDtypeStruct((b, width), F32),
                   jax.ShapeDtypeStruct((b, width), F32)],
        compiler_params=pltpu.CompilerParams(vmem_limit_bytes=VMEM_LIMIT_BYTES),
        name="ssm_step",
    )(u, h_re, h_im, abar, bbar, c_re, c_im)


def _attn_body(q_ref, kc_ref, kp_ref, vc_ref, vp_ref, o_ref, l_ref, *, group, dilation):
    i = pl.program_id(2)
    tq = q_ref.shape[0]
    q = q_ref[...].astype(BF16)
    k = jnp.concatenate([kp_ref[...], kc_ref[...]], axis=0).astype(BF16)
    v = jnp.concatenate([vp_ref[...], vc_ref[...]], axis=0).astype(BF16)
    row = lax.broadcasted_iota(jnp.int32, (tq, 2 * tq), 0)
    col = lax.broadcasted_iota(jnp.int32, (tq, 2 * tq), 1)
    back = row + tq - col
    valid = (back >= 0) & (back <= KEYS_BACK) & ((col >= tq) | (i > 0))
    dist = (back * dilation).astype(F32)
    outs, lses = [], []
    for h in range(HEADS):
        hd = slice(h * HEAD_DIM, (h + 1) * HEAD_DIM)
        s = lax.dot_general(q[:, hd], k[:, hd], (((1,), (1,)), ((), ())), preferred_element_type=F32)
        s = jnp.where(valid, s - _slope(group, h) * dist, -jnp.inf)
        m = jnp.max(s, axis=-1, keepdims=True)
        p = jnp.exp(s - m)
        den = jnp.sum(p, axis=-1, keepdims=True)
        o = jnp.dot(p.astype(BF16), v[:, hd], preferred_element_type=F32) / den
        outs.append(o)
        lses.append(jnp.broadcast_to(m + jnp.log(den), (tq, HEAD_DIM)))
    o_ref[...] = jnp.concatenate(outs, axis=1)
    l_ref[...] = jnp.concatenate(lses, axis=1)


def _attn_prompt(q, k, v, group, batch, seq):
    d = DIL_RATES[group]
    tq = KEYS_BACK
    n_blocks = seq // d // tq
    qv = q.reshape(batch, seq // d, d * ATTN_WIDTH)
    kv = k.reshape(batch, seq // d, d * ATTN_WIDTH)
    vv = v.reshape(batch, seq // d, d * ATTN_WIDTH)
    cur = pl.BlockSpec((None, tq, SLOT_WIDTH), lambda b, r, i: (b, i, r * N_DIL + group))
    prev = pl.BlockSpec((None, tq, SLOT_WIDTH), lambda b, r, i: (b, jnp.maximum(i - 1, 0), r * N_DIL + group))
    out = pl.BlockSpec((None, tq, SLOT_WIDTH), lambda b, r, i: (b, i, r))
    shape = jax.ShapeDtypeStruct((batch, seq // d, d * SLOT_WIDTH), F32)
    o, lse = pl.pallas_call(
        functools.partial(_attn_body, group=group, dilation=d),
        grid=(batch, d, n_blocks),
        in_specs=[cur, cur, prev, cur, prev],
        out_specs=[out, out],
        out_shape=[shape, shape],
        compiler_params=_params("parallel", "parallel", "arbitrary"),
        name="attn_prompt_%d" % DIL_WINDOWS[group],
    )(qv, kv, kv, vv, vv)
    return o.reshape(batch * seq, SLOT_WIDTH), lse.reshape(batch * seq, SLOT_WIDTH)


def _as_column(row_vec):
    n = row_vec.shape[1]
    eye = lax.broadcasted_iota(jnp.int32, (n, n), 0) == lax.broadcasted_iota(jnp.int32, (n, n), 1)
    return jnp.sum(jnp.where(eye, row_vec, 0.0), axis=1, keepdims=True)


def _attn_sample_body(q_ref, k_ref, v_ref, c0_ref, c1_ref, c2_ref, o_ref, l_ref, n0_ref, n1_ref, n2_ref):
    b = pl.program_id(0)
    hp = lax.Precision.HIGHEST
    q_row = q_ref[pl.ds(b, 1), :]
    k_row = k_ref[pl.ds(b, 1), :]
    v_row = v_ref[pl.ds(b, 1), :]
    for g, (c_ref, n_ref) in enumerate(((c0_ref, n0_ref), (c1_ref, n1_ref), (c2_ref, n2_ref))):
        w = DIL_WINDOWS[g]
        d = DIL_RATES[g]
        lane = lax.broadcasted_iota(jnp.int32, (1, w), 1)
        dist = w - lane
        valid = (dist & (d - 1)) == 0
        for h in range(HEADS):
            at = g * SLOT_WIDTH + h * HEAD_DIM
            q_h = q_row[:, at:at + HEAD_DIM]
            k_new = k_row[:, at:at + HEAD_DIM]
            v_new = v_row[:, at:at + HEAD_DIM]
            kt = c_ref[0, h]
            vt = c_ref[1, h]
            q8 = jnp.broadcast_to(q_h, (SUBLANES, HEAD_DIM))
            s = jnp.dot(q8, kt, precision=hp, preferred_element_type=F32)[0:1]
            s = jnp.where(valid, s - _slope(g, h) * dist.astype(F32), -jnp.inf)
            s_new = jnp.sum(q_h * k_new, axis=1, keepdims=True)
            m = jnp.maximum(jnp.max(s, axis=1, keepdims=True), s_new)
            p = jnp.exp(s - m)
            p_new = jnp.exp(s_new - m)
            den = jnp.sum(p, axis=1, keepdims=True) + p_new
            p8 = jnp.broadcast_to(p, (SUBLANES, w))
            pv = lax.dot_general(p8, vt, (((1,), (1,)), ((), ())), precision=hp,
                                 preferred_element_type=F32)[0:1]
            col = slice(at, at + HEAD_DIM)
            o_ref[:, col] = (pv + p_new * v_new) / den
            l_ref[:, col] = jnp.broadcast_to(m + jnp.log(den), (1, HEAD_DIM))
            last = lax.broadcasted_iota(jnp.int32, (HEAD_DIM, w), 1) == w - 1
            n_ref[0, h] = jnp.where(last, _as_column(k_new), pltpu.roll(kt, w - 1, 1))
            n_ref[1, h] = jnp.where(last, _as_column(v_new), pltpu.roll(vt, w - 1, 1))


def _attn_sample(q, k, v, caches):
    b = q.shape[0]
    full = _const_spec((b, ATTN_WIDTH))
    win = [pl.BlockSpec((None, 2, HEADS, HEAD_DIM, w), lambda i: (i, 0, 0, 0, 0)) for w in DIL_WINDOWS]
    row = pl.BlockSpec((None, 1, ATTN_WIDTH), lambda i: (i, 0, 0))
    row_shape = jax.ShapeDtypeStruct((b, 1, ATTN_WIDTH), F32)
    outs = pl.pallas_call(
        _attn_sample_body,
        grid=(b,),
        in_specs=[full, full, full] + win,
        out_specs=[row, row] + win,
        out_shape=[row_shape, row_shape] + [jax.ShapeDtypeStruct(c.shape, F32) for c in caches],
        compiler_params=_params("parallel"),
        name="attn_sample",
    )(q, k, v, *caches)
    return outs[0].reshape(b, ATTN_WIDTH), outs[1].reshape(b, ATTN_WIDTH), outs[2:]


def _mix_out_body(x_ref, y_ref, u_ref, d_ref, o0_ref, o1_ref, o2_ref, l0_ref, l1_ref, l2_ref,
                  ga_ref, gb_ref, wglu_ref, wpa_ref, wpb_ref, wout_ref, out_ref):
    y_raw = jnp.concatenate([y_ref[blk] for blk in range(LANE_BLOCKS)], axis=1)
    u = jnp.concatenate([u_ref[blk] for blk in range(LANE_BLOCKS)], axis=1)
    y = jax.nn.gelu(y_raw + d_ref[...] * u)
    yb = y.astype(BF16)
    y_a = y * jax.nn.sigmoid(jnp.dot(yb, wglu_ref[...], preferred_element_type=F32))
    branch_a = jnp.dot(y_a.astype(BF16), wpa_ref[...], preferred_element_type=F32)

    l0, l1, l2 = l0_ref[...], l1_ref[...], l2_ref[...]
    m = jnp.maximum(jnp.maximum(l0, l1), l2)
    e0, e1, e2 = jnp.exp(l0 - m), jnp.exp(l1 - m), jnp.exp(l2 - m)
    o_b = (e0 * o0_ref[...] + e1 * o1_ref[...] + e2 * o2_ref[...]) / (e0 + e1 + e2)
    branch_b = jnp.dot(o_b.astype(BF16), wpb_ref[...], preferred_element_type=F32)

    merged = ga_ref[...] * branch_a + gb_ref[...] * branch_b
    out_ref[...] = x_ref[...] + jnp.dot(merged.astype(BF16), wout_ref[...], preferred_element_type=F32)


def _mix_out(x, y, u, ssm_d, o, lse, ga, gb, w_glu, w_pa, w_pb, w_out, tm):
    t = x.shape[0]

    def row(width):
        return pl.BlockSpec((tm, width), lambda i: (i, 0))

    wide = row(D_MODEL)
    slot = row(SLOT_WIDTH)
    blocked = _lane_blocked_spec(tm, lambda i: i)
    return pl.pallas_call(
        _mix_out_body,
        grid=(t // tm,),
        in_specs=[wide, blocked, blocked, _const_spec((1, D_MODEL)), slot, slot, slot, slot, slot, slot, wide, wide,
                  _const_spec((D_MODEL, D_MODEL)), _const_spec((D_MODEL, D_MODEL)),
                  _const_spec((SLOT_WIDTH, D_MODEL)), _const_spec((D_MODEL, D_MODEL))],
        out_specs=wide,
        out_shape=jax.ShapeDtypeStruct((t, D_MODEL), F32),
        compiler_params=_params("parallel"),
        name="mix_out",
    )(x, y, u, ssm_d, *o, *lse, ga, gb, w_glu, w_pa, w_pb, w_out)


def _segment_matrices():
    lane = jnp.arange(ATTN_WIDTH) // HEAD_DIM
    sred = (lane[:, None] == jnp.arange(LANES)[None, :]).astype(BF16)
    return sred, sred.T


def kernel(x_prompt, x_sample, state_ssm_re, state_ssm_im, cache_kv_w128, cache_kv_w512, cache_kv_w2048, ffn1_norm, ffn1_w_gate, ffn1_w_up, ffn1_w_down, mix_norm, w_in, ssm_lambda_re, ssm_lambda_im, ssm_b_re, ssm_b_im, ssm_c_re, ssm_c_im, ssm_d, ssm_log_dt, w_glu, q_gain, k_gain, w_proj_a, w_proj_b, w_out, ffn2_norm, ffn2_w_gate, ffn2_w_up, ffn2_w_down):
    depth = ffn1_norm.shape[0]
    assert depth == 1, "single-layer step"
    batch, seq, _ = x_prompt.shape
    dec_batch, dec_seq, _ = x_sample.shape
    assert dec_seq == 1 and seq % (DIL_RATES[-1] * KEYS_BACK) == 0
    layer = 0
    bf = lambda w: w[layer].astype(BF16)
    vec = lambda w: w[layer][None]
    ffn1 = (vec(ffn1_norm), bf(ffn1_w_gate), bf(ffn1_w_up), bf(ffn1_w_down))
    ffn2 = (vec(ffn2_norm), bf(ffn2_w_gate), bf(ffn2_w_up), bf(ffn2_w_down))
    mix_norm, ssm_d = vec(mix_norm), vec(ssm_d)
    w_in_b, w_glu_b, w_pa_b, w_pb_b, w_out_b = bf(w_in), bf(w_glu), bf(w_proj_a), bf(w_proj_b), bf(w_out)
    q_gain_t = jnp.tile(vec(q_gain), (1, N_DIL * HEADS))
    k_gain_t = jnp.tile(vec(k_gain), (1, N_DIL * HEADS))
    sred, sexp = _segment_matrices()

    apow, bbar, ab, ca, kern = _ssm_prep(
        ssm_lambda_re[layer], ssm_lambda_im[layer], ssm_log_dt[layer],
        ssm_b_re[layer].transpose(0, 2, 1), ssm_b_im[layer].transpose(0, 2, 1),
        ssm_c_re[layer], ssm_c_im[layer])
    toep, win, wout = _ssm_chunk_matrices(ab, ca, kern)
    a8 = apow[:, SSM_CHUNK].reshape(2, SSM_PAIRS, LANES)

    def front(x, tm):
        x1 = _ffn(x, *ffn1, tm)
        return (x1,) + tuple(_mix_in(x1, mix_norm, w_in_b, q_gain_t, k_gain_t, sred, sexp, tm))

    def back(x1, y, u, o, lse, ga, gb, tm):
        x2 = _mix_out(x1, y, u, ssm_d, o, lse, ga, gb, w_glu_b, w_pa_b, w_pb_b, w_out_b, tm)
        return _ffn(x2, *ffn2, tm)

    tm = 512
    xp = x_prompt.reshape(batch * seq, D_MODEL)
    x1, u, q, k, v, ga, gb = front(xp, tm)
    y, h_fin = _ssm_prompt(u, toep, win, wout, a8, batch, seq, tile=1024)
    attn = [_attn_prompt(q, k, v, g, batch, seq) for g in range(N_DIL)]
    yp = back(x1, y, u, [a[0] for a in attn], [a[1] for a in attn], ga, gb, tm)
    yp = yp.reshape(batch, seq, D_MODEL)
    h_fin = h_fin.reshape(batch, 2, SSM_GROUPS, SSM_STATE)
    p_re = h_fin[None, :, 0]
    p_im = h_fin[None, :, 1]
    k5 = k.reshape(batch, seq, N_DIL, HEADS, HEAD_DIM)
    v5 = v.reshape(batch, seq, N_DIL, HEADS, HEAD_DIM)
    p_kv = []
    for g, w in enumerate(DIL_WINDOWS):
        keep = min(w, seq)
        p_kv.append(jnp.stack([k5[:, seq - keep:, g], v5[:, seq - keep:, g]], axis=2)[None])

    xs = x_sample.reshape(dec_batch, D_MODEL)
    x1, u, q, k, v, ga, gb = front(xs, dec_batch)
    width = SSM_GROUPS * SSM_STATE
    y, s_re, s_im = _ssm_step(u, state_ssm_re[layer].reshape(dec_batch, width),
                              state_ssm_im[layer].reshape(dec_batch, width),
                              apow[:, 1], bbar, ssm_c_re[layer], ssm_c_im[layer])
    caches = [c[layer].transpose(0, 2, 3, 4, 1) for c in (cache_kv_w128, cache_kv_w512, cache_kv_w2048)]
    o, lse, new_caches = _attn_sample(q, k, v, caches)
    split = lambda a: [a[:, g * SLOT_WIDTH:(g + 1) * SLOT_WIDTH] for g in range(N_DIL)]
    ys = back(x1, y, u, split(o), split(lse), ga, gb, dec_batch)
    ys = ys.reshape(dec_batch, 1, D_MODEL)
    s_re = s_re.reshape(1, dec_batch, SSM_GROUPS, SSM_STATE)
    s_im = s_im.reshape(1, dec_batch, SSM_GROUPS, SSM_STATE)
    s_kv = [c.transpose(0, 4, 1, 2, 3)[None] for c in new_caches]

    return (yp, ys, p_re, p_im, p_kv[0], p_kv[1], p_kv[2], s_re, s_im, s_kv[0], s_kv[1], s_kv[2])
```

```python
import functools
import math

import jax
import jax.numpy as jnp
from jax import lax
from jax.experimental import pallas as pl
from jax.experimental.pallas import tpu as pltpu

F32 = jnp.float32
BF16 = jnp.bfloat16

D_MODEL = 1024
SSM_GROUP = 16
SSM_GROUPS = 64
SSM_STATE = 64
SSM_PAIRS = SSM_GROUPS // 2
HEAD_DIM = 64
HEADS = 4
DIL_WINDOWS = (128, 512, 2048)
DIL_RATES = (1, 4, 16)
N_DIL = 3
KEYS_BACK = 128
ATTN_WIDTH = N_DIL * HEADS * HEAD_DIM
SLOT_WIDTH = HEADS * HEAD_DIM
D_FF = 2816
RMS_EPS = 1e-6
ALIBI_MAX_EXP = 8.0
IN_WIDTH = D_MODEL + 3 * ATTN_WIDTH + 2 * D_MODEL

LANES = 128
SUBLANES = 8
MXU_DIM = 256
VMEM_LIMIT_BYTES = 56 * 1024 * 1024

LANE_BLOCKS = D_MODEL // LANES

SSM_CHUNK = SUBLANES
SSM_ROW_PITCH = 40
FF_CHUNKS = ((0, 1024), (1024, 2048), (2048, 2816))


def _slope(group, head):
    return 2.0 ** (-ALIBI_MAX_EXP * (group * HEADS + head + 1) / (N_DIL * HEADS))


def _const_spec(shape):
    zeros = (0,) * len(shape)
    return pl.BlockSpec(shape, lambda *_: zeros, pipeline_mode=pl.Buffered(1))


def _lane_blocked_spec(rows, row_block):
    return pl.BlockSpec((LANE_BLOCKS, rows, LANES), lambda *idx: (0, row_block(*idx), 0))


def _params(*semantics):
    return pltpu.CompilerParams(dimension_semantics=semantics, vmem_limit_bytes=VMEM_LIMIT_BYTES)


def _rms(x, w):
    return x * lax.rsqrt(jnp.mean(x * x, axis=-1, keepdims=True) + RMS_EPS) * w


def _split_bf16(x, terms):
    parts = []
    for _ in range(terms):
        p = x.astype(BF16)
        parts.append(p)
        x = x - p.astype(F32)
    return parts


def _ffn_body(x_ref, nw_ref, wg_ref, wu_ref, wd_ref, o_ref):
    x = x_ref[...]
    h = _rms(x, nw_ref[...]).astype(BF16)
    acc = jnp.zeros_like(x)
    for lo, hi in FF_CHUNKS:
        g = jnp.dot(h, wg_ref[:, lo:hi], preferred_element_type=F32)
        u = jnp.dot(h, wu_ref[:, lo:hi], preferred_element_type=F32)
        a = (jax.nn.silu(g) * u).astype(BF16)
        acc = acc + jnp.dot(a, wd_ref[lo:hi, :], preferred_element_type=F32)
    o_ref[...] = x + 0.5 * acc


def _ffn(x, norm_w, wg, wu, wd, tm):
    t = x.shape[0]
    row = pl.BlockSpec((tm, D_MODEL), lambda i: (i, 0))
    return pl.pallas_call(
        _ffn_body,
        grid=(t // tm,),
        in_specs=[row, _const_spec((1, D_MODEL)), _const_spec((D_MODEL, D_FF)),
                  _const_spec((D_MODEL, D_FF)), _const_spec((D_FF, D_MODEL))],
        out_specs=row,
        out_shape=jax.ShapeDtypeStruct((t, D_MODEL), F32),
        compiler_params=_params("parallel"),
        name="ffn",
    )(x, norm_w, wg, wu, wd)


def _head_norm(x, gain, sred, sexp):
    ss = sum(jnp.dot(p, sred, preferred_element_type=F32) for p in _split_bf16(x * x, 2))
    r = lax.rsqrt(ss * (1.0 / HEAD_DIM) + RMS_EPS)
    rb = sum(jnp.dot(p, sexp, preferred_element_type=F32) for p in _split_bf16(r, 3))
    return x * rb * gain


def _mix_in_body(x_ref, nw_ref, w_ref, qg_ref, kg_ref, sred_ref, sexp_ref,
                 u_ref, q_ref, k_ref, v_ref, ga_ref, gb_ref, *by_residue):
    h = _rms(x_ref[...], nw_ref[...]).astype(BF16)

    def proj(lo, hi):
        return jnp.dot(h, w_ref[:, lo:hi], preferred_element_type=F32)

    c0 = D_MODEL
    c1 = c0 + ATTN_WIDTH
    c2 = c1 + ATTN_WIDTH
    c3 = c2 + ATTN_WIDTH
    c4 = c3 + D_MODEL
    u = proj(0, c0)
    for blk in range(LANE_BLOCKS):
        u_ref[blk] = u[:, blk * LANES:(blk + 1) * LANES]
    sred = sred_ref[...]
    sexp = sexp_ref[...]
    q = _head_norm(proj(c0, c1), qg_ref[...], sred, sexp) * (HEAD_DIM ** -0.5)
    k = _head_norm(proj(c1, c2), kg_ref[...], sred, sexp)
    v = proj(c2, c3)
    q_ref[...] = q
    k_ref[...] = k
    v_ref[...] = v
    ga_ref[...] = jax.nn.sigmoid(proj(c3, c4))
    gb_ref[...] = jax.nn.sigmoid(proj(c4, IN_WIDTH))

    if by_residue:
        stage_ref = by_residue[-1]
        rows = x_ref.shape[0]
        slabs = SLOT_WIDTH // LANES
        for ti, x in enumerate((q, k, v)):
            outs = by_residue[ti * N_DIL:(ti + 1) * N_DIL]
            outs[0][0] = x[:, :SLOT_WIDTH].astype(BF16)
            for g in range(1, N_DIL):
                d = DIL_RATES[g]
                for s in range(slabs):
                    lo = g * SLOT_WIDTH + s * LANES
                    stage_ref[ti, (g - 1) * slabs + s] = x[:, lo:lo + LANES]
                for r in range(d):
                    for s in range(slabs):
                        piece = stage_ref[ti, (g - 1) * slabs + s, pl.ds(r, rows // d, stride=d), :]
                        outs[g][r, :, s * LANES:(s + 1) * LANES] = piece.astype(BF16)


def _mix_in(x, norm_w, w_in, q_gain, k_gain, sred, sexp, tm, seq=None):
    t = x.shape[0]

    def row(width):
        return pl.BlockSpec((tm, width), lambda i: (i, 0))

    widths = (ATTN_WIDTH, ATTN_WIDTH, ATTN_WIDTH, D_MODEL, D_MODEL)
    out_specs = [_lane_blocked_spec(tm, lambda i: i)] + [row(w) for w in widths]
    out_shape = ([jax.ShapeDtypeStruct((LANE_BLOCKS, t, LANES), F32)]
                 + [jax.ShapeDtypeStruct((t, w), F32) for w in widths])
    scratch = []
    if seq is not None:
        tiles = seq // tm
        for _ in range(3):
            for d in DIL_RATES:
                out_specs.append(pl.BlockSpec((None, d, tm // d, SLOT_WIDTH),
                                              lambda i: (i // tiles, 0, i % tiles, 0)))
                out_shape.append(jax.ShapeDtypeStruct((t // seq, d, seq // d, SLOT_WIDTH), BF16))
        scratch = [pltpu.VMEM((3, (N_DIL - 1) * SLOT_WIDTH // LANES, tm, LANES), F32)]
    return pl.pallas_call(
        _mix_in_body,
        grid=(t // tm,),
        in_specs=[row(D_MODEL), _const_spec((1, D_MODEL)), _const_spec((D_MODEL, IN_WIDTH)),
                  _const_spec((1, ATTN_WIDTH)), _const_spec((1, ATTN_WIDTH)),
                  _const_spec((ATTN_WIDTH, LANES)), _const_spec((LANES, ATTN_WIDTH))],
        out_specs=out_specs,
        out_shape=out_shape,
        scratch_shapes=scratch,
        compiler_params=_params("parallel"),
        name="mix_in",
    )(x, norm_w, w_in, q_gain, k_gain, sred, sexp)


def _cmul(ar, ai, br, bi):
    return ar * br - ai * bi, ar * bi + ai * br


def _ssm_prep_body(lre_ref, lim_ref, ldt_ref, bre_ref, bim_ref, cre_ref, cim_ref,
                   apow_ref, bbar_ref, ab_ref, ca_ref, kern_ref):
    lr = jnp.minimum(lre_ref[...], -1e-4)
    li = lim_ref[...]
    dt = jnp.exp(ldt_ref[...])
    mag = jnp.exp(lr * dt)
    ar = mag * jnp.cos(li * dt)
    ai = mag * jnp.sin(li * dt)
    nr = ar - 1.0
    ni = ai
    den = lr * lr + li * li
    fr = (nr * lr + ni * li) / den
    fi = (ni * lr - nr * li) / den
    bbr, bbi = _cmul(fr, fi, bre_ref[...], bim_ref[...])
    bbar_ref[0] = bbr
    bbar_ref[1] = bbi
    cre = cre_ref[...]
    cim = cim_ref[...]
    pr = jnp.ones_like(ar)
    pi = jnp.zeros_like(ai)
    nt = (((2,), (2,)), ((0,), (0,)))
    for t in range(SSM_CHUNK + 1):
        apow_ref[0, t] = pr
        apow_ref[1, t] = pi
        car, cai = _cmul(cre, cim, pr, pi)
        ca_ref[0, t] = car
        ca_ref[1, t] = cai
        if t < SSM_CHUNK:
            abr, abi = _cmul(pr, pi, bbr, bbi)
            ab_ref[0, t] = abr
            ab_ref[1, t] = abi
            kern_ref[t] = (
                lax.dot_general(car, bbr, nt, precision=lax.Precision.HIGHEST, preferred_element_type=F32)
                - lax.dot_general(cai, bbi, nt, precision=lax.Precision.HIGHEST, preferred_element_type=F32))
        pr, pi = _cmul(pr, pi, ar, ai)


def _ssm_prep(lam_re, lam_im, log_dt, b_re_t, b_im_t, c_re, c_im):
    g, p, c = SSM_GROUPS, SSM_STATE, SSM_GROUP
    n = SSM_CHUNK
    return pl.pallas_call(
        _ssm_prep_body,
        out_shape=[jax.ShapeDtypeStruct((2, n + 1, g, 1, p), F32),
                   jax.ShapeDtypeStruct((2, g, c, p), F32),
                   jax.ShapeDtypeStruct((2, n, g, c, p), F32),
                   jax.ShapeDtypeStruct((2, n + 1, g, c, p), F32),
                   jax.ShapeDtypeStruct((n, g, c, c), F32)],
        compiler_params=pltpu.CompilerParams(vmem_limit_bytes=VMEM_LIMIT_BYTES),
        name="ssm_prep",
    )(lam_re.reshape(g, 1, p), lam_im.reshape(g, 1, p), log_dt.reshape(g, 1, 1), b_re_t, b_im_t, c_re, c_im)


def _pair_block_diag(m):
    g, r, c = m.shape
    m = m.reshape(g // 2, 2, r, c)
    z = jnp.zeros_like(m[:, 0])
    top = jnp.concatenate([m[:, 0], z], axis=2)
    bot = jnp.concatenate([z, m[:, 1]], axis=2)
    return jnp.concatenate([top, bot], axis=1)


def _ssm_chunk_matrices(ab, ca, kern):
    n, g, c, p = SSM_CHUNK, SSM_GROUPS, SSM_GROUP, SSM_STATE
    s_idx = jnp.arange(n)[:, None]
    t_idx = jnp.arange(n)[None, :]
    lag = jnp.clip(t_idx - s_idx, 0, n - 1)
    blocks = kern[lag]
    blocks = jnp.where((t_idx >= s_idx)[:, :, None, None, None], blocks, 0.0)
    toep = blocks.transpose(2, 0, 4, 1, 3).reshape(g, n * c, n * c)
    toep = _pair_block_diag(toep)
    rev = ab[:, ::-1]
    win = rev.transpose(0, 2, 1, 3, 4).reshape(2, g, n * c, p)
    win = jnp.concatenate([_pair_block_diag(win[0]), _pair_block_diag(win[1])], axis=2)
    ca1 = ca[:, 1:]
    wout = ca1.transpose(0, 2, 4, 1, 3).reshape(2, g, p, n * c)
    wout = jnp.concatenate([_pair_block_diag(wout[0]), -_pair_block_diag(wout[1])], axis=1)
    return toep.astype(BF16), win.astype(BF16), wout.astype(BF16)


def _fold_tokens(load, n_src, n_dst, out_of):
    block = lax.broadcasted_iota(jnp.int32, (1, LANES), 1) // SSM_GROUP
    for a in range(n_dst):
        acc = None
        for b in range(n_src):
            x = load(b)
            shift = (SSM_GROUP * (b - a)) % LANES
            if shift:
                x = pltpu.roll(x, shift, 1)
            acc = jnp.where(block == b, x, 0.0 if acc is None else acc)
        out_of(a, acc)


def _ssm_body(u_ref, toep_ref, win_ref, wout_ref, a8_ref, y_ref, hfin_ref,
              lhs_ref, bst_ref, hst_ref, yfl_ref, carry_ref, *, rows):
    i = pl.program_id(1)
    n_groups_per_vreg = LANES // SSM_GROUP

    @pl.when(i == 0)
    def _():
        carry_ref[...] = jnp.zeros_like(carry_ref)

    for gb in range(LANE_BLOCKS):

        def put(q, val, gb=gb):
            pair = gb * (n_groups_per_vreg // 2) + q // 2
            lhs_ref[pair, :, (q % 2) * LANES:(q % 2 + 1) * LANES] = val.astype(BF16)

        _fold_tokens(lambda s, gb=gb: u_ref[gb, pl.ds(s, rows, stride=SSM_CHUNK), :],
                     SSM_CHUNK, n_groups_per_vreg, put)

    def state_in(r, _):
        b = jnp.dot(lhs_ref[r], win_ref[r], preferred_element_type=F32)
        bst_ref[0, pl.ds(r, rows, stride=SSM_ROW_PITCH), :] = b[:, :LANES]
        bst_ref[1, pl.ds(r, rows, stride=SSM_ROW_PITCH), :] = b[:, LANES:]
        return 0

    lax.fori_loop(0, SSM_PAIRS, state_in, 0, unroll=4)

    a_re = a8_ref[0]
    a_im = a8_ref[1]

    def step(j, h):
        h_re, h_im = h
        base = pl.multiple_of(j * SSM_ROW_PITCH, SUBLANES)
        hst_ref[0, pl.ds(base, SSM_PAIRS), :] = h_re
        hst_ref[1, pl.ds(base, SSM_PAIRS), :] = h_im
        n_re = a_re * h_re - a_im * h_im + bst_ref[0, pl.ds(base, SSM_PAIRS), :]
        n_im = a_re * h_im + a_im * h_re + bst_ref[1, pl.ds(base, SSM_PAIRS), :]
        return n_re, n_im

    h_re, h_im = lax.fori_loop(0, rows, step, (carry_ref[0], carry_ref[1]), unroll=4)
    carry_ref[0] = h_re
    carry_ref[1] = h_im
    hfin_ref[0] = h_re
    hfin_ref[1] = h_im

    def chunk_out(r, _):
        hcat = jnp.concatenate([hst_ref[0, pl.ds(r, rows, stride=SSM_ROW_PITCH), :],
                                hst_ref[1, pl.ds(r, rows, stride=SSM_ROW_PITCH), :]], axis=1).astype(BF16)
        yfl_ref[r] = (jnp.dot(lhs_ref[r], toep_ref[r], preferred_element_type=F32)
                      + jnp.dot(hcat, wout_ref[r], preferred_element_type=F32))
        return 0

    lax.fori_loop(0, SSM_PAIRS, chunk_out, 0, unroll=4)

    for gb in range(LANE_BLOCKS):

        def get(q, gb=gb):
            pair = gb * (n_groups_per_vreg // 2) + q // 2
            return yfl_ref[pair, :, (q % 2) * LANES:(q % 2 + 1) * LANES]

        def put(t, val, gb=gb):
            y_ref[gb, pl.ds(t, rows, stride=SSM_CHUNK), :] = val

        _fold_tokens(get, n_groups_per_vreg, SSM_CHUNK, put)


def _ssm_prompt(u, toep, win, wout, a8, batch, seq, tile):
    rows = tile // SSM_CHUNK
    n_tiles = seq // tile
    tok = _lane_blocked_spec(tile, lambda b, i: b * n_tiles + i)
    pair_w = _const_spec((SSM_PAIRS, MXU_DIM, MXU_DIM))
    return pl.pallas_call(
        functools.partial(_ssm_body, rows=rows),
        grid=(batch, n_tiles),
        in_specs=[tok, pair_w, pair_w, pair_w, _const_spec((2, SSM_PAIRS, LANES))],
        out_specs=[tok, pl.BlockSpec((None, 2, SSM_PAIRS, LANES), lambda b, i: (b, 0, 0, 0))],
        out_shape=[jax.ShapeDtypeStruct((LANE_BLOCKS, batch * seq, LANES), F32),
                   jax.ShapeDtypeStruct((batch, 2, SSM_PAIRS, LANES), F32)],
        scratch_shapes=[pltpu.VMEM((SSM_PAIRS, rows, MXU_DIM), BF16),
                        pltpu.VMEM((2, rows * SSM_ROW_PITCH, LANES), F32),
                        pltpu.VMEM((2, rows * SSM_ROW_PITCH, LANES), F32),
                        pltpu.VMEM((SSM_PAIRS, rows, MXU_DIM), F32),
                        pltpu.VMEM((2, SSM_PAIRS, LANES), F32)],
        compiler_params=_params("parallel", "arbitrary"),
        name="ssm_prompt",
    )(u, toep, win, wout, a8)


def _ssm_step_body(u_ref, hre_ref, him_ref, abar_ref, bbar_ref, cre_ref, cim_ref,
                   y_ref, ore_ref, oim_ref):
    hp = lax.Precision.HIGHEST
    nt = (((1,), (1,)), ((), ()))
    groups_per_block = LANES // SSM_GROUP
    for g in range(SSM_GROUPS):
        blk = g // groups_per_block
        ch = slice((g % groups_per_block) * SSM_GROUP, (g % groups_per_block + 1) * SSM_GROUP)
        st = slice(g * SSM_STATE, (g + 1) * SSM_STATE)
        ug = u_ref[blk, :, ch]
        bu_re = jnp.dot(ug, bbar_ref[0, g], precision=hp, preferred_element_type=F32)
        bu_im = jnp.dot(ug, bbar_ref[1, g], precision=hp, preferred_element_type=F32)
        a_re = abar_ref[0, g]
        a_im = abar_ref[1, g]
        h_re = hre_ref[:, st]
        h_im = him_ref[:, st]
        n_re = a_re * h_re - a_im * h_im + bu_re
        n_im = a_re * h_im + a_im * h_re + bu_im
        ore_ref[:, st] = n_re
        oim_ref[:, st] = n_im
        y_ref[blk, :, ch] = (lax.dot_general(n_re, cre_ref[g], nt, precision=hp, preferred_element_type=F32)
                        - lax.dot_general(n_im, cim_ref[g], nt, precision=hp, preferred_element_type=F32))


def _ssm_step(u, h_re, h_im, abar, bbar, c_re, c_im):
    b = u.shape[1]
    width = SSM_GROUPS * SSM_STATE
    return pl.pallas_call(
        _ssm_step_body,
        out_shape=[jax.ShapeDtypeStruct((LANE_BLOCKS, b, LANES), F32),
                   jax.ShapeDtypeStruct((b, width), F32),
                   jax.ShapeDtypeStruct((b, width), F32)],
        compiler_params=pltpu.CompilerParams(vmem_limit_bytes=VMEM_LIMIT_BYTES),
        name="ssm_step",
    )(u, h_re, h_im, abar, bbar, c_re, c_im)


def _attn_prompt_body(*refs, seq):
    qkv = refs[:3 * N_DIL]
    o_ref, m_ref, l_ref, acc_ref, bias_ref = refs[3 * N_DIL:]
    tq = KEYS_BACK
    row = lax.broadcasted_iota(jnp.int32, (tq, 2 * tq), 0)
    col = lax.broadcasted_iota(jnp.int32, (tq, 2 * tq), 1)
    back = row + tq - col
    in_window = (back >= 0) & (back <= KEYS_BACK)
    first_head = lax.broadcasted_iota(jnp.int32, (1, LANES), 1) < HEAD_DIM
    nt = (((1,), (1,)), ((), ()))

    for g in range(N_DIL):
        q_ref, k_ref, v_ref = qkv[g], qkv[N_DIL + g], qkv[2 * N_DIL + g]
        d = DIL_RATES[g]
        blocks_per_residue = seq // d // tq
        dist = (back * d).astype(F32)
        for h in range(HEADS):
            bias = jnp.where(in_window, -_slope(g, h) * dist, -jnp.inf)
            bias_ref[h] = bias
            bias_ref[HEADS + h] = jnp.where(col >= tq, bias, -jnp.inf)

        def block(mi, _, g=g, d=d, q_ref=q_ref, k_ref=k_ref, v_ref=v_ref, blocks_per_residue=blocks_per_residue):
            residue = mi // blocks_per_residue
            n = mi % blocks_per_residue
            cur = pl.multiple_of(mi * tq, tq)
            prev = pl.multiple_of(jnp.maximum(mi - 1, 0) * tq, tq)
            bias_at = jnp.where(n == 0, HEADS, 0)
            token0 = residue + d * tq * n
            rows = pl.ds(pl.multiple_of(token0, tq), tq) if d == 1 else pl.ds(token0, tq, stride=d)
            pairs = range(HEADS // 2)
            old = [(m_ref[pair, rows, :], l_ref[pair, rows, :], acc_ref[pair, rows, :]) for pair in pairs] if g else None
            new = []
            for pair in pairs:
                lanes = slice(pair * LANES, (pair + 1) * LANES)
                qp = q_ref[pl.ds(cur, tq), lanes]
                kp = jnp.concatenate([k_ref[pl.ds(prev, tq), lanes], k_ref[pl.ds(cur, tq), lanes]], axis=0)
                vp = jnp.concatenate([v_ref[pl.ds(prev, tq), lanes], v_ref[pl.ds(cur, tq), lanes]], axis=0)
                stats = []
                for e in range(2):
                    qm = jnp.where(first_head if e == 0 else ~first_head, qp, jnp.zeros_like(qp))
                    s = lax.dot_general(qm, kp, nt, preferred_element_type=F32)
                    s = s + bias_ref[bias_at + 2 * pair + e]
                    m = jnp.max(s, axis=-1, keepdims=True)
                    p = jnp.exp(s - m)
                    den = jnp.sum(p, axis=-1, keepdims=True)
                    pv = jnp.dot(p.astype(BF16), vp, preferred_element_type=F32)
                    stats.append((m, den, pv))
                m_new, l_new, acc_new = (jnp.where(first_head, a, b) for a, b in zip(*stats))
                if g > 0:
                    m_old, l_old, acc_old = old[pair]
                    m_tot = jnp.maximum(m_old, m_new)
                    w_old = jnp.exp(m_old - m_tot)
                    w_new = jnp.exp(m_new - m_tot)
                    l_new = w_old * l_old + w_new * l_new
                    acc_new = w_old * acc_old + w_new * acc_new
                    m_new = m_tot
                new.append((m_new, l_new, acc_new))
            for pair, (m_new, l_new, acc_new) in zip(pairs, new):
                if g < N_DIL - 1:
                    m_ref[pair, rows, :] = m_new
                    l_ref[pair, rows, :] = l_new
                    acc_ref[pair, rows, :] = acc_new
                else:
                    o_ref[pair, rows, :] = acc_new / l_new
            return 0

        lax.fori_loop(0, seq // tq, block, 0, unroll=2)


def _attn_prompt(qkv, batch, seq):
    slabs = SLOT_WIDTH // LANES
    seq_spec = pl.BlockSpec((None, seq, SLOT_WIDTH), lambda b: (b, 0, 0), pipeline_mode=pl.Buffered(1))
    running = pltpu.VMEM((slabs, seq, LANES), F32)
    return pl.pallas_call(
        functools.partial(_attn_prompt_body, seq=seq),
        grid=(batch,),
        in_specs=[seq_spec] * len(qkv),
        out_specs=pl.BlockSpec((slabs, seq, LANES), lambda b: (0, b, 0)),
        out_shape=jax.ShapeDtypeStruct((slabs, batch * seq, LANES), F32),
        scratch_shapes=[running, running, running, pltpu.VMEM((2 * HEADS, KEYS_BACK, 2 * KEYS_BACK), F32)],
        compiler_params=_params("parallel"),
        name="attn_prompt",
    )(*[a.reshape(batch, seq, SLOT_WIDTH) for a in qkv])


def _as_column(row_vec):
    n = row_vec.shape[1]
    eye = lax.broadcasted_iota(jnp.int32, (n, n), 0) == lax.broadcasted_iota(jnp.int32, (n, n), 1)
    return jnp.sum(jnp.where(eye, row_vec, 0.0), axis=1, keepdims=True)


def _attn_sample_body(q_ref, k_ref, v_ref, c0_ref, c1_ref, c2_ref, ob_ref, n0_ref, n1_ref, n2_ref,
                      o_ref, l_ref):
    b = pl.program_id(0)
    hp = lax.Precision.HIGHEST
    q_row = q_ref[pl.ds(b, 1), :]
    k_row = k_ref[pl.ds(b, 1), :]
    v_row = v_ref[pl.ds(b, 1), :]
    for g, (c_ref, n_ref) in enumerate(((c0_ref, n0_ref), (c1_ref, n1_ref), (c2_ref, n2_ref))):
        w = DIL_WINDOWS[g]
        d = DIL_RATES[g]
        lane = lax.broadcasted_iota(jnp.int32, (1, w), 1)
        dist = w - lane
        valid = (dist & (d - 1)) == 0
        for h in range(HEADS):
            at = g * SLOT_WIDTH + h * HEAD_DIM
            q_h = q_row[:, at:at + HEAD_DIM]
            k_new = k_row[:, at:at + HEAD_DIM]
            v_new = v_row[:, at:at + HEAD_DIM]
            kt = c_ref[0, h]
            vt = c_ref[1, h]
            q8 = jnp.broadcast_to(q_h, (SUBLANES, HEAD_DIM))
            s = jnp.dot(q8, kt, precision=hp, preferred_element_type=F32)[0:1]
            s = jnp.where(valid, s - _slope(g, h) * dist.astype(F32), -jnp.inf)
            s_new = jnp.sum(q_h * k_new, axis=1, keepdims=True)
            m = jnp.maximum(jnp.max(s, axis=1, keepdims=True), s_new)
            p = jnp.exp(s - m)
            p_new = jnp.exp(s_new - m)
            den = jnp.sum(p, axis=1, keepdims=True) + p_new
            p8 = jnp.broadcast_to(p, (SUBLANES, w))
            pv = lax.dot_general(p8, vt, (((1,), (1,)), ((), ())), precision=hp,
                                 preferred_element_type=F32)[0:1]
            col = slice(at, at + HEAD_DIM)
            o_ref[:, col] = (pv + p_new * v_new) / den
            l_ref[:, col] = jnp.broadcast_to(m + jnp.log(den), (1, HEAD_DIM))
            last = lax.broadcasted_iota(jnp.int32, (HEAD_DIM, w), 1) == w - 1
            n_ref[0, h] = jnp.where(last, _as_column(k_new), pltpu.roll(kt, w - 1, 1))
            n_ref[1, h] = jnp.where(last, _as_column(v_new), pltpu.roll(vt, w - 1, 1))
    lse = [l_ref[:, g * SLOT_WIDTH:(g + 1) * SLOT_WIDTH] for g in range(N_DIL)]
    top = functools.reduce(jnp.maximum, lse)
    wts = [jnp.exp(l - top) for l in lse]
    mixed = sum(w_g * o_ref[:, g * SLOT_WIDTH:(g + 1) * SLOT_WIDTH] for g, w_g in enumerate(wts))
    ob_ref[...] = mixed / sum(wts)


def _attn_sample(q, k, v, caches):
    b = q.shape[0]
    full = _const_spec((b, ATTN_WIDTH))
    win = [pl.BlockSpec((None, 2, HEADS, HEAD_DIM, w), lambda i: (i, 0, 0, 0, 0)) for w in DIL_WINDOWS]
    outs = pl.pallas_call(
        _attn_sample_body,
        grid=(b,),
        in_specs=[full, full, full] + win,
        out_specs=[pl.BlockSpec((None, 1, SLOT_WIDTH), lambda i: (i, 0, 0))] + win,
        out_shape=([jax.ShapeDtypeStruct((b, 1, SLOT_WIDTH), F32)]
                   + [jax.ShapeDtypeStruct(c.shape, F32) for c in caches]),
        scratch_shapes=[pltpu.VMEM((1, ATTN_WIDTH), F32), pltpu.VMEM((1, ATTN_WIDTH), F32)],
        compiler_params=_params("parallel"),
        name="attn_sample",
    )(q, k, v, *caches)
    return outs[0].reshape(b, SLOT_WIDTH), outs[1:]


def _mix_out_body(x_ref, y_ref, u_ref, d_ref, ob_ref,
                  ga_ref, gb_ref, wglu_ref, wpa_ref, wpb_ref, wout_ref, out_ref):
    y_raw = jnp.concatenate([y_ref[blk] for blk in range(LANE_BLOCKS)], axis=1)
    u = jnp.concatenate([u_ref[blk] for blk in range(LANE_BLOCKS)], axis=1)
    y = jax.nn.gelu(y_raw + d_ref[...] * u)
    yb = y.astype(BF16)
    y_a = y * jax.nn.sigmoid(jnp.dot(yb, wglu_ref[...], preferred_element_type=F32))
    branch_a = jnp.dot(y_a.astype(BF16), wpa_ref[...], preferred_element_type=F32)

    o_b = jnp.concatenate([ob_ref[s] for s in range(SLOT_WIDTH // LANES)], axis=1)
    branch_b = jnp.dot(o_b.astype(BF16), wpb_ref[...], preferred_element_type=F32)

    merged = ga_ref[...] * branch_a + gb_ref[...] * branch_b
    out_ref[...] = x_ref[...] + jnp.dot(merged.astype(BF16), wout_ref[...], preferred_element_type=F32)


def _mix_out(x, y, u, ssm_d, o_b, ga, gb, w_glu, w_pa, w_pb, w_out, tm):
    t = x.shape[0]
    wide = pl.BlockSpec((tm, D_MODEL), lambda i: (i, 0))
    blocked = _lane_blocked_spec(tm, lambda i: i)
    slot = pl.BlockSpec((SLOT_WIDTH // LANES, tm, LANES), lambda i: (0, i, 0))
    return pl.pallas_call(
        _mix_out_body,
        grid=(t // tm,),
        in_specs=[wide, blocked, blocked, _const_spec((1, D_MODEL)), slot, wide, wide,
                  _const_spec((D_MODEL, D_MODEL)), _const_spec((D_MODEL, D_MODEL)),
                  _const_spec((SLOT_WIDTH, D_MODEL)), _const_spec((D_MODEL, D_MODEL))],
        out_specs=wide,
        out_shape=jax.ShapeDtypeStruct((t, D_MODEL), F32),
        compiler_params=_params("parallel"),
        name="mix_out",
    )(x, y, u, ssm_d, o_b, ga, gb, w_glu, w_pa, w_pb, w_out)


def _segment_matrices():
    lane = jnp.arange(ATTN_WIDTH) // HEAD_DIM
    sred = (lane[:, None] == jnp.arange(LANES)[None, :]).astype(BF16)
    return sred, sred.T


def kernel(x_prompt, x_sample, state_ssm_re, state_ssm_im, cache_kv_w128, cache_kv_w512, cache_kv_w2048, ffn1_norm, ffn1_w_gate, ffn1_w_up, ffn1_w_down, mix_norm, w_in, ssm_lambda_re, ssm_lambda_im, ssm_b_re, ssm_b_im, ssm_c_re, ssm_c_im, ssm_d, ssm_log_dt, w_glu, q_gain, k_gain, w_proj_a, w_proj_b, w_out, ffn2_norm, ffn2_w_gate, ffn2_w_up, ffn2_w_down):
    depth = ffn1_norm.shape[0]
    assert depth == 1, "single-layer step"
    batch, seq, _ = x_prompt.shape
    dec_batch, dec_seq, _ = x_sample.shape
    assert dec_seq == 1 and seq % (DIL_RATES[-1] * KEYS_BACK) == 0
    layer = 0
    bf = lambda w: w[layer].astype(BF16)
    vec = lambda w: w[layer][None]
    ffn1 = (vec(ffn1_norm), bf(ffn1_w_gate), bf(ffn1_w_up), bf(ffn1_w_down))
    ffn2 = (vec(ffn2_norm), bf(ffn2_w_gate), bf(ffn2_w_up), bf(ffn2_w_down))
    mix_norm, ssm_d = vec(mix_norm), vec(ssm_d)
    w_in_b, w_glu_b, w_pa_b, w_pb_b, w_out_b = bf(w_in), bf(w_glu), bf(w_proj_a), bf(w_proj_b), bf(w_out)
    q_gain_t = jnp.tile(vec(q_gain), (1, N_DIL * HEADS))
    k_gain_t = jnp.tile(vec(k_gain), (1, N_DIL * HEADS))
    sred, sexp = _segment_matrices()

    apow, bbar, ab, ca, kern = _ssm_prep(
        ssm_lambda_re[layer], ssm_lambda_im[layer], ssm_log_dt[layer],
        ssm_b_re[layer].transpose(0, 2, 1), ssm_b_im[layer].transpose(0, 2, 1),
        ssm_c_re[layer], ssm_c_im[layer])
    toep, win, wout = _ssm_chunk_matrices(ab, ca, kern)
    a8 = apow[:, SSM_CHUNK].reshape(2, SSM_PAIRS, LANES)

    def front(x, tm, seq=None):
        x1 = _ffn(x, *ffn1, tm)
        return (x1,) + tuple(_mix_in(x1, mix_norm, w_in_b, q_gain_t, k_gain_t, sred, sexp, tm, seq))

    def back(x1, y, u, o_b, ga, gb, tm):
        x2 = _mix_out(x1, y, u, ssm_d, o_b, ga, gb, w_glu_b, w_pa_b, w_pb_b, w_out_b, tm)
        return _ffn(x2, *ffn2, tm)

    tm = 512
    xp = x_prompt.reshape(batch * seq, D_MODEL)
    x1, u, q, k, v, ga, gb, *qkv = front(xp, tm, seq)
    y, h_fin = _ssm_prompt(u, toep, win, wout, a8, batch, seq, tile=1024)
    o_b = _attn_prompt(qkv, batch, seq)
    yp = back(x1, y, u, o_b, ga, gb, tm)
    yp = yp.reshape(batch, seq, D_MODEL)
    h_fin = h_fin.reshape(batch, 2, SSM_GROUPS, SSM_STATE)
    p_re = h_fin[None, :, 0]
    p_im = h_fin[None, :, 1]
    k5 = k.reshape(batch, seq, N_DIL, HEADS, HEAD_DIM)
    v5 = v.reshape(batch, seq, N_DIL, HEADS, HEAD_DIM)
    p_kv = []
    for g, w in enumerate(DIL_WINDOWS):
        keep = min(w, seq)
        p_kv.append(jnp.stack([k5[:, seq - keep:, g], v5[:, seq - keep:, g]], axis=2)[None])

    xs = x_sample.reshape(dec_batch, D_MODEL)
    x1, u, q, k, v, ga, gb = front(xs, dec_batch)
    width = SSM_GROUPS * SSM_STATE
    y, s_re, s_im = _ssm_step(u, state_ssm_re[layer].reshape(dec_batch, width),
                              state_ssm_im[layer].reshape(dec_batch, width),
                              apow[:, 1], bbar, ssm_c_re[layer], ssm_c_im[layer])
    caches = [c[layer].transpose(0, 2, 3, 4, 1) for c in (cache_kv_w128, cache_kv_w512, cache_kv_w2048)]
    o_b, new_caches = _attn_sample(q, k, v, caches)
    o_b = o_b.reshape(dec_batch, SLOT_WIDTH // LANES, LANES).transpose(1, 0, 2)
    ys = back(x1, y, u, o_b, ga, gb, dec_batch)
    ys = ys.reshape(dec_batch, 1, D_MODEL)
    s_re = s_re.reshape(1, dec_batch, SSM_GROUPS, SSM_STATE)
    s_im = s_im.reshape(1, dec_batch, SSM_GROUPS, SSM_STATE)
    s_kv = [c.transpose(0, 4, 1, 2, 3)[None] for c in new_caches]

    return (yp, ys, p_re, p_im, p_kv[0], p_kv[1], p_kv[2], s_re, s_im, s_kv[0], s_kv[1], s_kv[2])
```

```python
import functools

import jax
import jax.numpy as jnp
from jax import lax
from jax.experimental import pallas as pl
from jax.experimental.pallas import tpu as pltpu

F32 = jnp.float32
BF16 = jnp.bfloat16

D_MODEL = 1024
SSM_GROUP = 16
SSM_GROUPS = 64
SSM_STATE = 64
SSM_PAIRS = SSM_GROUPS // 2
HEAD_DIM = 64
HEADS = 4
DIL_WINDOWS = (128, 512, 2048)
DIL_RATES = (1, 4, 16)
N_DIL = 3
KEYS_BACK = 128
ATTN_WIDTH = N_DIL * HEADS * HEAD_DIM
SLOT_WIDTH = HEADS * HEAD_DIM
D_FF = 2816
RMS_EPS = 1e-6
ALIBI_MAX_EXP = 8.0
IN_WIDTH = D_MODEL + 3 * ATTN_WIDTH + 2 * D_MODEL

LANES = 128
SUBLANES = 8
MXU_DIM = 256
VMEM_LIMIT_BYTES = 56 * 1024 * 1024

LANE_BLOCKS = D_MODEL // LANES

SSM_CHUNK = SUBLANES
SSM_ROW_PITCH = 40
FF_CHUNKS = ((0, 1024), (1024, 2048), (2048, 2816))


def _slope(group, head):
    return 2.0 ** (-ALIBI_MAX_EXP * (group * HEADS + head + 1) / (N_DIL * HEADS))


def _const_spec(shape):
    zeros = (0,) * len(shape)
    return pl.BlockSpec(shape, lambda *_: zeros, pipeline_mode=pl.Buffered(1))


def _lane_blocked_spec(rows, row_block):
    return pl.BlockSpec((LANE_BLOCKS, rows, LANES), lambda *idx: (0, row_block(*idx), 0))


def _params(*semantics):
    return pltpu.CompilerParams(dimension_semantics=semantics, vmem_limit_bytes=VMEM_LIMIT_BYTES)


def _rms(x, w):
    return x * lax.rsqrt(jnp.mean(x * x, axis=-1, keepdims=True) + RMS_EPS) * w


def _split_bf16(x, terms):
    parts = []
    for _ in range(terms):
        p = x.astype(BF16)
        parts.append(p)
        x = x - p.astype(F32)
    return parts


def _ffn_body(x_ref, nw_ref, wg_ref, wu_ref, wd_ref, o_ref):
    x = x_ref[...]
    h = _rms(x, nw_ref[...]).astype(BF16)
    acc = jnp.zeros_like(x)
    for lo, hi in FF_CHUNKS:
        g = jnp.dot(h, wg_ref[:, lo:hi], preferred_element_type=F32)
        u = jnp.dot(h, wu_ref[:, lo:hi], preferred_element_type=F32)
        a = (jax.nn.silu(g) * u).astype(BF16)
        acc = acc + jnp.dot(a, wd_ref[lo:hi, :], preferred_element_type=F32)
    o_ref[...] = x + 0.5 * acc


def _ffn(x, norm_w, wg, wu, wd, tm):
    t = x.shape[0]
    row = pl.BlockSpec((tm, D_MODEL), lambda i: (i, 0))
    return pl.pallas_call(
        _ffn_body,
        grid=(t // tm,),
        in_specs=[row, _const_spec((1, D_MODEL)), _const_spec((D_MODEL, D_FF)),
                  _const_spec((D_MODEL, D_FF)), _const_spec((D_FF, D_MODEL))],
        out_specs=row,
        out_shape=jax.ShapeDtypeStruct((t, D_MODEL), F32),
        compiler_params=_params("parallel"),
        name="ffn",
    )(x, norm_w, wg, wu, wd)


def _head_norm(x, gain, sred, sexp):
    ss = sum(jnp.dot(p, sred, preferred_element_type=F32) for p in _split_bf16(x * x, 2))
    r = lax.rsqrt(ss * (1.0 / HEAD_DIM) + RMS_EPS)
    rb = sum(jnp.dot(p, sexp, preferred_element_type=F32) for p in _split_bf16(r, 3))
    return x * rb * gain


def _transpose_granules(xs):
    xs = list(xs)
    n = len(xs)
    block = lax.broadcasted_iota(jnp.int32, (1, LANES), 1) // SSM_GROUP
    bit = n // 2
    while bit:
        upper = (block & bit) != 0
        shift = SSM_GROUP * bit
        for lo in range(n):
            if lo & bit:
                continue
            hi = lo + bit
            x_lo, x_hi = xs[lo], xs[hi]
            xs[lo] = jnp.where(upper, pltpu.roll(x_hi, shift, 1), x_lo)
            xs[hi] = jnp.where(upper, x_hi, pltpu.roll(x_lo, LANES - shift, 1))
        bit //= 2
    return xs


def _mix_in_body(x_ref, nw_ref, w_ref, qg_ref, kg_ref, sred_ref, sexp_ref, u_ref, *rest, prompt):
    h = _rms(x_ref[...], nw_ref[...]).astype(BF16)

    def proj(lo, hi):
        return jnp.dot(h, w_ref[:, lo:hi], preferred_element_type=F32)

    c0 = D_MODEL
    c1 = c0 + ATTN_WIDTH
    c2 = c1 + ATTN_WIDTH
    c3 = c2 + ATTN_WIDTH
    c4 = c3 + D_MODEL
    u = proj(0, c0)
    for blk in range(LANE_BLOCKS):
        u_ref[blk] = u[:, blk * LANES:(blk + 1) * LANES]
    sred = sred_ref[...]
    sexp = sexp_ref[...]
    q = _head_norm(proj(c0, c1), qg_ref[...], sred, sexp) * (HEAD_DIM ** -0.5)
    k = _head_norm(proj(c1, c2), kg_ref[...], sred, sexp)
    v = proj(c2, c3)
    if not prompt:
        q_ref, k_ref, v_ref, ga_ref, gb_ref = rest
        q_ref[...] = q
    else:
        k_ref, v_ref, ga_ref, gb_ref, fold_ref = rest[:5]
        by_residue = rest[5:5 + 3 * N_DIL]
        stage_ref = rest[-1]
    k_ref[...] = k
    v_ref[...] = v
    ga_ref[...] = jax.nn.sigmoid(proj(c3, c4))
    gb_ref[...] = jax.nn.sigmoid(proj(c4, IN_WIDTH))
    if not prompt:
        return

    rows = x_ref.shape[0]
    groups_per_block = LANES // SSM_GROUP
    for blk in range(LANE_BLOCKS):
        tokens = [u_ref[blk, pl.ds(s, rows // SSM_CHUNK, stride=SSM_CHUNK), :] for s in range(SSM_CHUNK)]
        for q_i, folded in enumerate(_transpose_granules(tokens)):
            pair = blk * (groups_per_block // 2) + q_i // 2
            fold_ref[pair, :, (q_i % 2) * LANES:(q_i % 2 + 1) * LANES] = folded.astype(BF16)

    slabs = SLOT_WIDTH // LANES
    for ti, x in enumerate((q, k, v)):
        outs = by_residue[ti * N_DIL:(ti + 1) * N_DIL]
        outs[0][0] = x[:, :SLOT_WIDTH].astype(BF16)
        for g in range(1, N_DIL):
            d = DIL_RATES[g]
            for s in range(slabs):
                lo = g * SLOT_WIDTH + s * LANES
                stage_ref[ti, (g - 1) * slabs + s] = x[:, lo:lo + LANES]
            for r in range(d):
                for s in range(slabs):
                    piece = stage_ref[ti, (g - 1) * slabs + s, pl.ds(r, rows // d, stride=d), :]
                    outs[g][r, :, s * LANES:(s + 1) * LANES] = piece.astype(BF16)


def _mix_in(x, norm_w, w_in, q_gain, k_gain, sred, sexp, tm, seq=None):
    t = x.shape[0]
    prompt = seq is not None

    def row(width):
        return pl.BlockSpec((tm, width), lambda i: (i, 0))

    def rows_f32(width):
        return jax.ShapeDtypeStruct((t, width), F32)

    out_specs = [_lane_blocked_spec(tm, lambda i: i)]
    out_shape = [jax.ShapeDtypeStruct((LANE_BLOCKS, t, LANES), F32)]
    scratch = []
    if not prompt:
        out_specs += [row(ATTN_WIDTH)] * 3 + [row(D_MODEL)] * 2
        out_shape += [rows_f32(ATTN_WIDTH)] * 3 + [rows_f32(D_MODEL)] * 2
    else:
        tiles = seq // tm
        tail = min(DIL_WINDOWS[-1], seq)
        skip = tiles - tail // tm
        tail_spec = pl.BlockSpec((None, tm, ATTN_WIDTH), lambda i: (i // tiles, jnp.maximum(i % tiles - skip, 0), 0))
        out_specs += [tail_spec] * 2 + [row(D_MODEL)] * 2
        out_shape += [jax.ShapeDtypeStruct((t // seq, tail, ATTN_WIDTH), F32)] * 2 + [rows_f32(D_MODEL)] * 2
        out_specs.append(pl.BlockSpec((SSM_PAIRS, tm // SSM_CHUNK, MXU_DIM), lambda i: (0, i, 0)))
        out_shape.append(jax.ShapeDtypeStruct((SSM_PAIRS, t // SSM_CHUNK, MXU_DIM), BF16))
        for _ in range(3):
            for d in DIL_RATES:
                out_specs.append(pl.BlockSpec((None, d, tm // d, SLOT_WIDTH),
                                              lambda i: (i // tiles, 0, i % tiles, 0)))
                out_shape.append(jax.ShapeDtypeStruct((t // seq, d, seq // d, SLOT_WIDTH), BF16))
        scratch = [pltpu.VMEM((3, (N_DIL - 1) * SLOT_WIDTH // LANES, tm, LANES), F32)]
    return pl.pallas_call(
        functools.partial(_mix_in_body, prompt=prompt),
        grid=(t // tm,),
        in_specs=[row(D_MODEL), _const_spec((1, D_MODEL)), _const_spec((D_MODEL, IN_WIDTH)),
                  _const_spec((1, ATTN_WIDTH)), _const_spec((1, ATTN_WIDTH)),
                  _const_spec((ATTN_WIDTH, LANES)), _const_spec((LANES, ATTN_WIDTH))],
        out_specs=out_specs,
        out_shape=out_shape,
        scratch_shapes=scratch,
        compiler_params=_params("arbitrary"),
        name="mix_in",
    )(x, norm_w, w_in, q_gain, k_gain, sred, sexp)


def _cmul(ar, ai, br, bi):
    return ar * br - ai * bi, ar * bi + ai * br


def _ssm_prep_body(lre_ref, lim_ref, ldt_ref, bre_ref, bim_ref, cre_ref, cim_ref,
                   apow_ref, bbar_ref, ab_ref, ca_ref, kern_ref):
    lr = jnp.minimum(lre_ref[...], -1e-4)
    li = lim_ref[...]
    dt = jnp.exp(ldt_ref[...])
    mag = jnp.exp(lr * dt)
    ar = mag * jnp.cos(li * dt)
    ai = mag * jnp.sin(li * dt)
    nr = ar - 1.0
    ni = ai
    den = lr * lr + li * li
    fr = (nr * lr + ni * li) / den
    fi = (ni * lr - nr * li) / den
    bbr, bbi = _cmul(fr, fi, bre_ref[...], bim_ref[...])
    bbar_ref[0] = bbr
    bbar_ref[1] = bbi
    cre = cre_ref[...]
    cim = cim_ref[...]
    pr = jnp.ones_like(ar)
    pi = jnp.zeros_like(ai)
    nt = (((2,), (2,)), ((0,), (0,)))
    for t in range(SSM_CHUNK + 1):
        apow_ref[0, t] = pr
        apow_ref[1, t] = pi
        car, cai = _cmul(cre, cim, pr, pi)
        ca_ref[0, t] = car
        ca_ref[1, t] = cai
        if t < SSM_CHUNK:
            abr, abi = _cmul(pr, pi, bbr, bbi)
            ab_ref[0, t] = abr
            ab_ref[1, t] = abi
            kern_ref[t] = (
                lax.dot_general(car, bbr, nt, precision=lax.Precision.HIGHEST, preferred_element_type=F32)
                - lax.dot_general(cai, bbi, nt, precision=lax.Precision.HIGHEST, preferred_element_type=F32))
        pr, pi = _cmul(pr, pi, ar, ai)


def _ssm_prep(lam_re, lam_im, log_dt, b_re_t, b_im_t, c_re, c_im):
    g, p, c = SSM_GROUPS, SSM_STATE, SSM_GROUP
    n = SSM_CHUNK
    return pl.pallas_call(
        _ssm_prep_body,
        out_shape=[jax.ShapeDtypeStruct((2, n + 1, g, 1, p), F32),
                   jax.ShapeDtypeStruct((2, g, c, p), F32),
                   jax.ShapeDtypeStruct((2, n, g, c, p), F32),
                   jax.ShapeDtypeStruct((2, n + 1, g, c, p), F32),
                   jax.ShapeDtypeStruct((n, g, c, c), F32)],
        compiler_params=pltpu.CompilerParams(vmem_limit_bytes=VMEM_LIMIT_BYTES),
        name="ssm_prep",
    )(lam_re.reshape(g, 1, p), lam_im.reshape(g, 1, p), log_dt.reshape(g, 1, 1), b_re_t, b_im_t, c_re, c_im)


def _pair_block_diag(m):
    g, r, c = m.shape
    m = m.reshape(g // 2, 2, r, c)
    z = jnp.zeros_like(m[:, 0])
    top = jnp.concatenate([m[:, 0], z], axis=2)
    bot = jnp.concatenate([z, m[:, 1]], axis=2)
    return jnp.concatenate([top, bot], axis=1)


def _ssm_chunk_matrices(ab, ca, kern):
    n, g, c, p = SSM_CHUNK, SSM_GROUPS, SSM_GROUP, SSM_STATE
    s_idx = jnp.arange(n)[:, None]
    t_idx = jnp.arange(n)[None, :]
    lag = jnp.clip(t_idx - s_idx, 0, n - 1)
    blocks = kern[lag]
    blocks = jnp.where((t_idx >= s_idx)[:, :, None, None, None], blocks, 0.0)
    toep = blocks.transpose(2, 0, 4, 1, 3).reshape(g, n * c, n * c)
    toep = _pair_block_diag(toep)
    rev = ab[:, ::-1]
    win = rev.transpose(0, 2, 1, 3, 4).reshape(2, g, n * c, p)
    win = jnp.concatenate([_pair_block_diag(win[0]), _pair_block_diag(win[1])], axis=2)
    ca1 = ca[:, 1:]
    wout = ca1.transpose(0, 2, 4, 1, 3).reshape(2, g, p, n * c)
    wout = jnp.concatenate([_pair_block_diag(wout[0]), -_pair_block_diag(wout[1])], axis=1)
    return toep.astype(BF16), win.astype(BF16), wout.astype(BF16)


def _ssm_body(lhs_ref, toep_ref, win_ref, wout_ref, a8_ref, yfl_ref, hfin_ref, st_ref, carry_ref, *, rows):
    i = pl.program_id(1)

    @pl.when(i == 0)
    def _():
        carry_ref[...] = jnp.zeros_like(carry_ref)

    def state_in(r, _):
        b = jnp.dot(lhs_ref[r], win_ref[r], preferred_element_type=F32)
        st_ref[0, pl.ds(r, rows, stride=SSM_ROW_PITCH), :] = b[:, :LANES]
        st_ref[1, pl.ds(r, rows, stride=SSM_ROW_PITCH), :] = b[:, LANES:]
        return 0

    lax.fori_loop(0, SSM_PAIRS, state_in, 0, unroll=4)

    a_re = a8_ref[0]
    a_im = a8_ref[1]

    def step(j, h):
        h_re, h_im = h
        base = pl.multiple_of(j * SSM_ROW_PITCH, SUBLANES)
        n_re = a_re * h_re - a_im * h_im + st_ref[0, pl.ds(base, SSM_PAIRS), :]
        n_im = a_re * h_im + a_im * h_re + st_ref[1, pl.ds(base, SSM_PAIRS), :]
        st_ref[0, pl.ds(base, SSM_PAIRS), :] = h_re
        st_ref[1, pl.ds(base, SSM_PAIRS), :] = h_im
        return n_re, n_im

    h_re, h_im = lax.fori_loop(0, rows, step, (carry_ref[0], carry_ref[1]), unroll=4)
    carry_ref[0] = h_re
    carry_ref[1] = h_im
    hfin_ref[0] = h_re
    hfin_ref[1] = h_im

    def chunk_out(r, _):
        hcat = jnp.concatenate([st_ref[0, pl.ds(r, rows, stride=SSM_ROW_PITCH), :],
                                st_ref[1, pl.ds(r, rows, stride=SSM_ROW_PITCH), :]], axis=1).astype(BF16)
        yfl_ref[r] = (jnp.dot(lhs_ref[r], toep_ref[r], preferred_element_type=F32)
                      + jnp.dot(hcat, wout_ref[r], preferred_element_type=F32))
        return 0

    lax.fori_loop(0, SSM_PAIRS, chunk_out, 0, unroll=4)


def _ssm_prompt(lhs, toep, win, wout, a8, batch, seq, tile):
    rows = tile // SSM_CHUNK
    n_tiles = seq // tile
    tok = pl.BlockSpec((SSM_PAIRS, rows, MXU_DIM), lambda b, i: (0, b * n_tiles + i, 0))
    pair_w = _const_spec((SSM_PAIRS, MXU_DIM, MXU_DIM))
    return pl.pallas_call(
        functools.partial(_ssm_body, rows=rows),
        grid=(batch, n_tiles),
        in_specs=[tok, pair_w, pair_w, pair_w, _const_spec((2, SSM_PAIRS, LANES))],
        out_specs=[tok, pl.BlockSpec((None, 2, SSM_PAIRS, LANES), lambda b, i: (b, 0, 0, 0))],
        out_shape=[jax.ShapeDtypeStruct((SSM_PAIRS, batch * seq // SSM_CHUNK, MXU_DIM), F32),
                   jax.ShapeDtypeStruct((batch, 2, SSM_PAIRS, LANES), F32)],
        scratch_shapes=[pltpu.VMEM((2, rows * SSM_ROW_PITCH, LANES), F32),
                        pltpu.VMEM((2, SSM_PAIRS, LANES), F32)],
        compiler_params=_params("parallel", "arbitrary"),
        name="ssm_prompt",
    )(lhs, toep, win, wout, a8)


def _ssm_step_body(u_ref, hre_ref, him_ref, abar_ref, bbar_ref, cre_ref, cim_ref,
                   y_ref, ore_ref, oim_ref):
    hp = lax.Precision.HIGHEST
    nt = (((1,), (1,)), ((), ()))
    groups_per_block = LANES // SSM_GROUP
    for g in range(SSM_GROUPS):
        blk = g // groups_per_block
        ch = slice((g % groups_per_block) * SSM_GROUP, (g % groups_per_block + 1) * SSM_GROUP)
        st = slice(g * SSM_STATE, (g + 1) * SSM_STATE)
        ug = u_ref[blk, :, ch]
        bu_re = jnp.dot(ug, bbar_ref[0, g], precision=hp, preferred_element_type=F32)
        bu_im = jnp.dot(ug, bbar_ref[1, g], precision=hp, preferred_element_type=F32)
        a_re = abar_ref[0, g]
        a_im = abar_ref[1, g]
        h_re = hre_ref[:, st]
        h_im = him_ref[:, st]
        n_re = a_re * h_re - a_im * h_im + bu_re
        n_im = a_re * h_im + a_im * h_re + bu_im
        ore_ref[:, st] = n_re
        oim_ref[:, st] = n_im
        y_ref[blk, :, ch] = (lax.dot_general(n_re, cre_ref[g], nt, precision=hp, preferred_element_type=F32)
                             - lax.dot_general(n_im, cim_ref[g], nt, precision=hp, preferred_element_type=F32))


def _ssm_step(u, h_re, h_im, abar, bbar, c_re, c_im):
    b = u.shape[1]
    width = SSM_GROUPS * SSM_STATE
    return pl.pallas_call(
        _ssm_step_body,
        out_shape=[jax.ShapeDtypeStruct((LANE_BLOCKS, b, LANES), F32),
                   jax.ShapeDtypeStruct((b, width), F32),
                   jax.ShapeDtypeStruct((b, width), F32)],
        compiler_params=pltpu.CompilerParams(vmem_limit_bytes=VMEM_LIMIT_BYTES),
        name="ssm_step",
    )(u, h_re, h_im, abar, bbar, c_re, c_im)


def _attn_prompt_body(*refs, seq):
    qkv = refs[:3 * N_DIL]
    o_ref, m_ref, l_ref, acc_ref, bias_ref = refs[3 * N_DIL:]
    tq = KEYS_BACK
    row = lax.broadcasted_iota(jnp.int32, (tq, 2 * tq), 0)
    col = lax.broadcasted_iota(jnp.int32, (tq, 2 * tq), 1)
    back = row + tq - col
    in_window = (back >= 0) & (back <= KEYS_BACK)
    first_head = lax.broadcasted_iota(jnp.int32, (1, LANES), 1) < HEAD_DIM
    nt = (((1,), (1,)), ((), ()))

    for g in range(N_DIL):
        q_ref, k_ref, v_ref = qkv[g], qkv[N_DIL + g], qkv[2 * N_DIL + g]
        d = DIL_RATES[g]
        blocks_per_residue = seq // d // tq
        dist = (back * d).astype(F32)
        for h in range(HEADS):
            bias = jnp.where(in_window, -_slope(g, h) * dist, -jnp.inf)
            bias_ref[h] = bias
            bias_ref[HEADS + h] = jnp.where(col >= tq, bias, -jnp.inf)

        def block(mi, _, g=g, d=d, q_ref=q_ref, k_ref=k_ref, v_ref=v_ref, blocks_per_residue=blocks_per_residue):
            residue = mi // blocks_per_residue
            n = mi % blocks_per_residue
            cur = pl.multiple_of(mi * tq, tq)
            prev = pl.multiple_of(jnp.maximum(mi - 1, 0) * tq, tq)
            bias_at = jnp.where(n == 0, HEADS, 0)
            token0 = residue + d * tq * n
            rows = pl.ds(pl.multiple_of(token0, tq), tq) if d == 1 else pl.ds(token0, tq, stride=d)
            pairs = range(HEADS // 2)
            old = [(m_ref[pair, rows, :], l_ref[pair, rows, :], acc_ref[pair, rows, :]) for pair in pairs] if g else None
            new = []
            for pair in pairs:
                lanes = slice(pair * LANES, (pair + 1) * LANES)
                qp = q_ref[pl.ds(cur, tq), lanes]
                kp = jnp.concatenate([k_ref[pl.ds(prev, tq), lanes], k_ref[pl.ds(cur, tq), lanes]], axis=0)
                vp = jnp.concatenate([v_ref[pl.ds(prev, tq), lanes], v_ref[pl.ds(cur, tq), lanes]], axis=0)
                stats = []
                for e in range(2):
                    qm = jnp.where(first_head if e == 0 else ~first_head, qp, jnp.zeros_like(qp))
                    s = lax.dot_general(qm, kp, nt, preferred_element_type=F32)
                    s = s + bias_ref[bias_at + 2 * pair + e]
                    m = jnp.max(s, axis=-1, keepdims=True)
                    p = jnp.exp(s - m)
                    den = jnp.sum(p, axis=-1, keepdims=True)
                    pv = jnp.dot(p.astype(BF16), vp, preferred_element_type=F32)
                    stats.append((m, den, pv))
                m_new, l_new, acc_new = (jnp.where(first_head, a, b) for a, b in zip(*stats))
                if g > 0:
                    m_old, l_old, acc_old = old[pair]
                    m_tot = jnp.maximum(m_old, m_new)
                    w_old = jnp.exp(m_old - m_tot)
                    w_new = jnp.exp(m_new - m_tot)
                    l_new = w_old * l_old + w_new * l_new
                    acc_new = w_old * acc_old + w_new * acc_new
                    m_new = m_tot
                new.append((m_new, l_new, acc_new))
            for pair, (m_new, l_new, acc_new) in zip(pairs, new):
                if g < N_DIL - 1:
                    m_ref[pair, rows, :] = m_new
                    l_ref[pair, rows, :] = l_new
                    acc_ref[pair, rows, :] = acc_new
                else:
                    o_ref[pair, rows, :] = acc_new / l_new
            return 0

        lax.fori_loop(0, seq // tq, block, 0, unroll=2)


def _attn_prompt(qkv, batch, seq):
    slabs = SLOT_WIDTH // LANES
    seq_spec = pl.BlockSpec((None, seq, SLOT_WIDTH), lambda b: (b, 0, 0), pipeline_mode=pl.Buffered(1))
    running = pltpu.VMEM((slabs, seq, LANES), F32)
    return pl.pallas_call(
        functools.partial(_attn_prompt_body, seq=seq),
        grid=(batch,),
        in_specs=[seq_spec] * len(qkv),
        out_specs=pl.BlockSpec((slabs, seq, LANES), lambda b: (0, b, 0)),
        out_shape=jax.ShapeDtypeStruct((slabs, batch * seq, LANES), F32),
        scratch_shapes=[running, running, running, pltpu.VMEM((2 * HEADS, KEYS_BACK, 2 * KEYS_BACK), F32)],
        compiler_params=_params("parallel"),
        name="attn_prompt",
    )(*[a.reshape(batch, seq, SLOT_WIDTH) for a in qkv])


def _as_column(row_vec):
    n = row_vec.shape[1]
    eye = lax.broadcasted_iota(jnp.int32, (n, n), 0) == lax.broadcasted_iota(jnp.int32, (n, n), 1)
    return jnp.sum(jnp.where(eye, row_vec, 0.0), axis=1, keepdims=True)


def _attn_sample_body(q_ref, k_ref, v_ref, c0_ref, c1_ref, c2_ref, ob_ref, n0_ref, n1_ref, n2_ref):
    b = pl.program_id(0)
    q_row = q_ref[pl.ds(b, 1), :]
    k_row = k_ref[pl.ds(b, 1), :]
    v_row = v_ref[pl.ds(b, 1), :]
    head_row = lax.broadcasted_iota(jnp.int32, (SUBLANES, SLOT_WIDTH), 0)
    own_head = lax.broadcasted_iota(jnp.int32, (SUBLANES, SLOT_WIDTH), 1) // HEAD_DIM == head_row
    head_col = lax.broadcasted_iota(jnp.int32, (SUBLANES, 1), 0)
    outs, lses = [], []
    for g, (c_ref, n_ref) in enumerate(((c0_ref, n0_ref), (c1_ref, n1_ref), (c2_ref, n2_ref))):
        w = DIL_WINDOWS[g]
        d = DIL_RATES[g]
        cols = slice(g * SLOT_WIDTH, (g + 1) * SLOT_WIDTH)
        q_g, k_new, v_new = q_row[:, cols], k_row[:, cols], v_row[:, cols]
        kt = c_ref[0].reshape(SLOT_WIDTH, w)
        vt = c_ref[1].reshape(SLOT_WIDTH, w)
        q_heads = jnp.where(own_head, q_g, 0.0)
        s = jnp.dot(q_heads.astype(BF16), kt.astype(BF16), preferred_element_type=F32)
        dist = w - lax.broadcasted_iota(jnp.int32, (1, w), 1)
        slope = functools.reduce(lambda acc, h: jnp.where(head_col == h, _slope(g, h), acc), range(HEADS), 0.0)
        s = jnp.where((dist & (d - 1)) == 0, s - slope * dist.astype(F32), -jnp.inf)
        s_new = jnp.sum(q_heads * k_new, axis=1, keepdims=True)
        m = jnp.maximum(jnp.max(s, axis=1, keepdims=True), s_new)
        p = jnp.exp(s - m)
        p_new = jnp.exp(s_new - m)
        den = jnp.sum(p, axis=1, keepdims=True) + p_new
        pv = lax.dot_general(p.astype(BF16), vt.astype(BF16), (((1,), (1,)), ((), ())),
                             preferred_element_type=F32)
        o_heads = (pv + p_new * v_new) / den
        outs.append(jnp.sum(jnp.where(own_head, o_heads, 0.0), axis=0, keepdims=True))
        lses.append(jnp.sum(jnp.where(own_head, m + jnp.log(den), 0.0), axis=0, keepdims=True))
        last = lax.broadcasted_iota(jnp.int32, (SLOT_WIDTH, w), 1) == w - 1
        shape = (HEADS, HEAD_DIM, w)
        n_ref[0] = jnp.where(last, _as_column(k_new), pltpu.roll(kt, w - 1, 1)).reshape(shape)
        n_ref[1] = jnp.where(last, _as_column(v_new), pltpu.roll(vt, w - 1, 1)).reshape(shape)
    top = functools.reduce(jnp.maximum, lses)
    wts = [jnp.exp(l - top) for l in lses]
    ob_ref[...] = sum(w_g * o_g for w_g, o_g in zip(wts, outs)) / sum(wts)


def _attn_sample(q, k, v, caches):
    b = q.shape[0]
    full = _const_spec((b, ATTN_WIDTH))
    win = [pl.BlockSpec((None, 2, HEADS, HEAD_DIM, w), lambda i: (i, 0, 0, 0, 0)) for w in DIL_WINDOWS]
    outs = pl.pallas_call(
        _attn_sample_body,
        grid=(b,),
        in_specs=[full, full, full] + win,
        out_specs=[pl.BlockSpec((None, 1, SLOT_WIDTH), lambda i: (i, 0, 0))] + win,
        out_shape=([jax.ShapeDtypeStruct((b, 1, SLOT_WIDTH), F32)]
                   + [jax.ShapeDtypeStruct(c.shape, F32) for c in caches]),
        compiler_params=_params("parallel"),
        name="attn_sample",
    )(q, k, v, *caches)
    return outs[0].reshape(b, SLOT_WIDTH), outs[1:]


def _kv_window_body(k_ref, v_ref, *out_refs):
    tail = k_ref.shape[0]
    for g, o_ref in enumerate(out_refs):
        w = o_ref.shape[-1]
        for t, x_ref in enumerate((k_ref, v_ref)):
            for pair in range(HEADS // 2):
                lo = g * SLOT_WIDTH + pair * LANES
                xt = x_ref[tail - w:tail, lo:lo + LANES].T
                o_ref[t, 2 * pair] = xt[:HEAD_DIM]
                o_ref[t, 2 * pair + 1] = xt[HEAD_DIM:]


def _kv_windows(k_tail, v_tail):
    batch, tail, _ = k_tail.shape
    keeps = [min(w, tail) for w in DIL_WINDOWS]
    src = pl.BlockSpec((None, tail, ATTN_WIDTH), lambda b: (b, 0, 0))
    return pl.pallas_call(
        _kv_window_body,
        grid=(batch,),
        in_specs=[src, src],
        out_specs=[pl.BlockSpec((None, 2, HEADS, HEAD_DIM, w), lambda b: (b, 0, 0, 0, 0)) for w in keeps],
        out_shape=[jax.ShapeDtypeStruct((batch, 2, HEADS, HEAD_DIM, w), F32) for w in keeps],
        compiler_params=_params("parallel"),
        name="kv_windows",
    )(k_tail, v_tail)


def _mix_out_body(x_ref, y_ref, u_ref, d_ref, ob_ref,
                  ga_ref, gb_ref, wglu_ref, wpa_ref, wpb_ref, wout_ref, out_ref, *unfold):
    if unfold:
        y_ref, folded_ref = unfold[0], y_ref
        chunks = folded_ref.shape[1]
        groups_per_block = LANES // SSM_GROUP
        for blk in range(LANE_BLOCKS):
            groups = [folded_ref[blk * (groups_per_block // 2) + q_i // 2, :, (q_i % 2) * LANES:(q_i % 2 + 1) * LANES]
                      for q_i in range(groups_per_block)]
            for t, rows_t in enumerate(_transpose_granules(groups)):
                y_ref[blk, pl.ds(t, chunks, stride=SSM_CHUNK), :] = rows_t
    y_raw = jnp.concatenate([y_ref[blk] for blk in range(LANE_BLOCKS)], axis=1)
    u = jnp.concatenate([u_ref[blk] for blk in range(LANE_BLOCKS)], axis=1)
    y = jax.nn.gelu(y_raw + d_ref[...] * u)
    yb = y.astype(BF16)
    y_a = y * jax.nn.sigmoid(jnp.dot(yb, wglu_ref[...], preferred_element_type=F32))
    branch_a = jnp.dot(y_a.astype(BF16), wpa_ref[...], preferred_element_type=F32)

    o_b = jnp.concatenate([ob_ref[s] for s in range(SLOT_WIDTH // LANES)], axis=1)
    branch_b = jnp.dot(o_b.astype(BF16), wpb_ref[...], preferred_element_type=F32)

    merged = ga_ref[...] * branch_a + gb_ref[...] * branch_b
    out_ref[...] = x_ref[...] + jnp.dot(merged.astype(BF16), wout_ref[...], preferred_element_type=F32)


def _mix_out(x, y, u, ssm_d, o_b, ga, gb, w_glu, w_pa, w_pb, w_out, tm, folded_y):
    t = x.shape[0]
    wide = pl.BlockSpec((tm, D_MODEL), lambda i: (i, 0))
    blocked = _lane_blocked_spec(tm, lambda i: i)
    y_spec = pl.BlockSpec((SSM_PAIRS, tm // SSM_CHUNK, MXU_DIM), lambda i: (0, i, 0)) if folded_y else blocked
    scratch = [pltpu.VMEM((LANE_BLOCKS, tm, LANES), F32)] if folded_y else []
    slot = pl.BlockSpec((SLOT_WIDTH // LANES, tm, LANES), lambda i: (0, i, 0))
    return pl.pallas_call(
        _mix_out_body,
        grid=(t // tm,),
        in_specs=[wide, y_spec, blocked, _const_spec((1, D_MODEL)), slot, wide, wide,
                  _const_spec((D_MODEL, D_MODEL)), _const_spec((D_MODEL, D_MODEL)),
                  _const_spec((SLOT_WIDTH, D_MODEL)), _const_spec((D_MODEL, D_MODEL))],
        out_specs=wide,
        out_shape=jax.ShapeDtypeStruct((t, D_MODEL), F32),
        scratch_shapes=scratch,
        compiler_params=_params("parallel"),
        name="mix_out",
    )(x, y, u, ssm_d, o_b, ga, gb, w_glu, w_pa, w_pb, w_out)


def _segment_matrices():
    lane = jnp.arange(ATTN_WIDTH) // HEAD_DIM
    sred = (lane[:, None] == jnp.arange(LANES)[None, :]).astype(BF16)
    return sred, sred.T


def kernel(x_prompt, x_sample, state_ssm_re, state_ssm_im, cache_kv_w128, cache_kv_w512, cache_kv_w2048, ffn1_norm, ffn1_w_gate, ffn1_w_up, ffn1_w_down, mix_norm, w_in, ssm_lambda_re, ssm_lambda_im, ssm_b_re, ssm_b_im, ssm_c_re, ssm_c_im, ssm_d, ssm_log_dt, w_glu, q_gain, k_gain, w_proj_a, w_proj_b, w_out, ffn2_norm, ffn2_w_gate, ffn2_w_up, ffn2_w_down):
    depth = ffn1_norm.shape[0]
    assert depth == 1, "single-layer step"
    batch, seq, _ = x_prompt.shape
    dec_batch, dec_seq, _ = x_sample.shape
    assert dec_seq == 1 and seq % (DIL_RATES[-1] * KEYS_BACK) == 0
    layer = 0
    bf = lambda w: w[layer].astype(BF16)
    vec = lambda w: w[layer][None]
    ffn1 = (vec(ffn1_norm), bf(ffn1_w_gate), bf(ffn1_w_up), bf(ffn1_w_down))
    ffn2 = (vec(ffn2_norm), bf(ffn2_w_gate), bf(ffn2_w_up), bf(ffn2_w_down))
    mix_norm, ssm_d = vec(mix_norm), vec(ssm_d)
    w_in_b, w_glu_b, w_pa_b, w_pb_b, w_out_b = bf(w_in), bf(w_glu), bf(w_proj_a), bf(w_proj_b), bf(w_out)
    q_gain_t = jnp.tile(vec(q_gain), (1, N_DIL * HEADS))
    k_gain_t = jnp.tile(vec(k_gain), (1, N_DIL * HEADS))
    sred, sexp = _segment_matrices()

    apow, bbar, ab, ca, kern = _ssm_prep(
        ssm_lambda_re[layer], ssm_lambda_im[layer], ssm_log_dt[layer],
        ssm_b_re[layer].transpose(0, 2, 1), ssm_b_im[layer].transpose(0, 2, 1),
        ssm_c_re[layer], ssm_c_im[layer])
    toep, win, wout = _ssm_chunk_matrices(ab, ca, kern)
    a8 = apow[:, SSM_CHUNK].reshape(2, SSM_PAIRS, LANES)

    def front(x, tm, seq=None):
        x1 = _ffn(x, *ffn1, tm)
        return (x1,) + tuple(_mix_in(x1, mix_norm, w_in_b, q_gain_t, k_gain_t, sred, sexp, tm, seq))

    def back(x1, y, u, o_b, ga, gb, tm, folded_y):
        x2 = _mix_out(x1, y, u, ssm_d, o_b, ga, gb, w_glu_b, w_pa_b, w_pb_b, w_out_b, tm, folded_y)
        return _ffn(x2, *ffn2, tm)

    tm = 512
    xp = x_prompt.reshape(batch * seq, D_MODEL)
    x1, u, k_tail, v_tail, ga, gb, u_folded, *qkv = front(xp, tm, seq)
    y, h_fin = _ssm_prompt(u_folded, toep, win, wout, a8, batch, seq, tile=1024)
    o_b = _attn_prompt(qkv, batch, seq)
    yp = back(x1, y, u, o_b, ga, gb, tm, folded_y=True)
    yp = yp.reshape(batch, seq, D_MODEL)
    h_fin = h_fin.reshape(batch, 2, SSM_GROUPS, SSM_STATE)
    p_re = h_fin[None, :, 0]
    p_im = h_fin[None, :, 1]
    p_kv = [c.transpose(0, 4, 1, 2, 3)[None] for c in _kv_windows(k_tail, v_tail)]

    xs = x_sample.reshape(dec_batch, D_MODEL)
    x1, u, q, k, v, ga, gb = front(xs, dec_batch)
    width = SSM_GROUPS * SSM_STATE
    y, s_re, s_im = _ssm_step(u, state_ssm_re[layer].reshape(dec_batch, width),
                              state_ssm_im[layer].reshape(dec_batch, width),
                              apow[:, 1], bbar, ssm_c_re[layer], ssm_c_im[layer])
    caches = [c[layer].transpose(0, 2, 3, 4, 1) for c in (cache_kv_w128, cache_kv_w512, cache_kv_w2048)]
    o_b, new_caches = _attn_sample(q, k, v, caches)
    o_b = o_b.reshape(dec_batch, SLOT_WIDTH // LANES, LANES).transpose(1, 0, 2)
    ys = back(x1, y, u, o_b, ga, gb, dec_batch, folded_y=False)
    ys = ys.reshape(dec_batch, 1, D_MODEL)
    s_re = s_re.reshape(1, dec_batch, SSM_GROUPS, SSM_STATE)
    s_im = s_im.reshape(1, dec_batch, SSM_GROUPS, SSM_STATE)
    s_kv = [c.transpose(0, 4, 1, 2, 3)[None] for c in new_caches]

    return (yp, ys, p_re, p_im, p_kv[0], p_kv[1], p_kv[2], s_re, s_im, s_kv[0], s_kv[1], s_kv[2])
```

```python
import functools

import jax
import jax.numpy as jnp
from jax import lax
from jax.experimental import pallas as pl
from jax.experimental.pallas import tpu as pltpu

F32 = jnp.float32
BF16 = jnp.bfloat16

D_MODEL = 1024
SSM_GROUP = 16
SSM_GROUPS = 64
SSM_STATE = 64
SSM_PAIRS = SSM_GROUPS // 2
HEAD_DIM = 64
HEADS = 4
DIL_WINDOWS = (128, 512, 2048)
DIL_RATES = (1, 4, 16)
N_DIL = 3
KEYS_BACK = 128
ATTN_WIDTH = N_DIL * HEADS * HEAD_DIM
SLOT_WIDTH = HEADS * HEAD_DIM
D_FF = 2816
RMS_EPS = 1e-6
ALIBI_MAX_EXP = 8.0
IN_WIDTH = D_MODEL + 3 * ATTN_WIDTH + 2 * D_MODEL

LANES = 128
SUBLANES = 8
MXU_DIM = 256
VMEM_LIMIT_BYTES = 56 * 1024 * 1024

LANE_BLOCKS = D_MODEL // LANES

SSM_CHUNK = SUBLANES
SSM_ROW_PITCH = 40
FF_CHUNKS = ((0, 1024), (1024, 2048), (2048, 2816))


def _slope(group, head):
    return 2.0 ** (-ALIBI_MAX_EXP * (group * HEADS + head + 1) / (N_DIL * HEADS))


def _const_spec(shape):
    zeros = (0,) * len(shape)
    return pl.BlockSpec(shape, lambda *_: zeros, pipeline_mode=pl.Buffered(1))


def _lane_blocked_spec(rows, row_block):
    return pl.BlockSpec((LANE_BLOCKS, rows, LANES), lambda *idx: (0, row_block(*idx), 0))


def _params(*semantics):
    return pltpu.CompilerParams(dimension_semantics=semantics, vmem_limit_bytes=VMEM_LIMIT_BYTES)


def _rms(x, w):
    return x * lax.rsqrt(jnp.mean(x * x, axis=-1, keepdims=True) + RMS_EPS) * w


def _split_bf16(x, terms):
    parts = []
    for _ in range(terms):
        p = x.astype(BF16)
        parts.append(p)
        x = x - p.astype(F32)
    return parts


def _ffn_body(x_ref, nw_ref, wg_ref, wu_ref, wd_ref, o_ref):
    x = x_ref[...]
    h = _rms(x, nw_ref[...]).astype(BF16)
    acc = jnp.zeros_like(x)
    for lo, hi in FF_CHUNKS:
        g = jnp.dot(h, wg_ref[:, lo:hi], preferred_element_type=F32)
        u = jnp.dot(h, wu_ref[:, lo:hi], preferred_element_type=F32)
        a = (jax.nn.silu(g) * u).astype(BF16)
        acc = acc + jnp.dot(a, wd_ref[lo:hi, :], preferred_element_type=F32)
    o_ref[...] = x + 0.5 * acc


def _ffn(x, norm_w, wg, wu, wd, tm):
    t = x.shape[0]
    row = pl.BlockSpec((tm, D_MODEL), lambda i: (i, 0))
    return pl.pallas_call(
        _ffn_body,
        grid=(t // tm,),
        in_specs=[row, _const_spec((1, D_MODEL)), _const_spec((D_MODEL, D_FF)),
                  _const_spec((D_MODEL, D_FF)), _const_spec((D_FF, D_MODEL))],
        out_specs=row,
        out_shape=jax.ShapeDtypeStruct((t, D_MODEL), F32),
        compiler_params=_params("parallel"),
        name="ffn",
    )(x, norm_w, wg, wu, wd)


def _head_norm(x, gain, seg_ones):
    parts = _split_bf16(x * x, 2)
    blocks = []
    for lo in range(0, x.shape[1], MXU_DIM):
        blocks.append(sum(jnp.dot(p[:, lo:lo + MXU_DIM], seg_ones, preferred_element_type=F32) for p in parts))
    ss = jnp.concatenate(blocks, axis=1)
    return x * lax.rsqrt(ss * (1.0 / HEAD_DIM) + RMS_EPS) * gain


def _transpose_granules(xs):
    xs = list(xs)
    n = len(xs)
    block = lax.broadcasted_iota(jnp.int32, (1, LANES), 1) // SSM_GROUP
    bit = n // 2
    while bit:
        upper = (block & bit) != 0
        shift = SSM_GROUP * bit
        for lo in range(n):
            if lo & bit:
                continue
            hi = lo + bit
            x_lo, x_hi = xs[lo], xs[hi]
            xs[lo] = jnp.where(upper, pltpu.roll(x_hi, shift, 1), x_lo)
            xs[hi] = jnp.where(upper, x_hi, pltpu.roll(x_lo, LANES - shift, 1))
        bit //= 2
    return xs


def _mix_in_body(x_ref, nw_ref, w_ref, qg_ref, kg_ref, seg_ref, u_ref, *rest, prompt):
    h = _rms(x_ref[...], nw_ref[...]).astype(BF16)

    def proj(lo, hi):
        return jnp.dot(h, w_ref[:, lo:hi], preferred_element_type=F32)

    c0 = D_MODEL
    c1 = c0 + ATTN_WIDTH
    c2 = c1 + ATTN_WIDTH
    c3 = c2 + ATTN_WIDTH
    c4 = c3 + D_MODEL
    u = proj(0, c0)
    for blk in range(LANE_BLOCKS):
        u_ref[blk] = u[:, blk * LANES:(blk + 1) * LANES]
    seg_ones = seg_ref[...]
    q = _head_norm(proj(c0, c1), qg_ref[...], seg_ones) * (HEAD_DIM ** -0.5)
    k = _head_norm(proj(c1, c2), kg_ref[...], seg_ones)
    v = proj(c2, c3)
    if not prompt:
        q_ref, k_ref, v_ref, ga_ref, gb_ref = rest
        q_ref[...] = q
    else:
        k_ref, v_ref, ga_ref, gb_ref, fold_ref = rest[:5]
        by_residue = rest[5:5 + 3 * N_DIL]
        stage_ref = rest[-1]
    k_ref[...] = k
    v_ref[...] = v
    ga_ref[...] = jax.nn.sigmoid(proj(c3, c4))
    gb_ref[...] = jax.nn.sigmoid(proj(c4, IN_WIDTH))
    if not prompt:
        return

    rows = x_ref.shape[0]
    groups_per_block = LANES // SSM_GROUP
    for blk in range(LANE_BLOCKS):
        tokens = [u_ref[blk, pl.ds(s, rows // SSM_CHUNK, stride=SSM_CHUNK), :] for s in range(SSM_CHUNK)]
        for q_i, folded in enumerate(_transpose_granules(tokens)):
            pair = blk * (groups_per_block // 2) + q_i // 2
            fold_ref[pair, :, (q_i % 2) * LANES:(q_i % 2 + 1) * LANES] = folded.astype(BF16)

    slabs = SLOT_WIDTH // LANES
    for ti, x in enumerate((q, k, v)):
        outs = by_residue[ti * N_DIL:(ti + 1) * N_DIL]
        outs[0][0] = x[:, :SLOT_WIDTH].astype(BF16)
        for g in range(1, N_DIL):
            d = DIL_RATES[g]
            for s in range(slabs):
                lo = g * SLOT_WIDTH + s * LANES
                stage_ref[ti, (g - 1) * slabs + s] = x[:, lo:lo + LANES]
            for r in range(d):
                for s in range(slabs):
                    piece = stage_ref[ti, (g - 1) * slabs + s, pl.ds(r, rows // d, stride=d), :]
                    outs[g][r, :, s * LANES:(s + 1) * LANES] = piece.astype(BF16)


def _mix_in(x, norm_w, w_in, q_gain, k_gain, seg_ones, tm, seq=None):
    t = x.shape[0]
    prompt = seq is not None

    def row(width):
        return pl.BlockSpec((tm, width), lambda i: (i, 0))

    def rows_f32(width):
        return jax.ShapeDtypeStruct((t, width), F32)

    out_specs = [_lane_blocked_spec(tm, lambda i: i)]
    out_shape = [jax.ShapeDtypeStruct((LANE_BLOCKS, t, LANES), F32)]
    scratch = []
    if not prompt:
        out_specs += [row(ATTN_WIDTH)] * 3 + [row(D_MODEL)] * 2
        out_shape += [rows_f32(ATTN_WIDTH)] * 3 + [rows_f32(D_MODEL)] * 2
    else:
        tiles = seq // tm
        tail = min(DIL_WINDOWS[-1], seq)
        skip = tiles - tail // tm
        tail_spec = pl.BlockSpec((None, tm, ATTN_WIDTH), lambda i: (i // tiles, jnp.maximum(i % tiles - skip, 0), 0))
        out_specs += [tail_spec] * 2 + [row(D_MODEL)] * 2
        out_shape += [jax.ShapeDtypeStruct((t // seq, tail, ATTN_WIDTH), F32)] * 2 + [rows_f32(D_MODEL)] * 2
        out_specs.append(pl.BlockSpec((SSM_PAIRS, tm // SSM_CHUNK, MXU_DIM), lambda i: (0, i, 0)))
        out_shape.append(jax.ShapeDtypeStruct((SSM_PAIRS, t // SSM_CHUNK, MXU_DIM), BF16))
        for _ in range(3):
            for d in DIL_RATES:
                out_specs.append(pl.BlockSpec((None, d, tm // d, SLOT_WIDTH),
                                              lambda i: (i // tiles, 0, i % tiles, 0)))
                out_shape.append(jax.ShapeDtypeStruct((t // seq, d, seq // d, SLOT_WIDTH), BF16))
        scratch = [pltpu.VMEM((3, (N_DIL - 1) * SLOT_WIDTH // LANES, tm, LANES), F32)]
    return pl.pallas_call(
        functools.partial(_mix_in_body, prompt=prompt),
        grid=(t // tm,),
        in_specs=[row(D_MODEL), _const_spec((1, D_MODEL)), _const_spec((D_MODEL, IN_WIDTH)),
                  _const_spec((1, ATTN_WIDTH)), _const_spec((1, ATTN_WIDTH)),
                  _const_spec((MXU_DIM, MXU_DIM))],
        out_specs=out_specs,
        out_shape=out_shape,
        scratch_shapes=scratch,
        compiler_params=_params("arbitrary"),
        name="mix_in",
    )(x, norm_w, w_in, q_gain, k_gain, seg_ones)


def _cmul(ar, ai, br, bi):
    return ar * br - ai * bi, ar * bi + ai * br


def _ssm_prep_body(lre_ref, lim_ref, ldt_ref, bre_ref, bim_ref, cre_ref, cim_ref,
                   apow_ref, bbar_ref, ab_ref, ca_ref, kern_ref):
    lr = jnp.minimum(lre_ref[...], -1e-4)
    li = lim_ref[...]
    dt = jnp.exp(ldt_ref[...])
    mag = jnp.exp(lr * dt)
    ar = mag * jnp.cos(li * dt)
    ai = mag * jnp.sin(li * dt)
    nr = ar - 1.0
    ni = ai
    den = lr * lr + li * li
    fr = (nr * lr + ni * li) / den
    fi = (ni * lr - nr * li) / den
    bbr, bbi = _cmul(fr, fi, bre_ref[...], bim_ref[...])
    bbar_ref[0] = bbr
    bbar_ref[1] = bbi
    cre = cre_ref[...]
    cim = cim_ref[...]
    pr = jnp.ones_like(ar)
    pi = jnp.zeros_like(ai)
    nt = (((2,), (2,)), ((0,), (0,)))
    for t in range(SSM_CHUNK + 1):
        apow_ref[0, t] = pr
        apow_ref[1, t] = pi
        car, cai = _cmul(cre, cim, pr, pi)
        ca_ref[0, t] = car
        ca_ref[1, t] = cai
        if t < SSM_CHUNK:
            abr, abi = _cmul(pr, pi, bbr, bbi)
            ab_ref[0, t] = abr
            ab_ref[1, t] = abi
            kern_ref[t] = (
                lax.dot_general(car, bbr, nt, precision=lax.Precision.HIGHEST, preferred_element_type=F32)
                - lax.dot_general(cai, bbi, nt, precision=lax.Precision.HIGHEST, preferred_element_type=F32))
        pr, pi = _cmul(pr, pi, ar, ai)


def _ssm_prep(lam_re, lam_im, log_dt, b_re_t, b_im_t, c_re, c_im):
    g, p, c = SSM_GROUPS, SSM_STATE, SSM_GROUP
    n = SSM_CHUNK
    return pl.pallas_call(
        _ssm_prep_body,
        out_shape=[jax.ShapeDtypeStruct((2, n + 1, g, 1, p), F32),
                   jax.ShapeDtypeStruct((2, g, c, p), F32),
                   jax.ShapeDtypeStruct((2, n, g, c, p), F32),
                   jax.ShapeDtypeStruct((2, n + 1, g, c, p), F32),
                   jax.ShapeDtypeStruct((n, g, c, c), F32)],
        compiler_params=pltpu.CompilerParams(vmem_limit_bytes=VMEM_LIMIT_BYTES),
        name="ssm_prep",
    )(lam_re.reshape(g, 1, p), lam_im.reshape(g, 1, p), log_dt.reshape(g, 1, 1), b_re_t, b_im_t, c_re, c_im)


def _pair_block_diag(m):
    g, r, c = m.shape
    m = m.reshape(g // 2, 2, r, c)
    z = jnp.zeros_like(m[:, 0])
    top = jnp.concatenate([m[:, 0], z], axis=2)
    bot = jnp.concatenate([z, m[:, 1]], axis=2)
    return jnp.concatenate([top, bot], axis=1)


def _ssm_chunk_matrices(ab, ca, kern):
    n, g, c, p = SSM_CHUNK, SSM_GROUPS, SSM_GROUP, SSM_STATE
    s_idx = jnp.arange(n)[:, None]
    t_idx = jnp.arange(n)[None, :]
    lag = jnp.clip(t_idx - s_idx, 0, n - 1)
    blocks = kern[lag]
    blocks = jnp.where((t_idx >= s_idx)[:, :, None, None, None], blocks, 0.0)
    toep = blocks.transpose(2, 0, 4, 1, 3).reshape(g, n * c, n * c)
    toep = _pair_block_diag(toep)
    rev = ab[:, ::-1]
    win = rev.transpose(0, 2, 1, 3, 4).reshape(2, g, n * c, p)
    win = jnp.concatenate([_pair_block_diag(win[0]), _pair_block_diag(win[1])], axis=2)
    ca1 = ca[:, 1:]
    wout = ca1.transpose(0, 2, 4, 1, 3).reshape(2, g, p, n * c)
    wout = jnp.concatenate([_pair_block_diag(wout[0]), -_pair_block_diag(wout[1])], axis=1)
    return toep.astype(BF16), win.astype(BF16), wout.astype(BF16)


def _ssm_body(lhs_ref, toep_ref, win_ref, wout_ref, a8_ref, yfl_ref, hfin_ref, st_ref, carry_ref, *, rows):
    i = pl.program_id(1)

    @pl.when(i == 0)
    def _():
        carry_ref[...] = jnp.zeros_like(carry_ref)

    def state_in(r, _):
        b = jnp.dot(lhs_ref[r], win_ref[r], preferred_element_type=F32)
        st_ref[0, pl.ds(r, rows, stride=SSM_ROW_PITCH), :] = b[:, :LANES]
        st_ref[1, pl.ds(r, rows, stride=SSM_ROW_PITCH), :] = b[:, LANES:]
        return 0

    lax.fori_loop(0, SSM_PAIRS, state_in, 0, unroll=4)

    a_re = a8_ref[0]
    a_im = a8_ref[1]

    def step(j, h):
        h_re, h_im = h
        base = pl.multiple_of(j * SSM_ROW_PITCH, SUBLANES)
        n_re = a_re * h_re - a_im * h_im + st_ref[0, pl.ds(base, SSM_PAIRS), :]
        n_im = a_re * h_im + a_im * h_re + st_ref[1, pl.ds(base, SSM_PAIRS), :]
        st_ref[0, pl.ds(base, SSM_PAIRS), :] = h_re
        st_ref[1, pl.ds(base, SSM_PAIRS), :] = h_im
        return n_re, n_im

    h_re, h_im = lax.fori_loop(0, rows, step, (carry_ref[0], carry_ref[1]), unroll=4)
    carry_ref[0] = h_re
    carry_ref[1] = h_im
    hfin_ref[0] = h_re
    hfin_ref[1] = h_im

    def chunk_out(r, _):
        hcat = jnp.concatenate([st_ref[0, pl.ds(r, rows, stride=SSM_ROW_PITCH), :],
                                st_ref[1, pl.ds(r, rows, stride=SSM_ROW_PITCH), :]], axis=1).astype(BF16)
        yfl_ref[r] = (jnp.dot(lhs_ref[r], toep_ref[r], preferred_element_type=F32)
                      + jnp.dot(hcat, wout_ref[r], preferred_element_type=F32))
        return 0

    lax.fori_loop(0, SSM_PAIRS, chunk_out, 0, unroll=4)


def _ssm_prompt(lhs, toep, win, wout, a8, batch, seq, tile):
    rows = tile // SSM_CHUNK
    n_tiles = seq // tile
    tok = pl.BlockSpec((SSM_PAIRS, rows, MXU_DIM), lambda b, i: (0, b * n_tiles + i, 0))
    pair_w = _const_spec((SSM_PAIRS, MXU_DIM, MXU_DIM))
    return pl.pallas_call(
        functools.partial(_ssm_body, rows=rows),
        grid=(batch, n_tiles),
        in_specs=[tok, pair_w, pair_w, pair_w, _const_spec((2, SSM_PAIRS, LANES))],
        out_specs=[tok, pl.BlockSpec((None, 2, SSM_PAIRS, LANES), lambda b, i: (b, 0, 0, 0))],
        out_shape=[jax.ShapeDtypeStruct((SSM_PAIRS, batch * seq // SSM_CHUNK, MXU_DIM), F32),
                   jax.ShapeDtypeStruct((batch, 2, SSM_PAIRS, LANES), F32)],
        scratch_shapes=[pltpu.VMEM((2, rows * SSM_ROW_PITCH, LANES), F32),
                        pltpu.VMEM((2, SSM_PAIRS, LANES), F32)],
        compiler_params=_params("parallel", "arbitrary"),
        name="ssm_prompt",
    )(lhs, toep, win, wout, a8)


def _ssm_step_body(u_ref, hre_ref, him_ref, abar_ref, bbar_ref, cre_ref, cim_ref,
                   y_ref, ore_ref, oim_ref):
    hp = lax.Precision.HIGHEST
    nt = (((1,), (1,)), ((), ()))
    groups_per_block = LANES // SSM_GROUP
    for g in range(SSM_GROUPS):
        blk = g // groups_per_block
        ch = slice((g % groups_per_block) * SSM_GROUP, (g % groups_per_block + 1) * SSM_GROUP)
        st = slice(g * SSM_STATE, (g + 1) * SSM_STATE)
        ug = u_ref[blk, :, ch]
        bu_re = jnp.dot(ug, bbar_ref[0, g], precision=hp, preferred_element_type=F32)
        bu_im = jnp.dot(ug, bbar_ref[1, g], precision=hp, preferred_element_type=F32)
        a_re = abar_ref[0, g]
        a_im = abar_ref[1, g]
        h_re = hre_ref[:, st]
        h_im = him_ref[:, st]
        n_re = a_re * h_re - a_im * h_im + bu_re
        n_im = a_re * h_im + a_im * h_re + bu_im
        ore_ref[:, st] = n_re
        oim_ref[:, st] = n_im
        y_ref[blk, :, ch] = (lax.dot_general(n_re, cre_ref[g], nt, precision=hp, preferred_element_type=F32)
                             - lax.dot_general(n_im, cim_ref[g], nt, precision=hp, preferred_element_type=F32))


def _ssm_step(u, h_re, h_im, abar, bbar, c_re, c_im):
    b = u.shape[1]
    width = SSM_GROUPS * SSM_STATE
    return pl.pallas_call(
        _ssm_step_body,
        out_shape=[jax.ShapeDtypeStruct((LANE_BLOCKS, b, LANES), F32),
                   jax.ShapeDtypeStruct((b, width), F32),
                   jax.ShapeDtypeStruct((b, width), F32)],
        compiler_params=pltpu.CompilerParams(vmem_limit_bytes=VMEM_LIMIT_BYTES),
        name="ssm_step",
    )(u, h_re, h_im, abar, bbar, c_re, c_im)


def _attn_prompt_body(*refs, seq):
    qkv = refs[:3 * N_DIL]
    o_ref, m_ref, l_ref, acc_ref, bias_ref = refs[3 * N_DIL:]
    tq = KEYS_BACK
    row = lax.broadcasted_iota(jnp.int32, (tq, 2 * tq), 0)
    col = lax.broadcasted_iota(jnp.int32, (tq, 2 * tq), 1)
    back = row + tq - col
    in_window = (back >= 0) & (back <= KEYS_BACK)
    first_head = lax.broadcasted_iota(jnp.int32, (1, LANES), 1) < HEAD_DIM
    nt = (((1,), (1,)), ((), ()))

    order = tuple(reversed(range(N_DIL)))
    ones = jnp.ones((2 * tq, LANES), BF16)
    for g in order:
        q_ref, k_ref, v_ref = qkv[g], qkv[N_DIL + g], qkv[2 * N_DIL + g]
        d = DIL_RATES[g]
        blocks_per_residue = seq // d // tq
        dist = (back * d).astype(F32)
        for h in range(HEADS):
            bias = jnp.where(in_window, -_slope(g, h) * dist, -jnp.inf)
            bias_ref[h] = bias
            bias_ref[HEADS + h] = jnp.where(col >= tq, bias, -jnp.inf)

        def block(mi, _, g=g, d=d, q_ref=q_ref, k_ref=k_ref, v_ref=v_ref, blocks_per_residue=blocks_per_residue):
            residue = mi // blocks_per_residue
            n = mi % blocks_per_residue
            cur = pl.multiple_of(mi * tq, tq)
            prev = pl.multiple_of(jnp.maximum(mi - 1, 0) * tq, tq)
            bias_at = jnp.where(n == 0, HEADS, 0)
            token0 = residue + d * tq * n
            rows = pl.ds(pl.multiple_of(token0, tq), tq) if d == 1 else pl.ds(token0, tq, stride=d)
            pairs = range(HEADS // 2)
            first, final = g == order[0], g == order[-1]
            old = None if first else [(m_ref[pair, rows, :], l_ref[pair, rows, :], acc_ref[pair, rows, :])
                                      for pair in pairs]
            new = []
            for pair in pairs:
                lanes = slice(pair * LANES, (pair + 1) * LANES)
                qp = q_ref[pl.ds(cur, tq), lanes]
                kp = jnp.concatenate([k_ref[pl.ds(prev, tq), lanes], k_ref[pl.ds(cur, tq), lanes]], axis=0)
                vp = jnp.concatenate([v_ref[pl.ds(prev, tq), lanes], v_ref[pl.ds(cur, tq), lanes]], axis=0)
                vp = jnp.concatenate([vp, ones], axis=1)
                stats = []
                for e in range(2):
                    qm = jnp.where(first_head if e == 0 else ~first_head, qp, jnp.zeros_like(qp))
                    s = lax.dot_general(qm, kp, nt, preferred_element_type=F32)
                    s = s + bias_ref[bias_at + 2 * pair + e]
                    m = jnp.max(s, axis=-1, keepdims=True)
                    p = jnp.exp(s - m)
                    pv = jnp.dot(p.astype(BF16), vp, preferred_element_type=F32)
                    stats.append((m, pv[:, LANES:], pv[:, :LANES]))
                m_new, l_new, acc_new = (jnp.where(first_head, a, b) for a, b in zip(*stats))
                if not first:
                    m_old, l_old, acc_old = old[pair]
                    m_tot = jnp.maximum(m_old, m_new)
                    w_old = jnp.exp(m_old - m_tot)
                    w_new = jnp.exp(m_new - m_tot)
                    l_new = w_old * l_old + w_new * l_new
                    acc_new = w_old * acc_old + w_new * acc_new
                    m_new = m_tot
                new.append((m_new, l_new, acc_new))
            for pair, (m_new, l_new, acc_new) in zip(pairs, new):
                if not final:
                    m_ref[pair, rows, :] = m_new
                    l_ref[pair, rows, :] = l_new
                    acc_ref[pair, rows, :] = acc_new
                else:
                    o_ref[pair, rows, :] = acc_new / l_new
            return 0

        lax.fori_loop(0, seq // tq, block, 0, unroll=2)


def _attn_prompt(qkv, batch, seq):
    slabs = SLOT_WIDTH // LANES
    seq_spec = pl.BlockSpec((None, seq, SLOT_WIDTH), lambda b: (b, 0, 0), pipeline_mode=pl.Buffered(1))
    running = pltpu.VMEM((slabs, seq, LANES), F32)
    return pl.pallas_call(
        functools.partial(_attn_prompt_body, seq=seq),
        grid=(batch,),
        in_specs=[seq_spec] * len(qkv),
        out_specs=pl.BlockSpec((slabs, seq, LANES), lambda b: (0, b, 0)),
        out_shape=jax.ShapeDtypeStruct((slabs, batch * seq, LANES), F32),
        scratch_shapes=[running, running, running, pltpu.VMEM((2 * HEADS, KEYS_BACK, 2 * KEYS_BACK), F32)],
        compiler_params=_params("parallel"),
        name="attn_prompt",
    )(*[a.reshape(batch, seq, SLOT_WIDTH) for a in qkv])


def _as_column(row_vec):
    n = row_vec.shape[1]
    eye = lax.broadcasted_iota(jnp.int32, (n, n), 0) == lax.broadcasted_iota(jnp.int32, (n, n), 1)
    return jnp.sum(jnp.where(eye, row_vec, 0.0), axis=1, keepdims=True)


def _attn_sample_body(q_ref, k_ref, v_ref, c0_ref, c1_ref, c2_ref, ob_ref, n0_ref, n1_ref, n2_ref):
    b = pl.program_id(0)
    q_row = q_ref[pl.ds(b, 1), :]
    k_row = k_ref[pl.ds(b, 1), :]
    v_row = v_ref[pl.ds(b, 1), :]
    head_row = lax.broadcasted_iota(jnp.int32, (SUBLANES, SLOT_WIDTH), 0)
    own_head = lax.broadcasted_iota(jnp.int32, (SUBLANES, SLOT_WIDTH), 1) // HEAD_DIM == head_row
    head_col = lax.broadcasted_iota(jnp.int32, (SUBLANES, 1), 0)
    outs, lses = [], []
    for g, (c_ref, n_ref) in enumerate(((c0_ref, n0_ref), (c1_ref, n1_ref), (c2_ref, n2_ref))):
        w = DIL_WINDOWS[g]
        d = DIL_RATES[g]
        cols = slice(g * SLOT_WIDTH, (g + 1) * SLOT_WIDTH)
        q_g, k_new, v_new = q_row[:, cols], k_row[:, cols], v_row[:, cols]
        kt = c_ref[0].reshape(SLOT_WIDTH, w)
        vt = c_ref[1].reshape(SLOT_WIDTH, w)
        q_heads = jnp.where(own_head, q_g, 0.0)
        s = jnp.dot(q_heads.astype(BF16), kt.astype(BF16), preferred_element_type=F32)
        dist = w - lax.broadcasted_iota(jnp.int32, (1, w), 1)
        slope = functools.reduce(lambda acc, h: jnp.where(head_col == h, _slope(g, h), acc), range(HEADS), 0.0)
        s = jnp.where((dist & (d - 1)) == 0, s - slope * dist.astype(F32), -jnp.inf)
        s_new = jnp.sum(q_heads * k_new, axis=1, keepdims=True)
        m = jnp.maximum(jnp.max(s, axis=1, keepdims=True), s_new)
        p = jnp.exp(s - m)
        p_new = jnp.exp(s_new - m)
        den = jnp.sum(p, axis=1, keepdims=True) + p_new
        pv = lax.dot_general(p.astype(BF16), vt.astype(BF16), (((1,), (1,)), ((), ())),
                             preferred_element_type=F32)
        o_heads = (pv + p_new * v_new) / den
        outs.append(jnp.sum(jnp.where(own_head, o_heads, 0.0), axis=0, keepdims=True))
        lses.append(jnp.sum(jnp.where(own_head, m + jnp.log(den), 0.0), axis=0, keepdims=True))
        last = lax.broadcasted_iota(jnp.int32, (SLOT_WIDTH, w), 1) == w - 1
        shape = (HEADS, HEAD_DIM, w)
        n_ref[0] = jnp.where(last, _as_column(k_new), pltpu.roll(kt, w - 1, 1)).reshape(shape)
        n_ref[1] = jnp.where(last, _as_column(v_new), pltpu.roll(vt, w - 1, 1)).reshape(shape)
    top = functools.reduce(jnp.maximum, lses)
    wts = [jnp.exp(l - top) for l in lses]
    ob_ref[...] = sum(w_g * o_g for w_g, o_g in zip(wts, outs)) / sum(wts)


def _attn_sample(q, k, v, caches):
    b = q.shape[0]
    full = _const_spec((b, ATTN_WIDTH))
    win = [pl.BlockSpec((None, 2, HEADS, HEAD_DIM, w), lambda i: (i, 0, 0, 0, 0)) for w in DIL_WINDOWS]
    outs = pl.pallas_call(
        _attn_sample_body,
        grid=(b,),
        in_specs=[full, full, full] + win,
        out_specs=[pl.BlockSpec((None, 1, SLOT_WIDTH), lambda i: (i, 0, 0))] + win,
        out_shape=([jax.ShapeDtypeStruct((b, 1, SLOT_WIDTH), F32)]
                   + [jax.ShapeDtypeStruct(c.shape, F32) for c in caches]),
        compiler_params=_params("parallel"),
        name="attn_sample",
    )(q, k, v, *caches)
    return outs[0].reshape(b, SLOT_WIDTH), outs[1:]


def _kv_window_body(k_ref, v_ref, *out_refs):
    tail = k_ref.shape[0]
    for g, o_ref in enumerate(out_refs):
        w = o_ref.shape[-1]
        for t, x_ref in enumerate((k_ref, v_ref)):
            for pair in range(HEADS // 2):
                lo = g * SLOT_WIDTH + pair * LANES
                xt = x_ref[tail - w:tail, lo:lo + LANES].T
                o_ref[t, 2 * pair] = xt[:HEAD_DIM]
                o_ref[t, 2 * pair + 1] = xt[HEAD_DIM:]


def _kv_windows(k_tail, v_tail):
    batch, tail, _ = k_tail.shape
    keeps = [min(w, tail) for w in DIL_WINDOWS]
    src = pl.BlockSpec((None, tail, ATTN_WIDTH), lambda b: (b, 0, 0))
    return pl.pallas_call(
        _kv_window_body,
        grid=(batch,),
        in_specs=[src, src],
        out_specs=[pl.BlockSpec((None, 2, HEADS, HEAD_DIM, w), lambda b: (b, 0, 0, 0, 0)) for w in keeps],
        out_shape=[jax.ShapeDtypeStruct((batch, 2, HEADS, HEAD_DIM, w), F32) for w in keeps],
        compiler_params=_params("parallel"),
        name="kv_windows",
    )(k_tail, v_tail)


def _mix_out_body(x_ref, y_ref, u_ref, d_ref, ob_ref,
                  ga_ref, gb_ref, wglu_ref, wpa_ref, wpb_ref, wout_ref, out_ref, *unfold):
    if unfold:
        y_ref, folded_ref = unfold[0], y_ref
        chunks = folded_ref.shape[1]
        groups_per_block = LANES // SSM_GROUP
        for blk in range(LANE_BLOCKS):
            groups = [folded_ref[blk * (groups_per_block // 2) + q_i // 2, :, (q_i % 2) * LANES:(q_i % 2 + 1) * LANES]
                      for q_i in range(groups_per_block)]
            for t, rows_t in enumerate(_transpose_granules(groups)):
                y_ref[blk, pl.ds(t, chunks, stride=SSM_CHUNK), :] = rows_t
    y_raw = jnp.concatenate([y_ref[blk] for blk in range(LANE_BLOCKS)], axis=1)
    u = jnp.concatenate([u_ref[blk] for blk in range(LANE_BLOCKS)], axis=1)
    y = jax.nn.gelu(y_raw + d_ref[...] * u)
    yb = y.astype(BF16)
    y_a = y * jax.nn.sigmoid(jnp.dot(yb, wglu_ref[...], preferred_element_type=F32))
    branch_a = jnp.dot(y_a.astype(BF16), wpa_ref[...], preferred_element_type=F32)

    o_b = jnp.concatenate([ob_ref[s] for s in range(SLOT_WIDTH // LANES)], axis=1)
    branch_b = jnp.dot(o_b.astype(BF16), wpb_ref[...], preferred_element_type=F32)

    merged = ga_ref[...] * branch_a + gb_ref[...] * branch_b
    out_ref[...] = x_ref[...] + jnp.dot(merged.astype(BF16), wout_ref[...], preferred_element_type=F32)


def _mix_out(x, y, u, ssm_d, o_b, ga, gb, w_glu, w_pa, w_pb, w_out, tm, folded_y):
    t = x.shape[0]
    wide = pl.BlockSpec((tm, D_MODEL), lambda i: (i, 0))
    blocked = _lane_blocked_spec(tm, lambda i: i)
    y_spec = pl.BlockSpec((SSM_PAIRS, tm // SSM_CHUNK, MXU_DIM), lambda i: (0, i, 0)) if folded_y else blocked
    scratch = [pltpu.VMEM((LANE_BLOCKS, tm, LANES), F32)] if folded_y else []
    slot = pl.BlockSpec((SLOT_WIDTH // LANES, tm, LANES), lambda i: (0, i, 0))
    return pl.pallas_call(
        _mix_out_body,
        grid=(t // tm,),
        in_specs=[wide, y_spec, blocked, _const_spec((1, D_MODEL)), slot, wide, wide,
                  _const_spec((D_MODEL, D_MODEL)), _const_spec((D_MODEL, D_MODEL)),
                  _const_spec((SLOT_WIDTH, D_MODEL)), _const_spec((D_MODEL, D_MODEL))],
        out_specs=wide,
        out_shape=jax.ShapeDtypeStruct((t, D_MODEL), F32),
        scratch_shapes=scratch,
        compiler_params=_params("parallel"),
        name="mix_out",
    )(x, y, u, ssm_d, o_b, ga, gb, w_glu, w_pa, w_pb, w_out)


def _head_segment_ones():
    head = jnp.arange(MXU_DIM) // HEAD_DIM
    return (head[:, None] == head[None, :]).astype(BF16)


def kernel(x_prompt, x_sample, state_ssm_re, state_ssm_im, cache_kv_w128, cache_kv_w512, cache_kv_w2048, ffn1_norm, ffn1_w_gate, ffn1_w_up, ffn1_w_down, mix_norm, w_in, ssm_lambda_re, ssm_lambda_im, ssm_b_re, ssm_b_im, ssm_c_re, ssm_c_im, ssm_d, ssm_log_dt, w_glu, q_gain, k_gain, w_proj_a, w_proj_b, w_out, ffn2_norm, ffn2_w_gate, ffn2_w_up, ffn2_w_down):
    depth = ffn1_norm.shape[0]
    assert depth == 1, "single-layer step"
    batch, seq, _ = x_prompt.shape
    dec_batch, dec_seq, _ = x_sample.shape
    assert dec_seq == 1 and seq % (DIL_RATES[-1] * KEYS_BACK) == 0
    layer = 0
    bf = lambda w: w[layer].astype(BF16)
    vec = lambda w: w[layer][None]
    ffn1 = (vec(ffn1_norm), bf(ffn1_w_gate), bf(ffn1_w_up), bf(ffn1_w_down))
    ffn2 = (vec(ffn2_norm), bf(ffn2_w_gate), bf(ffn2_w_up), bf(ffn2_w_down))
    mix_norm, ssm_d = vec(mix_norm), vec(ssm_d)
    w_in_b, w_glu_b, w_pa_b, w_pb_b, w_out_b = bf(w_in), bf(w_glu), bf(w_proj_a), bf(w_proj_b), bf(w_out)
    q_gain_t = jnp.tile(vec(q_gain), (1, N_DIL * HEADS))
    k_gain_t = jnp.tile(vec(k_gain), (1, N_DIL * HEADS))
    seg_ones = _head_segment_ones()

    apow, bbar, ab, ca, kern = _ssm_prep(
        ssm_lambda_re[layer], ssm_lambda_im[layer], ssm_log_dt[layer],
        ssm_b_re[layer].transpose(0, 2, 1), ssm_b_im[layer].transpose(0, 2, 1),
        ssm_c_re[layer], ssm_c_im[layer])
    toep, win, wout = _ssm_chunk_matrices(ab, ca, kern)
    a8 = apow[:, SSM_CHUNK].reshape(2, SSM_PAIRS, LANES)

    def front(x, tm_ffn, tm, seq=None):
        x1 = _ffn(x, *ffn1, tm_ffn)
        return (x1,) + tuple(_mix_in(x1, mix_norm, w_in_b, q_gain_t, k_gain_t, seg_ones, tm, seq))

    def back(x1, y, u, o_b, ga, gb, tm_ffn, tm, folded_y):
        x2 = _mix_out(x1, y, u, ssm_d, o_b, ga, gb, w_glu_b, w_pa_b, w_pb_b, w_out_b, tm, folded_y)
        return _ffn(x2, *ffn2, tm_ffn)

    tm = 512
    tm_ffn = 1024
    xp = x_prompt.reshape(batch * seq, D_MODEL)
    x1, u, k_tail, v_tail, ga, gb, u_folded, *qkv = front(xp, tm_ffn, tm, seq)
    y, h_fin = _ssm_prompt(u_folded, toep, win, wout, a8, batch, seq, tile=1024)
    o_b = _attn_prompt(qkv, batch, seq)
    yp = back(x1, y, u, o_b, ga, gb, tm_ffn, tm, folded_y=True)
    yp = yp.reshape(batch, seq, D_MODEL)
    h_fin = h_fin.reshape(batch, 2, SSM_GROUPS, SSM_STATE)
    p_re = h_fin[None, :, 0]
    p_im = h_fin[None, :, 1]
    p_kv = [c.transpose(0, 4, 1, 2, 3)[None] for c in _kv_windows(k_tail, v_tail)]

    xs = x_sample.reshape(dec_batch, D_MODEL)
    x1, u, q, k, v, ga, gb = front(xs, dec_batch, dec_batch)
    width = SSM_GROUPS * SSM_STATE
    y, s_re, s_im = _ssm_step(u, state_ssm_re[layer].reshape(dec_batch, width),
                              state_ssm_im[layer].reshape(dec_batch, width),
                              apow[:, 1], bbar, ssm_c_re[layer], ssm_c_im[layer])
    caches = [c[layer].transpose(0, 2, 3, 4, 1) for c in (cache_kv_w128, cache_kv_w512, cache_kv_w2048)]
    o_b, new_caches = _attn_sample(q, k, v, caches)
    o_b = o_b.reshape(dec_batch, SLOT_WIDTH // LANES, LANES).transpose(1, 0, 2)
    ys = back(x1, y, u, o_b, ga, gb, dec_batch, dec_batch, folded_y=False)
    ys = ys.reshape(dec_batch, 1, D_MODEL)
    s_re = s_re.reshape(1, dec_batch, SSM_GROUPS, SSM_STATE)
    s_im = s_im.reshape(1, dec_batch, SSM_GROUPS, SSM_STATE)
    s_kv = [c.transpose(0, 4, 1, 2, 3)[None] for c in new_caches]

    return (yp, ys, p_re, p_im, p_kv[0], p_kv[1], p_kv[2], s_re, s_im, s_kv[0], s_kv[1], s_kv[2])
```

```python
import functools

import jax
import jax.numpy as jnp
from jax import lax
from jax.experimental import pallas as pl
from jax.experimental.pallas import tpu as pltpu

F32 = jnp.float32
BF16 = jnp.bfloat16

D_MODEL = 1024
SSM_GROUP = 16
SSM_GROUPS = 64
SSM_STATE = 64
SSM_PAIRS = SSM_GROUPS // 2
HEAD_DIM = 64
HEADS = 4
DIL_WINDOWS = (128, 512, 2048)
DIL_RATES = (1, 4, 16)
N_DIL = 3
KEYS_BACK = 128
ATTN_WIDTH = N_DIL * HEADS * HEAD_DIM
SLOT_WIDTH = HEADS * HEAD_DIM
D_FF = 2816
RMS_EPS = 1e-6
ALIBI_MAX_EXP = 8.0
IN_WIDTH = D_MODEL + 3 * ATTN_WIDTH + 2 * D_MODEL

LANES = 128
SUBLANES = 8
MXU_DIM = 256
VMEM_LIMIT_BYTES = 56 * 1024 * 1024

LANE_BLOCKS = D_MODEL // LANES

SSM_CHUNK = SUBLANES
SSM_ROW_PITCH = 40
FF_CHUNKS = ((0, 1024), (1024, 2048), (2048, 2816))
MIX_OUT_PIECES = 2


def _slope(group, head):
    return 2.0 ** (-ALIBI_MAX_EXP * (group * HEADS + head + 1) / (N_DIL * HEADS))


def _const_spec(shape):
    zeros = (0,) * len(shape)
    return pl.BlockSpec(shape, lambda *_: zeros, pipeline_mode=pl.Buffered(1))


def _lane_blocked_spec(rows, row_block):
    return pl.BlockSpec((LANE_BLOCKS, rows, LANES), lambda *idx: (0, row_block(*idx), 0))


def _params(*semantics):
    return pltpu.CompilerParams(dimension_semantics=semantics, vmem_limit_bytes=VMEM_LIMIT_BYTES)


def _rms(x, w):
    return x * lax.rsqrt(jnp.mean(x * x, axis=-1, keepdims=True) + RMS_EPS) * w


def _split_bf16(x, terms):
    parts = []
    for _ in range(terms):
        p = x.astype(BF16)
        parts.append(p)
        x = x - p.astype(F32)
    return parts


def _ffn_body(x_ref, nw_ref, wg_ref, wu_ref, wd_ref, o_ref):
    x = x_ref[...]
    h = _rms(x, nw_ref[...]).astype(BF16)
    acc = jnp.zeros_like(x)
    for lo, hi in FF_CHUNKS:
        g = jnp.dot(h, wg_ref[:, lo:hi], preferred_element_type=F32)
        u = jnp.dot(h, wu_ref[:, lo:hi], preferred_element_type=F32)
        a = (jax.nn.silu(g) * u).astype(BF16)
        acc = acc + jnp.dot(a, wd_ref[lo:hi, :], preferred_element_type=F32)
    o_ref[...] = x + 0.5 * acc


def _ffn(x, norm_w, wg, wu, wd, tm):
    t = x.shape[0]
    row = pl.BlockSpec((tm, D_MODEL), lambda i: (i, 0))
    return pl.pallas_call(
        _ffn_body,
        grid=(t // tm,),
        in_specs=[row, _const_spec((1, D_MODEL)), _const_spec((D_MODEL, D_FF)),
                  _const_spec((D_MODEL, D_FF)), _const_spec((D_FF, D_MODEL))],
        out_specs=row,
        out_shape=jax.ShapeDtypeStruct((t, D_MODEL), F32),
        compiler_params=_params("parallel"),
        name="ffn",
    )(x, norm_w, wg, wu, wd)


def _head_norm(x, gain, seg_ones):
    parts = _split_bf16(x * x, 2)
    blocks = []
    for lo in range(0, x.shape[1], MXU_DIM):
        blocks.append(sum(jnp.dot(p[:, lo:lo + MXU_DIM], seg_ones, preferred_element_type=F32) for p in parts))
    ss = jnp.concatenate(blocks, axis=1)
    return x * lax.rsqrt(ss * (1.0 / HEAD_DIM) + RMS_EPS) * gain


def _transpose_granules(xs):
    xs = list(xs)
    n = len(xs)
    block = lax.broadcasted_iota(jnp.int32, (1, LANES), 1) // SSM_GROUP
    bit = n // 2
    while bit:
        upper = (block & bit) != 0
        shift = SSM_GROUP * bit
        for lo in range(n):
            if lo & bit:
                continue
            hi = lo + bit
            x_lo, x_hi = xs[lo], xs[hi]
            xs[lo] = jnp.where(upper, pltpu.roll(x_hi, shift, 1), x_lo)
            xs[hi] = jnp.where(upper, x_hi, pltpu.roll(x_lo, LANES - shift, 1))
        bit //= 2
    return xs


def _mix_in_body(x_ref, nw_ref, w_ref, qg_ref, kg_ref, seg_ref, u_ref, *rest, prompt):
    h = _rms(x_ref[...], nw_ref[...]).astype(BF16)

    def proj(lo, hi):
        return jnp.dot(h, w_ref[:, lo:hi], preferred_element_type=F32)

    c0 = D_MODEL
    c1 = c0 + ATTN_WIDTH
    c2 = c1 + ATTN_WIDTH
    c3 = c2 + ATTN_WIDTH
    c4 = c3 + D_MODEL
    u = proj(0, c0)
    for blk in range(LANE_BLOCKS):
        u_ref[blk] = u[:, blk * LANES:(blk + 1) * LANES]
    seg_ones = seg_ref[...]
    q = _head_norm(proj(c0, c1), qg_ref[...], seg_ones) * (HEAD_DIM ** -0.5)
    k = _head_norm(proj(c1, c2), kg_ref[...], seg_ones)
    v = proj(c2, c3)
    if not prompt:
        q_ref, k_ref, v_ref, ga_ref, gb_ref = rest
        q_ref[...] = q
    else:
        k_ref, v_ref, ga_ref, gb_ref, fold_ref = rest[:5]
        by_residue = rest[5:5 + 3 * N_DIL]
        stage_ref = rest[-1]
    k_ref[...] = k
    v_ref[...] = v
    ga_ref[...] = jax.nn.sigmoid(proj(c3, c4))
    gb_ref[...] = jax.nn.sigmoid(proj(c4, IN_WIDTH))
    if not prompt:
        return

    rows = x_ref.shape[0]
    groups_per_block = LANES // SSM_GROUP
    for blk in range(LANE_BLOCKS):
        tokens = [u_ref[blk, pl.ds(s, rows // SSM_CHUNK, stride=SSM_CHUNK), :] for s in range(SSM_CHUNK)]
        for q_i, folded in enumerate(_transpose_granules(tokens)):
            pair = blk * (groups_per_block // 2) + q_i // 2
            fold_ref[pair, :, (q_i % 2) * LANES:(q_i % 2 + 1) * LANES] = folded.astype(BF16)

    slabs = SLOT_WIDTH // LANES
    for ti, x in enumerate((q, k, v)):
        outs = by_residue[ti * N_DIL:(ti + 1) * N_DIL]
        outs[0][0] = x[:, :SLOT_WIDTH].astype(BF16)
        for g in range(1, N_DIL):
            d = DIL_RATES[g]
            for s in range(slabs):
                lo = g * SLOT_WIDTH + s * LANES
                stage_ref[ti, (g - 1) * slabs + s] = x[:, lo:lo + LANES]
            for r in range(d):
                for s in range(slabs):
                    piece = stage_ref[ti, (g - 1) * slabs + s, pl.ds(r, rows // d, stride=d), :]
                    outs[g][r, :, s * LANES:(s + 1) * LANES] = piece.astype(BF16)


def _mix_in(x, norm_w, w_in, q_gain, k_gain, seg_ones, tm, seq=None):
    t = x.shape[0]
    prompt = seq is not None

    def row(width):
        return pl.BlockSpec((tm, width), lambda i: (i, 0))

    def rows_f32(width):
        return jax.ShapeDtypeStruct((t, width), F32)

    out_specs = [_lane_blocked_spec(tm, lambda i: i)]
    out_shape = [jax.ShapeDtypeStruct((LANE_BLOCKS, t, LANES), F32)]
    scratch = []
    if not prompt:
        out_specs += [row(ATTN_WIDTH)] * 3 + [row(D_MODEL)] * 2
        out_shape += [rows_f32(ATTN_WIDTH)] * 3 + [rows_f32(D_MODEL)] * 2
    else:
        tiles = seq // tm
        tail = min(DIL_WINDOWS[-1], seq)
        skip = tiles - tail // tm
        tail_spec = pl.BlockSpec((None, tm, ATTN_WIDTH), lambda i: (i // tiles, jnp.maximum(i % tiles - skip, 0), 0))
        out_specs += [tail_spec] * 2 + [row(D_MODEL)] * 2
        out_shape += [jax.ShapeDtypeStruct((t // seq, tail, ATTN_WIDTH), F32)] * 2 + [rows_f32(D_MODEL)] * 2
        out_specs.append(pl.BlockSpec((SSM_PAIRS, tm // SSM_CHUNK, MXU_DIM), lambda i: (0, i, 0)))
        out_shape.append(jax.ShapeDtypeStruct((SSM_PAIRS, t // SSM_CHUNK, MXU_DIM), BF16))
        for _ in range(3):
            for d in DIL_RATES:
                out_specs.append(pl.BlockSpec((None, d, tm // d, SLOT_WIDTH),
                                              lambda i: (i // tiles, 0, i % tiles, 0)))
                out_shape.append(jax.ShapeDtypeStruct((t // seq, d, seq // d, SLOT_WIDTH), BF16))
        scratch = [pltpu.VMEM((3, (N_DIL - 1) * SLOT_WIDTH // LANES, tm, LANES), F32)]
    return pl.pallas_call(
        functools.partial(_mix_in_body, prompt=prompt),
        grid=(t // tm,),
        in_specs=[row(D_MODEL), _const_spec((1, D_MODEL)), _const_spec((D_MODEL, IN_WIDTH)),
                  _const_spec((1, ATTN_WIDTH)), _const_spec((1, ATTN_WIDTH)),
                  _const_spec((MXU_DIM, MXU_DIM))],
        out_specs=out_specs,
        out_shape=out_shape,
        scratch_shapes=scratch,
        compiler_params=_params("arbitrary"),
        name="mix_in",
    )(x, norm_w, w_in, q_gain, k_gain, seg_ones)


def _cmul(ar, ai, br, bi):
    return ar * br - ai * bi, ar * bi + ai * br


def _ssm_prep_body(lrow_re_ref, lrow_im_ref, lcol_re_ref, lcol_im_ref, ldt_ref, bre_ref, bim_ref,
                   ct_re_ref, ct_im_ref, abar_ref, a8_ref, bbar_ref, toep_ref, win_ref, wout_ref):
    n = SSM_CHUNK
    dt = jnp.exp(ldt_ref[...])

    def discretise(lam_re, lam_im):
        lr = jnp.minimum(lam_re, -1e-4)
        mag = jnp.exp(lr * dt)
        return lr, lam_im, mag * jnp.cos(lam_im * dt), mag * jnp.sin(lam_im * dt)

    def powers(ar, ai):
        out = [(jnp.ones_like(ar), jnp.zeros_like(ai))]
        for _ in range(n):
            out.append(_cmul(*out[-1], ar, ai))
        return out

    lr, li, ar, ai = discretise(lrow_re_ref[...], lrow_im_ref[...])
    den = lr * lr + li * li
    fr = ((ar - 1.0) * lr + ai * li) / den
    fi = (ai * lr - (ar - 1.0) * li) / den
    bbr, bbi = _cmul(fr, fi, bre_ref[...], bim_ref[...])
    row_pow = powers(ar, ai)
    abar_ref[0], abar_ref[1] = ar, ai
    a8_ref[0], a8_ref[1] = row_pow[n]
    bbar_ref[0], bbar_ref[1] = bbr, bbi

    def pair_halves(x):
        x = x.reshape((x.shape[0] // 2, 2) + x.shape[1:])
        return x[:, 0], x[:, 1]

    toep_ref[...] = jnp.zeros_like(toep_ref)
    win_ref[...] = jnp.zeros_like(win_ref)
    wout_ref[...] = jnp.zeros_like(wout_ref)
    half = n * SSM_GROUP

    for s in range(n):
        w_re, w_im = _cmul(*row_pow[n - 1 - s], bbr, bbi)
        rows_e = slice(s * SSM_GROUP, (s + 1) * SSM_GROUP)
        rows_o = slice(half + s * SSM_GROUP, half + (s + 1) * SSM_GROUP)
        for part, x in enumerate((w_re, w_im)):
            x_e, x_o = pair_halves(x.astype(BF16))
            win_ref[:, rows_e, (2 * part) * SSM_STATE:(2 * part + 1) * SSM_STATE] = x_e
            win_ref[:, rows_o, (2 * part + 1) * SSM_STATE:(2 * part + 2) * SSM_STATE] = x_o

    _, _, ar_c, ai_c = discretise(lcol_re_ref[...], lcol_im_ref[...])
    col_pow = powers(ar_c, ai_c)
    lane_t = lax.broadcasted_iota(jnp.int32, (1, 1, LANES), 2) // SSM_GROUP

    def spread(first):
        re = im = jnp.zeros((1, 1, LANES), F32)
        for t in range(n):
            re = jnp.where(lane_t == t, col_pow[first + t][0], re)
            im = jnp.where(lane_t == t, col_pow[first + t][1], im)
        return re, im

    ct = (ct_re_ref[...], ct_im_ref[...])
    m0_re, m0_im = _cmul(*ct, *spread(0))
    m1_re, m1_im = _cmul(*ct, *spread(1))

    for part, x in enumerate((m1_re, -m1_im)):
        x_e, x_o = pair_halves(x.astype(BF16))
        wout_ref[:, (2 * part) * SSM_STATE:(2 * part + 1) * SSM_STATE, :half] = x_e
        wout_ref[:, (2 * part + 1) * SSM_STATE:(2 * part + 2) * SSM_STATE, half:] = x_o

    nn = (((2,), (1,)), ((0,), (0,)))
    hp = lax.Precision.HIGHEST
    kern = (lax.dot_general(bbr, m0_re, nn, precision=hp, preferred_element_type=F32)
            - lax.dot_general(bbi, m0_im, nn, precision=hp, preferred_element_type=F32))
    lane = lax.broadcasted_iota(jnp.int32, (1, 1, LANES), 2)
    for s in range(n):
        shifted = kern if s == 0 else jnp.where(lane >= s * SSM_GROUP, pltpu.roll(kern, s * SSM_GROUP, 2), 0.0)
        x_e, x_o = pair_halves(shifted.astype(BF16))
        toep_ref[:, s * SSM_GROUP:(s + 1) * SSM_GROUP, :half] = x_e
        toep_ref[:, half + s * SSM_GROUP:half + (s + 1) * SSM_GROUP, half:] = x_o


def _ssm_prep(lam_re, lam_im, log_dt, b_re_t, b_im_t, c_re, c_im, groups_per_step=16):
    g, p, c = SSM_GROUPS, SSM_STATE, SSM_GROUP
    gb = groups_per_step

    def spec(*tail):
        return pl.BlockSpec((gb,) + tail, lambda i: (i,) + (0,) * len(tail))

    def stacked(*tail):
        return pl.BlockSpec((2, gb) + tail, lambda i: (0, i) + (0,) * len(tail))

    pair_spec = pl.BlockSpec((gb // 2, MXU_DIM, MXU_DIM), lambda i: (i, 0, 0))
    pair_shape = jax.ShapeDtypeStruct((SSM_PAIRS, MXU_DIM, MXU_DIM), BF16)
    ct_re = jnp.tile(c_re.transpose(0, 2, 1), (1, 1, SSM_CHUNK))
    ct_im = jnp.tile(c_im.transpose(0, 2, 1), (1, 1, SSM_CHUNK))
    return pl.pallas_call(
        _ssm_prep_body,
        grid=(g // gb,),
        in_specs=[spec(1, p), spec(1, p), spec(p, 1), spec(p, 1), spec(1, 1), spec(c, p), spec(c, p),
                  spec(p, LANES), spec(p, LANES)],
        out_specs=[stacked(1, p), stacked(1, p), stacked(c, p), pair_spec, pair_spec, pair_spec],
        out_shape=[jax.ShapeDtypeStruct((2, g, 1, p), F32), jax.ShapeDtypeStruct((2, g, 1, p), F32),
                   jax.ShapeDtypeStruct((2, g, c, p), F32), pair_shape, pair_shape, pair_shape],
        compiler_params=_params("parallel"),
        name="ssm_prep",
    )(lam_re.reshape(g, 1, p), lam_im.reshape(g, 1, p), lam_re.reshape(g, p, 1), lam_im.reshape(g, p, 1),
      log_dt.reshape(g, 1, 1), b_re_t, b_im_t, ct_re, ct_im)


def _ssm_body(lhs_ref, toep_ref, win_ref, wout_ref, a8_ref, yfl_ref, hfin_ref, st_ref, carry_ref, *, rows):
    i = pl.program_id(1)

    @pl.when(i == 0)
    def _():
        carry_ref[...] = jnp.zeros_like(carry_ref)

    def state_in(r, _):
        b = jnp.dot(lhs_ref[r], win_ref[r], preferred_element_type=F32)
        st_ref[0, pl.ds(r, rows, stride=SSM_ROW_PITCH), :] = b[:, :LANES]
        st_ref[1, pl.ds(r, rows, stride=SSM_ROW_PITCH), :] = b[:, LANES:]
        return 0

    lax.fori_loop(0, SSM_PAIRS, state_in, 0, unroll=4)

    a_re = a8_ref[0]
    a_im = a8_ref[1]

    def step(j, h):
        h_re, h_im = h
        base = pl.multiple_of(j * SSM_ROW_PITCH, SUBLANES)
        n_re = a_re * h_re - a_im * h_im + st_ref[0, pl.ds(base, SSM_PAIRS), :]
        n_im = a_re * h_im + a_im * h_re + st_ref[1, pl.ds(base, SSM_PAIRS), :]
        st_ref[0, pl.ds(base, SSM_PAIRS), :] = h_re
        st_ref[1, pl.ds(base, SSM_PAIRS), :] = h_im
        return n_re, n_im

    h_re, h_im = lax.fori_loop(0, rows, step, (carry_ref[0], carry_ref[1]), unroll=4)
    carry_ref[0] = h_re
    carry_ref[1] = h_im
    hfin_ref[0] = h_re
    hfin_ref[1] = h_im

    def chunk_out(r, _):
        hcat = jnp.concatenate([st_ref[0, pl.ds(r, rows, stride=SSM_ROW_PITCH), :],
                                st_ref[1, pl.ds(r, rows, stride=SSM_ROW_PITCH), :]], axis=1).astype(BF16)
        yfl_ref[r] = (jnp.dot(lhs_ref[r], toep_ref[r], preferred_element_type=F32)
                      + jnp.dot(hcat, wout_ref[r], preferred_element_type=F32))
        return 0

    lax.fori_loop(0, SSM_PAIRS, chunk_out, 0, unroll=4)


def _ssm_prompt(lhs, toep, win, wout, a8, batch, seq, tile):
    rows = tile // SSM_CHUNK
    n_tiles = seq // tile
    tok = pl.BlockSpec((SSM_PAIRS, rows, MXU_DIM), lambda b, i: (0, b * n_tiles + i, 0))
    pair_w = _const_spec((SSM_PAIRS, MXU_DIM, MXU_DIM))
    return pl.pallas_call(
        functools.partial(_ssm_body, rows=rows),
        grid=(batch, n_tiles),
        in_specs=[tok, pair_w, pair_w, pair_w, _const_spec((2, SSM_PAIRS, LANES))],
        out_specs=[tok, pl.BlockSpec((None, 2, SSM_PAIRS, LANES), lambda b, i: (b, 0, 0, 0))],
        out_shape=[jax.ShapeDtypeStruct((SSM_PAIRS, batch * seq // SSM_CHUNK, MXU_DIM), F32),
                   jax.ShapeDtypeStruct((batch, 2, SSM_PAIRS, LANES), F32)],
        scratch_shapes=[pltpu.VMEM((2, rows * SSM_ROW_PITCH, LANES), F32),
                        pltpu.VMEM((2, SSM_PAIRS, LANES), F32)],
        compiler_params=_params("parallel", "arbitrary"),
        name="ssm_prompt",
    )(lhs, toep, win, wout, a8)


def _ssm_step_body(u_ref, hre_ref, him_ref, abar_ref, bbar_ref, cre_ref, cim_ref,
                   y_ref, ore_ref, oim_ref):
    hp = lax.Precision.HIGHEST
    nt = (((1,), (1,)), ((), ()))
    groups_per_block = LANES // SSM_GROUP
    for g in range(SSM_GROUPS):
        blk = g // groups_per_block
        ch = slice((g % groups_per_block) * SSM_GROUP, (g % groups_per_block + 1) * SSM_GROUP)
        st = slice(g * SSM_STATE, (g + 1) * SSM_STATE)
        ug = u_ref[blk, :, ch]
        bu_re = jnp.dot(ug, bbar_ref[0, g], precision=hp, preferred_element_type=F32)
        bu_im = jnp.dot(ug, bbar_ref[1, g], precision=hp, preferred_element_type=F32)
        a_re = abar_ref[0, g]
        a_im = abar_ref[1, g]
        h_re = hre_ref[:, st]
        h_im = him_ref[:, st]
        n_re = a_re * h_re - a_im * h_im + bu_re
        n_im = a_re * h_im + a_im * h_re + bu_im
        ore_ref[:, st] = n_re
        oim_ref[:, st] = n_im
        y_ref[blk, :, ch] = (lax.dot_general(n_re, cre_ref[g], nt, precision=hp, preferred_element_type=F32)
                             - lax.dot_general(n_im, cim_ref[g], nt, precision=hp, preferred_element_type=F32))


def _ssm_step(u, h_re, h_im, abar, bbar, c_re, c_im):
    b = u.shape[1]
    width = SSM_GROUPS * SSM_STATE
    return pl.pallas_call(
        _ssm_step_body,
        out_shape=[jax.ShapeDtypeStruct((LANE_BLOCKS, b, LANES), F32),
                   jax.ShapeDtypeStruct((b, width), F32),
                   jax.ShapeDtypeStruct((b, width), F32)],
        compiler_params=pltpu.CompilerParams(vmem_limit_bytes=VMEM_LIMIT_BYTES),
        name="ssm_step",
    )(u, h_re, h_im, abar, bbar, c_re, c_im)


def _attn_prompt_body(*refs, seq):
    qkv = refs[:3 * N_DIL]
    o_ref, m_ref, l_ref, acc_ref, bias_ref = refs[3 * N_DIL:]
    tq = KEYS_BACK
    row = lax.broadcasted_iota(jnp.int32, (tq, 2 * tq), 0)
    col = lax.broadcasted_iota(jnp.int32, (tq, 2 * tq), 1)
    back = row + tq - col
    in_window = (back >= 0) & (back <= KEYS_BACK)
    first_head = lax.broadcasted_iota(jnp.int32, (1, LANES), 1) < HEAD_DIM
    nt = (((1,), (1,)), ((), ()))

    order = tuple(reversed(range(N_DIL)))
    ones = jnp.ones((2 * tq, LANES), BF16)
    for g in order:
        q_ref, k_ref, v_ref = qkv[g], qkv[N_DIL + g], qkv[2 * N_DIL + g]
        d = DIL_RATES[g]
        blocks_per_residue = seq // d // tq
        dist = (back * d).astype(F32)
        for h in range(HEADS):
            bias = jnp.where(in_window, -_slope(g, h) * dist, -jnp.inf)
            bias_ref[h] = bias
            bias_ref[HEADS + h] = jnp.where(col >= tq, bias, -jnp.inf)

        def block(mi, _, g=g, d=d, q_ref=q_ref, k_ref=k_ref, v_ref=v_ref, blocks_per_residue=blocks_per_residue):
            residue = mi // blocks_per_residue
            n = mi % blocks_per_residue
            cur = pl.multiple_of(mi * tq, tq)
            prev = pl.multiple_of(jnp.maximum(mi - 1, 0) * tq, tq)
            bias_at = jnp.where(n == 0, HEADS, 0)
            token0 = residue + d * tq * n
            rows = pl.ds(pl.multiple_of(token0, tq), tq) if d == 1 else pl.ds(token0, tq, stride=d)
            pairs = range(HEADS // 2)
            first, final = g == order[0], g == order[-1]
            old = None if first else [(m_ref[pair, rows, :], l_ref[pair, rows, :], acc_ref[pair, rows, :])
                                      for pair in pairs]
            new = []
            for pair in pairs:
                lanes = slice(pair * LANES, (pair + 1) * LANES)
                qp = q_ref[pl.ds(cur, tq), lanes]
                kp = jnp.concatenate([k_ref[pl.ds(prev, tq), lanes], k_ref[pl.ds(cur, tq), lanes]], axis=0)
                vp = jnp.concatenate([v_ref[pl.ds(prev, tq), lanes], v_ref[pl.ds(cur, tq), lanes]], axis=0)
                vp = jnp.concatenate([vp, ones], axis=1)
                stats = []
                for e in range(2):
                    qm = jnp.where(first_head if e == 0 else ~first_head, qp, jnp.zeros_like(qp))
                    s = lax.dot_general(qm, kp, nt, preferred_element_type=F32)
                    s = s + bias_ref[bias_at + 2 * pair + e]
                    m = jnp.max(s, axis=-1, keepdims=True)
                    p = jnp.exp(s - m)
                    pv = jnp.dot(p.astype(BF16), vp, preferred_element_type=F32)
                    stats.append((m, pv[:, LANES:], pv[:, :LANES]))
                m_new, l_new, acc_new = (jnp.where(first_head, a, b) for a, b in zip(*stats))
                if not first:
                    m_old, l_old, acc_old = old[pair]
                    m_tot = jnp.maximum(m_old, m_new)
                    w_old = jnp.exp(m_old - m_tot)
                    w_new = jnp.exp(m_new - m_tot)
                    l_new = w_old * l_old + w_new * l_new
                    acc_new = w_old * acc_old + w_new * acc_new
                    m_new = m_tot
                new.append((m_new, l_new, acc_new))
            for pair, (m_new, l_new, acc_new) in zip(pairs, new):
                if not final:
                    m_ref[pair, rows, :] = m_new
                    l_ref[pair, rows, :] = l_new
                    acc_ref[pair, rows, :] = acc_new
                else:
                    o_ref[pair, rows, :] = acc_new / l_new
            return 0

        lax.fori_loop(0, seq // tq, block, 0, unroll=2)


def _attn_prompt(qkv, batch, seq):
    slabs = SLOT_WIDTH // LANES
    seq_spec = pl.BlockSpec((None, seq, SLOT_WIDTH), lambda b: (b, 0, 0), pipeline_mode=pl.Buffered(1))
    running = pltpu.VMEM((slabs, seq, LANES), F32)
    return pl.pallas_call(
        functools.partial(_attn_prompt_body, seq=seq),
        grid=(batch,),
        in_specs=[seq_spec] * len(qkv),
        out_specs=pl.BlockSpec((slabs, seq, LANES), lambda b: (0, b, 0)),
        out_shape=jax.ShapeDtypeStruct((slabs, batch * seq, LANES), F32),
        scratch_shapes=[running, running, running, pltpu.VMEM((2 * HEADS, KEYS_BACK, 2 * KEYS_BACK), F32)],
        compiler_params=_params("parallel"),
        name="attn_prompt",
    )(*[a.reshape(batch, seq, SLOT_WIDTH) for a in qkv])


def _as_column(row_vec):
    n = row_vec.shape[1]
    eye = lax.broadcasted_iota(jnp.int32, (n, n), 0) == lax.broadcasted_iota(jnp.int32, (n, n), 1)
    return jnp.sum(jnp.where(eye, row_vec, 0.0), axis=1, keepdims=True)


def _attn_sample_body(q_ref, k_ref, v_ref, c0_ref, c1_ref, c2_ref, ob_ref, n0_ref, n1_ref, n2_ref):
    b = pl.program_id(0)
    q_row = q_ref[pl.ds(b, 1), :]
    k_row = k_ref[pl.ds(b, 1), :]
    v_row = v_ref[pl.ds(b, 1), :]
    head_row = lax.broadcasted_iota(jnp.int32, (SUBLANES, SLOT_WIDTH), 0)
    own_head = lax.broadcasted_iota(jnp.int32, (SUBLANES, SLOT_WIDTH), 1) // HEAD_DIM == head_row
    head_col = lax.broadcasted_iota(jnp.int32, (SUBLANES, 1), 0)
    outs, lses = [], []
    for g, (c_ref, n_ref) in enumerate(((c0_ref, n0_ref), (c1_ref, n1_ref), (c2_ref, n2_ref))):
        w = DIL_WINDOWS[g]
        d = DIL_RATES[g]
        cols = slice(g * SLOT_WIDTH, (g + 1) * SLOT_WIDTH)
        q_g, k_new, v_new = q_row[:, cols], k_row[:, cols], v_row[:, cols]
        kt = c_ref[0].reshape(SLOT_WIDTH, w)
        vt = c_ref[1].reshape(SLOT_WIDTH, w)
        q_heads = jnp.where(own_head, q_g, 0.0)
        s = jnp.dot(q_heads.astype(BF16), kt.astype(BF16), preferred_element_type=F32)
        dist = w - lax.broadcasted_iota(jnp.int32, (1, w), 1)
        slope = functools.reduce(lambda acc, h: jnp.where(head_col == h, _slope(g, h), acc), range(HEADS), 0.0)
        s = jnp.where((dist & (d - 1)) == 0, s - slope * dist.astype(F32), -jnp.inf)
        s_new = jnp.sum(q_heads * k_new, axis=1, keepdims=True)
        m = jnp.maximum(jnp.max(s, axis=1, keepdims=True), s_new)
        p = jnp.exp(s - m)
        p_new = jnp.exp(s_new - m)
        den = jnp.sum(p, axis=1, keepdims=True) + p_new
        pv = lax.dot_general(p.astype(BF16), vt.astype(BF16), (((1,), (1,)), ((), ())),
                             preferred_element_type=F32)
        o_heads = (pv + p_new * v_new) / den
        outs.append(jnp.sum(jnp.where(own_head, o_heads, 0.0), axis=0, keepdims=True))
        lses.append(jnp.sum(jnp.where(own_head, m + jnp.log(den), 0.0), axis=0, keepdims=True))
        last = lax.broadcasted_iota(jnp.int32, (SLOT_WIDTH, w), 1) == w - 1
        shape = (HEADS, HEAD_DIM, w)
        n_ref[0] = jnp.where(last, _as_column(k_new), pltpu.roll(kt, w - 1, 1)).reshape(shape)
        n_ref[1] = jnp.where(last, _as_column(v_new), pltpu.roll(vt, w - 1, 1)).reshape(shape)
    top = functools.reduce(jnp.maximum, lses)
    wts = [jnp.exp(l - top) for l in lses]
    ob_ref[...] = sum(w_g * o_g for w_g, o_g in zip(wts, outs)) / sum(wts)


def _attn_sample(q, k, v, caches):
    b = q.shape[0]
    full = _const_spec((b, ATTN_WIDTH))
    win = [pl.BlockSpec((None, 2, HEADS, HEAD_DIM, w), lambda i: (i, 0, 0, 0, 0)) for w in DIL_WINDOWS]
    outs = pl.pallas_call(
        _attn_sample_body,
        grid=(b,),
        in_specs=[full, full, full] + win,
        out_specs=[pl.BlockSpec((None, 1, SLOT_WIDTH), lambda i: (i, 0, 0))] + win,
        out_shape=([jax.ShapeDtypeStruct((b, 1, SLOT_WIDTH), F32)]
                   + [jax.ShapeDtypeStruct(c.shape, F32) for c in caches]),
        compiler_params=_params("parallel"),
        name="attn_sample",
    )(q, k, v, *caches)
    return outs[0].reshape(b, SLOT_WIDTH), outs[1:]


def _kv_window_body(k_ref, v_ref, *out_refs):
    tail = k_ref.shape[0]
    for g, o_ref in enumerate(out_refs):
        w = o_ref.shape[-1]
        for t, x_ref in enumerate((k_ref, v_ref)):
            for pair in range(HEADS // 2):
                lo = g * SLOT_WIDTH + pair * LANES
                xt = x_ref[tail - w:tail, lo:lo + LANES].T
                o_ref[t, 2 * pair] = xt[:HEAD_DIM]
                o_ref[t, 2 * pair + 1] = xt[HEAD_DIM:]


def _kv_windows(k_tail, v_tail):
    batch, tail, _ = k_tail.shape
    keeps = [min(w, tail) for w in DIL_WINDOWS]
    src = pl.BlockSpec((None, tail, ATTN_WIDTH), lambda b: (b, 0, 0))
    return pl.pallas_call(
        _kv_window_body,
        grid=(batch,),
        in_specs=[src, src],
        out_specs=[pl.BlockSpec((None, 2, HEADS, HEAD_DIM, w), lambda b: (b, 0, 0, 0, 0)) for w in keeps],
        out_shape=[jax.ShapeDtypeStruct((batch, 2, HEADS, HEAD_DIM, w), F32) for w in keeps],
        compiler_params=_params("parallel"),
        name="kv_windows",
    )(k_tail, v_tail)


def _mix_out_body(x_ref, y_ref, u_ref, d_ref, ob_ref,
                  ga_ref, gb_ref, wglu_ref, wpa_ref, wpb_ref, wout_ref, out_ref, *unfold):
    tm = x_ref.shape[0]
    if unfold:
        y_ref, folded_ref = unfold[0], y_ref
        groups_per_block = LANES // SSM_GROUP
    pieces = MIX_OUT_PIECES if unfold else 1
    rows = tm // pieces
    for piece in range(pieces):
        at = slice(piece * rows, (piece + 1) * rows)
        if unfold:
            chunks = rows // SSM_CHUNK
            chunk_at = slice(piece * chunks, (piece + 1) * chunks)
            for blk in range(LANE_BLOCKS):
                groups = [folded_ref[blk * (groups_per_block // 2) + q_i // 2, chunk_at,
                                     (q_i % 2) * LANES:(q_i % 2 + 1) * LANES] for q_i in range(groups_per_block)]
                for t, rows_t in enumerate(_transpose_granules(groups)):
                    y_ref[blk, pl.ds(piece * rows + t, chunks, stride=SSM_CHUNK), :] = rows_t
        y_raw = jnp.concatenate([y_ref[blk, at, :] for blk in range(LANE_BLOCKS)], axis=1)
        u = jnp.concatenate([u_ref[blk, at, :] for blk in range(LANE_BLOCKS)], axis=1)
        y = jax.nn.gelu(y_raw + d_ref[...] * u)
        yb = y.astype(BF16)
        y_a = y * jax.nn.sigmoid(jnp.dot(yb, wglu_ref[...], preferred_element_type=F32))
        branch_a = jnp.dot(y_a.astype(BF16), wpa_ref[...], preferred_element_type=F32)

        o_b = jnp.concatenate([ob_ref[s, at, :] for s in range(SLOT_WIDTH // LANES)], axis=1)
        branch_b = jnp.dot(o_b.astype(BF16), wpb_ref[...], preferred_element_type=F32)

        merged = ga_ref[at, :] * branch_a + gb_ref[at, :] * branch_b
        out_ref[at, :] = x_ref[at, :] + jnp.dot(merged.astype(BF16), wout_ref[...], preferred_element_type=F32)


def _mix_out(x, y, u, ssm_d, o_b, ga, gb, w_glu, w_pa, w_pb, w_out, tm, folded_y):
    t = x.shape[0]
    wide = pl.BlockSpec((tm, D_MODEL), lambda i: (i, 0))
    blocked = _lane_blocked_spec(tm, lambda i: i)
    y_spec = pl.BlockSpec((SSM_PAIRS, tm // SSM_CHUNK, MXU_DIM), lambda i: (0, i, 0)) if folded_y else blocked
    scratch = [pltpu.VMEM((LANE_BLOCKS, tm, LANES), F32)] if folded_y else []
    slot = pl.BlockSpec((SLOT_WIDTH // LANES, tm, LANES), lambda i: (0, i, 0))
    return pl.pallas_call(
        _mix_out_body,
        grid=(t // tm,),
        in_specs=[wide, y_spec, blocked, _const_spec((1, D_MODEL)), slot, wide, wide,
                  _const_spec((D_MODEL, D_MODEL)), _const_spec((D_MODEL, D_MODEL)),
                  _const_spec((SLOT_WIDTH, D_MODEL)), _const_spec((D_MODEL, D_MODEL))],
        out_specs=wide,
        out_shape=jax.ShapeDtypeStruct((t, D_MODEL), F32),
        scratch_shapes=scratch,
        compiler_params=_params("parallel"),
        name="mix_out",
    )(x, y, u, ssm_d, o_b, ga, gb, w_glu, w_pa, w_pb, w_out)


def _head_segment_ones():
    head = jnp.arange(MXU_DIM) // HEAD_DIM
    return (head[:, None] == head[None, :]).astype(BF16)


def kernel(x_prompt, x_sample, state_ssm_re, state_ssm_im, cache_kv_w128, cache_kv_w512, cache_kv_w2048, ffn1_norm, ffn1_w_gate, ffn1_w_up, ffn1_w_down, mix_norm, w_in, ssm_lambda_re, ssm_lambda_im, ssm_b_re, ssm_b_im, ssm_c_re, ssm_c_im, ssm_d, ssm_log_dt, w_glu, q_gain, k_gain, w_proj_a, w_proj_b, w_out, ffn2_norm, ffn2_w_gate, ffn2_w_up, ffn2_w_down):
    depth = ffn1_norm.shape[0]
    assert depth == 1, "single-layer step"
    batch, seq, _ = x_prompt.shape
    dec_batch, dec_seq, _ = x_sample.shape
    assert dec_seq == 1 and seq % (DIL_RATES[-1] * KEYS_BACK) == 0
    layer = 0
    bf = lambda w: w[layer].astype(BF16)
    vec = lambda w: w[layer][None]
    ffn1 = (vec(ffn1_norm), bf(ffn1_w_gate), bf(ffn1_w_up), bf(ffn1_w_down))
    ffn2 = (vec(ffn2_norm), bf(ffn2_w_gate), bf(ffn2_w_up), bf(ffn2_w_down))
    mix_norm, ssm_d = vec(mix_norm), vec(ssm_d)
    w_in_b, w_glu_b, w_pa_b, w_pb_b, w_out_b = bf(w_in), bf(w_glu), bf(w_proj_a), bf(w_proj_b), bf(w_out)
    q_gain_t = jnp.tile(vec(q_gain), (1, N_DIL * HEADS))
    k_gain_t = jnp.tile(vec(k_gain), (1, N_DIL * HEADS))
    seg_ones = _head_segment_ones()

    abar, a8, bbar, toep, win, wout = _ssm_prep(
        ssm_lambda_re[layer], ssm_lambda_im[layer], ssm_log_dt[layer],
        ssm_b_re[layer].transpose(0, 2, 1), ssm_b_im[layer].transpose(0, 2, 1),
        ssm_c_re[layer], ssm_c_im[layer])
    a8 = a8.reshape(2, SSM_PAIRS, LANES)

    def front(x, tm_ffn, tm, seq=None):
        x1 = _ffn(x, *ffn1, tm_ffn)
        return (x1,) + tuple(_mix_in(x1, mix_norm, w_in_b, q_gain_t, k_gain_t, seg_ones, tm, seq))

    def back(x1, y, u, o_b, ga, gb, tm_ffn, tm, folded_y):
        x2 = _mix_out(x1, y, u, ssm_d, o_b, ga, gb, w_glu_b, w_pa_b, w_pb_b, w_out_b, tm, folded_y)
        return _ffn(x2, *ffn2, tm_ffn)

    tm = 512
    tm_ffn = 1024
    xp = x_prompt.reshape(batch * seq, D_MODEL)
    x1, u, k_tail, v_tail, ga, gb, u_folded, *qkv = front(xp, tm_ffn, tm, seq)
    y, h_fin = _ssm_prompt(u_folded, toep, win, wout, a8, batch, seq, tile=2048)
    o_b = _attn_prompt(qkv, batch, seq)
    yp = back(x1, y, u, o_b, ga, gb, tm_ffn, tm, folded_y=True)
    yp = yp.reshape(batch, seq, D_MODEL)
    h_fin = h_fin.reshape(batch, 2, SSM_GROUPS, SSM_STATE)
    p_re = h_fin[None, :, 0]
    p_im = h_fin[None, :, 1]
    p_kv = [c.transpose(0, 4, 1, 2, 3)[None] for c in _kv_windows(k_tail, v_tail)]

    xs = x_sample.reshape(dec_batch, D_MODEL)
    x1, u, q, k, v, ga, gb = front(xs, dec_batch, dec_batch)
    width = SSM_GROUPS * SSM_STATE
    y, s_re, s_im = _ssm_step(u, state_ssm_re[layer].reshape(dec_batch, width),
                              state_ssm_im[layer].reshape(dec_batch, width),
                              abar, bbar, ssm_c_re[layer], ssm_c_im[layer])
    caches = [c[layer].transpose(0, 2, 3, 4, 1) for c in (cache_kv_w128, cache_kv_w512, cache_kv_w2048)]
    o_b, new_caches = _attn_sample(q, k, v, caches)
    o_b = o_b.reshape(dec_batch, SLOT_WIDTH // LANES, LANES).transpose(1, 0, 2)
    ys = back(x1, y, u, o_b, ga, gb, dec_batch, dec_batch, folded_y=False)
    ys = ys.reshape(dec_batch, 1, D_MODEL)
    s_re = s_re.reshape(1, dec_batch, SSM_GROUPS, SSM_STATE)
    s_im = s_im.reshape(1, dec_batch, SSM_GROUPS, SSM_STATE)
    s_kv = [c.transpose(0, 4, 1, 2, 3)[None] for c in new_caches]

    return (yp, ys, p_re, p_im, p_kv[0], p_kv[1], p_kv[2], s_re, s_im, s_kv[0], s_kv[1], s_kv[2])
```

```python
import functools

import jax
import jax.numpy as jnp
from jax import lax
from jax.experimental import pallas as pl
from jax.experimental.pallas import tpu as pltpu

F32 = jnp.float32
BF16 = jnp.bfloat16

D_MODEL = 1024
SSM_GROUP = 16
SSM_GROUPS = 64
SSM_STATE = 64
SSM_PAIRS = SSM_GROUPS // 2
HEAD_DIM = 64
HEADS = 4
DIL_WINDOWS = (128, 512, 2048)
DIL_RATES = (1, 4, 16)
N_DIL = 3
KEYS_BACK = 128
ATTN_WIDTH = N_DIL * HEADS * HEAD_DIM
SLOT_WIDTH = HEADS * HEAD_DIM
D_FF = 2816
RMS_EPS = 1e-6
ALIBI_MAX_EXP = 8.0
IN_WIDTH = D_MODEL + 3 * ATTN_WIDTH + 2 * D_MODEL

LANES = 128
SUBLANES = 8
MXU_DIM = 256
VMEM_LIMIT_BYTES = 56 * 1024 * 1024

LANE_BLOCKS = D_MODEL // LANES

SSM_CHUNK = SUBLANES
SSM_ROW_PITCH = 40
FF_CHUNKS = ((0, 1024), (1024, 2048), (2048, 2816))
MIX_OUT_PIECES = 2


def _slope(group, head):
    return 2.0 ** (-ALIBI_MAX_EXP * (group * HEADS + head + 1) / (N_DIL * HEADS))


def _const_spec(shape):
    zeros = (0,) * len(shape)
    return pl.BlockSpec(shape, lambda *_: zeros, pipeline_mode=pl.Buffered(1))


def _lane_blocked_spec(rows, row_block):
    return pl.BlockSpec((LANE_BLOCKS, rows, LANES), lambda *idx: (0, row_block(*idx), 0))


def _params(*semantics):
    return pltpu.CompilerParams(dimension_semantics=semantics, vmem_limit_bytes=VMEM_LIMIT_BYTES)


def _rms(x, w):
    return x * lax.rsqrt(jnp.mean(x * x, axis=-1, keepdims=True) + RMS_EPS) * w


def _split_bf16(x, terms):
    parts = []
    for _ in range(terms):
        p = x.astype(BF16)
        parts.append(p)
        x = x - p.astype(F32)
    return parts


def _ffn_body(x_ref, nw_ref, wg_ref, wu_ref, wd_ref, o_ref):
    x = x_ref[...]
    h = _rms(x, nw_ref[...]).astype(BF16)
    acc = jnp.zeros_like(x)
    for lo, hi in FF_CHUNKS:
        g = jnp.dot(h, wg_ref[:, lo:hi], preferred_element_type=F32)
        u = jnp.dot(h, wu_ref[:, lo:hi], preferred_element_type=F32)
        a = (jax.nn.silu(g) * u).astype(BF16)
        acc = acc + jnp.dot(a, wd_ref[lo:hi, :], preferred_element_type=F32)
    o_ref[...] = x + 0.5 * acc


def _ffn(x, norm_w, wg, wu, wd, tm):
    t = x.shape[0]
    row = pl.BlockSpec((tm, D_MODEL), lambda i: (i, 0))
    return pl.pallas_call(
        _ffn_body,
        grid=(t // tm,),
        in_specs=[row, _const_spec((1, D_MODEL)), _const_spec((D_MODEL, D_FF)),
                  _const_spec((D_MODEL, D_FF)), _const_spec((D_FF, D_MODEL))],
        out_specs=row,
        out_shape=jax.ShapeDtypeStruct((t, D_MODEL), F32),
        compiler_params=_params("parallel"),
        name="ffn",
    )(x, norm_w, wg, wu, wd)


def _head_norm(x, gain, seg_ones):
    parts = _split_bf16(x * x, 2)
    blocks = []
    for lo in range(0, x.shape[1], MXU_DIM):
        blocks.append(sum(jnp.dot(p[:, lo:lo + MXU_DIM], seg_ones, preferred_element_type=F32) for p in parts))
    ss = jnp.concatenate(blocks, axis=1)
    return x * lax.rsqrt(ss * (1.0 / HEAD_DIM) + RMS_EPS) * gain


def _transpose_granules(xs):
    xs = list(xs)
    n = len(xs)
    block = lax.broadcasted_iota(jnp.int32, (1, LANES), 1) // SSM_GROUP
    bit = n // 2
    while bit:
        upper = (block & bit) != 0
        shift = SSM_GROUP * bit
        for lo in range(n):
            if lo & bit:
                continue
            hi = lo + bit
            x_lo, x_hi = xs[lo], xs[hi]
            xs[lo] = jnp.where(upper, pltpu.roll(x_hi, shift, 1), x_lo)
            xs[hi] = jnp.where(upper, x_hi, pltpu.roll(x_lo, LANES - shift, 1))
        bit //= 2
    return xs


def _mix_in_body(x_ref, nw_ref, w_ref, qg_ref, kg_ref, seg_ref, u_ref, *rest, prompt):
    h = _rms(x_ref[...], nw_ref[...]).astype(BF16)

    def proj(lo, hi):
        return jnp.dot(h, w_ref[:, lo:hi], preferred_element_type=F32)

    c0 = D_MODEL
    c1 = c0 + ATTN_WIDTH
    c2 = c1 + ATTN_WIDTH
    c3 = c2 + ATTN_WIDTH
    c4 = c3 + D_MODEL
    u = proj(0, c0)
    for blk in range(LANE_BLOCKS):
        u_ref[blk] = u[:, blk * LANES:(blk + 1) * LANES]
    seg_ones = seg_ref[...]
    q = _head_norm(proj(c0, c1), qg_ref[...], seg_ones) * (HEAD_DIM ** -0.5)
    k = _head_norm(proj(c1, c2), kg_ref[...], seg_ones)
    v = proj(c2, c3)
    if not prompt:
        q_ref, k_ref, v_ref, ga_ref, gb_ref = rest
        q_ref[...] = q
    else:
        k_ref, v_ref, ga_ref, gb_ref, fold_ref = rest[:5]
        by_residue = rest[5:5 + 3 * N_DIL]
        stage_ref = rest[-1]
    k_ref[...] = k
    v_ref[...] = v
    ga_ref[...] = jax.nn.sigmoid(proj(c3, c4))
    gb_ref[...] = jax.nn.sigmoid(proj(c4, IN_WIDTH))
    if not prompt:
        return

    rows = x_ref.shape[0]
    groups_per_block = LANES // SSM_GROUP
    for blk in range(LANE_BLOCKS):
        tokens = [u_ref[blk, pl.ds(s, rows // SSM_CHUNK, stride=SSM_CHUNK), :] for s in range(SSM_CHUNK)]
        for q_i, folded in enumerate(_transpose_granules(tokens)):
            pair = blk * (groups_per_block // 2) + q_i // 2
            fold_ref[pair, :, (q_i % 2) * LANES:(q_i % 2 + 1) * LANES] = folded.astype(BF16)

    slabs = SLOT_WIDTH // LANES
    for ti, x in enumerate((q, k, v)):
        outs = by_residue[ti * N_DIL:(ti + 1) * N_DIL]
        outs[0][0] = x[:, :SLOT_WIDTH].astype(BF16)
        for g in range(1, N_DIL):
            d = DIL_RATES[g]
            for s in range(slabs):
                lo = g * SLOT_WIDTH + s * LANES
                stage_ref[ti, (g - 1) * slabs + s] = x[:, lo:lo + LANES]
            for r in range(d):
                for s in range(slabs):
                    piece = stage_ref[ti, (g - 1) * slabs + s, pl.ds(r, rows // d, stride=d), :]
                    outs[g][r, :, s * LANES:(s + 1) * LANES] = piece.astype(BF16)


def _mix_in(x, norm_w, w_in, q_gain, k_gain, seg_ones, tm, seq=None):
    t = x.shape[0]
    prompt = seq is not None

    def row(width):
        return pl.BlockSpec((tm, width), lambda i: (i, 0))

    def rows_f32(width):
        return jax.ShapeDtypeStruct((t, width), F32)

    out_specs = [_lane_blocked_spec(tm, lambda i: i)]
    out_shape = [jax.ShapeDtypeStruct((LANE_BLOCKS, t, LANES), F32)]
    scratch = []
    if not prompt:
        out_specs += [row(ATTN_WIDTH)] * 3 + [row(D_MODEL)] * 2
        out_shape += [rows_f32(ATTN_WIDTH)] * 3 + [rows_f32(D_MODEL)] * 2
    else:
        tiles = seq // tm
        tail = min(DIL_WINDOWS[-1], seq)
        skip = tiles - tail // tm
        tail_spec = pl.BlockSpec((None, tm, ATTN_WIDTH), lambda i: (i // tiles, jnp.maximum(i % tiles - skip, 0), 0))
        out_specs += [tail_spec] * 2 + [row(D_MODEL)] * 2
        out_shape += [jax.ShapeDtypeStruct((t // seq, tail, ATTN_WIDTH), F32)] * 2 + [rows_f32(D_MODEL)] * 2
        out_specs.append(pl.BlockSpec((SSM_PAIRS, tm // SSM_CHUNK, MXU_DIM), lambda i: (0, i, 0)))
        out_shape.append(jax.ShapeDtypeStruct((SSM_PAIRS, t // SSM_CHUNK, MXU_DIM), BF16))
        for _ in range(3):
            for d in DIL_RATES:
                out_specs.append(pl.BlockSpec((None, d, tm // d, SLOT_WIDTH),
                                              lambda i: (i // tiles, 0, i % tiles, 0)))
                out_shape.append(jax.ShapeDtypeStruct((t // seq, d, seq // d, SLOT_WIDTH), BF16))
        scratch = [pltpu.VMEM((3, (N_DIL - 1) * SLOT_WIDTH // LANES, tm, LANES), F32)]
    return pl.pallas_call(
        functools.partial(_mix_in_body, prompt=prompt),
        grid=(t // tm,),
        in_specs=[row(D_MODEL), _const_spec((1, D_MODEL)), _const_spec((D_MODEL, IN_WIDTH)),
                  _const_spec((1, ATTN_WIDTH)), _const_spec((1, ATTN_WIDTH)),
                  _const_spec((MXU_DIM, MXU_DIM))],
        out_specs=out_specs,
        out_shape=out_shape,
        scratch_shapes=scratch,
        compiler_params=_params("arbitrary"),
        name="mix_in",
    )(x, norm_w, w_in, q_gain, k_gain, seg_ones)


def _cmul(ar, ai, br, bi):
    return ar * br - ai * bi, ar * bi + ai * br


def _ssm_prep_body(lrow_re_ref, lrow_im_ref, lcol_re_ref, lcol_im_ref, ldt_ref, bre_ref, bim_ref,
                   ct_re_ref, ct_im_ref, abar_ref, a8_ref, bbar_ref, toep_ref, win_ref, wout_ref):
    n = SSM_CHUNK
    dt = jnp.exp(ldt_ref[...])

    def discretise(lam_re, lam_im):
        lr = jnp.minimum(lam_re, -1e-4)
        mag = jnp.exp(lr * dt)
        return lr, lam_im, mag * jnp.cos(lam_im * dt), mag * jnp.sin(lam_im * dt)

    def powers(ar, ai):
        out = [(jnp.ones_like(ar), jnp.zeros_like(ai))]
        for _ in range(n):
            out.append(_cmul(*out[-1], ar, ai))
        return out

    lr, li, ar, ai = discretise(lrow_re_ref[...], lrow_im_ref[...])
    den = lr * lr + li * li
    fr = ((ar - 1.0) * lr + ai * li) / den
    fi = (ai * lr - (ar - 1.0) * li) / den
    bbr, bbi = _cmul(fr, fi, bre_ref[...], bim_ref[...])
    row_pow = powers(ar, ai)
    abar_ref[0], abar_ref[1] = ar, ai
    a8_ref[0], a8_ref[1] = row_pow[n]
    bbar_ref[0], bbar_ref[1] = bbr, bbi

    def pair_halves(x):
        x = x.reshape((x.shape[0] // 2, 2) + x.shape[1:])
        return x[:, 0], x[:, 1]

    toep_ref[...] = jnp.zeros_like(toep_ref)
    win_ref[...] = jnp.zeros_like(win_ref)
    wout_ref[...] = jnp.zeros_like(wout_ref)
    half = n * SSM_GROUP

    for s in range(n):
        w_re, w_im = _cmul(*row_pow[n - 1 - s], bbr, bbi)
        rows_e = slice(s * SSM_GROUP, (s + 1) * SSM_GROUP)
        rows_o = slice(half + s * SSM_GROUP, half + (s + 1) * SSM_GROUP)
        for part, x in enumerate((w_re, w_im)):
            x_e, x_o = pair_halves(x.astype(BF16))
            win_ref[:, rows_e, (2 * part) * SSM_STATE:(2 * part + 1) * SSM_STATE] = x_e
            win_ref[:, rows_o, (2 * part + 1) * SSM_STATE:(2 * part + 2) * SSM_STATE] = x_o

    _, _, ar_c, ai_c = discretise(lcol_re_ref[...], lcol_im_ref[...])
    col_pow = powers(ar_c, ai_c)
    lane_t = lax.broadcasted_iota(jnp.int32, (1, 1, LANES), 2) // SSM_GROUP

    def spread(first):
        re = im = jnp.zeros((1, 1, LANES), F32)
        for t in range(n):
            re = jnp.where(lane_t == t, col_pow[first + t][0], re)
            im = jnp.where(lane_t == t, col_pow[first + t][1], im)
        return re, im

    ct = (ct_re_ref[...], ct_im_ref[...])
    m0_re, m0_im = _cmul(*ct, *spread(0))
    m1_re, m1_im = _cmul(*ct, *spread(1))

    for part, x in enumerate((m1_re, -m1_im)):
        x_e, x_o = pair_halves(x.astype(BF16))
        wout_ref[:, (2 * part) * SSM_STATE:(2 * part + 1) * SSM_STATE, :half] = x_e
        wout_ref[:, (2 * part + 1) * SSM_STATE:(2 * part + 2) * SSM_STATE, half:] = x_o

    nn = (((2,), (1,)), ((0,), (0,)))
    hp = lax.Precision.HIGHEST
    kern = (lax.dot_general(bbr, m0_re, nn, precision=hp, preferred_element_type=F32)
            - lax.dot_general(bbi, m0_im, nn, precision=hp, preferred_element_type=F32))
    lane = lax.broadcasted_iota(jnp.int32, (1, 1, LANES), 2)
    for s in range(n):
        shifted = kern if s == 0 else jnp.where(lane >= s * SSM_GROUP, pltpu.roll(kern, s * SSM_GROUP, 2), 0.0)
        x_e, x_o = pair_halves(shifted.astype(BF16))
        toep_ref[:, s * SSM_GROUP:(s + 1) * SSM_GROUP, :half] = x_e
        toep_ref[:, half + s * SSM_GROUP:half + (s + 1) * SSM_GROUP, half:] = x_o


def _ssm_prep(lam_re, lam_im, log_dt, b_re_t, b_im_t, c_re, c_im, groups_per_step=16):
    g, p, c = SSM_GROUPS, SSM_STATE, SSM_GROUP
    gb = groups_per_step

    def spec(*tail):
        return pl.BlockSpec((gb,) + tail, lambda i: (i,) + (0,) * len(tail))

    def stacked(*tail):
        return pl.BlockSpec((2, gb) + tail, lambda i: (0, i) + (0,) * len(tail))

    pair_spec = pl.BlockSpec((gb // 2, MXU_DIM, MXU_DIM), lambda i: (i, 0, 0))
    pair_shape = jax.ShapeDtypeStruct((SSM_PAIRS, MXU_DIM, MXU_DIM), BF16)
    ct_re = jnp.tile(c_re.transpose(0, 2, 1), (1, 1, SSM_CHUNK))
    ct_im = jnp.tile(c_im.transpose(0, 2, 1), (1, 1, SSM_CHUNK))
    return pl.pallas_call(
        _ssm_prep_body,
        grid=(g // gb,),
        in_specs=[spec(1, p), spec(1, p), spec(p, 1), spec(p, 1), spec(1, 1), spec(c, p), spec(c, p),
                  spec(p, LANES), spec(p, LANES)],
        out_specs=[stacked(1, p), stacked(1, p), stacked(c, p), pair_spec, pair_spec, pair_spec],
        out_shape=[jax.ShapeDtypeStruct((2, g, 1, p), F32), jax.ShapeDtypeStruct((2, g, 1, p), F32),
                   jax.ShapeDtypeStruct((2, g, c, p), F32), pair_shape, pair_shape, pair_shape],
        compiler_params=_params("parallel"),
        name="ssm_prep",
    )(lam_re.reshape(g, 1, p), lam_im.reshape(g, 1, p), lam_re.reshape(g, p, 1), lam_im.reshape(g, p, 1),
      log_dt.reshape(g, 1, 1), b_re_t, b_im_t, ct_re, ct_im)


def _ssm_body(lhs_ref, toep_ref, win_ref, wout_ref, a8_ref, yfl_ref, hfin_ref, st_ref, carry_ref, *, rows):
    i = pl.program_id(1)

    @pl.when(i == 0)
    def _():
        carry_ref[...] = jnp.zeros_like(carry_ref)

    def state_in(r, _):
        b = jnp.dot(lhs_ref[r], win_ref[r], preferred_element_type=F32)
        st_ref[0, pl.ds(r, rows, stride=SSM_ROW_PITCH), :] = b[:, :LANES]
        st_ref[1, pl.ds(r, rows, stride=SSM_ROW_PITCH), :] = b[:, LANES:]
        return 0

    lax.fori_loop(0, SSM_PAIRS, state_in, 0, unroll=4)

    a_re = a8_ref[0]
    a_im = a8_ref[1]

    def step(j, h):
        h_re, h_im = h
        base = pl.multiple_of(j * SSM_ROW_PITCH, SUBLANES)
        n_re = a_re * h_re - a_im * h_im + st_ref[0, pl.ds(base, SSM_PAIRS), :]
        n_im = a_re * h_im + a_im * h_re + st_ref[1, pl.ds(base, SSM_PAIRS), :]
        st_ref[0, pl.ds(base, SSM_PAIRS), :] = h_re
        st_ref[1, pl.ds(base, SSM_PAIRS), :] = h_im
        return n_re, n_im

    h_re, h_im = lax.fori_loop(0, rows, step, (carry_ref[0], carry_ref[1]), unroll=4)
    carry_ref[0] = h_re
    carry_ref[1] = h_im
    hfin_ref[0] = h_re
    hfin_ref[1] = h_im

    def chunk_out(r, _):
        hcat = jnp.concatenate([st_ref[0, pl.ds(r, rows, stride=SSM_ROW_PITCH), :],
                                st_ref[1, pl.ds(r, rows, stride=SSM_ROW_PITCH), :]], axis=1).astype(BF16)
        yfl_ref[r] = (jnp.dot(lhs_ref[r], toep_ref[r], preferred_element_type=F32)
                      + jnp.dot(hcat, wout_ref[r], preferred_element_type=F32))
        return 0

    lax.fori_loop(0, SSM_PAIRS, chunk_out, 0, unroll=4)


def _ssm_prompt(lhs, toep, win, wout, a8, batch, seq, tile):
    rows = tile // SSM_CHUNK
    n_tiles = seq // tile
    tok = pl.BlockSpec((SSM_PAIRS, rows, MXU_DIM), lambda b, i: (0, b * n_tiles + i, 0))
    pair_w = _const_spec((SSM_PAIRS, MXU_DIM, MXU_DIM))
    return pl.pallas_call(
        functools.partial(_ssm_body, rows=rows),
        grid=(batch, n_tiles),
        in_specs=[tok, pair_w, pair_w, pair_w, _const_spec((2, SSM_PAIRS, LANES))],
        out_specs=[tok, pl.BlockSpec((None, 2, SSM_PAIRS, LANES), lambda b, i: (b, 0, 0, 0))],
        out_shape=[jax.ShapeDtypeStruct((SSM_PAIRS, batch * seq // SSM_CHUNK, MXU_DIM), F32),
                   jax.ShapeDtypeStruct((batch, 2, SSM_PAIRS, LANES), F32)],
        scratch_shapes=[pltpu.VMEM((2, rows * SSM_ROW_PITCH, LANES), F32),
                        pltpu.VMEM((2, SSM_PAIRS, LANES), F32)],
        compiler_params=_params("parallel", "arbitrary"),
        name="ssm_prompt",
    )(lhs, toep, win, wout, a8)


def _ssm_step_body(u_ref, hre_ref, him_ref, abar_ref, bbar_ref, cre_ref, cim_ref,
                   y_ref, ore_ref, oim_ref):
    hp = lax.Precision.HIGHEST
    nt = (((1,), (1,)), ((), ()))
    groups_per_block = LANES // SSM_GROUP
    for g in range(SSM_GROUPS):
        blk = g // groups_per_block
        ch = slice((g % groups_per_block) * SSM_GROUP, (g % groups_per_block + 1) * SSM_GROUP)
        st = slice(g * SSM_STATE, (g + 1) * SSM_STATE)
        ug = u_ref[blk, :, ch]
        bu_re = jnp.dot(ug, bbar_ref[0, g], precision=hp, preferred_element_type=F32)
        bu_im = jnp.dot(ug, bbar_ref[1, g], precision=hp, preferred_element_type=F32)
        a_re = abar_ref[0, g]
        a_im = abar_ref[1, g]
        h_re = hre_ref[:, st]
        h_im = him_ref[:, st]
        n_re = a_re * h_re - a_im * h_im + bu_re
        n_im = a_re * h_im + a_im * h_re + bu_im
        ore_ref[:, st] = n_re
        oim_ref[:, st] = n_im
        y_ref[blk, :, ch] = (lax.dot_general(n_re, cre_ref[g], nt, precision=hp, preferred_element_type=F32)
                             - lax.dot_general(n_im, cim_ref[g], nt, precision=hp, preferred_element_type=F32))


def _ssm_step(u, h_re, h_im, abar, bbar, c_re, c_im):
    b = u.shape[1]
    width = SSM_GROUPS * SSM_STATE
    return pl.pallas_call(
        _ssm_step_body,
        out_shape=[jax.ShapeDtypeStruct((LANE_BLOCKS, b, LANES), F32),
                   jax.ShapeDtypeStruct((b, width), F32),
                   jax.ShapeDtypeStruct((b, width), F32)],
        compiler_params=pltpu.CompilerParams(vmem_limit_bytes=VMEM_LIMIT_BYTES),
        name="ssm_step",
    )(u, h_re, h_im, abar, bbar, c_re, c_im)


def _attn_prompt_body(*refs, part_rows):
    per_group = 5
    ins = refs[:per_group * N_DIL]
    o_ref, m_ref, l_ref, acc_ref, bias_ref = refs[per_group * N_DIL:]
    part = pl.program_id(1)
    tq = KEYS_BACK
    row = lax.broadcasted_iota(jnp.int32, (tq, 2 * tq), 0)
    col = lax.broadcasted_iota(jnp.int32, (tq, 2 * tq), 1)
    back = row + tq - col
    in_window = (back >= 0) & (back <= KEYS_BACK)
    first_head = lax.broadcasted_iota(jnp.int32, (1, LANES), 1) < HEAD_DIM
    nt = (((1,), (1,)), ((), ()))

    order = tuple(reversed(range(N_DIL)))
    ones = jnp.ones((2 * tq, LANES), BF16)
    for g in order:
        q_ref, k_ref, kb_ref, v_ref, vb_ref = ins[per_group * g:per_group * (g + 1)]
        d = DIL_RATES[g]
        blocks_per_residue = part_rows // d // tq
        dist = (back * d).astype(F32)
        for h in range(HEADS):
            bias = jnp.where(in_window, -_slope(g, h) * dist, -jnp.inf)
            bias_ref[h] = bias
            bias_ref[HEADS + h] = jnp.where(col >= tq, bias, -jnp.inf)

        def block(mi, _, g=g, d=d, q_ref=q_ref, k_ref=k_ref, kb_ref=kb_ref, v_ref=v_ref, vb_ref=vb_ref,
                  blocks_per_residue=blocks_per_residue):
            residue = mi // blocks_per_residue
            n = mi % blocks_per_residue
            cur = pl.multiple_of(n * tq, tq)
            prev = pl.multiple_of(jnp.maximum(n - 1, 0) * tq, tq)
            bias_at = jnp.where((n == 0) & (part == 0), HEADS, 0)
            token0 = residue + d * tq * n
            rows = pl.ds(pl.multiple_of(token0, tq), tq) if d == 1 else pl.ds(token0, tq, stride=d)
            pairs = range(HEADS // 2)
            first, final = g == order[0], g == order[-1]
            old = None if first else [(m_ref[pair, rows, :], l_ref[pair, rows, :], acc_ref[pair, rows, :])
                                      for pair in pairs]
            new = []
            for pair in pairs:
                lanes = slice(pair * LANES, (pair + 1) * LANES)
                qp = q_ref[residue, pl.ds(cur, tq), lanes]
                k_prev = jnp.where(n == 0, kb_ref[residue, :, lanes], k_ref[residue, pl.ds(prev, tq), lanes])
                v_prev = jnp.where(n == 0, vb_ref[residue, :, lanes], v_ref[residue, pl.ds(prev, tq), lanes])
                kp = jnp.concatenate([k_prev, k_ref[residue, pl.ds(cur, tq), lanes]], axis=0)
                vp = jnp.concatenate([v_prev, v_ref[residue, pl.ds(cur, tq), lanes]], axis=0)
                vp = jnp.concatenate([vp, ones], axis=1)
                stats = []
                for e in range(2):
                    qm = jnp.where(first_head if e == 0 else ~first_head, qp, jnp.zeros_like(qp))
                    s = lax.dot_general(qm, kp, nt, preferred_element_type=F32)
                    s = s + bias_ref[bias_at + 2 * pair + e]
                    m = jnp.max(s, axis=-1, keepdims=True)
                    p = jnp.exp(s - m)
                    pv = jnp.dot(p.astype(BF16), vp, preferred_element_type=F32)
                    stats.append((m, pv[:, LANES:], pv[:, :LANES]))
                m_new, l_new, acc_new = (jnp.where(first_head, a, b) for a, b in zip(*stats))
                if not first:
                    m_old, l_old, acc_old = old[pair]
                    m_tot = jnp.maximum(m_old, m_new)
                    w_old = jnp.exp(m_old - m_tot)
                    w_new = jnp.exp(m_new - m_tot)
                    l_new = w_old * l_old + w_new * l_new
                    acc_new = w_old * acc_old + w_new * acc_new
                    m_new = m_tot
                new.append((m_new, l_new, acc_new))
            for pair, (m_new, l_new, acc_new) in zip(pairs, new):
                if not final:
                    m_ref[pair, rows, :] = m_new
                    l_ref[pair, rows, :] = l_new
                    acc_ref[pair, rows, :] = acc_new
                else:
                    o_ref[pair, rows, :] = acc_new / l_new
            return 0

        lax.fori_loop(0, part_rows // tq, block, 0, unroll=2)


def _attn_prompt(qkv, batch, seq):
    slabs = SLOT_WIDTH // LANES
    tq = KEYS_BACK
    part_rows = DIL_RATES[-1] * tq
    parts = seq // part_rows
    args, specs = [], []
    for g, d in enumerate(DIL_RATES):
        rows = part_rows // d
        cur = pl.BlockSpec((None, d, rows, SLOT_WIDTH), lambda b, p: (b, 0, p, 0))
        before = pl.BlockSpec((None, d, tq, SLOT_WIDTH),
                              functools.partial(lambda b, p, step: (b, 0, jnp.maximum(p * step - 1, 0), 0),
                                                step=rows // tq))
        q, k, v = qkv[g], qkv[N_DIL + g], qkv[2 * N_DIL + g]
        args += [q, k, k, v, v]
        specs += [cur, cur, before, cur, before]
    running = pltpu.VMEM((slabs, part_rows, LANES), F32)
    return pl.pallas_call(
        functools.partial(_attn_prompt_body, part_rows=part_rows),
        grid=(batch, parts),
        in_specs=specs,
        out_specs=pl.BlockSpec((slabs, part_rows, LANES), lambda b, p: (0, b * parts + p, 0)),
        out_shape=jax.ShapeDtypeStruct((slabs, batch * seq, LANES), F32),
        scratch_shapes=[running, running, running, pltpu.VMEM((2 * HEADS, tq, 2 * tq), F32)],
        compiler_params=_params("parallel", "arbitrary"),
        name="attn_prompt",
    )(*args)


def _as_column(row_vec):
    n = row_vec.shape[1]
    eye = lax.broadcasted_iota(jnp.int32, (n, n), 0) == lax.broadcasted_iota(jnp.int32, (n, n), 1)
    return jnp.sum(jnp.where(eye, row_vec, 0.0), axis=1, keepdims=True)


def _attn_sample_body(q_ref, k_ref, v_ref, c0_ref, c1_ref, c2_ref, ob_ref, n0_ref, n1_ref, n2_ref):
    b = pl.program_id(0)
    q_row = q_ref[pl.ds(b, 1), :]
    k_row = k_ref[pl.ds(b, 1), :]
    v_row = v_ref[pl.ds(b, 1), :]
    head_row = lax.broadcasted_iota(jnp.int32, (SUBLANES, SLOT_WIDTH), 0)
    own_head = lax.broadcasted_iota(jnp.int32, (SUBLANES, SLOT_WIDTH), 1) // HEAD_DIM == head_row
    head_col = lax.broadcasted_iota(jnp.int32, (SUBLANES, 1), 0)
    outs, lses = [], []
    for g, (c_ref, n_ref) in enumerate(((c0_ref, n0_ref), (c1_ref, n1_ref), (c2_ref, n2_ref))):
        w = DIL_WINDOWS[g]
        d = DIL_RATES[g]
        cols = slice(g * SLOT_WIDTH, (g + 1) * SLOT_WIDTH)
        q_g, k_new, v_new = q_row[:, cols], k_row[:, cols], v_row[:, cols]
        kt = c_ref[0].reshape(SLOT_WIDTH, w)
        vt = c_ref[1].reshape(SLOT_WIDTH, w)
        q_heads = jnp.where(own_head, q_g, 0.0)
        s = jnp.dot(q_heads.astype(BF16), kt.astype(BF16), preferred_element_type=F32)
        dist = w - lax.broadcasted_iota(jnp.int32, (1, w), 1)
        slope = functools.reduce(lambda acc, h: jnp.where(head_col == h, _slope(g, h), acc), range(HEADS), 0.0)
        s = jnp.where((dist & (d - 1)) == 0, s - slope * dist.astype(F32), -jnp.inf)
        s_new = jnp.sum(q_heads * k_new, axis=1, keepdims=True)
        m = jnp.maximum(jnp.max(s, axis=1, keepdims=True), s_new)
        p = jnp.exp(s - m)
        p_new = jnp.exp(s_new - m)
        den = jnp.sum(p, axis=1, keepdims=True) + p_new
        pv = lax.dot_general(p.astype(BF16), vt.astype(BF16), (((1,), (1,)), ((), ())),
                             preferred_element_type=F32)
        o_heads = (pv + p_new * v_new) / den
        outs.append(jnp.sum(jnp.where(own_head, o_heads, 0.0), axis=0, keepdims=True))
        lses.append(jnp.sum(jnp.where(own_head, m + jnp.log(den), 0.0), axis=0, keepdims=True))
        last = lax.broadcasted_iota(jnp.int32, (SLOT_WIDTH, w), 1) == w - 1
        shape = (HEADS, HEAD_DIM, w)
        n_ref[0] = jnp.where(last, _as_column(k_new), pltpu.roll(kt, w - 1, 1)).reshape(shape)
        n_ref[1] = jnp.where(last, _as_column(v_new), pltpu.roll(vt, w - 1, 1)).reshape(shape)
    top = functools.reduce(jnp.maximum, lses)
    wts = [jnp.exp(l - top) for l in lses]
    ob_ref[...] = sum(w_g * o_g for w_g, o_g in zip(wts, outs)) / sum(wts)


def _attn_sample(q, k, v, caches):
    b = q.shape[0]
    full = _const_spec((b, ATTN_WIDTH))
    win = [pl.BlockSpec((None, 2, HEADS, HEAD_DIM, w), lambda i: (i, 0, 0, 0, 0)) for w in DIL_WINDOWS]
    outs = pl.pallas_call(
        _attn_sample_body,
        grid=(b,),
        in_specs=[full, full, full] + win,
        out_specs=[pl.BlockSpec((None, 1, SLOT_WIDTH), lambda i: (i, 0, 0))] + win,
        out_shape=([jax.ShapeDtypeStruct((b, 1, SLOT_WIDTH), F32)]
                   + [jax.ShapeDtypeStruct(c.shape, F32) for c in caches]),
        compiler_params=_params("parallel"),
        name="attn_sample",
    )(q, k, v, *caches)
    return outs[0].reshape(b, SLOT_WIDTH), outs[1:]


def _kv_window_body(*refs):
    k_refs, v_refs, out_refs = refs[:N_DIL], refs[N_DIL:2 * N_DIL], refs[2 * N_DIL:]
    for g, o_ref in enumerate(out_refs):
        for t, x_ref in enumerate((k_refs[g], v_refs[g])):
            for pair in range(HEADS // 2):
                xt = x_ref[:, pair * LANES:(pair + 1) * LANES].T
                o_ref[t, 2 * pair] = xt[:HEAD_DIM]
                o_ref[t, 2 * pair + 1] = xt[HEAD_DIM:]


def _kv_windows(k_tail, v_tail):
    batch, tail, _ = k_tail.shape
    keeps = [min(w, tail) for w in DIL_WINDOWS]
    assert all(tail % w == 0 for w in keeps)
    src = [pl.BlockSpec((None, w, SLOT_WIDTH), functools.partial(lambda b, at, g: (b, at, g), at=tail // w - 1, g=g))
           for g, w in enumerate(keeps)]
    return pl.pallas_call(
        _kv_window_body,
        grid=(batch,),
        in_specs=src + src,
        out_specs=[pl.BlockSpec((None, 2, HEADS, HEAD_DIM, w), lambda b: (b, 0, 0, 0, 0)) for w in keeps],
        out_shape=[jax.ShapeDtypeStruct((batch, 2, HEADS, HEAD_DIM, w), F32) for w in keeps],
        compiler_params=_params("parallel"),
        name="kv_windows",
    )(*([k_tail] * N_DIL + [v_tail] * N_DIL))


def _mix_out_body(x_ref, y_ref, u_ref, d_ref, ob_ref,
                  ga_ref, gb_ref, wglu_ref, wpa_ref, wpb_ref, wout_ref, out_ref, *unfold):
    tm = x_ref.shape[0]
    if unfold:
        y_ref, folded_ref = unfold[0], y_ref
        groups_per_block = LANES // SSM_GROUP
    pieces = MIX_OUT_PIECES if unfold else 1
    rows = tm // pieces
    for piece in range(pieces):
        at = slice(piece * rows, (piece + 1) * rows)
        if unfold:
            chunks = rows // SSM_CHUNK
            chunk_at = slice(piece * chunks, (piece + 1) * chunks)
            for blk in range(LANE_BLOCKS):
                groups = [folded_ref[blk * (groups_per_block // 2) + q_i // 2, chunk_at,
                                     (q_i % 2) * LANES:(q_i % 2 + 1) * LANES] for q_i in range(groups_per_block)]
                for t, rows_t in enumerate(_transpose_granules(groups)):
                    y_ref[blk, pl.ds(piece * rows + t, chunks, stride=SSM_CHUNK), :] = rows_t
        y_raw = jnp.concatenate([y_ref[blk, at, :] for blk in range(LANE_BLOCKS)], axis=1)
        u = jnp.concatenate([u_ref[blk, at, :] for blk in range(LANE_BLOCKS)], axis=1)
        y = jax.nn.gelu(y_raw + d_ref[...] * u)
        yb = y.astype(BF16)
        y_a = y * jax.nn.sigmoid(jnp.dot(yb, wglu_ref[...], preferred_element_type=F32))
        branch_a = jnp.dot(y_a.astype(BF16), wpa_ref[...], preferred_element_type=F32)

        o_b = jnp.concatenate([ob_ref[s, at, :] for s in range(SLOT_WIDTH // LANES)], axis=1)
        branch_b = jnp.dot(o_b.astype(BF16), wpb_ref[...], preferred_element_type=F32)

        merged = ga_ref[at, :] * branch_a + gb_ref[at, :] * branch_b
        out_ref[at, :] = x_ref[at, :] + jnp.dot(merged.astype(BF16), wout_ref[...], preferred_element_type=F32)


def _mix_out(x, y, u, ssm_d, o_b, ga, gb, w_glu, w_pa, w_pb, w_out, tm, folded_y):
    t = x.shape[0]
    wide = pl.BlockSpec((tm, D_MODEL), lambda i: (i, 0))
    blocked = _lane_blocked_spec(tm, lambda i: i)
    y_spec = pl.BlockSpec((SSM_PAIRS, tm // SSM_CHUNK, MXU_DIM), lambda i: (0, i, 0)) if folded_y else blocked
    scratch = [pltpu.VMEM((LANE_BLOCKS, tm, LANES), F32)] if folded_y else []
    slot = pl.BlockSpec((SLOT_WIDTH // LANES, tm, LANES), lambda i: (0, i, 0))
    return pl.pallas_call(
        _mix_out_body,
        grid=(t // tm,),
        in_specs=[wide, y_spec, blocked, _const_spec((1, D_MODEL)), slot, wide, wide,
                  _const_spec((D_MODEL, D_MODEL)), _const_spec((D_MODEL, D_MODEL)),
                  _const_spec((SLOT_WIDTH, D_MODEL)), _const_spec((D_MODEL, D_MODEL))],
        out_specs=wide,
        out_shape=jax.ShapeDtypeStruct((t, D_MODEL), F32),
        scratch_shapes=scratch,
        compiler_params=_params("parallel"),
        name="mix_out",
    )(x, y, u, ssm_d, o_b, ga, gb, w_glu, w_pa, w_pb, w_out)


def _head_segment_ones():
    head = jnp.arange(MXU_DIM) // HEAD_DIM
    return (head[:, None] == head[None, :]).astype(BF16)


def kernel(x_prompt, x_sample, state_ssm_re, state_ssm_im, cache_kv_w128, cache_kv_w512, cache_kv_w2048, ffn1_norm, ffn1_w_gate, ffn1_w_up, ffn1_w_down, mix_norm, w_in, ssm_lambda_re, ssm_lambda_im, ssm_b_re, ssm_b_im, ssm_c_re, ssm_c_im, ssm_d, ssm_log_dt, w_glu, q_gain, k_gain, w_proj_a, w_proj_b, w_out, ffn2_norm, ffn2_w_gate, ffn2_w_up, ffn2_w_down):
    depth = ffn1_norm.shape[0]
    assert depth == 1, "single-layer step"
    batch, seq, _ = x_prompt.shape
    dec_batch, dec_seq, _ = x_sample.shape
    assert dec_seq == 1 and seq % (DIL_RATES[-1] * KEYS_BACK) == 0
    layer = 0
    bf = lambda w: w[layer].astype(BF16)
    vec = lambda w: w[layer][None]
    ffn1 = (vec(ffn1_norm), bf(ffn1_w_gate), bf(ffn1_w_up), bf(ffn1_w_down))
    ffn2 = (vec(ffn2_norm), bf(ffn2_w_gate), bf(ffn2_w_up), bf(ffn2_w_down))
    mix_norm, ssm_d = vec(mix_norm), vec(ssm_d)
    w_in_b, w_glu_b, w_pa_b, w_pb_b, w_out_b = bf(w_in), bf(w_glu), bf(w_proj_a), bf(w_proj_b), bf(w_out)
    q_gain_t = jnp.tile(vec(q_gain), (1, N_DIL * HEADS))
    k_gain_t = jnp.tile(vec(k_gain), (1, N_DIL * HEADS))
    seg_ones = _head_segment_ones()

    abar, a8, bbar, toep, win, wout = _ssm_prep(
        ssm_lambda_re[layer], ssm_lambda_im[layer], ssm_log_dt[layer],
        ssm_b_re[layer].transpose(0, 2, 1), ssm_b_im[layer].transpose(0, 2, 1),
        ssm_c_re[layer], ssm_c_im[layer])
    a8 = a8.reshape(2, SSM_PAIRS, LANES)

    def front(x, tm_ffn, tm, seq=None):
        x1 = _ffn(x, *ffn1, tm_ffn)
        return (x1,) + tuple(_mix_in(x1, mix_norm, w_in_b, q_gain_t, k_gain_t, seg_ones, tm, seq))

    def back(x1, y, u, o_b, ga, gb, tm_ffn, tm, folded_y):
        x2 = _mix_out(x1, y, u, ssm_d, o_b, ga, gb, w_glu_b, w_pa_b, w_pb_b, w_out_b, tm, folded_y)
        return _ffn(x2, *ffn2, tm_ffn)

    tm = 512
    tm_ffn = 1024
    xp = x_prompt.reshape(batch * seq, D_MODEL)
    x1, u, k_tail, v_tail, ga, gb, u_folded, *qkv = front(xp, tm_ffn, tm, seq)
    y, h_fin = _ssm_prompt(u_folded, toep, win, wout, a8, batch, seq, tile=2048)
    o_b = _attn_prompt(qkv, batch, seq)
    yp = back(x1, y, u, o_b, ga, gb, tm_ffn, tm, folded_y=True)
    yp = yp.reshape(batch, seq, D_MODEL)
    h_fin = h_fin.reshape(batch, 2, SSM_GROUPS, SSM_STATE)
    p_re = h_fin[None, :, 0]
    p_im = h_fin[None, :, 1]
    p_kv = [c.transpose(0, 4, 1, 2, 3)[None] for c in _kv_windows(k_tail, v_tail)]

    xs = x_sample.reshape(dec_batch, D_MODEL)
    x1, u, q, k, v, ga, gb = front(xs, dec_batch, dec_batch)
    width = SSM_GROUPS * SSM_STATE
    y, s_re, s_im = _ssm_step(u, state_ssm_re[layer].reshape(dec_batch, width),
                              state_ssm_im[layer].reshape(dec_batch, width),
                              abar, bbar, ssm_c_re[layer], ssm_c_im[layer])
    caches = [c[layer].transpose(0, 2, 3, 4, 1) for c in (cache_kv_w128, cache_kv_w512, cache_kv_w2048)]
    o_b, new_caches = _attn_sample(q, k, v, caches)
    o_b = o_b.reshape(dec_batch, SLOT_WIDTH // LANES, LANES).transpose(1, 0, 2)
    ys = back(x1, y, u, o_b, ga, gb, dec_batch, dec_batch, folded_y=False)
    ys = ys.reshape(dec_batch, 1, D_MODEL)
    s_re = s_re.reshape(1, dec_batch, SSM_GROUPS, SSM_STATE)
    s_im = s_im.reshape(1, dec_batch, SSM_GROUPS, SSM_STATE)
    s_kv = [c.transpose(0, 4, 1, 2, 3)[None] for c in new_caches]

    return (yp, ys, p_re, p_im, p_kv[0], p_kv[1], p_kv[2], s_re, s_im, s_kv[0], s_kv[1], s_kv[2])
```

```python
import functools

import jax
import jax.numpy as jnp
from jax import lax
from jax.experimental import pallas as pl
from jax.experimental.pallas import tpu as pltpu

F32 = jnp.float32
BF16 = jnp.bfloat16

D_MODEL = 1024
SSM_GROUP = 16
SSM_GROUPS = 64
SSM_STATE = 64
SSM_PAIRS = SSM_GROUPS // 2
HEAD_DIM = 64
HEADS = 4
DIL_WINDOWS = (128, 512, 2048)
DIL_RATES = (1, 4, 16)
N_DIL = 3
KEYS_BACK = 128
ATTN_WIDTH = N_DIL * HEADS * HEAD_DIM
SLOT_WIDTH = HEADS * HEAD_DIM
D_FF = 2816
RMS_EPS = 1e-6
ALIBI_MAX_EXP = 8.0
IN_WIDTH = D_MODEL + 3 * ATTN_WIDTH + 2 * D_MODEL

LANES = 128
SUBLANES = 8
MXU_DIM = 256
VMEM_LIMIT_BYTES = 56 * 1024 * 1024

LANE_BLOCKS = D_MODEL // LANES

SSM_CHUNK = SUBLANES
SSM_ROW_PITCH = 40
FF_CHUNKS = ((0, 1024), (1024, 2048), (2048, 2816))
MIX_OUT_PIECES = 2


def _slope(group, head):
    return 2.0 ** (-ALIBI_MAX_EXP * (group * HEADS + head + 1) / (N_DIL * HEADS))


def _const_spec(shape):
    zeros = (0,) * len(shape)
    return pl.BlockSpec(shape, lambda *_: zeros, pipeline_mode=pl.Buffered(1))


def _lane_blocked_spec(rows, row_block):
    return pl.BlockSpec((LANE_BLOCKS, rows, LANES), lambda *idx: (0, row_block(*idx), 0))


def _params(*semantics):
    return pltpu.CompilerParams(dimension_semantics=semantics, vmem_limit_bytes=VMEM_LIMIT_BYTES)


def _rms(x, w):
    return x * lax.rsqrt(jnp.mean(x * x, axis=-1, keepdims=True) + RMS_EPS) * w


def _split_bf16(x, terms):
    parts = []
    for _ in range(terms):
        p = x.astype(BF16)
        parts.append(p)
        x = x - p.astype(F32)
    return parts


def _ffn_body(x_ref, nw_ref, wg_ref, wu_ref, wd_ref, o_ref, side_jobs=()):
    x = x_ref[...]
    h = _rms(x, nw_ref[...]).astype(BF16)
    acc = jnp.zeros_like(x)
    side_jobs = list(side_jobs)
    for lo, hi in FF_CHUNKS:
        g = jnp.dot(h, wg_ref[:, lo:hi], preferred_element_type=F32)
        u = jnp.dot(h, wu_ref[:, lo:hi], preferred_element_type=F32)
        a = (jax.nn.silu(g) * u).astype(BF16)
        acc = acc + jnp.dot(a, wd_ref[lo:hi, :], preferred_element_type=F32)
        if side_jobs:
            side_jobs.pop(0)()
    for job in side_jobs:
        job()
    o_ref[...] = x + 0.5 * acc


def _ffn(x, norm_w, wg, wu, wd, tm):
    t = x.shape[0]
    row = pl.BlockSpec((tm, D_MODEL), lambda i: (i, 0))
    return pl.pallas_call(
        _ffn_body,
        grid=(t // tm,),
        in_specs=[row, _const_spec((1, D_MODEL)), _const_spec((D_MODEL, D_FF)),
                  _const_spec((D_MODEL, D_FF)), _const_spec((D_FF, D_MODEL))],
        out_specs=row,
        out_shape=jax.ShapeDtypeStruct((t, D_MODEL), F32),
        compiler_params=_params("parallel"),
        name="ffn",
    )(x, norm_w, wg, wu, wd)


def _head_norm(x, gain, seg_ones):
    parts = _split_bf16(x * x, 2)
    blocks = []
    for lo in range(0, x.shape[1], MXU_DIM):
        blocks.append(sum(jnp.dot(p[:, lo:lo + MXU_DIM], seg_ones, preferred_element_type=F32) for p in parts))
    ss = jnp.concatenate(blocks, axis=1)
    return x * lax.rsqrt(ss * (1.0 / HEAD_DIM) + RMS_EPS) * gain


def _transpose_granules(xs):
    xs = list(xs)
    n = len(xs)
    block = lax.broadcasted_iota(jnp.int32, (1, LANES), 1) // SSM_GROUP
    bit = n // 2
    while bit:
        upper = (block & bit) != 0
        shift = SSM_GROUP * bit
        for lo in range(n):
            if lo & bit:
                continue
            hi = lo + bit
            x_lo, x_hi = xs[lo], xs[hi]
            xs[lo] = jnp.where(upper, pltpu.roll(x_hi, shift, 1), x_lo)
            xs[hi] = jnp.where(upper, x_hi, pltpu.roll(x_lo, LANES - shift, 1))
        bit //= 2
    return xs


def _mix_in_body(x_ref, nw_ref, w_ref, qg_ref, kg_ref, seg_ref, u_ref, *rest, prompt):
    h = _rms(x_ref[...], nw_ref[...]).astype(BF16)

    def proj(lo, hi):
        return jnp.dot(h, w_ref[:, lo:hi], preferred_element_type=F32)

    c0 = D_MODEL
    c1 = c0 + ATTN_WIDTH
    c2 = c1 + ATTN_WIDTH
    c3 = c2 + ATTN_WIDTH
    c4 = c3 + D_MODEL
    u = proj(0, c0)
    for blk in range(LANE_BLOCKS):
        u_ref[blk] = u[:, blk * LANES:(blk + 1) * LANES]
    seg_ones = seg_ref[...]
    q = _head_norm(proj(c0, c1), qg_ref[...], seg_ones) * (HEAD_DIM ** -0.5)
    k = _head_norm(proj(c1, c2), kg_ref[...], seg_ones)
    v = proj(c2, c3)
    if not prompt:
        q_ref, k_ref, v_ref, ga_ref, gb_ref = rest
        q_ref[...] = q
    else:
        k_ref, v_ref, ga_ref, gb_ref, fold_ref = rest[:5]
        by_residue = rest[5:5 + 3 * N_DIL]
        stage_ref = rest[-1]
    k_ref[...] = k
    v_ref[...] = v
    ga_ref[...] = jax.nn.sigmoid(proj(c3, c4))
    gb_ref[...] = jax.nn.sigmoid(proj(c4, IN_WIDTH))
    if not prompt:
        return

    rows = x_ref.shape[0]
    groups_per_block = LANES // SSM_GROUP
    for blk in range(LANE_BLOCKS):
        tokens = [u_ref[blk, pl.ds(s, rows // SSM_CHUNK, stride=SSM_CHUNK), :] for s in range(SSM_CHUNK)]
        for q_i, folded in enumerate(_transpose_granules(tokens)):
            pair = blk * (groups_per_block // 2) + q_i // 2
            fold_ref[pair, :, (q_i % 2) * LANES:(q_i % 2 + 1) * LANES] = folded.astype(BF16)

    slabs = SLOT_WIDTH // LANES
    for ti, x in enumerate((q, k, v)):
        outs = by_residue[ti * N_DIL:(ti + 1) * N_DIL]
        outs[0][0] = x[:, :SLOT_WIDTH].astype(BF16)
        for g in range(1, N_DIL):
            d = DIL_RATES[g]
            for s in range(slabs):
                lo = g * SLOT_WIDTH + s * LANES
                stage_ref[ti, (g - 1) * slabs + s] = x[:, lo:lo + LANES]
            for r in range(d):
                for s in range(slabs):
                    piece = stage_ref[ti, (g - 1) * slabs + s, pl.ds(r, rows // d, stride=d), :]
                    outs[g][r, :, s * LANES:(s + 1) * LANES] = piece.astype(BF16)


def _mix_in(x, norm_w, w_in, q_gain, k_gain, seg_ones, tm, seq=None):
    t = x.shape[0]
    prompt = seq is not None

    def row(width):
        return pl.BlockSpec((tm, width), lambda i: (i, 0))

    def rows_f32(width):
        return jax.ShapeDtypeStruct((t, width), F32)

    out_specs = [_lane_blocked_spec(tm, lambda i: i)]
    out_shape = [jax.ShapeDtypeStruct((LANE_BLOCKS, t, LANES), F32)]
    scratch = []
    if not prompt:
        out_specs += [row(ATTN_WIDTH)] * 3 + [row(D_MODEL)] * 2
        out_shape += [rows_f32(ATTN_WIDTH)] * 3 + [rows_f32(D_MODEL)] * 2
    else:
        tiles = seq // tm
        tail = min(DIL_WINDOWS[-1], seq)
        skip = tiles - tail // tm
        tail_spec = pl.BlockSpec((None, tm, ATTN_WIDTH), lambda i: (i // tiles, jnp.maximum(i % tiles - skip, 0), 0))
        out_specs += [tail_spec] * 2 + [row(D_MODEL)] * 2
        out_shape += [jax.ShapeDtypeStruct((t // seq, tail, ATTN_WIDTH), F32)] * 2 + [rows_f32(D_MODEL)] * 2
        out_specs.append(pl.BlockSpec((SSM_PAIRS, tm // SSM_CHUNK, MXU_DIM), lambda i: (0, i, 0)))
        out_shape.append(jax.ShapeDtypeStruct((SSM_PAIRS, t // SSM_CHUNK, MXU_DIM), BF16))
        for _ in range(3):
            for d in DIL_RATES:
                out_specs.append(pl.BlockSpec((None, d, tm // d, SLOT_WIDTH),
                                              lambda i: (i // tiles, 0, i % tiles, 0)))
                out_shape.append(jax.ShapeDtypeStruct((t // seq, d, seq // d, SLOT_WIDTH), BF16))
        scratch = [pltpu.VMEM((3, (N_DIL - 1) * SLOT_WIDTH // LANES, tm, LANES), F32)]
    return pl.pallas_call(
        functools.partial(_mix_in_body, prompt=prompt),
        grid=(t // tm,),
        in_specs=[row(D_MODEL), _const_spec((1, D_MODEL)), _const_spec((D_MODEL, IN_WIDTH)),
                  _const_spec((1, ATTN_WIDTH)), _const_spec((1, ATTN_WIDTH)),
                  _const_spec((MXU_DIM, MXU_DIM))],
        out_specs=out_specs,
        out_shape=out_shape,
        scratch_shapes=scratch,
        compiler_params=_params("arbitrary"),
        name="mix_in",
    )(x, norm_w, w_in, q_gain, k_gain, seg_ones)


def _cmul(ar, ai, br, bi):
    return ar * br - ai * bi, ar * bi + ai * br


def _ssm_prep_body(lrow_re_ref, lrow_im_ref, lcol_re_ref, lcol_im_ref, ldt_ref, bre_ref, bim_ref,
                   ct_re_ref, ct_im_ref, abar_ref, a8_ref, bbar_ref, toep_ref, win_ref, wout_ref):
    n = SSM_CHUNK
    dt = jnp.exp(ldt_ref[...])

    def discretise(lam_re, lam_im):
        lr = jnp.minimum(lam_re, -1e-4)
        mag = jnp.exp(lr * dt)
        return lr, lam_im, mag * jnp.cos(lam_im * dt), mag * jnp.sin(lam_im * dt)

    def powers(ar, ai):
        out = [(jnp.ones_like(ar), jnp.zeros_like(ai))]
        for _ in range(n):
            out.append(_cmul(*out[-1], ar, ai))
        return out

    lr, li, ar, ai = discretise(lrow_re_ref[...], lrow_im_ref[...])
    den = lr * lr + li * li
    fr = ((ar - 1.0) * lr + ai * li) / den
    fi = (ai * lr - (ar - 1.0) * li) / den
    bbr, bbi = _cmul(fr, fi, bre_ref[...], bim_ref[...])
    row_pow = powers(ar, ai)
    abar_ref[0], abar_ref[1] = ar, ai
    a8_ref[0], a8_ref[1] = row_pow[n]
    bbar_ref[0], bbar_ref[1] = bbr, bbi

    def pair_halves(x):
        x = x.reshape((x.shape[0] // 2, 2) + x.shape[1:])
        return x[:, 0], x[:, 1]

    toep_ref[...] = jnp.zeros_like(toep_ref)
    win_ref[...] = jnp.zeros_like(win_ref)
    wout_ref[...] = jnp.zeros_like(wout_ref)
    half = n * SSM_GROUP

    for s in range(n):
        w_re, w_im = _cmul(*row_pow[n - 1 - s], bbr, bbi)
        rows_e = slice(s * SSM_GROUP, (s + 1) * SSM_GROUP)
        rows_o = slice(half + s * SSM_GROUP, half + (s + 1) * SSM_GROUP)
        for part, x in enumerate((w_re, w_im)):
            x_e, x_o = pair_halves(x.astype(BF16))
            win_ref[:, rows_e, (2 * part) * SSM_STATE:(2 * part + 1) * SSM_STATE] = x_e
            win_ref[:, rows_o, (2 * part + 1) * SSM_STATE:(2 * part + 2) * SSM_STATE] = x_o

    _, _, ar_c, ai_c = discretise(lcol_re_ref[...], lcol_im_ref[...])
    col_pow = powers(ar_c, ai_c)
    lane_t = lax.broadcasted_iota(jnp.int32, (1, 1, LANES), 2) // SSM_GROUP

    def spread(first):
        re = im = jnp.zeros((1, 1, LANES), F32)
        for t in range(n):
            re = jnp.where(lane_t == t, col_pow[first + t][0], re)
            im = jnp.where(lane_t == t, col_pow[first + t][1], im)
        return re, im

    ct = (ct_re_ref[...], ct_im_ref[...])
    m0_re, m0_im = _cmul(*ct, *spread(0))
    m1_re, m1_im = _cmul(*ct, *spread(1))

    for part, x in enumerate((m1_re, -m1_im)):
        x_e, x_o = pair_halves(x.astype(BF16))
        wout_ref[:, (2 * part) * SSM_STATE:(2 * part + 1) * SSM_STATE, :half] = x_e
        wout_ref[:, (2 * part + 1) * SSM_STATE:(2 * part + 2) * SSM_STATE, half:] = x_o

    nn = (((2,), (1,)), ((0,), (0,)))
    hp = lax.Precision.HIGHEST
    kern = (lax.dot_general(bbr, m0_re, nn, precision=hp, preferred_element_type=F32)
            - lax.dot_general(bbi, m0_im, nn, precision=hp, preferred_element_type=F32))
    lane = lax.broadcasted_iota(jnp.int32, (1, 1, LANES), 2)
    for s in range(n):
        shifted = kern if s == 0 else jnp.where(lane >= s * SSM_GROUP, pltpu.roll(kern, s * SSM_GROUP, 2), 0.0)
        x_e, x_o = pair_halves(shifted.astype(BF16))
        toep_ref[:, s * SSM_GROUP:(s + 1) * SSM_GROUP, :half] = x_e
        toep_ref[:, half + s * SSM_GROUP:half + (s + 1) * SSM_GROUP, half:] = x_o


def _ssm_prep(lam_re, lam_im, log_dt, b_re_t, b_im_t, c_re, c_im, groups_per_step=16):
    g, p, c = SSM_GROUPS, SSM_STATE, SSM_GROUP
    gb = groups_per_step

    def spec(*tail):
        return pl.BlockSpec((gb,) + tail, lambda i: (i,) + (0,) * len(tail))

    def stacked(*tail):
        return pl.BlockSpec((2, gb) + tail, lambda i: (0, i) + (0,) * len(tail))

    pair_spec = pl.BlockSpec((gb // 2, MXU_DIM, MXU_DIM), lambda i: (i, 0, 0))
    pair_shape = jax.ShapeDtypeStruct((SSM_PAIRS, MXU_DIM, MXU_DIM), BF16)
    ct_re = jnp.tile(c_re.transpose(0, 2, 1), (1, 1, SSM_CHUNK))
    ct_im = jnp.tile(c_im.transpose(0, 2, 1), (1, 1, SSM_CHUNK))
    return pl.pallas_call(
        _ssm_prep_body,
        grid=(g // gb,),
        in_specs=[spec(1, p), spec(1, p), spec(p, 1), spec(p, 1), spec(1, 1), spec(c, p), spec(c, p),
                  spec(p, LANES), spec(p, LANES)],
        out_specs=[stacked(1, p), stacked(1, p), stacked(c, p), pair_spec, pair_spec, pair_spec],
        out_shape=[jax.ShapeDtypeStruct((2, g, 1, p), F32), jax.ShapeDtypeStruct((2, g, 1, p), F32),
                   jax.ShapeDtypeStruct((2, g, c, p), F32), pair_shape, pair_shape, pair_shape],
        compiler_params=_params("parallel"),
        name="ssm_prep",
    )(lam_re.reshape(g, 1, p), lam_im.reshape(g, 1, p), lam_re.reshape(g, p, 1), lam_im.reshape(g, p, 1),
      log_dt.reshape(g, 1, 1), b_re_t, b_im_t, ct_re, ct_im)


def _ssm_body(lhs_ref, toep_ref, win_ref, wout_ref, a8_ref, yfl_ref, hfin_ref, st_ref, carry_ref, *, rows):
    i = pl.program_id(1)

    @pl.when(i == 0)
    def _():
        carry_ref[...] = jnp.zeros_like(carry_ref)

    def state_in(r, _):
        b = jnp.dot(lhs_ref[r], win_ref[r], preferred_element_type=F32)
        st_ref[0, pl.ds(r, rows, stride=SSM_ROW_PITCH), :] = b[:, :LANES]
        st_ref[1, pl.ds(r, rows, stride=SSM_ROW_PITCH), :] = b[:, LANES:]
        return 0

    lax.fori_loop(0, SSM_PAIRS, state_in, 0, unroll=4)

    a_re = a8_ref[0]
    a_im = a8_ref[1]

    def step(j, h):
        h_re, h_im = h
        base = pl.multiple_of(j * SSM_ROW_PITCH, SUBLANES)
        n_re = a_re * h_re - a_im * h_im + st_ref[0, pl.ds(base, SSM_PAIRS), :]
        n_im = a_re * h_im + a_im * h_re + st_ref[1, pl.ds(base, SSM_PAIRS), :]
        st_ref[0, pl.ds(base, SSM_PAIRS), :] = h_re
        st_ref[1, pl.ds(base, SSM_PAIRS), :] = h_im
        return n_re, n_im

    h_re, h_im = lax.fori_loop(0, rows, step, (carry_ref[0], carry_ref[1]), unroll=4)
    carry_ref[0] = h_re
    carry_ref[1] = h_im
    hfin_ref[0] = h_re
    hfin_ref[1] = h_im

    def chunk_out(r, _):
        hcat = jnp.concatenate([st_ref[0, pl.ds(r, rows, stride=SSM_ROW_PITCH), :],
                                st_ref[1, pl.ds(r, rows, stride=SSM_ROW_PITCH), :]], axis=1).astype(BF16)
        yfl_ref[r] = (jnp.dot(lhs_ref[r], toep_ref[r], preferred_element_type=F32)
                      + jnp.dot(hcat, wout_ref[r], preferred_element_type=F32))
        return 0

    lax.fori_loop(0, SSM_PAIRS, chunk_out, 0, unroll=4)


def _ssm_prompt(lhs, toep, win, wout, a8, batch, seq, tile):
    rows = tile // SSM_CHUNK
    n_tiles = seq // tile
    tok = pl.BlockSpec((SSM_PAIRS, rows, MXU_DIM), lambda b, i: (0, b * n_tiles + i, 0))
    pair_w = _const_spec((SSM_PAIRS, MXU_DIM, MXU_DIM))
    return pl.pallas_call(
        functools.partial(_ssm_body, rows=rows),
        grid=(batch, n_tiles),
        in_specs=[tok, pair_w, pair_w, pair_w, _const_spec((2, SSM_PAIRS, LANES))],
        out_specs=[tok, pl.BlockSpec((None, 2, SSM_PAIRS, LANES), lambda b, i: (b, 0, 0, 0))],
        out_shape=[jax.ShapeDtypeStruct((SSM_PAIRS, batch * seq // SSM_CHUNK, MXU_DIM), F32),
                   jax.ShapeDtypeStruct((batch, 2, SSM_PAIRS, LANES), F32)],
        scratch_shapes=[pltpu.VMEM((2, rows * SSM_ROW_PITCH, LANES), F32),
                        pltpu.VMEM((2, SSM_PAIRS, LANES), F32)],
        compiler_params=_params("parallel", "arbitrary"),
        name="ssm_prompt",
    )(lhs, toep, win, wout, a8)


def _ssm_step_body(u_ref, hre_ref, him_ref, abar_ref, bbar_ref, cre_ref, cim_ref,
                   y_ref, ore_ref, oim_ref):
    hp = lax.Precision.HIGHEST
    nt = (((1,), (1,)), ((), ()))
    groups_per_block = LANES // SSM_GROUP
    for g in range(SSM_GROUPS):
        blk = g // groups_per_block
        ch = slice((g % groups_per_block) * SSM_GROUP, (g % groups_per_block + 1) * SSM_GROUP)
        st = slice(g * SSM_STATE, (g + 1) * SSM_STATE)
        ug = u_ref[blk, :, ch]
        bu_re = jnp.dot(ug, bbar_ref[0, g], precision=hp, preferred_element_type=F32)
        bu_im = jnp.dot(ug, bbar_ref[1, g], precision=hp, preferred_element_type=F32)
        a_re = abar_ref[0, g]
        a_im = abar_ref[1, g]
        h_re = hre_ref[:, st]
        h_im = him_ref[:, st]
        n_re = a_re * h_re - a_im * h_im + bu_re
        n_im = a_re * h_im + a_im * h_re + bu_im
        ore_ref[:, st] = n_re
        oim_ref[:, st] = n_im
        y_ref[blk, :, ch] = (lax.dot_general(n_re, cre_ref[g], nt, precision=hp, preferred_element_type=F32)
                             - lax.dot_general(n_im, cim_ref[g], nt, precision=hp, preferred_element_type=F32))


def _ssm_step(u, h_re, h_im, abar, bbar, c_re, c_im):
    b = u.shape[1]
    width = SSM_GROUPS * SSM_STATE
    return pl.pallas_call(
        _ssm_step_body,
        out_shape=[jax.ShapeDtypeStruct((LANE_BLOCKS, b, LANES), F32),
                   jax.ShapeDtypeStruct((b, width), F32),
                   jax.ShapeDtypeStruct((b, width), F32)],
        compiler_params=pltpu.CompilerParams(vmem_limit_bytes=VMEM_LIMIT_BYTES),
        name="ssm_step",
    )(u, h_re, h_im, abar, bbar, c_re, c_im)


def _attn_prompt_body(*refs, part_rows):
    per_group = 5
    ins = refs[:per_group * N_DIL]
    o_ref, m_ref, l_ref, acc_ref, bias_ref = refs[per_group * N_DIL:]
    part = pl.program_id(1)
    tq = KEYS_BACK
    row = lax.broadcasted_iota(jnp.int32, (tq, 2 * tq), 0)
    col = lax.broadcasted_iota(jnp.int32, (tq, 2 * tq), 1)
    back = row + tq - col
    in_window = (back >= 0) & (back <= KEYS_BACK)
    first_head = lax.broadcasted_iota(jnp.int32, (1, LANES), 1) < HEAD_DIM
    nt = (((1,), (1,)), ((), ()))

    order = tuple(reversed(range(N_DIL)))
    ones = jnp.ones((2 * tq, LANES), BF16)
    for g in order:
        q_ref, k_ref, kb_ref, v_ref, vb_ref = ins[per_group * g:per_group * (g + 1)]
        d = DIL_RATES[g]
        blocks_per_residue = part_rows // d // tq
        dist = (back * d).astype(F32)
        for h in range(HEADS):
            bias = jnp.where(in_window, -_slope(g, h) * dist, -jnp.inf)
            bias_ref[h] = bias
            bias_ref[HEADS + h] = jnp.where(col >= tq, bias, -jnp.inf)

        def block(mi, _, g=g, d=d, q_ref=q_ref, k_ref=k_ref, kb_ref=kb_ref, v_ref=v_ref, vb_ref=vb_ref,
                  blocks_per_residue=blocks_per_residue):
            residue = mi // blocks_per_residue
            n = mi % blocks_per_residue
            cur = pl.multiple_of(n * tq, tq)
            prev = pl.multiple_of(jnp.maximum(n - 1, 0) * tq, tq)
            bias_at = jnp.where((n == 0) & (part == 0), HEADS, 0)
            token0 = residue + d * tq * n
            rows = pl.ds(pl.multiple_of(token0, tq), tq) if d == 1 else pl.ds(token0, tq, stride=d)
            pairs = range(HEADS // 2)
            first, final = g == order[0], g == order[-1]
            old = None if first else [(m_ref[pair, rows, :], l_ref[pair, rows, :], acc_ref[pair, rows, :])
                                      for pair in pairs]
            new = []
            for pair in pairs:
                lanes = slice(pair * LANES, (pair + 1) * LANES)
                qp = q_ref[residue, pl.ds(cur, tq), lanes]
                k_prev = jnp.where(n == 0, kb_ref[residue, :, lanes], k_ref[residue, pl.ds(prev, tq), lanes])
                v_prev = jnp.where(n == 0, vb_ref[residue, :, lanes], v_ref[residue, pl.ds(prev, tq), lanes])
                kp = jnp.concatenate([k_prev, k_ref[residue, pl.ds(cur, tq), lanes]], axis=0)
                vp = jnp.concatenate([v_prev, v_ref[residue, pl.ds(cur, tq), lanes]], axis=0)
                vp = jnp.concatenate([vp, ones], axis=1)
                stats = []
                for e in range(2):
                    qm = jnp.where(first_head if e == 0 else ~first_head, qp, jnp.zeros_like(qp))
                    s = lax.dot_general(qm, kp, nt, preferred_element_type=F32)
                    s = s + bias_ref[bias_at + 2 * pair + e]
                    m = jnp.max(s, axis=-1, keepdims=True)
                    p = jnp.exp(s - m)
                    pv = jnp.dot(p.astype(BF16), vp, preferred_element_type=F32)
                    stats.append((m, pv[:, LANES:], pv[:, :LANES]))
                m_new, l_new, acc_new = (jnp.where(first_head, a, b) for a, b in zip(*stats))
                if not first:
                    m_old, l_old, acc_old = old[pair]
                    m_tot = jnp.maximum(m_old, m_new)
                    w_old = jnp.exp(m_old - m_tot)
                    w_new = jnp.exp(m_new - m_tot)
                    l_new = w_old * l_old + w_new * l_new
                    acc_new = w_old * acc_old + w_new * acc_new
                    m_new = m_tot
                new.append((m_new, l_new, acc_new))
            for pair, (m_new, l_new, acc_new) in zip(pairs, new):
                if not final:
                    m_ref[pair, rows, :] = m_new
                    l_ref[pair, rows, :] = l_new
                    acc_ref[pair, rows, :] = acc_new
                else:
                    o_ref[pair, rows, :] = acc_new / l_new
            return 0

        lax.fori_loop(0, part_rows // tq, block, 0, unroll=2)


def _attn_prompt(qkv, batch, seq):
    slabs = SLOT_WIDTH // LANES
    tq = KEYS_BACK
    part_rows = DIL_RATES[-1] * tq
    parts = seq // part_rows
    args, specs = [], []
    for g, d in enumerate(DIL_RATES):
        rows = part_rows // d
        cur = pl.BlockSpec((None, d, rows, SLOT_WIDTH), lambda b, p: (b, 0, p, 0))
        before = pl.BlockSpec((None, d, tq, SLOT_WIDTH),
                              functools.partial(lambda b, p, step: (b, 0, jnp.maximum(p * step - 1, 0), 0),
                                                step=rows // tq))
        q, k, v = qkv[g], qkv[N_DIL + g], qkv[2 * N_DIL + g]
        args += [q, k, k, v, v]
        specs += [cur, cur, before, cur, before]
    running = pltpu.VMEM((slabs, part_rows, LANES), F32)
    return pl.pallas_call(
        functools.partial(_attn_prompt_body, part_rows=part_rows),
        grid=(batch, parts),
        in_specs=specs,
        out_specs=pl.BlockSpec((slabs, part_rows, LANES), lambda b, p: (0, b * parts + p, 0)),
        out_shape=jax.ShapeDtypeStruct((slabs, batch * seq, LANES), F32),
        scratch_shapes=[running, running, running, pltpu.VMEM((2 * HEADS, tq, 2 * tq), F32)],
        compiler_params=_params("parallel", "arbitrary"),
        name="attn_prompt",
    )(*args)


def _as_column(row_vec):
    n = row_vec.shape[1]
    eye = lax.broadcasted_iota(jnp.int32, (n, n), 0) == lax.broadcasted_iota(jnp.int32, (n, n), 1)
    return jnp.sum(jnp.where(eye, row_vec, 0.0), axis=1, keepdims=True)


def _sample_window_group(g, q_row, k_row, v_row, c_ref, n_ref):
    head_row = lax.broadcasted_iota(jnp.int32, (SUBLANES, SLOT_WIDTH), 0)
    own_head = lax.broadcasted_iota(jnp.int32, (SUBLANES, SLOT_WIDTH), 1) // HEAD_DIM == head_row
    head_col = lax.broadcasted_iota(jnp.int32, (SUBLANES, 1), 0)
    w = DIL_WINDOWS[g]
    d = DIL_RATES[g]
    cols = slice(g * SLOT_WIDTH, (g + 1) * SLOT_WIDTH)
    q_g, k_new, v_new = q_row[:, cols], k_row[:, cols], v_row[:, cols]
    kt = c_ref[0].reshape(SLOT_WIDTH, w)
    vt = c_ref[1].reshape(SLOT_WIDTH, w)
    q_heads = jnp.where(own_head, q_g, 0.0)
    s = jnp.dot(q_heads.astype(BF16), kt.astype(BF16), preferred_element_type=F32)
    dist = w - lax.broadcasted_iota(jnp.int32, (1, w), 1)
    slope = functools.reduce(lambda acc, h: jnp.where(head_col == h, _slope(g, h), acc), range(HEADS), 0.0)
    s = jnp.where((dist & (d - 1)) == 0, s - slope * dist.astype(F32), -jnp.inf)
    s_new = jnp.sum(q_heads * k_new, axis=1, keepdims=True)
    m = jnp.maximum(jnp.max(s, axis=1, keepdims=True), s_new)
    p = jnp.exp(s - m)
    p_new = jnp.exp(s_new - m)
    den = jnp.sum(p, axis=1, keepdims=True) + p_new
    pv = lax.dot_general(p.astype(BF16), vt.astype(BF16), (((1,), (1,)), ((), ())),
                         preferred_element_type=F32)
    o_heads = (pv + p_new * v_new) / den
    last = lax.broadcasted_iota(jnp.int32, (SLOT_WIDTH, w), 1) == w - 1
    shape = (HEADS, HEAD_DIM, w)
    n_ref[0] = jnp.where(last, _as_column(k_new), pltpu.roll(kt, w - 1, 1)).reshape(shape)
    n_ref[1] = jnp.where(last, _as_column(v_new), pltpu.roll(vt, w - 1, 1)).reshape(shape)
    return (jnp.sum(jnp.where(own_head, o_heads, 0.0), axis=0, keepdims=True),
            jnp.sum(jnp.where(own_head, m + jnp.log(den), 0.0), axis=0, keepdims=True))


def _mix_groups(outs, lses):
    top = functools.reduce(jnp.maximum, lses)
    wts = [jnp.exp(l - top) for l in lses]
    return sum(w_g * o_g for w_g, o_g in zip(wts, outs)) / sum(wts)


def _ffn_windows_body(x_ref, nw_ref, wg_ref, wu_ref, wd_ref, q_ref, k_ref, v_ref, c0_ref, c1_ref, c2_ref,
                      o_ref, ob_ref, n0_ref, n1_ref, n2_ref):
    b = pl.program_id(0)
    rows = [r[pl.ds(b, 1), :] for r in (q_ref, k_ref, v_ref)]
    windows = ((c0_ref, n0_ref), (c1_ref, n1_ref), (c2_ref, n2_ref))
    results = {}

    def job(g):
        def run():
            results[g] = _sample_window_group(g, *rows, *windows[g])
        return run

    _ffn_body(x_ref, nw_ref, wg_ref, wu_ref, wd_ref, o_ref, side_jobs=[job(g) for g in reversed(range(N_DIL))])
    outs, lses = zip(*(results[g] for g in range(N_DIL)))
    ob_ref[...] = _mix_groups(outs, lses)


def _ffn_with_sample_windows(x, norm_w, wg, wu, wd, q, k, v, caches):
    t = x.shape[0]
    b = q.shape[0]
    assert t % b == 0 and (t // b) % SUBLANES == 0
    tm = t // b
    row = pl.BlockSpec((tm, D_MODEL), lambda i: (i, 0))
    full = _const_spec((b, ATTN_WIDTH))
    win = [pl.BlockSpec((None, 2, HEADS, HEAD_DIM, w), lambda i: (i, 0, 0, 0, 0)) for w in DIL_WINDOWS]
    outs = pl.pallas_call(
        _ffn_windows_body,
        grid=(b,),
        in_specs=[row, _const_spec((1, D_MODEL)), _const_spec((D_MODEL, D_FF)),
                  _const_spec((D_MODEL, D_FF)), _const_spec((D_FF, D_MODEL)), full, full, full] + win,
        out_specs=[row, pl.BlockSpec((None, 1, SLOT_WIDTH), lambda i: (i, 0, 0))] + win,
        out_shape=([jax.ShapeDtypeStruct((t, D_MODEL), F32), jax.ShapeDtypeStruct((b, 1, SLOT_WIDTH), F32)]
                   + [jax.ShapeDtypeStruct(c.shape, F32) for c in caches]),
        compiler_params=_params("parallel"),
        name="ffn_windows",
    )(x, norm_w, wg, wu, wd, q, k, v, *caches)
    return outs[0], outs[1].reshape(b, SLOT_WIDTH), outs[2:]


def _kv_window_body(*refs):
    k_refs, v_refs, out_refs = refs[:N_DIL], refs[N_DIL:2 * N_DIL], refs[2 * N_DIL:]
    for g, o_ref in enumerate(out_refs):
        for t, x_ref in enumerate((k_refs[g], v_refs[g])):
            for pair in range(HEADS // 2):
                xt = x_ref[:, pair * LANES:(pair + 1) * LANES].T
                o_ref[t, 2 * pair] = xt[:HEAD_DIM]
                o_ref[t, 2 * pair + 1] = xt[HEAD_DIM:]


def _kv_windows(k_tail, v_tail):
    batch, tail, _ = k_tail.shape
    keeps = [min(w, tail) for w in DIL_WINDOWS]
    assert all(tail % w == 0 for w in keeps)
    src = [pl.BlockSpec((None, w, SLOT_WIDTH), functools.partial(lambda b, at, g: (b, at, g), at=tail // w - 1, g=g))
           for g, w in enumerate(keeps)]
    return pl.pallas_call(
        _kv_window_body,
        grid=(batch,),
        in_specs=src + src,
        out_specs=[pl.BlockSpec((None, 2, HEADS, HEAD_DIM, w), lambda b: (b, 0, 0, 0, 0)) for w in keeps],
        out_shape=[jax.ShapeDtypeStruct((batch, 2, HEADS, HEAD_DIM, w), F32) for w in keeps],
        compiler_params=_params("parallel"),
        name="kv_windows",
    )(*([k_tail] * N_DIL + [v_tail] * N_DIL))


def _mix_out_body(x_ref, y_ref, u_ref, d_ref, ob_ref,
                  ga_ref, gb_ref, wglu_ref, wpa_ref, wpb_ref, wout_ref, out_ref, *unfold):
    tm = x_ref.shape[0]
    if unfold:
        y_ref, folded_ref = unfold[0], y_ref
        groups_per_block = LANES // SSM_GROUP
    pieces = MIX_OUT_PIECES if unfold else 1
    rows = tm // pieces
    for piece in range(pieces):
        at = slice(piece * rows, (piece + 1) * rows)
        if unfold:
            chunks = rows // SSM_CHUNK
            chunk_at = slice(piece * chunks, (piece + 1) * chunks)
            for blk in range(LANE_BLOCKS):
                groups = [folded_ref[blk * (groups_per_block // 2) + q_i // 2, chunk_at,
                                     (q_i % 2) * LANES:(q_i % 2 + 1) * LANES] for q_i in range(groups_per_block)]
                for t, rows_t in enumerate(_transpose_granules(groups)):
                    y_ref[blk, pl.ds(piece * rows + t, chunks, stride=SSM_CHUNK), :] = rows_t
        y_raw = jnp.concatenate([y_ref[blk, at, :] for blk in range(LANE_BLOCKS)], axis=1)
        u = jnp.concatenate([u_ref[blk, at, :] for blk in range(LANE_BLOCKS)], axis=1)
        y = jax.nn.gelu(y_raw + d_ref[...] * u)
        yb = y.astype(BF16)
        y_a = y * jax.nn.sigmoid(jnp.dot(yb, wglu_ref[...], preferred_element_type=F32))
        branch_a = jnp.dot(y_a.astype(BF16), wpa_ref[...], preferred_element_type=F32)

        o_b = jnp.concatenate([ob_ref[s, at, :] for s in range(SLOT_WIDTH // LANES)], axis=1)
        branch_b = jnp.dot(o_b.astype(BF16), wpb_ref[...], preferred_element_type=F32)

        merged = ga_ref[at, :] * branch_a + gb_ref[at, :] * branch_b
        out_ref[at, :] = x_ref[at, :] + jnp.dot(merged.astype(BF16), wout_ref[...], preferred_element_type=F32)


def _mix_out(x, y, u, ssm_d, o_b, ga, gb, w_glu, w_pa, w_pb, w_out, tm, folded_y):
    t = x.shape[0]
    wide = pl.BlockSpec((tm, D_MODEL), lambda i: (i, 0))
    blocked = _lane_blocked_spec(tm, lambda i: i)
    y_spec = pl.BlockSpec((SSM_PAIRS, tm // SSM_CHUNK, MXU_DIM), lambda i: (0, i, 0)) if folded_y else blocked
    scratch = [pltpu.VMEM((LANE_BLOCKS, tm, LANES), F32)] if folded_y else []
    slot = pl.BlockSpec((SLOT_WIDTH // LANES, tm, LANES), lambda i: (0, i, 0))
    return pl.pallas_call(
        _mix_out_body,
        grid=(t // tm,),
        in_specs=[wide, y_spec, blocked, _const_spec((1, D_MODEL)), slot, wide, wide,
                  _const_spec((D_MODEL, D_MODEL)), _const_spec((D_MODEL, D_MODEL)),
                  _const_spec((SLOT_WIDTH, D_MODEL)), _const_spec((D_MODEL, D_MODEL))],
        out_specs=wide,
        out_shape=jax.ShapeDtypeStruct((t, D_MODEL), F32),
        scratch_shapes=scratch,
        compiler_params=_params("parallel"),
        name="mix_out",
    )(x, y, u, ssm_d, o_b, ga, gb, w_glu, w_pa, w_pb, w_out)


def _head_segment_ones():
    head = jnp.arange(MXU_DIM) // HEAD_DIM
    return (head[:, None] == head[None, :]).astype(BF16)


def kernel(x_prompt, x_sample, state_ssm_re, state_ssm_im, cache_kv_w128, cache_kv_w512, cache_kv_w2048, ffn1_norm, ffn1_w_gate, ffn1_w_up, ffn1_w_down, mix_norm, w_in, ssm_lambda_re, ssm_lambda_im, ssm_b_re, ssm_b_im, ssm_c_re, ssm_c_im, ssm_d, ssm_log_dt, w_glu, q_gain, k_gain, w_proj_a, w_proj_b, w_out, ffn2_norm, ffn2_w_gate, ffn2_w_up, ffn2_w_down):
    depth = ffn1_norm.shape[0]
    assert depth == 1, "single-layer step"
    batch, seq, _ = x_prompt.shape
    dec_batch, dec_seq, _ = x_sample.shape
    assert dec_seq == 1 and seq % (DIL_RATES[-1] * KEYS_BACK) == 0
    layer = 0
    bf = lambda w: w[layer].astype(BF16)
    vec = lambda w: w[layer][None]
    ffn1 = (vec(ffn1_norm), bf(ffn1_w_gate), bf(ffn1_w_up), bf(ffn1_w_down))
    ffn2 = (vec(ffn2_norm), bf(ffn2_w_gate), bf(ffn2_w_up), bf(ffn2_w_down))
    mix_norm, ssm_d = vec(mix_norm), vec(ssm_d)
    w_in_b, w_glu_b, w_pa_b, w_pb_b, w_out_b = bf(w_in), bf(w_glu), bf(w_proj_a), bf(w_proj_b), bf(w_out)
    q_gain_t = jnp.tile(vec(q_gain), (1, N_DIL * HEADS))
    k_gain_t = jnp.tile(vec(k_gain), (1, N_DIL * HEADS))
    seg_ones = _head_segment_ones()

    abar, a8, bbar, toep, win, wout = _ssm_prep(
        ssm_lambda_re[layer], ssm_lambda_im[layer], ssm_log_dt[layer],
        ssm_b_re[layer].transpose(0, 2, 1), ssm_b_im[layer].transpose(0, 2, 1),
        ssm_c_re[layer], ssm_c_im[layer])
    a8 = a8.reshape(2, SSM_PAIRS, LANES)

    def front(x, tm_ffn, tm, seq=None):
        x1 = _ffn(x, *ffn1, tm_ffn)
        return (x1,) + tuple(_mix_in(x1, mix_norm, w_in_b, q_gain_t, k_gain_t, seg_ones, tm, seq))

    def mix_out(x1, y, u, o_b, ga, gb, tm, folded_y):
        return _mix_out(x1, y, u, ssm_d, o_b, ga, gb, w_glu_b, w_pa_b, w_pb_b, w_out_b, tm, folded_y)

    xs = x_sample.reshape(dec_batch, D_MODEL)
    xs1, us, qs, ks, vs, gas, gbs = front(xs, dec_batch, dec_batch)
    caches = [c[layer].transpose(0, 2, 3, 4, 1) for c in (cache_kv_w128, cache_kv_w512, cache_kv_w2048)]

    tm = 512
    tm_ffn = 1024
    xp = x_prompt.reshape(batch * seq, D_MODEL)
    x1, u, k_tail, v_tail, ga, gb, u_folded, *qkv = front(xp, tm_ffn, tm, seq)
    y, h_fin = _ssm_prompt(u_folded, toep, win, wout, a8, batch, seq, tile=2048)
    o_b = _attn_prompt(qkv, batch, seq)
    x2 = mix_out(x1, y, u, o_b, ga, gb, tm, folded_y=True)
    yp, o_bs, new_caches = _ffn_with_sample_windows(x2, *ffn2, qs, ks, vs, caches)
    yp = yp.reshape(batch, seq, D_MODEL)
    h_fin = h_fin.reshape(batch, 2, SSM_GROUPS, SSM_STATE)
    p_re = h_fin[None, :, 0]
    p_im = h_fin[None, :, 1]
    p_kv = [c.transpose(0, 4, 1, 2, 3)[None] for c in _kv_windows(k_tail, v_tail)]

    width = SSM_GROUPS * SSM_STATE
    ys_ssm, s_re, s_im = _ssm_step(us, state_ssm_re[layer].reshape(dec_batch, width),
                                   state_ssm_im[layer].reshape(dec_batch, width),
                                   abar, bbar, ssm_c_re[layer], ssm_c_im[layer])
    o_bs = o_bs.reshape(dec_batch, SLOT_WIDTH // LANES, LANES).transpose(1, 0, 2)
    xs2 = mix_out(xs1, ys_ssm, us, o_bs, gas, gbs, dec_batch, folded_y=False)
    ys = _ffn(xs2, *ffn2, dec_batch)
    ys = ys.reshape(dec_batch, 1, D_MODEL)
    s_re = s_re.reshape(1, dec_batch, SSM_GROUPS, SSM_STATE)
    s_im = s_im.reshape(1, dec_batch, SSM_GROUPS, SSM_STATE)
    s_kv = [c.transpose(0, 4, 1, 2, 3)[None] for c in new_caches]

    return (yp, ys, p_re, p_im, p_kv[0], p_kv[1], p_kv[2], s_re, s_im, s_kv[0], s_kv[1], s_kv[2])
```

```python
import functools

import jax
import jax.numpy as jnp
from jax import lax
from jax.experimental import pallas as pl
from jax.experimental.pallas import tpu as pltpu

F32 = jnp.float32
BF16 = jnp.bfloat16

D_MODEL = 1024
SSM_GROUP = 16
SSM_GROUPS = 64
SSM_STATE = 64
SSM_PAIRS = SSM_GROUPS // 2
HEAD_DIM = 64
HEADS = 4
DIL_WINDOWS = (128, 512, 2048)
DIL_RATES = (1, 4, 16)
N_DIL = 3
KEYS_BACK = 128
ATTN_WIDTH = N_DIL * HEADS * HEAD_DIM
SLOT_WIDTH = HEADS * HEAD_DIM
D_FF = 2816
RMS_EPS = 1e-6
ALIBI_MAX_EXP = 8.0
LOG2_E = 1.4426950408889634
IN_WIDTH = D_MODEL + 3 * ATTN_WIDTH + 2 * D_MODEL

LANES = 128
SUBLANES = 8
MXU_DIM = 256
VMEM_LIMIT_BYTES = 56 * 1024 * 1024

LANE_BLOCKS = D_MODEL // LANES

SSM_CHUNK = SUBLANES
SSM_ROW_PITCH = 40
FF_CHUNKS = ((0, 1024), (1024, 2048), (2048, 2816))
MIX_OUT_PIECES = 2


def _slope(group, head):
    return 2.0 ** (-ALIBI_MAX_EXP * (group * HEADS + head + 1) / (N_DIL * HEADS))


def _const_spec(shape):
    zeros = (0,) * len(shape)
    return pl.BlockSpec(shape, lambda *_: zeros, pipeline_mode=pl.Buffered(1))


def _lane_blocked_spec(rows, row_block):
    return pl.BlockSpec((LANE_BLOCKS, rows, LANES), lambda *idx: (0, row_block(*idx), 0))


def _params(*semantics):
    return pltpu.CompilerParams(dimension_semantics=semantics, vmem_limit_bytes=VMEM_LIMIT_BYTES)


def _rms(x, w):
    return x * lax.rsqrt(jnp.mean(x * x, axis=-1, keepdims=True) + RMS_EPS) * w


def _split_bf16(x, terms):
    parts = []
    for _ in range(terms):
        p = x.astype(BF16)
        parts.append(p)
        x = x - p.astype(F32)
    return parts


def _ffn_tile(x_ref, nw_ref, w_refs, o_ref, side_jobs=()):
    x = x_ref[...]
    h = _rms(x, nw_ref[...]).astype(BF16)
    acc = jnp.zeros_like(x)
    side_jobs = list(side_jobs)
    for c in range(len(FF_CHUNKS)):
        wg_ref, wu_ref, wd_ref = w_refs[3 * c:3 * c + 3]
        g = jnp.dot(h, wg_ref[...], preferred_element_type=F32)
        u = jnp.dot(h, wu_ref[...], preferred_element_type=F32)
        a = (jax.nn.silu(g) * u).astype(BF16)
        acc = acc + jnp.dot(a, wd_ref[...], preferred_element_type=F32)
        if side_jobs:
            side_jobs.pop(0)()
    for job in side_jobs:
        job()
    o_ref[...] = x + 0.5 * acc


def _ffn_body(x_ref, nw_ref, *refs):
    _ffn_tile(x_ref, nw_ref, refs[:-1], refs[-1])


def _ffn_weight_specs():
    specs = []
    for lo, hi in FF_CHUNKS:
        specs += [_const_spec((D_MODEL, hi - lo)), _const_spec((D_MODEL, hi - lo)), _const_spec((hi - lo, D_MODEL))]
    return specs


def _ffn_chunk_weights(w_gate, w_up, w_down):
    out = []
    for lo, hi in FF_CHUNKS:
        out += [w_gate[:, lo:hi].astype(BF16), w_up[:, lo:hi].astype(BF16), w_down[lo:hi, :].astype(BF16)]
    return tuple(out)


def _ffn(x, norm_w, weights, tm):
    t = x.shape[0]
    row = pl.BlockSpec((tm, D_MODEL), lambda i: (i, 0))
    return pl.pallas_call(
        _ffn_body,
        grid=(t // tm,),
        in_specs=[row, _const_spec((1, D_MODEL))] + _ffn_weight_specs(),
        out_specs=row,
        out_shape=jax.ShapeDtypeStruct((t, D_MODEL), F32),
        compiler_params=_params("parallel"),
        name="ffn",
    )(x, norm_w, *weights)


def _head_norm(x, gain, seg_ones):
    parts = _split_bf16(x * x, 2)
    blocks = []
    for lo in range(0, x.shape[1], MXU_DIM):
        blocks.append(sum(jnp.dot(p[:, lo:lo + MXU_DIM], seg_ones, preferred_element_type=F32) for p in parts))
    ss = jnp.concatenate(blocks, axis=1)
    return x * lax.rsqrt(ss * (1.0 / HEAD_DIM) + RMS_EPS) * gain


def _transpose_granules(xs):
    xs = list(xs)
    n = len(xs)
    block = lax.broadcasted_iota(jnp.int32, (1, LANES), 1) // SSM_GROUP
    bit = n // 2
    while bit:
        upper = (block & bit) != 0
        shift = SSM_GROUP * bit
        for lo in range(n):
            if lo & bit:
                continue
            hi = lo + bit
            x_lo, x_hi = xs[lo], xs[hi]
            xs[lo] = jnp.where(upper, pltpu.roll(x_hi, shift, 1), x_lo)
            xs[hi] = jnp.where(upper, x_hi, pltpu.roll(x_lo, LANES - shift, 1))
        bit //= 2
    return xs


def _mix_in_body(x_ref, nw_ref, wu_ref, wq_ref, wk_ref, wv_ref, wga_ref, wgb_ref, qg_ref, kg_ref, seg_ref,
                 u_ref, *rest, prompt):
    h = _rms(x_ref[...], nw_ref[...]).astype(BF16)

    def proj(w_ref):
        return jnp.dot(h, w_ref[...], preferred_element_type=F32)

    u = proj(wu_ref)
    for blk in range(LANE_BLOCKS):
        u_ref[blk] = u[:, blk * LANES:(blk + 1) * LANES]
    seg_ones = seg_ref[...]
    q = _head_norm(proj(wq_ref), qg_ref[...], seg_ones) * (HEAD_DIM ** -0.5)
    k = _head_norm(proj(wk_ref), kg_ref[...], seg_ones)
    v = proj(wv_ref)
    if not prompt:
        q_ref, k_ref, v_ref, ga_ref, gb_ref = rest
        q_ref[...] = q
    else:
        k_ref, v_ref, ga_ref, gb_ref, fold_ref = rest[:5]
        by_residue = rest[5:5 + 3 * N_DIL]
        stage_ref = rest[-1]
    k_ref[...] = k
    v_ref[...] = v
    ga_ref[...] = jax.nn.sigmoid(proj(wga_ref))
    gb_ref[...] = jax.nn.sigmoid(proj(wgb_ref))
    if not prompt:
        return

    rows = x_ref.shape[0]
    groups_per_block = LANES // SSM_GROUP
    for blk in range(LANE_BLOCKS):
        tokens = [u_ref[blk, pl.ds(s, rows // SSM_CHUNK, stride=SSM_CHUNK), :] for s in range(SSM_CHUNK)]
        for q_i, folded in enumerate(_transpose_granules(tokens)):
            pair = blk * (groups_per_block // 2) + q_i // 2
            fold_ref[pair, :, (q_i % 2) * LANES:(q_i % 2 + 1) * LANES] = folded.astype(BF16)

    slabs = SLOT_WIDTH // LANES
    for ti, x in enumerate((q * LOG2_E, k, v)):
        outs = by_residue[ti * N_DIL:(ti + 1) * N_DIL]
        outs[0][0] = x[:, :SLOT_WIDTH].astype(BF16)
        for g in range(1, N_DIL):
            d = DIL_RATES[g]
            for s in range(slabs):
                lo = g * SLOT_WIDTH + s * LANES
                stage_ref[ti, (g - 1) * slabs + s] = x[:, lo:lo + LANES]
            for r in range(d):
                for s in range(slabs):
                    piece = stage_ref[ti, (g - 1) * slabs + s, pl.ds(r, rows // d, stride=d), :]
                    outs[g][r, :, s * LANES:(s + 1) * LANES] = piece.astype(BF16)


IN_SEGMENTS = (D_MODEL, ATTN_WIDTH, ATTN_WIDTH, ATTN_WIDTH, D_MODEL, D_MODEL)


def _split_w_in(w_in):
    out, lo = [], 0
    for width in IN_SEGMENTS:
        out.append(w_in[:, lo:lo + width].astype(BF16))
        lo += width
    return tuple(out)


def _mix_in(x, norm_w, w_in, q_gain, k_gain, seg_ones, tm, seq=None):
    t = x.shape[0]
    prompt = seq is not None

    def row(width):
        return pl.BlockSpec((tm, width), lambda i: (i, 0))

    def rows_f32(width):
        return jax.ShapeDtypeStruct((t, width), F32)

    out_specs = [_lane_blocked_spec(tm, lambda i: i)]
    out_shape = [jax.ShapeDtypeStruct((LANE_BLOCKS, t, LANES), F32)]
    scratch = []
    if not prompt:
        out_specs += [row(ATTN_WIDTH)] * 3 + [row(D_MODEL)] * 2
        out_shape += [rows_f32(ATTN_WIDTH)] * 3 + [rows_f32(D_MODEL)] * 2
    else:
        tiles = seq // tm
        tail = min(DIL_WINDOWS[-1], seq)
        skip = tiles - tail // tm
        tail_spec = pl.BlockSpec((None, tm, ATTN_WIDTH), lambda i: (i // tiles, jnp.maximum(i % tiles - skip, 0), 0))
        out_specs += [tail_spec] * 2 + [row(D_MODEL)] * 2
        out_shape += [jax.ShapeDtypeStruct((t // seq, tail, ATTN_WIDTH), F32)] * 2 + [rows_f32(D_MODEL)] * 2
        out_specs.append(pl.BlockSpec((SSM_PAIRS, tm // SSM_CHUNK, MXU_DIM), lambda i: (0, i, 0)))
        out_shape.append(jax.ShapeDtypeStruct((SSM_PAIRS, t // SSM_CHUNK, MXU_DIM), BF16))
        for _ in range(3):
            for d in DIL_RATES:
                out_specs.append(pl.BlockSpec((None, d, tm // d, SLOT_WIDTH),
                                              lambda i: (i // tiles, 0, i % tiles, 0)))
                out_shape.append(jax.ShapeDtypeStruct((t // seq, d, seq // d, SLOT_WIDTH), BF16))
        scratch = [pltpu.VMEM((3, (N_DIL - 1) * SLOT_WIDTH // LANES, tm, LANES), F32)]
    return pl.pallas_call(
        functools.partial(_mix_in_body, prompt=prompt),
        grid=(t // tm,),
        in_specs=([row(D_MODEL), _const_spec((1, D_MODEL))] + [_const_spec((D_MODEL, w)) for w in IN_SEGMENTS]
                  + [_const_spec((1, ATTN_WIDTH)), _const_spec((1, ATTN_WIDTH)), _const_spec((MXU_DIM, MXU_DIM))]),
        out_specs=out_specs,
        out_shape=out_shape,
        scratch_shapes=scratch,
        compiler_params=_params("arbitrary"),
        name="mix_in",
    )(x, norm_w, *w_in, q_gain, k_gain, seg_ones)


def _cmul(ar, ai, br, bi):
    return ar * br - ai * bi, ar * bi + ai * br


def _ssm_prep_body(lrow_re_ref, lrow_im_ref, lcol_re_ref, lcol_im_ref, ldt_ref, bre_ref, bim_ref,
                   ct_re_ref, ct_im_ref, abar_ref, a8_ref, bstep_ref, cstep_ref, toep_ref, win_ref, wout_ref):
    n = SSM_CHUNK
    dt = jnp.exp(ldt_ref[...])

    def discretise(lam_re, lam_im):
        lr = jnp.minimum(lam_re, -1e-4)
        mag = jnp.exp(lr * dt)
        return lr, lam_im, mag * jnp.cos(lam_im * dt), mag * jnp.sin(lam_im * dt)

    def powers(ar, ai):
        out = [(jnp.ones_like(ar), jnp.zeros_like(ai))]
        for _ in range(n):
            out.append(_cmul(*out[-1], ar, ai))
        return out

    lr, li, ar, ai = discretise(lrow_re_ref[...], lrow_im_ref[...])
    den = lr * lr + li * li
    fr = ((ar - 1.0) * lr + ai * li) / den
    fi = (ai * lr - (ar - 1.0) * li) / den
    bbr, bbi = _cmul(fr, fi, bre_ref[...], bim_ref[...])
    row_pow = powers(ar, ai)
    abar_ref[0], abar_ref[1] = ar, ai
    a8_ref[0], a8_ref[1] = row_pow[n]

    def pair_halves(x):
        x = x.reshape((x.shape[0] // 2, 2) + x.shape[1:])
        return x[:, 0], x[:, 1]

    bstep_ref[...] = jnp.zeros_like(bstep_ref)
    cstep_ref[...] = jnp.zeros_like(cstep_ref)
    for part, x in enumerate((bbr, bbi)):
        x_e, x_o = pair_halves(x)
        bstep_ref[:, :SSM_GROUP, (2 * part) * SSM_STATE:(2 * part + 1) * SSM_STATE] = x_e
        bstep_ref[:, SSM_GROUP:, (2 * part + 1) * SSM_STATE:(2 * part + 2) * SSM_STATE] = x_o
    for part, ct_ref in enumerate((ct_re_ref, ct_im_ref)):
        x_e, x_o = pair_halves(ct_ref[:, :, :SSM_GROUP])
        cstep_ref[part, :, :SSM_STATE, :SSM_GROUP] = x_e
        cstep_ref[part, :, SSM_STATE:, SSM_GROUP:] = x_o

    toep_ref[...] = jnp.zeros_like(toep_ref)
    win_ref[...] = jnp.zeros_like(win_ref)
    wout_ref[...] = jnp.zeros_like(wout_ref)
    half = n * SSM_GROUP

    for s in range(n):
        w_re, w_im = _cmul(*row_pow[n - 1 - s], bbr, bbi)
        rows_e = slice(s * SSM_GROUP, (s + 1) * SSM_GROUP)
        rows_o = slice(half + s * SSM_GROUP, half + (s + 1) * SSM_GROUP)
        for part, x in enumerate((w_re, w_im)):
            x_e, x_o = pair_halves(x.astype(BF16))
            win_ref[:, rows_e, (2 * part) * SSM_STATE:(2 * part + 1) * SSM_STATE] = x_e
            win_ref[:, rows_o, (2 * part + 1) * SSM_STATE:(2 * part + 2) * SSM_STATE] = x_o

    _, _, ar_c, ai_c = discretise(lcol_re_ref[...], lcol_im_ref[...])
    col_pow = powers(ar_c, ai_c)
    lane_t = lax.broadcasted_iota(jnp.int32, (1, 1, LANES), 2) // SSM_GROUP

    def spread(first):
        re = im = jnp.zeros((1, 1, LANES), F32)
        for t in range(n):
            re = jnp.where(lane_t == t, col_pow[first + t][0], re)
            im = jnp.where(lane_t == t, col_pow[first + t][1], im)
        return re, im

    ct = (ct_re_ref[...], ct_im_ref[...])
    m0_re, m0_im = _cmul(*ct, *spread(0))
    m1_re, m1_im = _cmul(*ct, *spread(1))

    for part, x in enumerate((m1_re, -m1_im)):
        x_e, x_o = pair_halves(x.astype(BF16))
        wout_ref[:, (2 * part) * SSM_STATE:(2 * part + 1) * SSM_STATE, :half] = x_e
        wout_ref[:, (2 * part + 1) * SSM_STATE:(2 * part + 2) * SSM_STATE, half:] = x_o

    nn = (((2,), (1,)), ((0,), (0,)))
    hp = lax.Precision.HIGHEST
    kern = (lax.dot_general(bbr, m0_re, nn, precision=hp, preferred_element_type=F32)
            - lax.dot_general(bbi, m0_im, nn, precision=hp, preferred_element_type=F32))
    lane = lax.broadcasted_iota(jnp.int32, (1, 1, LANES), 2)
    for s in range(n):
        shifted = kern if s == 0 else jnp.where(lane >= s * SSM_GROUP, pltpu.roll(kern, s * SSM_GROUP, 2), 0.0)
        x_e, x_o = pair_halves(shifted.astype(BF16))
        toep_ref[:, s * SSM_GROUP:(s + 1) * SSM_GROUP, :half] = x_e
        toep_ref[:, half + s * SSM_GROUP:half + (s + 1) * SSM_GROUP, half:] = x_o


def _ssm_prep(lam_re, lam_im, log_dt, b_re_t, b_im_t, c_re, c_im, groups_per_step=16):
    g, p, c = SSM_GROUPS, SSM_STATE, SSM_GROUP
    gb = groups_per_step

    def spec(*tail):
        return pl.BlockSpec((gb,) + tail, lambda i: (i,) + (0,) * len(tail))

    def stacked(*tail):
        return pl.BlockSpec((2, gb) + tail, lambda i: (0, i) + (0,) * len(tail))

    pair_spec = pl.BlockSpec((gb // 2, MXU_DIM, MXU_DIM), lambda i: (i, 0, 0))
    pair_shape = jax.ShapeDtypeStruct((SSM_PAIRS, MXU_DIM, MXU_DIM), BF16)
    ct_re = jnp.tile(c_re.transpose(0, 2, 1), (1, 1, SSM_CHUNK))
    ct_im = jnp.tile(c_im.transpose(0, 2, 1), (1, 1, SSM_CHUNK))
    return pl.pallas_call(
        _ssm_prep_body,
        grid=(g // gb,),
        in_specs=[spec(1, p), spec(1, p), spec(p, 1), spec(p, 1), spec(1, 1), spec(c, p), spec(c, p),
                  spec(p, LANES), spec(p, LANES)],
        out_specs=[stacked(1, p), stacked(1, p),
                   pl.BlockSpec((gb // 2, 2 * c, 4 * p), lambda i: (i, 0, 0)),
                   pl.BlockSpec((2, gb // 2, 2 * p, 2 * c), lambda i: (0, i, 0, 0)),
                   pair_spec, pair_spec, pair_spec],
        out_shape=[jax.ShapeDtypeStruct((2, g, 1, p), F32), jax.ShapeDtypeStruct((2, g, 1, p), F32),
                   jax.ShapeDtypeStruct((SSM_PAIRS, 2 * c, 4 * p), F32),
                   jax.ShapeDtypeStruct((2, SSM_PAIRS, 2 * p, 2 * c), F32),
                   pair_shape, pair_shape, pair_shape],
        compiler_params=_params("parallel"),
        name="ssm_prep",
    )(lam_re.reshape(g, 1, p), lam_im.reshape(g, 1, p), lam_re.reshape(g, p, 1), lam_im.reshape(g, p, 1),
      log_dt.reshape(g, 1, 1), b_re_t, b_im_t, ct_re, ct_im)


def _ssm_body(lhs_ref, toep_ref, win_ref, wout_ref, a8_ref, yfl_ref, hfin_ref, st_ref, carry_ref, *, rows):
    i = pl.program_id(1)

    @pl.when(i == 0)
    def _():
        carry_ref[...] = jnp.zeros_like(carry_ref)

    def state_in(r, _):
        b = jnp.dot(lhs_ref[r], win_ref[r], preferred_element_type=F32)
        st_ref[0, pl.ds(r, rows, stride=SSM_ROW_PITCH), :] = b[:, :LANES]
        st_ref[1, pl.ds(r, rows, stride=SSM_ROW_PITCH), :] = b[:, LANES:]
        return 0

    lax.fori_loop(0, SSM_PAIRS, state_in, 0, unroll=4)

    a_re = a8_ref[0]
    a_im = a8_ref[1]

    def step(j, h):
        h_re, h_im = h
        base = pl.multiple_of(j * SSM_ROW_PITCH, SUBLANES)
        n_re = a_re * h_re - a_im * h_im + st_ref[0, pl.ds(base, SSM_PAIRS), :]
        n_im = a_re * h_im + a_im * h_re + st_ref[1, pl.ds(base, SSM_PAIRS), :]
        st_ref[0, pl.ds(base, SSM_PAIRS), :] = h_re
        st_ref[1, pl.ds(base, SSM_PAIRS), :] = h_im
        return n_re, n_im

    h_re, h_im = lax.fori_loop(0, rows, step, (carry_ref[0], carry_ref[1]), unroll=4)
    carry_ref[0] = h_re
    carry_ref[1] = h_im
    hfin_ref[0] = h_re
    hfin_ref[1] = h_im

    def chunk_out(r, _):
        hcat = jnp.concatenate([st_ref[0, pl.ds(r, rows, stride=SSM_ROW_PITCH), :],
                                st_ref[1, pl.ds(r, rows, stride=SSM_ROW_PITCH), :]], axis=1).astype(BF16)
        yfl_ref[r] = (jnp.dot(lhs_ref[r], toep_ref[r], preferred_element_type=F32)
                      + jnp.dot(hcat, wout_ref[r], preferred_element_type=F32))
        return 0

    lax.fori_loop(0, SSM_PAIRS, chunk_out, 0, unroll=4)


def _ssm_prompt(lhs, toep, win, wout, a8, batch, seq, tile):
    rows = tile // SSM_CHUNK
    n_tiles = seq // tile
    tok = pl.BlockSpec((SSM_PAIRS, rows, MXU_DIM), lambda b, i: (0, b * n_tiles + i, 0))
    pair_w = _const_spec((SSM_PAIRS, MXU_DIM, MXU_DIM))
    return pl.pallas_call(
        functools.partial(_ssm_body, rows=rows),
        grid=(batch, n_tiles),
        in_specs=[tok, pair_w, pair_w, pair_w, _const_spec((2, SSM_PAIRS, LANES))],
        out_specs=[tok, pl.BlockSpec((None, 2, SSM_PAIRS, LANES), lambda b, i: (b, 0, 0, 0))],
        out_shape=[jax.ShapeDtypeStruct((SSM_PAIRS, batch * seq // SSM_CHUNK, MXU_DIM), F32),
                   jax.ShapeDtypeStruct((batch, 2, SSM_PAIRS, LANES), F32)],
        scratch_shapes=[pltpu.VMEM((2, rows * SSM_ROW_PITCH, LANES), F32),
                        pltpu.VMEM((2, SSM_PAIRS, LANES), F32)],
        compiler_params=_params("parallel", "arbitrary"),
        name="ssm_prompt",
    )(lhs, toep, win, wout, a8)


def _ssm_step_body(u_ref, hre_ref, him_ref, abar_ref, bstep_ref, cstep_ref, y_ref, ore_ref, oim_ref):
    hp = lax.Precision.HIGHEST
    pair_ch = 2 * SSM_GROUP
    pairs_per_block = LANES // pair_ch
    for r in range(SSM_PAIRS):
        blk = r // pairs_per_block
        ch = slice((r % pairs_per_block) * pair_ch, (r % pairs_per_block + 1) * pair_ch)
        st = slice(r * LANES, (r + 1) * LANES)
        bu = jnp.dot(u_ref[blk, :, ch], bstep_ref[r], precision=hp, preferred_element_type=F32)
        a_re = abar_ref[0, r:r + 1, :]
        a_im = abar_ref[1, r:r + 1, :]
        h_re = hre_ref[:, st]
        h_im = him_ref[:, st]
        n_re = a_re * h_re - a_im * h_im + bu[:, :LANES]
        n_im = a_re * h_im + a_im * h_re + bu[:, LANES:]
        ore_ref[:, st] = n_re
        oim_ref[:, st] = n_im
        y_ref[blk, :, ch] = (jnp.dot(n_re, cstep_ref[0, r], precision=hp, preferred_element_type=F32)
                             - jnp.dot(n_im, cstep_ref[1, r], precision=hp, preferred_element_type=F32))


def _ssm_step(u, h_re, h_im, abar, bstep, cstep):
    b = u.shape[1]
    width = SSM_GROUPS * SSM_STATE
    return pl.pallas_call(
        _ssm_step_body,
        out_shape=[jax.ShapeDtypeStruct((LANE_BLOCKS, b, LANES), F32),
                   jax.ShapeDtypeStruct((b, width), F32),
                   jax.ShapeDtypeStruct((b, width), F32)],
        compiler_params=pltpu.CompilerParams(vmem_limit_bytes=VMEM_LIMIT_BYTES),
        name="ssm_step",
    )(u, h_re, h_im, abar, bstep, cstep)


def _attn_prompt_body(*refs, part_rows):
    per_group = 5
    ins = refs[:per_group * N_DIL]
    o_ref, m_ref, l_ref, acc_ref, bias_ref = refs[per_group * N_DIL:]
    part = pl.program_id(1)
    tq = KEYS_BACK
    row = lax.broadcasted_iota(jnp.int32, (tq, 2 * tq), 0)
    col = lax.broadcasted_iota(jnp.int32, (tq, 2 * tq), 1)
    back = row + tq - col
    in_window = (back >= 0) & (back <= KEYS_BACK)
    first_head = lax.broadcasted_iota(jnp.int32, (1, LANES), 1) < HEAD_DIM
    nt = (((1,), (1,)), ((), ()))

    order = tuple(reversed(range(N_DIL)))
    ones = jnp.ones((2 * tq, LANES), BF16)
    for g in order:
        q_ref, k_ref, kb_ref, v_ref, vb_ref = ins[per_group * g:per_group * (g + 1)]
        d = DIL_RATES[g]
        blocks_per_residue = part_rows // d // tq
        dist = (back * d).astype(F32)
        for h in range(HEADS):
            bias = jnp.where(in_window, -(_slope(g, h) * LOG2_E) * dist, -jnp.inf)
            bias_ref[h] = bias
            bias_ref[HEADS + h] = jnp.where(col >= tq, bias, -jnp.inf)

        def block(mi, _, g=g, d=d, q_ref=q_ref, k_ref=k_ref, kb_ref=kb_ref, v_ref=v_ref, vb_ref=vb_ref,
                  blocks_per_residue=blocks_per_residue):
            residue = mi // blocks_per_residue
            n = mi % blocks_per_residue
            cur = pl.multiple_of(n * tq, tq)
            prev = pl.multiple_of(jnp.maximum(n - 1, 0) * tq, tq)
            bias_at = jnp.where((n == 0) & (part == 0), HEADS, 0)
            token0 = residue + d * tq * n
            rows = pl.ds(pl.multiple_of(token0, tq), tq) if d == 1 else pl.ds(token0, tq, stride=d)
            pairs = range(HEADS // 2)
            first, final = g == order[0], g == order[-1]
            old = None if first else [(m_ref[pair, rows, :], l_ref[pair, rows, :], acc_ref[pair, rows, :])
                                      for pair in pairs]
            new = []
            for pair in pairs:
                lanes = slice(pair * LANES, (pair + 1) * LANES)
                qp = q_ref[residue, pl.ds(cur, tq), lanes]
                k_prev = jnp.where(n == 0, kb_ref[residue, :, lanes], k_ref[residue, pl.ds(prev, tq), lanes])
                v_prev = jnp.where(n == 0, vb_ref[residue, :, lanes], v_ref[residue, pl.ds(prev, tq), lanes])
                kp = jnp.concatenate([k_prev, k_ref[residue, pl.ds(cur, tq), lanes]], axis=0)
                vp = jnp.concatenate([v_prev, v_ref[residue, pl.ds(cur, tq), lanes]], axis=0)
                vp = jnp.concatenate([vp, ones], axis=1)
                stats = []
                for e in range(2):
                    qm = jnp.where(first_head if e == 0 else ~first_head, qp, jnp.zeros_like(qp))
                    s = lax.dot_general(qm, kp, nt, preferred_element_type=F32)
                    s = s + bias_ref[bias_at + 2 * pair + e]
                    m = jnp.max(s, axis=-1, keepdims=True)
                    p = jnp.exp2(s - m)
                    pv = jnp.dot(p.astype(BF16), vp, preferred_element_type=F32)
                    stats.append((m, pv[:, LANES:], pv[:, :LANES]))
                m_new, l_new, acc_new = (jnp.where(first_head, a, b) for a, b in zip(*stats))
                if not first:
                    m_old, l_old, acc_old = old[pair]
                    m_tot = jnp.maximum(m_old, m_new)
                    w_old = jnp.exp2(m_old - m_tot)
                    w_new = jnp.exp2(m_new - m_tot)
                    l_new = w_old * l_old + w_new * l_new
                    acc_new = w_old * acc_old + w_new * acc_new
                    m_new = m_tot
                new.append((m_new, l_new, acc_new))
            for pair, (m_new, l_new, acc_new) in zip(pairs, new):
                if not final:
                    m_ref[pair, rows, :] = m_new
                    l_ref[pair, rows, :] = l_new
                    acc_ref[pair, rows, :] = acc_new
                else:
                    o_ref[pair, rows, :] = acc_new / l_new
            return 0

        lax.fori_loop(0, part_rows // tq, block, 0, unroll=2)


def _attn_prompt(qkv, batch, seq):
    slabs = SLOT_WIDTH // LANES
    tq = KEYS_BACK
    part_rows = DIL_RATES[-1] * tq
    parts = seq // part_rows
    args, specs = [], []
    for g, d in enumerate(DIL_RATES):
        rows = part_rows // d
        cur = pl.BlockSpec((None, d, rows, SLOT_WIDTH), lambda b, p: (b, 0, p, 0))
        before = pl.BlockSpec((None, d, tq, SLOT_WIDTH),
                              functools.partial(lambda b, p, step: (b, 0, jnp.maximum(p * step - 1, 0), 0),
                                                step=rows // tq))
        q, k, v = qkv[g], qkv[N_DIL + g], qkv[2 * N_DIL + g]
        args += [q, k, k, v, v]
        specs += [cur, cur, before, cur, before]
    running = pltpu.VMEM((slabs, part_rows, LANES), F32)
    return pl.pallas_call(
        functools.partial(_attn_prompt_body, part_rows=part_rows),
        grid=(batch, parts),
        in_specs=specs,
        out_specs=pl.BlockSpec((slabs, part_rows, LANES), lambda b, p: (0, b * parts + p, 0)),
        out_shape=jax.ShapeDtypeStruct((slabs, batch * seq, LANES), F32),
        scratch_shapes=[running, running, running, pltpu.VMEM((2 * HEADS, tq, 2 * tq), F32)],
        compiler_params=_params("parallel", "arbitrary"),
        name="attn_prompt",
    )(*args)


def _as_column(row_vec):
    n = row_vec.shape[1]
    eye = lax.broadcasted_iota(jnp.int32, (n, n), 0) == lax.broadcasted_iota(jnp.int32, (n, n), 1)
    return jnp.sum(jnp.where(eye, row_vec, 0.0), axis=1, keepdims=True)


def _sample_window_group(g, q_row, k_row, v_row, c_ref, n_ref):
    head_row = lax.broadcasted_iota(jnp.int32, (SUBLANES, SLOT_WIDTH), 0)
    own_head = lax.broadcasted_iota(jnp.int32, (SUBLANES, SLOT_WIDTH), 1) // HEAD_DIM == head_row
    head_col = lax.broadcasted_iota(jnp.int32, (SUBLANES, 1), 0)
    w = DIL_WINDOWS[g]
    d = DIL_RATES[g]
    cols = slice(g * SLOT_WIDTH, (g + 1) * SLOT_WIDTH)
    q_g, k_new, v_new = q_row[:, cols], k_row[:, cols], v_row[:, cols]
    kt = c_ref[0].reshape(SLOT_WIDTH, w)
    vt = c_ref[1].reshape(SLOT_WIDTH, w)
    q_heads = jnp.where(own_head, q_g, 0.0)
    s = jnp.dot(q_heads.astype(BF16), kt.astype(BF16), preferred_element_type=F32)
    dist = w - lax.broadcasted_iota(jnp.int32, (1, w), 1)
    slope = functools.reduce(lambda acc, h: jnp.where(head_col == h, _slope(g, h), acc), range(HEADS), 0.0)
    s = jnp.where((dist & (d - 1)) == 0, s - slope * dist.astype(F32), -jnp.inf)
    s_new = jnp.sum(q_heads * k_new, axis=1, keepdims=True)
    m = jnp.maximum(jnp.max(s, axis=1, keepdims=True), s_new)
    p = jnp.exp(s - m)
    p_new = jnp.exp(s_new - m)
    den = jnp.sum(p, axis=1, keepdims=True) + p_new
    pv = lax.dot_general(p.astype(BF16), vt.astype(BF16), (((1,), (1,)), ((), ())),
                         preferred_element_type=F32)
    o_heads = (pv + p_new * v_new) / den
    last = lax.broadcasted_iota(jnp.int32, (SLOT_WIDTH, w), 1) == w - 1
    shape = (HEADS, HEAD_DIM, w)
    n_ref[0] = jnp.where(last, _as_column(k_new), pltpu.roll(kt, w - 1, 1)).reshape(shape)
    n_ref[1] = jnp.where(last, _as_column(v_new), pltpu.roll(vt, w - 1, 1)).reshape(shape)
    return (jnp.sum(jnp.where(own_head, o_heads, 0.0), axis=0, keepdims=True),
            jnp.sum(jnp.where(own_head, m + jnp.log(den), 0.0), axis=0, keepdims=True))


def _mix_groups(outs, lses):
    top = functools.reduce(jnp.maximum, lses)
    wts = [jnp.exp(l - top) for l in lses]
    return sum(w_g * o_g for w_g, o_g in zip(wts, outs)) / sum(wts)


def _ffn_windows_body(x_ref, nw_ref, *refs):
    n_w = 3 * len(FF_CHUNKS)
    w_refs = refs[:n_w]
    q_ref, k_ref, v_ref, c0_ref, c1_ref, c2_ref, o_ref, ob_ref, n0_ref, n1_ref, n2_ref = refs[n_w:]
    b = pl.program_id(0)
    rows = [r[pl.ds(b, 1), :] for r in (q_ref, k_ref, v_ref)]
    windows = ((c0_ref, n0_ref), (c1_ref, n1_ref), (c2_ref, n2_ref))
    results = {}

    def job(g):
        def run():
            results[g] = _sample_window_group(g, *rows, *windows[g])
        return run

    _ffn_tile(x_ref, nw_ref, w_refs, o_ref, side_jobs=[job(g) for g in reversed(range(N_DIL))])
    outs, lses = zip(*(results[g] for g in range(N_DIL)))
    ob_ref[...] = _mix_groups(outs, lses)


def _ffn_with_sample_windows(x, norm_w, weights, q, k, v, caches):
    t = x.shape[0]
    b = q.shape[0]
    assert t % b == 0 and (t // b) % SUBLANES == 0
    tm = t // b
    row = pl.BlockSpec((tm, D_MODEL), lambda i: (i, 0))
    full = _const_spec((b, ATTN_WIDTH))
    win = [pl.BlockSpec((None, 2, HEADS, HEAD_DIM, w), lambda i: (i, 0, 0, 0, 0)) for w in DIL_WINDOWS]
    outs = pl.pallas_call(
        _ffn_windows_body,
        grid=(b,),
        in_specs=[row, _const_spec((1, D_MODEL))] + _ffn_weight_specs() + [full, full, full] + win,
        out_specs=[row, pl.BlockSpec((None, 1, SLOT_WIDTH), lambda i: (i, 0, 0))] + win,
        out_shape=([jax.ShapeDtypeStruct((t, D_MODEL), F32), jax.ShapeDtypeStruct((b, 1, SLOT_WIDTH), F32)]
                   + [jax.ShapeDtypeStruct(c.shape, F32) for c in caches]),
        compiler_params=_params("parallel"),
        name="ffn_windows",
    )(x, norm_w, *weights, q, k, v, *caches)
    return outs[0], outs[1].reshape(b, SLOT_WIDTH), outs[2:]


def _kv_window_body(*refs):
    k_refs, v_refs, out_refs = refs[:N_DIL], refs[N_DIL:2 * N_DIL], refs[2 * N_DIL:]
    for g, o_ref in enumerate(out_refs):
        for t, x_ref in enumerate((k_refs[g], v_refs[g])):
            for pair in range(HEADS // 2):
                xt = x_ref[:, pair * LANES:(pair + 1) * LANES].T
                o_ref[t, 2 * pair] = xt[:HEAD_DIM]
                o_ref[t, 2 * pair + 1] = xt[HEAD_DIM:]


def _kv_windows(k_tail, v_tail):
    batch, tail, _ = k_tail.shape
    keeps = [min(w, tail) for w in DIL_WINDOWS]
    assert all(tail % w == 0 for w in keeps)
    src = [pl.BlockSpec((None, w, SLOT_WIDTH), functools.partial(lambda b, at, g: (b, at, g), at=tail // w - 1, g=g))
           for g, w in enumerate(keeps)]
    return pl.pallas_call(
        _kv_window_body,
        grid=(batch,),
        in_specs=src + src,
        out_specs=[pl.BlockSpec((None, 2, HEADS, HEAD_DIM, w), lambda b: (b, 0, 0, 0, 0)) for w in keeps],
        out_shape=[jax.ShapeDtypeStruct((batch, 2, HEADS, HEAD_DIM, w), F32) for w in keeps],
        compiler_params=_params("parallel"),
        name="kv_windows",
    )(*([k_tail] * N_DIL + [v_tail] * N_DIL))


def _mix_out_body(x_ref, y_ref, u_ref, d_ref, ob_ref,
                  ga_ref, gb_ref, wglu_ref, wpa_ref, wpb_ref, wout_ref, out_ref, *unfold):
    tm = x_ref.shape[0]
    if unfold:
        y_ref, folded_ref = unfold[0], y_ref
        groups_per_block = LANES // SSM_GROUP
    pieces = MIX_OUT_PIECES if unfold else 1
    rows = tm // pieces
    for piece in range(pieces):
        at = slice(piece * rows, (piece + 1) * rows)
        if unfold:
            chunks = rows // SSM_CHUNK
            chunk_at = slice(piece * chunks, (piece + 1) * chunks)
            for blk in range(LANE_BLOCKS):
                groups = [folded_ref[blk * (groups_per_block // 2) + q_i // 2, chunk_at,
                                     (q_i % 2) * LANES:(q_i % 2 + 1) * LANES] for q_i in range(groups_per_block)]
                for t, rows_t in enumerate(_transpose_granules(groups)):
                    y_ref[blk, pl.ds(piece * rows + t, chunks, stride=SSM_CHUNK), :] = rows_t
        y_raw = jnp.concatenate([y_ref[blk, at, :] for blk in range(LANE_BLOCKS)], axis=1)
        u = jnp.concatenate([u_ref[blk, at, :] for blk in range(LANE_BLOCKS)], axis=1)
        y = jax.nn.gelu(y_raw + d_ref[...] * u)
        yb = y.astype(BF16)
        y_a = y * jax.nn.sigmoid(jnp.dot(yb, wglu_ref[...], preferred_element_type=F32))
        branch_a = jnp.dot(y_a.astype(BF16), wpa_ref[...], preferred_element_type=F32)

        o_b = jnp.concatenate([ob_ref[s, at, :] for s in range(SLOT_WIDTH // LANES)], axis=1)
        branch_b = jnp.dot(o_b.astype(BF16), wpb_ref[...], preferred_element_type=F32)

        merged = ga_ref[at, :] * branch_a + gb_ref[at, :] * branch_b
        out_ref[at, :] = x_ref[at, :] + jnp.dot(merged.astype(BF16), wout_ref[...], preferred_element_type=F32)


def _mix_out(x, y, u, ssm_d, o_b, ga, gb, w_glu, w_pa, w_pb, w_out, tm, folded_y):
    t = x.shape[0]
    wide = pl.BlockSpec((tm, D_MODEL), lambda i: (i, 0))
    blocked = _lane_blocked_spec(tm, lambda i: i)
    y_spec = pl.BlockSpec((SSM_PAIRS, tm // SSM_CHUNK, MXU_DIM), lambda i: (0, i, 0)) if folded_y else blocked
    scratch = [pltpu.VMEM((LANE_BLOCKS, tm, LANES), F32)] if folded_y else []
    slot = pl.BlockSpec((SLOT_WIDTH // LANES, tm, LANES), lambda i: (0, i, 0))
    return pl.pallas_call(
        _mix_out_body,
        grid=(t // tm,),
        in_specs=[wide, y_spec, blocked, _const_spec((1, D_MODEL)), slot, wide, wide,
                  _const_spec((D_MODEL, D_MODEL)), _const_spec((D_MODEL, D_MODEL)),
                  _const_spec((SLOT_WIDTH, D_MODEL)), _const_spec((D_MODEL, D_MODEL))],
        out_specs=wide,
        out_shape=jax.ShapeDtypeStruct((t, D_MODEL), F32),
        scratch_shapes=scratch,
        compiler_params=_params("parallel"),
        name="mix_out",
    )(x, y, u, ssm_d, o_b, ga, gb, w_glu, w_pa, w_pb, w_out)


def _head_segment_ones():
    head = jnp.arange(MXU_DIM) // HEAD_DIM
    return (head[:, None] == head[None, :]).astype(BF16)


def kernel(x_prompt, x_sample, state_ssm_re, state_ssm_im, cache_kv_w128, cache_kv_w512, cache_kv_w2048, ffn1_norm, ffn1_w_gate, ffn1_w_up, ffn1_w_down, mix_norm, w_in, ssm_lambda_re, ssm_lambda_im, ssm_b_re, ssm_b_im, ssm_c_re, ssm_c_im, ssm_d, ssm_log_dt, w_glu, q_gain, k_gain, w_proj_a, w_proj_b, w_out, ffn2_norm, ffn2_w_gate, ffn2_w_up, ffn2_w_down):
    depth = ffn1_norm.shape[0]
    assert depth == 1, "single-layer step"
    batch, seq, _ = x_prompt.shape
    dec_batch, dec_seq, _ = x_sample.shape
    assert dec_seq == 1 and seq % (DIL_RATES[-1] * KEYS_BACK) == 0
    layer = 0
    bf = lambda w: w[layer].astype(BF16)
    vec = lambda w: w[layer][None]
    ffn1 = (vec(ffn1_norm), _ffn_chunk_weights(ffn1_w_gate[layer], ffn1_w_up[layer], ffn1_w_down[layer]))
    ffn2 = (vec(ffn2_norm), _ffn_chunk_weights(ffn2_w_gate[layer], ffn2_w_up[layer], ffn2_w_down[layer]))
    mix_norm, ssm_d = vec(mix_norm), vec(ssm_d)
    w_in_b = _split_w_in(w_in[layer])
    w_glu_b, w_pa_b, w_pb_b, w_out_b = bf(w_glu), bf(w_proj_a), bf(w_proj_b), bf(w_out)
    q_gain_t = jnp.tile(vec(q_gain), (1, N_DIL * HEADS))
    k_gain_t = jnp.tile(vec(k_gain), (1, N_DIL * HEADS))
    seg_ones = _head_segment_ones()

    abar, a8, bstep, cstep, toep, win, wout = _ssm_prep(
        ssm_lambda_re[layer], ssm_lambda_im[layer], ssm_log_dt[layer],
        ssm_b_re[layer].transpose(0, 2, 1), ssm_b_im[layer].transpose(0, 2, 1),
        ssm_c_re[layer], ssm_c_im[layer])
    a8 = a8.reshape(2, SSM_PAIRS, LANES)
    abar = abar.reshape(2, SSM_PAIRS, LANES)

    def front(x, tm_ffn, tm, seq=None):
        x1 = _ffn(x, *ffn1, tm_ffn)
        return (x1,) + tuple(_mix_in(x1, mix_norm, w_in_b, q_gain_t, k_gain_t, seg_ones, tm, seq))

    def mix_out(x1, y, u, o_b, ga, gb, tm, folded_y):
        return _mix_out(x1, y, u, ssm_d, o_b, ga, gb, w_glu_b, w_pa_b, w_pb_b, w_out_b, tm, folded_y)

    xs = x_sample.reshape(dec_batch, D_MODEL)
    xs1, us, qs, ks, vs, gas, gbs = front(xs, dec_batch, dec_batch)
    caches = [c[layer].transpose(0, 2, 3, 4, 1) for c in (cache_kv_w128, cache_kv_w512, cache_kv_w2048)]

    tm = 512
    tm_ffn = 1024
    xp = x_prompt.reshape(batch * seq, D_MODEL)
    x1, u, k_tail, v_tail, ga, gb, u_folded, *qkv = front(xp, tm_ffn, tm, seq)
    y, h_fin = _ssm_prompt(u_folded, toep, win, wout, a8, batch, seq, tile=2048)
    o_b = _attn_prompt(qkv, batch, seq)
    x2 = mix_out(x1, y, u, o_b, ga, gb, tm, folded_y=True)
    yp, o_bs, new_caches = _ffn_with_sample_windows(x2, *ffn2, qs, ks, vs, caches)
    yp = yp.reshape(batch, seq, D_MODEL)
    h_fin = h_fin.reshape(batch, 2, SSM_GROUPS, SSM_STATE)
    p_re = h_fin[None, :, 0]
    p_im = h_fin[None, :, 1]
    p_kv = [c.transpose(0, 4, 1, 2, 3)[None] for c in _kv_windows(k_tail, v_tail)]

    width = SSM_GROUPS * SSM_STATE
    ys_ssm, s_re, s_im = _ssm_step(us, state_ssm_re[layer].reshape(dec_batch, width),
                                   state_ssm_im[layer].reshape(dec_batch, width),
                                   abar, bstep, cstep)
    o_bs = o_bs.reshape(dec_batch, SLOT_WIDTH // LANES, LANES).transpose(1, 0, 2)
    xs2 = mix_out(xs1, ys_ssm, us, o_bs, gas, gbs, dec_batch, folded_y=False)
    ys = _ffn(xs2, *ffn2, dec_batch)
    ys = ys.reshape(dec_batch, 1, D_MODEL)
    s_re = s_re.reshape(1, dec_batch, SSM_GROUPS, SSM_STATE)
    s_im = s_im.reshape(1, dec_batch, SSM_GROUPS, SSM_STATE)
    s_kv = [c.transpose(0, 4, 1, 2, 3)[None] for c in new_caches]

    return (yp, ys, p_re, p_im, p_kv[0], p_kv[1], p_kv[2], s_re, s_im, s_kv[0], s_kv[1], s_kv[2])
```

```python
import functools

import jax
import jax.numpy as jnp
from jax import lax
from jax.experimental import pallas as pl
from jax.experimental.pallas import tpu as pltpu

F32 = jnp.float32
BF16 = jnp.bfloat16

D_MODEL = 1024
SSM_GROUP = 16
SSM_GROUPS = 64
SSM_STATE = 64
SSM_PAIRS = SSM_GROUPS // 2
HEAD_DIM = 64
HEADS = 4
DIL_WINDOWS = (128, 512, 2048)
DIL_RATES = (1, 4, 16)
N_DIL = 3
KEYS_BACK = 128
ATTN_WIDTH = N_DIL * HEADS * HEAD_DIM
SLOT_WIDTH = HEADS * HEAD_DIM
D_FF = 2816
RMS_EPS = 1e-6
ALIBI_MAX_EXP = 8.0
LOG2_E = 1.4426950408889634
IN_SEGMENTS = (D_MODEL, ATTN_WIDTH, ATTN_WIDTH, ATTN_WIDTH, D_MODEL, D_MODEL)
IN_WIDTH = sum(IN_SEGMENTS)

LANES = 128
SUBLANES = 8
MXU_DIM = 256
VMEM_LIMIT_BYTES = 56 * 1024 * 1024

LANE_BLOCKS = D_MODEL // LANES

SSM_CHUNK = SUBLANES
SSM_ROW_PITCH = 40
FF_CHUNKS = ((0, 1024), (1024, 2048), (2048, 2816))
MIX_OUT_PIECES = 2


def _slope(group, head):
    return 2.0 ** (-ALIBI_MAX_EXP * (group * HEADS + head + 1) / (N_DIL * HEADS))


def _const_spec(shape):
    zeros = (0,) * len(shape)
    return pl.BlockSpec(shape, lambda *_: zeros, pipeline_mode=pl.Buffered(1))


def _lane_blocked_spec(rows, row_block):
    return pl.BlockSpec((LANE_BLOCKS, rows, LANES), lambda *idx: (0, row_block(*idx), 0))


def _params(*semantics):
    return pltpu.CompilerParams(dimension_semantics=semantics, vmem_limit_bytes=VMEM_LIMIT_BYTES)


def _rms(x, w):
    return x * lax.rsqrt(jnp.mean(x * x, axis=-1, keepdims=True) + RMS_EPS) * w


def _split_bf16(x, terms):
    parts = []
    for _ in range(terms):
        p = x.astype(BF16)
        parts.append(p)
        x = x - p.astype(F32)
    return parts


def _ffn_body(x_ref, nw_ref, wg_ref, wu_ref, wd_ref, *rest, side_jobs=()):
    n_cast = (len(rest) - 1) // 2
    o_ref = rest[n_cast]
    for src_ref, dst_ref in zip(rest[:n_cast], rest[n_cast + 1:]):
        dst_ref[...] = src_ref[...].astype(BF16)
    x = x_ref[...]
    h = _rms(x, nw_ref[...]).astype(BF16)
    acc = jnp.zeros_like(x)
    side_jobs = list(side_jobs)
    for lo, hi in FF_CHUNKS:
        g = jnp.dot(h, wg_ref[:, lo:hi], preferred_element_type=F32)
        u = jnp.dot(h, wu_ref[:, lo:hi], preferred_element_type=F32)
        a = (jax.nn.silu(g) * u).astype(BF16)
        acc = acc + jnp.dot(a, wd_ref[lo:hi, :], preferred_element_type=F32)
        if side_jobs:
            side_jobs.pop(0)()
    for job in side_jobs:
        job()
    o_ref[...] = x + 0.5 * acc


def _ffn_weight_specs():
    return [_const_spec((D_MODEL, D_FF)), _const_spec((D_MODEL, D_FF)), _const_spec((D_FF, D_MODEL))]


def _ffn(x, norm_w, weights, tm, convert=()):
    t = x.shape[0]
    steps = t // tm
    row = pl.BlockSpec((tm, D_MODEL), lambda i: (i, 0))
    side = []
    for w in convert:
        rows = w.shape[0] // steps
        assert rows * steps == w.shape[0] and rows % (2 * SUBLANES) == 0
        side.append(pl.BlockSpec((rows, w.shape[1]), lambda i: (i, 0)))
    outs = pl.pallas_call(
        _ffn_body,
        grid=(steps,),
        in_specs=[row, _const_spec((1, D_MODEL))] + _ffn_weight_specs() + side,
        out_specs=[row] + side,
        out_shape=[jax.ShapeDtypeStruct((t, D_MODEL), F32)] + [jax.ShapeDtypeStruct(w.shape, BF16) for w in convert],
        compiler_params=_params("parallel"),
        name="ffn",
    )(x, norm_w, *weights, *convert)
    return outs if convert else outs[0]


def _head_norm(x, gain, seg_ones):
    parts = _split_bf16(x * x, 2)
    blocks = []
    for lo in range(0, x.shape[1], MXU_DIM):
        blocks.append(sum(jnp.dot(p[:, lo:lo + MXU_DIM], seg_ones, preferred_element_type=F32) for p in parts))
    ss = jnp.concatenate(blocks, axis=1)
    return x * lax.rsqrt(ss * (1.0 / HEAD_DIM) + RMS_EPS) * gain


def _transpose_granules(xs):
    xs = list(xs)
    n = len(xs)
    block = lax.broadcasted_iota(jnp.int32, (1, LANES), 1) // SSM_GROUP
    bit = n // 2
    while bit:
        upper = (block & bit) != 0
        shift = SSM_GROUP * bit
        for lo in range(n):
            if lo & bit:
                continue
            hi = lo + bit
            x_lo, x_hi = xs[lo], xs[hi]
            xs[lo] = jnp.where(upper, pltpu.roll(x_hi, shift, 1), x_lo)
            xs[hi] = jnp.where(upper, x_hi, pltpu.roll(x_lo, LANES - shift, 1))
        bit //= 2
    return xs


def _mix_in_body(x_ref, nw_ref, w_ref, qg_ref, kg_ref, seg_ref, u_ref, *rest, prompt):
    h = _rms(x_ref[...], nw_ref[...]).astype(BF16)

    edges = [0]
    for width in IN_SEGMENTS:
        edges.append(edges[-1] + width)

    def proj(i):
        return jnp.dot(h, w_ref[:, edges[i]:edges[i + 1]], preferred_element_type=F32)

    u = proj(0)
    for blk in range(LANE_BLOCKS):
        u_ref[blk] = u[:, blk * LANES:(blk + 1) * LANES]
    seg_ones = seg_ref[...]
    q = _head_norm(proj(1), qg_ref[...], seg_ones) * (HEAD_DIM ** -0.5)
    k = _head_norm(proj(2), kg_ref[...], seg_ones)
    v = proj(3)
    if not prompt:
        q_ref, k_ref, v_ref, ga_ref, gb_ref = rest
        q_ref[...] = q
    else:
        k_ref, v_ref, ga_ref, gb_ref, fold_ref = rest[:5]
        by_residue = rest[5:5 + 3 * N_DIL]
        stage_ref = rest[-1]
    k_ref[...] = k
    v_ref[...] = v
    ga_ref[...] = jax.nn.sigmoid(proj(4))
    gb_ref[...] = jax.nn.sigmoid(proj(5))
    if not prompt:
        return

    rows = x_ref.shape[0]
    groups_per_block = LANES // SSM_GROUP
    for blk in range(LANE_BLOCKS):
        tokens = [u_ref[blk, pl.ds(s, rows // SSM_CHUNK, stride=SSM_CHUNK), :] for s in range(SSM_CHUNK)]
        for q_i, folded in enumerate(_transpose_granules(tokens)):
            pair = blk * (groups_per_block // 2) + q_i // 2
            fold_ref[pair, :, (q_i % 2) * LANES:(q_i % 2 + 1) * LANES] = folded.astype(BF16)

    slabs = SLOT_WIDTH // LANES
    for ti, x in enumerate((q * LOG2_E, k, v)):
        outs = by_residue[ti * N_DIL:(ti + 1) * N_DIL]
        outs[0][0] = x[:, :SLOT_WIDTH].astype(BF16)
        for g in range(1, N_DIL):
            d = DIL_RATES[g]
            for s in range(slabs):
                lo = g * SLOT_WIDTH + s * LANES
                stage_ref[ti, (g - 1) * slabs + s] = x[:, lo:lo + LANES]
            for r in range(d):
                for s in range(slabs):
                    piece = stage_ref[ti, (g - 1) * slabs + s, pl.ds(r, rows // d, stride=d), :]
                    outs[g][r, :, s * LANES:(s + 1) * LANES] = piece.astype(BF16)


def _mix_in(x, norm_w, w_in, q_gain, k_gain, seg_ones, tm, seq=None):
    t = x.shape[0]
    prompt = seq is not None

    def row(width):
        return pl.BlockSpec((tm, width), lambda i: (i, 0))

    def rows_f32(width):
        return jax.ShapeDtypeStruct((t, width), F32)

    out_specs = [_lane_blocked_spec(tm, lambda i: i)]
    out_shape = [jax.ShapeDtypeStruct((LANE_BLOCKS, t, LANES), F32)]
    scratch = []
    if not prompt:
        out_specs += [row(ATTN_WIDTH)] * 3 + [row(D_MODEL)] * 2
        out_shape += [rows_f32(ATTN_WIDTH)] * 3 + [rows_f32(D_MODEL)] * 2
    else:
        tiles = seq // tm
        tail = min(DIL_WINDOWS[-1], seq)
        skip = tiles - tail // tm
        tail_spec = pl.BlockSpec((None, tm, ATTN_WIDTH), lambda i: (i // tiles, jnp.maximum(i % tiles - skip, 0), 0))
        out_specs += [tail_spec] * 2 + [row(D_MODEL)] * 2
        out_shape += [jax.ShapeDtypeStruct((t // seq, tail, ATTN_WIDTH), F32)] * 2 + [rows_f32(D_MODEL)] * 2
        out_specs.append(pl.BlockSpec((SSM_PAIRS, tm // SSM_CHUNK, MXU_DIM), lambda i: (0, i, 0)))
        out_shape.append(jax.ShapeDtypeStruct((SSM_PAIRS, t // SSM_CHUNK, MXU_DIM), BF16))
        for _ in range(3):
            for d in DIL_RATES:
                out_specs.append(pl.BlockSpec((None, d, tm // d, SLOT_WIDTH),
                                              lambda i: (i // tiles, 0, i % tiles, 0)))
                out_shape.append(jax.ShapeDtypeStruct((t // seq, d, seq // d, SLOT_WIDTH), BF16))
        scratch = [pltpu.VMEM((3, (N_DIL - 1) * SLOT_WIDTH // LANES, tm, LANES), F32)]
    return pl.pallas_call(
        functools.partial(_mix_in_body, prompt=prompt),
        grid=(t // tm,),
        in_specs=[row(D_MODEL), _const_spec((1, D_MODEL)), _const_spec((D_MODEL, IN_WIDTH)),
                  _const_spec((1, ATTN_WIDTH)), _const_spec((1, ATTN_WIDTH)), _const_spec((MXU_DIM, MXU_DIM))],
        out_specs=out_specs,
        out_shape=out_shape,
        scratch_shapes=scratch,
        compiler_params=_params("arbitrary"),
        name="mix_in",
    )(x, norm_w, w_in, q_gain, k_gain, seg_ones)


def _cmul(ar, ai, br, bi):
    return ar * br - ai * bi, ar * bi + ai * br


def _ssm_prep_body(lrow_re_ref, lrow_im_ref, lcol_re_ref, lcol_im_ref, ldt_ref, bre_ref, bim_ref,
                   ct_re_ref, ct_im_ref, abar_ref, a8_ref, bstep_ref, cstep_ref, toep_ref, win_ref, wout_ref):
    n = SSM_CHUNK
    dt = jnp.exp(ldt_ref[...])

    def discretise(lam_re, lam_im):
        lr = jnp.minimum(lam_re, -1e-4)
        mag = jnp.exp(lr * dt)
        return lr, lam_im, mag * jnp.cos(lam_im * dt), mag * jnp.sin(lam_im * dt)

    def powers(ar, ai):
        out = [(jnp.ones_like(ar), jnp.zeros_like(ai))]
        for _ in range(n):
            out.append(_cmul(*out[-1], ar, ai))
        return out

    lr, li, ar, ai = discretise(lrow_re_ref[...], lrow_im_ref[...])
    den = lr * lr + li * li
    fr = ((ar - 1.0) * lr + ai * li) / den
    fi = (ai * lr - (ar - 1.0) * li) / den
    bbr, bbi = _cmul(fr, fi, bre_ref[...], bim_ref[...])
    row_pow = powers(ar, ai)
    abar_ref[0], abar_ref[1] = ar, ai
    a8_ref[0], a8_ref[1] = row_pow[n]

    def pair_halves(x):
        x = x.reshape((x.shape[0] // 2, 2) + x.shape[1:])
        return x[:, 0], x[:, 1]

    bstep_ref[...] = jnp.zeros_like(bstep_ref)
    cstep_ref[...] = jnp.zeros_like(cstep_ref)
    for part, x in enumerate((bbr, bbi)):
        x_e, x_o = pair_halves(x)
        bstep_ref[:, :SSM_GROUP, (2 * part) * SSM_STATE:(2 * part + 1) * SSM_STATE] = x_e
        bstep_ref[:, SSM_GROUP:, (2 * part + 1) * SSM_STATE:(2 * part + 2) * SSM_STATE] = x_o
    for part, ct_ref in enumerate((ct_re_ref, ct_im_ref)):
        x_e, x_o = pair_halves(ct_ref[:, :, :SSM_GROUP])
        cstep_ref[part, :, :SSM_STATE, :SSM_GROUP] = x_e
        cstep_ref[part, :, SSM_STATE:, SSM_GROUP:] = x_o

    toep_ref[...] = jnp.zeros_like(toep_ref)
    win_ref[...] = jnp.zeros_like(win_ref)
    wout_ref[...] = jnp.zeros_like(wout_ref)
    half = n * SSM_GROUP

    for s in range(n):
        w_re, w_im = _cmul(*row_pow[n - 1 - s], bbr, bbi)
        rows_e = slice(s * SSM_GROUP, (s + 1) * SSM_GROUP)
        rows_o = slice(half + s * SSM_GROUP, half + (s + 1) * SSM_GROUP)
        for part, x in enumerate((w_re, w_im)):
            x_e, x_o = pair_halves(x.astype(BF16))
            win_ref[:, rows_e, (2 * part) * SSM_STATE:(2 * part + 1) * SSM_STATE] = x_e
            win_ref[:, rows_o, (2 * part + 1) * SSM_STATE:(2 * part + 2) * SSM_STATE] = x_o

    _, _, ar_c, ai_c = discretise(lcol_re_ref[...], lcol_im_ref[...])
    col_pow = powers(ar_c, ai_c)
    lane_t = lax.broadcasted_iota(jnp.int32, (1, 1, LANES), 2) // SSM_GROUP

    def spread(first):
        re = im = jnp.zeros((1, 1, LANES), F32)
        for t in range(n):
            re = jnp.where(lane_t == t, col_pow[first + t][0], re)
            im = jnp.where(lane_t == t, col_pow[first + t][1], im)
        return re, im

    ct = (ct_re_ref[...], ct_im_ref[...])
    m0_re, m0_im = _cmul(*ct, *spread(0))
    m1_re, m1_im = _cmul(*ct, *spread(1))

    for part, x in enumerate((m1_re, -m1_im)):
        x_e, x_o = pair_halves(x.astype(BF16))
        wout_ref[:, (2 * part) * SSM_STATE:(2 * part + 1) * SSM_STATE, :half] = x_e
        wout_ref[:, (2 * part + 1) * SSM_STATE:(2 * part + 2) * SSM_STATE, half:] = x_o

    nn = (((2,), (1,)), ((0,), (0,)))
    hp = lax.Precision.HIGHEST
    kern = (lax.dot_general(bbr, m0_re, nn, precision=hp, preferred_element_type=F32)
            - lax.dot_general(bbi, m0_im, nn, precision=hp, preferred_element_type=F32))
    lane = lax.broadcasted_iota(jnp.int32, (1, 1, LANES), 2)
    for s in range(n):
        shifted = kern if s == 0 else jnp.where(lane >= s * SSM_GROUP, pltpu.roll(kern, s * SSM_GROUP, 2), 0.0)
        x_e, x_o = pair_halves(shifted.astype(BF16))
        toep_ref[:, s * SSM_GROUP:(s + 1) * SSM_GROUP, :half] = x_e
        toep_ref[:, half + s * SSM_GROUP:half + (s + 1) * SSM_GROUP, half:] = x_o


def _ssm_prep(lam_re, lam_im, log_dt, b_re_t, b_im_t, c_re, c_im, groups_per_step=16):
    g, p, c = SSM_GROUPS, SSM_STATE, SSM_GROUP
    gb = groups_per_step

    def spec(*tail):
        return pl.BlockSpec((gb,) + tail, lambda i: (i,) + (0,) * len(tail))

    def stacked(*tail):
        return pl.BlockSpec((2, gb) + tail, lambda i: (0, i) + (0,) * len(tail))

    pair_spec = pl.BlockSpec((gb // 2, MXU_DIM, MXU_DIM), lambda i: (i, 0, 0))
    pair_shape = jax.ShapeDtypeStruct((SSM_PAIRS, MXU_DIM, MXU_DIM), BF16)
    ct_re = jnp.tile(c_re.transpose(0, 2, 1), (1, 1, SSM_CHUNK))
    ct_im = jnp.tile(c_im.transpose(0, 2, 1), (1, 1, SSM_CHUNK))
    return pl.pallas_call(
        _ssm_prep_body,
        grid=(g // gb,),
        in_specs=[spec(1, p), spec(1, p), spec(p, 1), spec(p, 1), spec(1, 1), spec(c, p), spec(c, p),
                  spec(p, LANES), spec(p, LANES)],
        out_specs=[stacked(1, p), stacked(1, p),
                   pl.BlockSpec((gb // 2, 2 * c, 4 * p), lambda i: (i, 0, 0)),
                   pl.BlockSpec((2, gb // 2, 2 * p, 2 * c), lambda i: (0, i, 0, 0)),
                   pair_spec, pair_spec, pair_spec],
        out_shape=[jax.ShapeDtypeStruct((2, g, 1, p), F32), jax.ShapeDtypeStruct((2, g, 1, p), F32),
                   jax.ShapeDtypeStruct((SSM_PAIRS, 2 * c, 4 * p), F32),
                   jax.ShapeDtypeStruct((2, SSM_PAIRS, 2 * p, 2 * c), F32),
                   pair_shape, pair_shape, pair_shape],
        compiler_params=_params("parallel"),
        name="ssm_prep",
    )(lam_re.reshape(g, 1, p), lam_im.reshape(g, 1, p), lam_re.reshape(g, p, 1), lam_im.reshape(g, p, 1),
      log_dt.reshape(g, 1, 1), b_re_t, b_im_t, ct_re, ct_im)


def _ssm_body(lhs_ref, toep_ref, win_ref, wout_ref, a8_ref, yfl_ref, hfin_ref, st_ref, carry_ref, *, rows):
    i = pl.program_id(1)

    @pl.when(i == 0)
    def _():
        carry_ref[...] = jnp.zeros_like(carry_ref)

    def state_in(r, _):
        b = jnp.dot(lhs_ref[r], win_ref[r], preferred_element_type=F32)
        st_ref[0, pl.ds(r, rows, stride=SSM_ROW_PITCH), :] = b[:, :LANES]
        st_ref[1, pl.ds(r, rows, stride=SSM_ROW_PITCH), :] = b[:, LANES:]
        return 0

    lax.fori_loop(0, SSM_PAIRS, state_in, 0, unroll=4)

    a_re = a8_ref[0]
    a_im = a8_ref[1]

    def step(j, h):
        h_re, h_im = h
        base = pl.multiple_of(j * SSM_ROW_PITCH, SUBLANES)
        n_re = a_re * h_re - a_im * h_im + st_ref[0, pl.ds(base, SSM_PAIRS), :]
        n_im = a_re * h_im + a_im * h_re + st_ref[1, pl.ds(base, SSM_PAIRS), :]
        st_ref[0, pl.ds(base, SSM_PAIRS), :] = h_re
        st_ref[1, pl.ds(base, SSM_PAIRS), :] = h_im
        return n_re, n_im

    h_re, h_im = lax.fori_loop(0, rows, step, (carry_ref[0], carry_ref[1]), unroll=4)
    carry_ref[0] = h_re
    carry_ref[1] = h_im
    hfin_ref[0] = h_re
    hfin_ref[1] = h_im

    def chunk_out(r, _):
        hcat = jnp.concatenate([st_ref[0, pl.ds(r, rows, stride=SSM_ROW_PITCH), :],
                                st_ref[1, pl.ds(r, rows, stride=SSM_ROW_PITCH), :]], axis=1).astype(BF16)
        yfl_ref[r] = (jnp.dot(lhs_ref[r], toep_ref[r], preferred_element_type=F32)
                      + jnp.dot(hcat, wout_ref[r], preferred_element_type=F32))
        return 0

    lax.fori_loop(0, SSM_PAIRS, chunk_out, 0, unroll=4)


def _ssm_prompt(lhs, toep, win, wout, a8, batch, seq, tile):
    rows = tile // SSM_CHUNK
    n_tiles = seq // tile
    tok = pl.BlockSpec((SSM_PAIRS, rows, MXU_DIM), lambda b, i: (0, b * n_tiles + i, 0))
    pair_w = _const_spec((SSM_PAIRS, MXU_DIM, MXU_DIM))
    return pl.pallas_call(
        functools.partial(_ssm_body, rows=rows),
        grid=(batch, n_tiles),
        in_specs=[tok, pair_w, pair_w, pair_w, _const_spec((2, SSM_PAIRS, LANES))],
        out_specs=[tok, pl.BlockSpec((None, 2, SSM_PAIRS, LANES), lambda b, i: (b, 0, 0, 0))],
        out_shape=[jax.ShapeDtypeStruct((SSM_PAIRS, batch * seq // SSM_CHUNK, MXU_DIM), F32),
                   jax.ShapeDtypeStruct((batch, 2, SSM_PAIRS, LANES), F32)],
        scratch_shapes=[pltpu.VMEM((2, rows * SSM_ROW_PITCH, LANES), F32),
                        pltpu.VMEM((2, SSM_PAIRS, LANES), F32)],
        compiler_params=_params("parallel", "arbitrary"),
        name="ssm_prompt",
    )(lhs, toep, win, wout, a8)


def _ssm_step_body(u_ref, hre_ref, him_ref, abar_ref, bstep_ref, cstep_ref, y_ref, ore_ref, oim_ref):
    hp = lax.Precision.HIGHEST
    pair_ch = 2 * SSM_GROUP
    pairs_per_block = LANES // pair_ch
    for r in range(SSM_PAIRS):
        blk = r // pairs_per_block
        ch = slice((r % pairs_per_block) * pair_ch, (r % pairs_per_block + 1) * pair_ch)
        st = slice(r * LANES, (r + 1) * LANES)
        bu = jnp.dot(u_ref[blk, :, ch], bstep_ref[r], precision=hp, preferred_element_type=F32)
        a_re = abar_ref[0, r:r + 1, :]
        a_im = abar_ref[1, r:r + 1, :]
        h_re = hre_ref[:, st]
        h_im = him_ref[:, st]
        n_re = a_re * h_re - a_im * h_im + bu[:, :LANES]
        n_im = a_re * h_im + a_im * h_re + bu[:, LANES:]
        ore_ref[:, st] = n_re
        oim_ref[:, st] = n_im
        y_ref[blk, :, ch] = (jnp.dot(n_re, cstep_ref[0, r], precision=hp, preferred_element_type=F32)
                             - jnp.dot(n_im, cstep_ref[1, r], precision=hp, preferred_element_type=F32))


def _ssm_step(u, h_re, h_im, abar, bstep, cstep):
    b = u.shape[1]
    width = SSM_GROUPS * SSM_STATE
    return pl.pallas_call(
        _ssm_step_body,
        out_shape=[jax.ShapeDtypeStruct((LANE_BLOCKS, b, LANES), F32),
                   jax.ShapeDtypeStruct((b, width), F32),
                   jax.ShapeDtypeStruct((b, width), F32)],
        compiler_params=pltpu.CompilerParams(vmem_limit_bytes=VMEM_LIMIT_BYTES),
        name="ssm_step",
    )(u, h_re, h_im, abar, bstep, cstep)


def _attn_prompt_body(*refs, part_rows):
    per_group = 5
    ins = refs[:per_group * N_DIL]
    o_ref, m_ref, l_ref, acc_ref, bias_ref = refs[per_group * N_DIL:]
    part = pl.program_id(1)
    tq = KEYS_BACK
    row = lax.broadcasted_iota(jnp.int32, (tq, 2 * tq), 0)
    col = lax.broadcasted_iota(jnp.int32, (tq, 2 * tq), 1)
    back = row + tq - col
    in_window = (back >= 0) & (back <= KEYS_BACK)
    first_head = lax.broadcasted_iota(jnp.int32, (1, LANES), 1) < HEAD_DIM
    nt = (((1,), (1,)), ((), ()))

    order = tuple(reversed(range(N_DIL)))
    ones = jnp.ones((2 * tq, LANES), BF16)
    for g in order:
        q_ref, k_ref, kb_ref, v_ref, vb_ref = ins[per_group * g:per_group * (g + 1)]
        d = DIL_RATES[g]
        blocks_per_residue = part_rows // d // tq
        dist = (back * d).astype(F32)
        for h in range(HEADS):
            bias = jnp.where(in_window, -(_slope(g, h) * LOG2_E) * dist, -jnp.inf)
            bias_ref[h] = bias
            bias_ref[HEADS + h] = jnp.where(col >= tq, bias, -jnp.inf)

        def block(mi, _, g=g, d=d, q_ref=q_ref, k_ref=k_ref, kb_ref=kb_ref, v_ref=v_ref, vb_ref=vb_ref,
                  blocks_per_residue=blocks_per_residue):
            residue = mi // blocks_per_residue
            n = mi % blocks_per_residue
            cur = pl.multiple_of(n * tq, tq)
            prev = pl.multiple_of(jnp.maximum(n - 1, 0) * tq, tq)
            bias_at = jnp.where((n == 0) & (part == 0), HEADS, 0)
            token0 = residue + d * tq * n
            rows = pl.ds(pl.multiple_of(token0, tq), tq) if d == 1 else pl.ds(token0, tq, stride=d)
            pairs = range(HEADS // 2)
            first, final = g == order[0], g == order[-1]
            old = None if first else [(m_ref[pair, rows, :], l_ref[pair, rows, :], acc_ref[pair, rows, :])
                                      for pair in pairs]
            new = []
            for pair in pairs:
                lanes = slice(pair * LANES, (pair + 1) * LANES)
                qp = q_ref[residue, pl.ds(cur, tq), lanes]
                k_prev = jnp.where(n == 0, kb_ref[residue, :, lanes], k_ref[residue, pl.ds(prev, tq), lanes])
                v_prev = jnp.where(n == 0, vb_ref[residue, :, lanes], v_ref[residue, pl.ds(prev, tq), lanes])
                kp = jnp.concatenate([k_prev, k_ref[residue, pl.ds(cur, tq), lanes]], axis=0)
                vp = jnp.concatenate([v_prev, v_ref[residue, pl.ds(cur, tq), lanes]], axis=0)
                vp = jnp.concatenate([vp, ones], axis=1)
                stats = []
                for e in range(2):
                    qm = jnp.where(first_head if e == 0 else ~first_head, qp, jnp.zeros_like(qp))
                    s = lax.dot_general(qm, kp, nt, preferred_element_type=F32)
                    s = s + bias_ref[bias_at + 2 * pair + e]
                    m = jnp.max(s, axis=-1, keepdims=True)
                    p = jnp.exp2(s - m)
                    pv = jnp.dot(p.astype(BF16), vp, preferred_element_type=F32)
                    stats.append((m, pv[:, LANES:], pv[:, :LANES]))
                m_new, l_new, acc_new = (jnp.where(first_head, a, b) for a, b in zip(*stats))
                if not first:
                    m_old, l_old, acc_old = old[pair]
                    m_tot = jnp.maximum(m_old, m_new)
                    w_old = jnp.exp2(m_old - m_tot)
                    w_new = jnp.exp2(m_new - m_tot)
                    l_new = w_old * l_old + w_new * l_new
                    acc_new = w_old * acc_old + w_new * acc_new
                    m_new = m_tot
                new.append((m_new, l_new, acc_new))
            for pair, (m_new, l_new, acc_new) in zip(pairs, new):
                if not final:
                    m_ref[pair, rows, :] = m_new
                    l_ref[pair, rows, :] = l_new
                    acc_ref[pair, rows, :] = acc_new
                else:
                    o_ref[pair, rows, :] = acc_new / l_new
            return 0

        lax.fori_loop(0, part_rows // tq, block, 0, unroll=2)


def _attn_prompt(qkv, batch, seq):
    slabs = SLOT_WIDTH // LANES
    tq = KEYS_BACK
    part_rows = DIL_RATES[-1] * tq
    parts = seq // part_rows
    args, specs = [], []
    for g, d in enumerate(DIL_RATES):
        rows = part_rows // d
        cur = pl.BlockSpec((None, d, rows, SLOT_WIDTH), lambda b, p: (b, 0, p, 0))
        before = pl.BlockSpec((None, d, tq, SLOT_WIDTH),
                              functools.partial(lambda b, p, step: (b, 0, jnp.maximum(p * step - 1, 0), 0),
                                                step=rows // tq))
        q, k, v = qkv[g], qkv[N_DIL + g], qkv[2 * N_DIL + g]
        args += [q, k, k, v, v]
        specs += [cur, cur, before, cur, before]
    running = pltpu.VMEM((slabs, part_rows, LANES), F32)
    return pl.pallas_call(
        functools.partial(_attn_prompt_body, part_rows=part_rows),
        grid=(batch, parts),
        in_specs=specs,
        out_specs=pl.BlockSpec((slabs, part_rows, LANES), lambda b, p: (0, b * parts + p, 0)),
        out_shape=jax.ShapeDtypeStruct((slabs, batch * seq, LANES), F32),
        scratch_shapes=[running, running, running, pltpu.VMEM((2 * HEADS, tq, 2 * tq), F32)],
        compiler_params=_params("parallel", "arbitrary"),
        name="attn_prompt",
    )(*args)


def _as_column(row_vec):
    n = row_vec.shape[1]
    eye = lax.broadcasted_iota(jnp.int32, (n, n), 0) == lax.broadcasted_iota(jnp.int32, (n, n), 1)
    return jnp.sum(jnp.where(eye, row_vec, 0.0), axis=1, keepdims=True)


def _sample_window_group(g, q_row, k_row, v_row, c_ref, n_ref):
    head_row = lax.broadcasted_iota(jnp.int32, (SUBLANES, SLOT_WIDTH), 0)
    own_head = lax.broadcasted_iota(jnp.int32, (SUBLANES, SLOT_WIDTH), 1) // HEAD_DIM == head_row
    head_col = lax.broadcasted_iota(jnp.int32, (SUBLANES, 1), 0)
    w = DIL_WINDOWS[g]
    d = DIL_RATES[g]
    cols = slice(g * SLOT_WIDTH, (g + 1) * SLOT_WIDTH)
    q_g, k_new, v_new = q_row[:, cols], k_row[:, cols], v_row[:, cols]
    kt = c_ref[0].reshape(SLOT_WIDTH, w)
    vt = c_ref[1].reshape(SLOT_WIDTH, w)
    q_heads = jnp.where(own_head, q_g, 0.0)
    s = jnp.dot(q_heads.astype(BF16), kt.astype(BF16), preferred_element_type=F32)
    dist = w - lax.broadcasted_iota(jnp.int32, (1, w), 1)
    slope = functools.reduce(lambda acc, h: jnp.where(head_col == h, _slope(g, h), acc), range(HEADS), 0.0)
    s = jnp.where((dist & (d - 1)) == 0, s - slope * dist.astype(F32), -jnp.inf)
    s_new = jnp.sum(q_heads * k_new, axis=1, keepdims=True)
    m = jnp.maximum(jnp.max(s, axis=1, keepdims=True), s_new)
    p = jnp.exp(s - m)
    p_new = jnp.exp(s_new - m)
    den = jnp.sum(p, axis=1, keepdims=True) + p_new
    pv = lax.dot_general(p.astype(BF16), vt.astype(BF16), (((1,), (1,)), ((), ())),
                         preferred_element_type=F32)
    o_heads = (pv + p_new * v_new) / den
    last = lax.broadcasted_iota(jnp.int32, (SLOT_WIDTH, w), 1) == w - 1
    shape = (HEADS, HEAD_DIM, w)
    n_ref[0] = jnp.where(last, _as_column(k_new), pltpu.roll(kt, w - 1, 1)).reshape(shape)
    n_ref[1] = jnp.where(last, _as_column(v_new), pltpu.roll(vt, w - 1, 1)).reshape(shape)
    return (jnp.sum(jnp.where(own_head, o_heads, 0.0), axis=0, keepdims=True),
            jnp.sum(jnp.where(own_head, m + jnp.log(den), 0.0), axis=0, keepdims=True))


def _mix_groups(outs, lses):
    top = functools.reduce(jnp.maximum, lses)
    wts = [jnp.exp(l - top) for l in lses]
    return sum(w_g * o_g for w_g, o_g in zip(wts, outs)) / sum(wts)


def _ffn_windows_body(x_ref, nw_ref, *refs):
    w_refs = refs[:3]
    q_ref, k_ref, v_ref, c0_ref, c1_ref, c2_ref, o_ref, ob_ref, n0_ref, n1_ref, n2_ref = refs[3:]
    b = pl.program_id(0)
    rows = [r[pl.ds(b, 1), :] for r in (q_ref, k_ref, v_ref)]
    windows = ((c0_ref, n0_ref), (c1_ref, n1_ref), (c2_ref, n2_ref))
    results = {}

    def job(g):
        def run():
            results[g] = _sample_window_group(g, *rows, *windows[g])
        return run

    _ffn_body(x_ref, nw_ref, *w_refs, o_ref, side_jobs=[job(g) for g in reversed(range(N_DIL))])
    outs, lses = zip(*(results[g] for g in range(N_DIL)))
    ob_ref[...] = _mix_groups(outs, lses)


def _ffn_with_sample_windows(x, norm_w, weights, q, k, v, caches):
    t = x.shape[0]
    b = q.shape[0]
    assert t % b == 0 and (t // b) % SUBLANES == 0
    tm = t // b
    row = pl.BlockSpec((tm, D_MODEL), lambda i: (i, 0))
    full = _const_spec((b, ATTN_WIDTH))
    win = [pl.BlockSpec((None, 2, HEADS, HEAD_DIM, w), lambda i: (i, 0, 0, 0, 0)) for w in DIL_WINDOWS]
    outs = pl.pallas_call(
        _ffn_windows_body,
        grid=(b,),
        in_specs=[row, _const_spec((1, D_MODEL))] + _ffn_weight_specs() + [full, full, full] + win,
        out_specs=[row, pl.BlockSpec((None, 1, SLOT_WIDTH), lambda i: (i, 0, 0))] + win,
        out_shape=([jax.ShapeDtypeStruct((t, D_MODEL), F32), jax.ShapeDtypeStruct((b, 1, SLOT_WIDTH), F32)]
                   + [jax.ShapeDtypeStruct(c.shape, F32) for c in caches]),
        compiler_params=_params("parallel"),
        name="ffn_windows",
    )(x, norm_w, *weights, q, k, v, *caches)
    return outs[0], outs[1].reshape(b, SLOT_WIDTH), outs[2:]


def _kv_window_body(*refs):
    k_refs, v_refs, out_refs = refs[:N_DIL], refs[N_DIL:2 * N_DIL], refs[2 * N_DIL:]
    for g, o_ref in enumerate(out_refs):
        for t, x_ref in enumerate((k_refs[g], v_refs[g])):
            for pair in range(HEADS // 2):
                xt = x_ref[:, pair * LANES:(pair + 1) * LANES].T
                o_ref[t, 2 * pair] = xt[:HEAD_DIM]
                o_ref[t, 2 * pair + 1] = xt[HEAD_DIM:]


def _kv_windows(k_tail, v_tail):
    batch, tail, _ = k_tail.shape
    keeps = [min(w, tail) for w in DIL_WINDOWS]
    assert all(tail % w == 0 for w in keeps)
    src = [pl.BlockSpec((None, w, SLOT_WIDTH), functools.partial(lambda b, at, g: (b, at, g), at=tail // w - 1, g=g))
           for g, w in enumerate(keeps)]
    return pl.pallas_call(
        _kv_window_body,
        grid=(batch,),
        in_specs=src + src,
        out_specs=[pl.BlockSpec((None, 2, HEADS, HEAD_DIM, w), lambda b: (b, 0, 0, 0, 0)) for w in keeps],
        out_shape=[jax.ShapeDtypeStruct((batch, 2, HEADS, HEAD_DIM, w), F32) for w in keeps],
        compiler_params=_params("parallel"),
        name="kv_windows",
    )(*([k_tail] * N_DIL + [v_tail] * N_DIL))


def _mix_out_body(x_ref, y_ref, u_ref, d_ref, ob_ref,
                  ga_ref, gb_ref, wglu_ref, wpa_ref, wpb_ref, wout_ref, out_ref, *unfold):
    tm = x_ref.shape[0]
    if unfold:
        y_ref, folded_ref = unfold[0], y_ref
        groups_per_block = LANES // SSM_GROUP
    pieces = MIX_OUT_PIECES if unfold else 1
    rows = tm // pieces
    for piece in range(pieces):
        at = slice(piece * rows, (piece + 1) * rows)
        if unfold:
            chunks = rows // SSM_CHUNK
            chunk_at = slice(piece * chunks, (piece + 1) * chunks)
            for blk in range(LANE_BLOCKS):
                groups = [folded_ref[blk * (groups_per_block // 2) + q_i // 2, chunk_at,
                                     (q_i % 2) * LANES:(q_i % 2 + 1) * LANES] for q_i in range(groups_per_block)]
                for t, rows_t in enumerate(_transpose_granules(groups)):
                    y_ref[blk, pl.ds(piece * rows + t, chunks, stride=SSM_CHUNK), :] = rows_t
        y_raw = jnp.concatenate([y_ref[blk, at, :] for blk in range(LANE_BLOCKS)], axis=1)
        u = jnp.concatenate([u_ref[blk, at, :] for blk in range(LANE_BLOCKS)], axis=1)
        y = jax.nn.gelu(y_raw + d_ref[...] * u)
        yb = y.astype(BF16)
        y_a = y * jax.nn.sigmoid(jnp.dot(yb, wglu_ref[...], preferred_element_type=F32))
        branch_a = jnp.dot(y_a.astype(BF16), wpa_ref[...], preferred_element_type=F32)

        o_b = jnp.concatenate([ob_ref[s, at, :] for s in range(SLOT_WIDTH // LANES)], axis=1)
        branch_b = jnp.dot(o_b.astype(BF16), wpb_ref[...], preferred_element_type=F32)

        merged = ga_ref[at, :] * branch_a + gb_ref[at, :] * branch_b
        out_ref[at, :] = x_ref[at, :] + jnp.dot(merged.astype(BF16), wout_ref[...], preferred_element_type=F32)


def _mix_out(x, y, u, ssm_d, o_b, ga, gb, w_glu, w_pa, w_pb, w_out, tm, folded_y):
    t = x.shape[0]
    wide = pl.BlockSpec((tm, D_MODEL), lambda i: (i, 0))
    blocked = _lane_blocked_spec(tm, lambda i: i)
    y_spec = pl.BlockSpec((SSM_PAIRS, tm // SSM_CHUNK, MXU_DIM), lambda i: (0, i, 0)) if folded_y else blocked
    scratch = [pltpu.VMEM((LANE_BLOCKS, tm, LANES), F32)] if folded_y else []
    slot = pl.BlockSpec((SLOT_WIDTH // LANES, tm, LANES), lambda i: (0, i, 0))
    return pl.pallas_call(
        _mix_out_body,
        grid=(t // tm,),
        in_specs=[wide, y_spec, blocked, _const_spec((1, D_MODEL)), slot, wide, wide,
                  _const_spec((D_MODEL, D_MODEL)), _const_spec((D_MODEL, D_MODEL)),
                  _const_spec((SLOT_WIDTH, D_MODEL)), _const_spec((D_MODEL, D_MODEL))],
        out_specs=wide,
        out_shape=jax.ShapeDtypeStruct((t, D_MODEL), F32),
        scratch_shapes=scratch,
        compiler_params=_params("parallel"),
        name="mix_out",
    )(x, y, u, ssm_d, o_b, ga, gb, w_glu, w_pa, w_pb, w_out)


def _head_segment_ones():
    head = jnp.arange(MXU_DIM) // HEAD_DIM
    return (head[:, None] == head[None, :]).astype(BF16)


def kernel(x_prompt, x_sample, state_ssm_re, state_ssm_im, cache_kv_w128, cache_kv_w512, cache_kv_w2048, ffn1_norm, ffn1_w_gate, ffn1_w_up, ffn1_w_down, mix_norm, w_in, ssm_lambda_re, ssm_lambda_im, ssm_b_re, ssm_b_im, ssm_c_re, ssm_c_im, ssm_d, ssm_log_dt, w_glu, q_gain, k_gain, w_proj_a, w_proj_b, w_out, ffn2_norm, ffn2_w_gate, ffn2_w_up, ffn2_w_down):
    depth = ffn1_norm.shape[0]
    assert depth == 1, "single-layer step"
    batch, seq, _ = x_prompt.shape
    dec_batch, dec_seq, _ = x_sample.shape
    assert dec_seq == 1 and seq % (DIL_RATES[-1] * KEYS_BACK) == 0
    layer = 0
    bf = lambda w: w[layer].astype(BF16)
    vec = lambda w: w[layer][None]
    ffn1 = (vec(ffn1_norm), (bf(ffn1_w_gate), bf(ffn1_w_up), bf(ffn1_w_down)))
    mix_norm, ssm_d = vec(mix_norm), vec(ssm_d)
    q_gain_t = jnp.tile(vec(q_gain), (1, N_DIL * HEADS))
    k_gain_t = jnp.tile(vec(k_gain), (1, N_DIL * HEADS))
    seg_ones = _head_segment_ones()

    abar, a8, bstep, cstep, toep, win, wout = _ssm_prep(
        ssm_lambda_re[layer], ssm_lambda_im[layer], ssm_log_dt[layer],
        ssm_b_re[layer].transpose(0, 2, 1), ssm_b_im[layer].transpose(0, 2, 1),
        ssm_c_re[layer], ssm_c_im[layer])
    a8 = a8.reshape(2, SSM_PAIRS, LANES)
    abar = abar.reshape(2, SSM_PAIRS, LANES)

    def mix_in(x1, tm, seq=None):
        return _mix_in(x1, mix_norm, w_in_b, q_gain_t, k_gain_t, seg_ones, tm, seq)

    def mix_out(x1, y, u, o_b, ga, gb, tm, folded_y):
        return _mix_out(x1, y, u, ssm_d, o_b, ga, gb, w_glu_b, w_pa_b, w_pb_b, w_out_b, tm, folded_y)

    tm = 512
    tm_ffn = 1024
    xp = x_prompt.reshape(batch * seq, D_MODEL)
    later = [w[layer] for w in (ffn2_w_gate, ffn2_w_up, ffn2_w_down, w_in, w_glu, w_proj_a, w_proj_b, w_out)]
    x1, *later_b = _ffn(xp, *ffn1, tm_ffn, convert=later)
    ffn2 = (vec(ffn2_norm), tuple(later_b[:3]))
    w_in_b, w_glu_b, w_pa_b, w_pb_b, w_out_b = later_b[3:]

    xs = x_sample.reshape(dec_batch, D_MODEL)
    xs1 = _ffn(xs, *ffn1, dec_batch)
    us, qs, ks, vs, gas, gbs = mix_in(xs1, dec_batch)
    caches = [c[layer].transpose(0, 2, 3, 4, 1) for c in (cache_kv_w128, cache_kv_w512, cache_kv_w2048)]

    u, k_tail, v_tail, ga, gb, u_folded, *qkv = mix_in(x1, tm, seq)
    y, h_fin = _ssm_prompt(u_folded, toep, win, wout, a8, batch, seq, tile=2048)
    o_b = _attn_prompt(qkv, batch, seq)
    x2 = mix_out(x1, y, u, o_b, ga, gb, tm, folded_y=True)
    yp, o_bs, new_caches = _ffn_with_sample_windows(x2, *ffn2, qs, ks, vs, caches)
    yp = yp.reshape(batch, seq, D_MODEL)
    h_fin = h_fin.reshape(batch, 2, SSM_GROUPS, SSM_STATE)
    p_re = h_fin[None, :, 0]
    p_im = h_fin[None, :, 1]
    p_kv = [c.transpose(0, 4, 1, 2, 3)[None] for c in _kv_windows(k_tail, v_tail)]

    width = SSM_GROUPS * SSM_STATE
    ys_ssm, s_re, s_im = _ssm_step(us, state_ssm_re[layer].reshape(dec_batch, width),
                                   state_ssm_im[layer].reshape(dec_batch, width),
                                   abar, bstep, cstep)
    o_bs = o_bs.reshape(dec_batch, SLOT_WIDTH // LANES, LANES).transpose(1, 0, 2)
    xs2 = mix_out(xs1, ys_ssm, us, o_bs, gas, gbs, dec_batch, folded_y=False)
    ys = _ffn(xs2, *ffn2, dec_batch)
    ys = ys.reshape(dec_batch, 1, D_MODEL)
    s_re = s_re.reshape(1, dec_batch, SSM_GROUPS, SSM_STATE)
    s_im = s_im.reshape(1, dec_batch, SSM_GROUPS, SSM_STATE)
    s_kv = [c.transpose(0, 4, 1, 2, 3)[None] for c in new_caches]

    return (yp, ys, p_re, p_im, p_kv[0], p_kv[1], p_kv[2], s_re, s_im, s_kv[0], s_kv[1], s_kv[2])
```

```python
import functools

import jax
import jax.numpy as jnp
from jax import lax
from jax.experimental import pallas as pl
from jax.experimental.pallas import tpu as pltpu

F32 = jnp.float32
BF16 = jnp.bfloat16

D_MODEL = 1024
SSM_GROUP = 16
SSM_GROUPS = 64
SSM_STATE = 64
SSM_PAIRS = SSM_GROUPS // 2
HEAD_DIM = 64
HEADS = 4
DIL_WINDOWS = (128, 512, 2048)
DIL_RATES = (1, 4, 16)
N_DIL = 3
KEYS_BACK = 128
ATTN_WIDTH = N_DIL * HEADS * HEAD_DIM
SLOT_WIDTH = HEADS * HEAD_DIM
D_FF = 2816
RMS_EPS = 1e-6
ALIBI_MAX_EXP = 8.0
LOG2_E = 1.4426950408889634
IN_SEGMENTS = (D_MODEL, ATTN_WIDTH, ATTN_WIDTH, ATTN_WIDTH, D_MODEL, D_MODEL)
IN_WIDTH = sum(IN_SEGMENTS)

LANES = 128
SUBLANES = 8
MXU_DIM = 256
VMEM_LIMIT_BYTES = 56 * 1024 * 1024

LANE_BLOCKS = D_MODEL // LANES

SSM_CHUNK = SUBLANES
SSM_ROW_PITCH = 40
FF_CHUNKS = ((0, 1024), (1024, 2048), (2048, 2816))
MIX_OUT_PIECES = 2


def _slope(group, head):
    return 2.0 ** (-ALIBI_MAX_EXP * (group * HEADS + head + 1) / (N_DIL * HEADS))


def _const_spec(shape):
    zeros = (0,) * len(shape)
    return pl.BlockSpec(shape, lambda *_: zeros, pipeline_mode=pl.Buffered(1))


def _lane_blocked_spec(rows, row_block):
    return pl.BlockSpec((LANE_BLOCKS, rows, LANES), lambda *idx: (0, row_block(*idx), 0))


def _params(*semantics):
    return pltpu.CompilerParams(dimension_semantics=semantics, vmem_limit_bytes=VMEM_LIMIT_BYTES)


def _rms(x, w):
    return x * lax.rsqrt(jnp.mean(x * x, axis=-1, keepdims=True) + RMS_EPS) * w


def _split_bf16(x, terms):
    parts = []
    for _ in range(terms):
        p = x.astype(BF16)
        parts.append(p)
        x = x - p.astype(F32)
    return parts


def _ffn_body(x_ref, nw_ref, wg_ref, wu_ref, wd_ref, *rest, side_jobs=()):
    n_cast = (len(rest) - 1) // 2
    o_ref = rest[n_cast]
    for src_ref, dst_ref in zip(rest[:n_cast], rest[n_cast + 1:]):
        dst_ref[...] = src_ref[...].astype(BF16)
    x = x_ref[...]
    h = _rms(x, nw_ref[...]).astype(BF16)
    acc = jnp.zeros_like(x)
    side_jobs = list(side_jobs)
    for lo, hi in FF_CHUNKS:
        g = jnp.dot(h, wg_ref[:, lo:hi], preferred_element_type=F32)
        u = jnp.dot(h, wu_ref[:, lo:hi], preferred_element_type=F32)
        a = (jax.nn.silu(g) * u).astype(BF16)
        acc = acc + jnp.dot(a, wd_ref[lo:hi, :], preferred_element_type=F32)
        if side_jobs:
            side_jobs.pop(0)()
    for job in side_jobs:
        job()
    o_ref[...] = x + 0.5 * acc


def _ffn_weight_specs():
    return [_const_spec((D_MODEL, D_FF)), _const_spec((D_MODEL, D_FF)), _const_spec((D_FF, D_MODEL))]


def _ffn(x, norm_w, weights, tm, convert=()):
    t = x.shape[0]
    steps = t // tm
    row = pl.BlockSpec((tm, D_MODEL), lambda i: (i, 0))
    side = []
    for w in convert:
        rows = w.shape[0] // steps
        assert rows * steps == w.shape[0] and rows % (2 * SUBLANES) == 0
        side.append(pl.BlockSpec((rows, w.shape[1]), lambda i: (i, 0)))
    outs = pl.pallas_call(
        _ffn_body,
        grid=(steps,),
        in_specs=[row, _const_spec((1, D_MODEL))] + _ffn_weight_specs() + side,
        out_specs=[row] + side,
        out_shape=[jax.ShapeDtypeStruct((t, D_MODEL), F32)] + [jax.ShapeDtypeStruct(w.shape, BF16) for w in convert],
        compiler_params=_params("parallel"),
        name="ffn",
    )(x, norm_w, *weights, *convert)
    return outs if convert else outs[0]


def _head_norm(x, gain, seg_ones):
    parts = _split_bf16(x * x, 2)
    blocks = []
    for lo in range(0, x.shape[1], MXU_DIM):
        blocks.append(sum(jnp.dot(p[:, lo:lo + MXU_DIM], seg_ones, preferred_element_type=F32) for p in parts))
    ss = jnp.concatenate(blocks, axis=1)
    return x * lax.rsqrt(ss * (1.0 / HEAD_DIM) + RMS_EPS) * gain


def _transpose_granules(xs):
    xs = list(xs)
    n = len(xs)
    block = lax.broadcasted_iota(jnp.int32, (1, LANES), 1) // SSM_GROUP
    bit = n // 2
    while bit:
        upper = (block & bit) != 0
        shift = SSM_GROUP * bit
        for lo in range(n):
            if lo & bit:
                continue
            hi = lo + bit
            x_lo, x_hi = xs[lo], xs[hi]
            xs[lo] = jnp.where(upper, pltpu.roll(x_hi, shift, 1), x_lo)
            xs[hi] = jnp.where(upper, x_hi, pltpu.roll(x_lo, LANES - shift, 1))
        bit //= 2
    return xs


def _mix_in_body(x_ref, nw_ref, w_ref, qg_ref, kg_ref, seg_ref, u_ref, *rest, prompt):
    h = _rms(x_ref[...], nw_ref[...]).astype(BF16)

    edges = [0]
    for width in IN_SEGMENTS:
        edges.append(edges[-1] + width)

    def proj(i):
        return jnp.dot(h, w_ref[:, edges[i]:edges[i + 1]], preferred_element_type=F32)

    u = proj(0)
    for blk in range(LANE_BLOCKS):
        u_ref[blk] = u[:, blk * LANES:(blk + 1) * LANES]
    seg_ones = seg_ref[...]
    q = _head_norm(proj(1), qg_ref[...], seg_ones) * (HEAD_DIM ** -0.5)
    k = _head_norm(proj(2), kg_ref[...], seg_ones)
    v = proj(3)
    if not prompt:
        q_ref, k_ref, v_ref, ga_ref, gb_ref = rest
        q_ref[...] = q
        k_ref[...] = k
        v_ref[...] = v
    else:
        ga_ref, gb_ref, fold_ref = rest[:3]
        windows = rest[3:3 + N_DIL]
        by_residue = rest[3 + N_DIL:3 + N_DIL + 3 * N_DIL]
        stage_ref = rest[-1]
    ga_ref[...] = jax.nn.sigmoid(proj(4))
    gb_ref[...] = jax.nn.sigmoid(proj(5))
    if not prompt:
        return

    rows = x_ref.shape[0]
    groups_per_block = LANES // SSM_GROUP
    for blk in range(LANE_BLOCKS):
        tokens = [u_ref[blk, pl.ds(s, rows // SSM_CHUNK, stride=SSM_CHUNK), :] for s in range(SSM_CHUNK)]
        for q_i, folded in enumerate(_transpose_granules(tokens)):
            pair = blk * (groups_per_block // 2) + q_i // 2
            fold_ref[pair, :, (q_i % 2) * LANES:(q_i % 2 + 1) * LANES] = folded.astype(BF16)

    slabs = SLOT_WIDTH // LANES
    for g, win_ref in enumerate(windows):
        keep = win_ref.shape[-1]
        for t, x in enumerate((k, v)):
            for s in range(slabs):
                lo = g * SLOT_WIDTH + s * LANES
                xt = x[:, lo:lo + LANES].T[:, rows - keep:]
                win_ref[t, 2 * s] = xt[:HEAD_DIM]
                win_ref[t, 2 * s + 1] = xt[HEAD_DIM:]

    for ti, x in enumerate((q * LOG2_E, k, v)):
        outs = by_residue[ti * N_DIL:(ti + 1) * N_DIL]
        outs[0][0] = x[:, :SLOT_WIDTH].astype(BF16)
        for g in range(1, N_DIL):
            d = DIL_RATES[g]
            for s in range(slabs):
                lo = g * SLOT_WIDTH + s * LANES
                stage_ref[ti, (g - 1) * slabs + s] = x[:, lo:lo + LANES]
            for r in range(d):
                for s in range(slabs):
                    piece = stage_ref[ti, (g - 1) * slabs + s, pl.ds(r, rows // d, stride=d), :]
                    outs[g][r, :, s * LANES:(s + 1) * LANES] = piece.astype(BF16)


def _mix_in(x, norm_w, w_in, q_gain, k_gain, seg_ones, tm, seq=None):
    t = x.shape[0]
    prompt = seq is not None

    def row(width):
        return pl.BlockSpec((tm, width), lambda i: (i, 0))

    def rows_f32(width):
        return jax.ShapeDtypeStruct((t, width), F32)

    out_specs = [_lane_blocked_spec(tm, lambda i: i)]
    out_shape = [jax.ShapeDtypeStruct((LANE_BLOCKS, t, LANES), F32)]
    scratch = []
    if not prompt:
        out_specs += [row(ATTN_WIDTH)] * 3 + [row(D_MODEL)] * 2
        out_shape += [rows_f32(ATTN_WIDTH)] * 3 + [rows_f32(D_MODEL)] * 2
    else:
        tiles = seq // tm
        out_specs += [row(D_MODEL)] * 2
        out_shape += [rows_f32(D_MODEL)] * 2
        out_specs.append(pl.BlockSpec((SSM_PAIRS, tm // SSM_CHUNK, MXU_DIM), lambda i: (0, i, 0)))
        out_shape.append(jax.ShapeDtypeStruct((SSM_PAIRS, t // SSM_CHUNK, MXU_DIM), BF16))
        for w in DIL_WINDOWS:
            keep = min(w, seq)
            cols = min(keep, tm)
            assert keep % cols == 0 and tm % cols == 0
            skip = tiles - keep // cols
            out_specs.append(pl.BlockSpec(
                (None, 2, HEADS, HEAD_DIM, cols),
                functools.partial(lambda i, skip: (i // tiles, 0, 0, 0, jnp.maximum(i % tiles - skip, 0)), skip=skip)))
            out_shape.append(jax.ShapeDtypeStruct((t // seq, 2, HEADS, HEAD_DIM, keep), F32))
        for _ in range(3):
            for d in DIL_RATES:
                out_specs.append(pl.BlockSpec((None, d, tm // d, SLOT_WIDTH),
                                              lambda i: (i // tiles, 0, i % tiles, 0)))
                out_shape.append(jax.ShapeDtypeStruct((t // seq, d, seq // d, SLOT_WIDTH), BF16))
        scratch = [pltpu.VMEM((3, (N_DIL - 1) * SLOT_WIDTH // LANES, tm, LANES), F32)]
    return pl.pallas_call(
        functools.partial(_mix_in_body, prompt=prompt),
        grid=(t // tm,),
        in_specs=[row(D_MODEL), _const_spec((1, D_MODEL)), _const_spec((D_MODEL, IN_WIDTH)),
                  _const_spec((1, ATTN_WIDTH)), _const_spec((1, ATTN_WIDTH)), _const_spec((MXU_DIM, MXU_DIM))],
        out_specs=out_specs,
        out_shape=out_shape,
        scratch_shapes=scratch,
        compiler_params=_params("arbitrary"),
        name="mix_in",
    )(x, norm_w, w_in, q_gain, k_gain, seg_ones)


def _cmul(ar, ai, br, bi):
    return ar * br - ai * bi, ar * bi + ai * br


def _ssm_prep_body(lrow_re_ref, lrow_im_ref, lcol_re_ref, lcol_im_ref, ldt_ref, bre_ref, bim_ref,
                   ct_re_ref, ct_im_ref, abar_ref, a8_ref, bstep_ref, cstep_ref, toep_ref, win_ref, wout_ref):
    n = SSM_CHUNK
    dt = jnp.exp(ldt_ref[...])

    def discretise(lam_re, lam_im):
        lr = jnp.minimum(lam_re, -1e-4)
        mag = jnp.exp(lr * dt)
        return lr, lam_im, mag * jnp.cos(lam_im * dt), mag * jnp.sin(lam_im * dt)

    def powers(ar, ai):
        out = [(jnp.ones_like(ar), jnp.zeros_like(ai))]
        for _ in range(n):
            out.append(_cmul(*out[-1], ar, ai))
        return out

    lr, li, ar, ai = discretise(lrow_re_ref[...], lrow_im_ref[...])
    den = lr * lr + li * li
    fr = ((ar - 1.0) * lr + ai * li) / den
    fi = (ai * lr - (ar - 1.0) * li) / den
    bbr, bbi = _cmul(fr, fi, bre_ref[...], bim_ref[...])
    row_pow = powers(ar, ai)
    abar_ref[0], abar_ref[1] = ar, ai
    a8_ref[0], a8_ref[1] = row_pow[n]

    def pair_halves(x):
        x = x.reshape((x.shape[0] // 2, 2) + x.shape[1:])
        return x[:, 0], x[:, 1]

    bstep_ref[...] = jnp.zeros_like(bstep_ref)
    cstep_ref[...] = jnp.zeros_like(cstep_ref)
    for part, x in enumerate((bbr, bbi)):
        x_e, x_o = pair_halves(x)
        bstep_ref[:, :SSM_GROUP, (2 * part) * SSM_STATE:(2 * part + 1) * SSM_STATE] = x_e
        bstep_ref[:, SSM_GROUP:, (2 * part + 1) * SSM_STATE:(2 * part + 2) * SSM_STATE] = x_o
    for part, ct_ref in enumerate((ct_re_ref, ct_im_ref)):
        x_e, x_o = pair_halves(ct_ref[:, :, :SSM_GROUP])
        cstep_ref[part, :, :SSM_STATE, :SSM_GROUP] = x_e
        cstep_ref[part, :, SSM_STATE:, SSM_GROUP:] = x_o

    toep_ref[...] = jnp.zeros_like(toep_ref)
    win_ref[...] = jnp.zeros_like(win_ref)
    wout_ref[...] = jnp.zeros_like(wout_ref)
    half = n * SSM_GROUP

    for s in range(n):
        w_re, w_im = _cmul(*row_pow[n - 1 - s], bbr, bbi)
        rows_e = slice(s * SSM_GROUP, (s + 1) * SSM_GROUP)
        rows_o = slice(half + s * SSM_GROUP, half + (s + 1) * SSM_GROUP)
        for part, x in enumerate((w_re, w_im)):
            x_e, x_o = pair_halves(x.astype(BF16))
            win_ref[:, rows_e, (2 * part) * SSM_STATE:(2 * part + 1) * SSM_STATE] = x_e
            win_ref[:, rows_o, (2 * part + 1) * SSM_STATE:(2 * part + 2) * SSM_STATE] = x_o

    _, _, ar_c, ai_c = discretise(lcol_re_ref[...], lcol_im_ref[...])
    col_pow = powers(ar_c, ai_c)
    lane_t = lax.broadcasted_iota(jnp.int32, (1, 1, LANES), 2) // SSM_GROUP

    def spread(first):
        re = im = jnp.zeros((1, 1, LANES), F32)
        for t in range(n):
            re = jnp.where(lane_t == t, col_pow[first + t][0], re)
            im = jnp.where(lane_t == t, col_pow[first + t][1], im)
        return re, im

    ct = (ct_re_ref[...], ct_im_ref[...])
    m0_re, m0_im = _cmul(*ct, *spread(0))
    m1_re, m1_im = _cmul(*ct, *spread(1))

    for part, x in enumerate((m1_re, -m1_im)):
        x_e, x_o = pair_halves(x.astype(BF16))
        wout_ref[:, (2 * part) * SSM_STATE:(2 * part + 1) * SSM_STATE, :half] = x_e
        wout_ref[:, (2 * part + 1) * SSM_STATE:(2 * part + 2) * SSM_STATE, half:] = x_o

    nn = (((2,), (1,)), ((0,), (0,)))
    hp = lax.Precision.HIGHEST
    kern = (lax.dot_general(bbr, m0_re, nn, precision=hp, preferred_element_type=F32)
            - lax.dot_general(bbi, m0_im, nn, precision=hp, preferred_element_type=F32))
    lane = lax.broadcasted_iota(jnp.int32, (1, 1, LANES), 2)
    for s in range(n):
        shifted = kern if s == 0 else jnp.where(lane >= s * SSM_GROUP, pltpu.roll(kern, s * SSM_GROUP, 2), 0.0)
        x_e, x_o = pair_halves(shifted.astype(BF16))
        toep_ref[:, s * SSM_GROUP:(s + 1) * SSM_GROUP, :half] = x_e
        toep_ref[:, half + s * SSM_GROUP:half + (s + 1) * SSM_GROUP, half:] = x_o


def _ssm_prep(lam_re, lam_im, log_dt, b_re_t, b_im_t, c_re, c_im, groups_per_step=16):
    g, p, c = SSM_GROUPS, SSM_STATE, SSM_GROUP
    gb = groups_per_step

    def spec(*tail):
        return pl.BlockSpec((gb,) + tail, lambda i: (i,) + (0,) * len(tail))

    def stacked(*tail):
        return pl.BlockSpec((2, gb) + tail, lambda i: (0, i) + (0,) * len(tail))

    pair_spec = pl.BlockSpec((gb // 2, MXU_DIM, MXU_DIM), lambda i: (i, 0, 0))
    pair_shape = jax.ShapeDtypeStruct((SSM_PAIRS, MXU_DIM, MXU_DIM), BF16)
    ct_re = jnp.tile(c_re.transpose(0, 2, 1), (1, 1, SSM_CHUNK))
    ct_im = jnp.tile(c_im.transpose(0, 2, 1), (1, 1, SSM_CHUNK))
    return pl.pallas_call(
        _ssm_prep_body,
        grid=(g // gb,),
        in_specs=[spec(1, p), spec(1, p), spec(p, 1), spec(p, 1), spec(1, 1), spec(c, p), spec(c, p),
                  spec(p, LANES), spec(p, LANES)],
        out_specs=[stacked(1, p), stacked(1, p),
                   pl.BlockSpec((gb // 2, 2 * c, 4 * p), lambda i: (i, 0, 0)),
                   pl.BlockSpec((2, gb // 2, 2 * p, 2 * c), lambda i: (0, i, 0, 0)),
                   pair_spec, pair_spec, pair_spec],
        out_shape=[jax.ShapeDtypeStruct((2, g, 1, p), F32), jax.ShapeDtypeStruct((2, g, 1, p), F32),
                   jax.ShapeDtypeStruct((SSM_PAIRS, 2 * c, 4 * p), F32),
                   jax.ShapeDtypeStruct((2, SSM_PAIRS, 2 * p, 2 * c), F32),
                   pair_shape, pair_shape, pair_shape],
        compiler_params=_params("parallel"),
        name="ssm_prep",
    )(lam_re.reshape(g, 1, p), lam_im.reshape(g, 1, p), lam_re.reshape(g, p, 1), lam_im.reshape(g, p, 1),
      log_dt.reshape(g, 1, 1), b_re_t, b_im_t, ct_re, ct_im)


def _ssm_body(lhs_ref, toep_ref, win_ref, wout_ref, a8_ref, yfl_ref, hfin_ref, st_ref, carry_ref, *, rows):
    i = pl.program_id(1)

    @pl.when(i == 0)
    def _():
        carry_ref[...] = jnp.zeros_like(carry_ref)

    def state_in(r, _):
        b = jnp.dot(lhs_ref[r], win_ref[r], preferred_element_type=F32)
        st_ref[0, pl.ds(r, rows, stride=SSM_ROW_PITCH), :] = b[:, :LANES]
        st_ref[1, pl.ds(r, rows, stride=SSM_ROW_PITCH), :] = b[:, LANES:]
        return 0

    lax.fori_loop(0, SSM_PAIRS, state_in, 0, unroll=4)

    a_re = a8_ref[0]
    a_im = a8_ref[1]

    def step(j, h):
        h_re, h_im = h
        base = pl.multiple_of(j * SSM_ROW_PITCH, SUBLANES)
        n_re = a_re * h_re - a_im * h_im + st_ref[0, pl.ds(base, SSM_PAIRS), :]
        n_im = a_re * h_im + a_im * h_re + st_ref[1, pl.ds(base, SSM_PAIRS), :]
        st_ref[0, pl.ds(base, SSM_PAIRS), :] = h_re
        st_ref[1, pl.ds(base, SSM_PAIRS), :] = h_im
        return n_re, n_im

    h_re, h_im = lax.fori_loop(0, rows, step, (carry_ref[0], carry_ref[1]), unroll=4)
    carry_ref[0] = h_re
    carry_ref[1] = h_im
    hfin_ref[0] = h_re
    hfin_ref[1] = h_im

    def chunk_out(r, _):
        hcat = jnp.concatenate([st_ref[0, pl.ds(r, rows, stride=SSM_ROW_PITCH), :],
                                st_ref[1, pl.ds(r, rows, stride=SSM_ROW_PITCH), :]], axis=1).astype(BF16)
        yfl_ref[r] = (jnp.dot(lhs_ref[r], toep_ref[r], preferred_element_type=F32)
                      + jnp.dot(hcat, wout_ref[r], preferred_element_type=F32))
        return 0

    lax.fori_loop(0, SSM_PAIRS, chunk_out, 0, unroll=4)


def _ssm_prompt(lhs, toep, win, wout, a8, batch, seq, tile):
    rows = tile // SSM_CHUNK
    n_tiles = seq // tile
    tok = pl.BlockSpec((SSM_PAIRS, rows, MXU_DIM), lambda b, i: (0, b * n_tiles + i, 0))
    pair_w = _const_spec((SSM_PAIRS, MXU_DIM, MXU_DIM))
    return pl.pallas_call(
        functools.partial(_ssm_body, rows=rows),
        grid=(batch, n_tiles),
        in_specs=[tok, pair_w, pair_w, pair_w, _const_spec((2, SSM_PAIRS, LANES))],
        out_specs=[tok, pl.BlockSpec((None, 2, SSM_PAIRS, LANES), lambda b, i: (b, 0, 0, 0))],
        out_shape=[jax.ShapeDtypeStruct((SSM_PAIRS, batch * seq // SSM_CHUNK, MXU_DIM), F32),
                   jax.ShapeDtypeStruct((batch, 2, SSM_PAIRS, LANES), F32)],
        scratch_shapes=[pltpu.VMEM((2, rows * SSM_ROW_PITCH, LANES), F32),
                        pltpu.VMEM((2, SSM_PAIRS, LANES), F32)],
        compiler_params=_params("parallel", "arbitrary"),
        name="ssm_prompt",
    )(lhs, toep, win, wout, a8)


def _ssm_step_body(u_ref, hre_ref, him_ref, abar_ref, bstep_ref, cstep_ref, y_ref, ore_ref, oim_ref):
    hp = lax.Precision.HIGHEST
    pair_ch = 2 * SSM_GROUP
    pairs_per_block = LANES // pair_ch
    for r in range(SSM_PAIRS):
        blk = r // pairs_per_block
        ch = slice((r % pairs_per_block) * pair_ch, (r % pairs_per_block + 1) * pair_ch)
        st = slice(r * LANES, (r + 1) * LANES)
        bu = jnp.dot(u_ref[blk, :, ch], bstep_ref[r], precision=hp, preferred_element_type=F32)
        a_re = abar_ref[0, r:r + 1, :]
        a_im = abar_ref[1, r:r + 1, :]
        h_re = hre_ref[:, st]
        h_im = him_ref[:, st]
        n_re = a_re * h_re - a_im * h_im + bu[:, :LANES]
        n_im = a_re * h_im + a_im * h_re + bu[:, LANES:]
        ore_ref[:, st] = n_re
        oim_ref[:, st] = n_im
        y_ref[blk, :, ch] = (jnp.dot(n_re, cstep_ref[0, r], precision=hp, preferred_element_type=F32)
                             - jnp.dot(n_im, cstep_ref[1, r], precision=hp, preferred_element_type=F32))


def _ssm_step(u, h_re, h_im, abar, bstep, cstep):
    b = u.shape[1]
    width = SSM_GROUPS * SSM_STATE
    return pl.pallas_call(
        _ssm_step_body,
        out_shape=[jax.ShapeDtypeStruct((LANE_BLOCKS, b, LANES), F32),
                   jax.ShapeDtypeStruct((b, width), F32),
                   jax.ShapeDtypeStruct((b, width), F32)],
        compiler_params=pltpu.CompilerParams(vmem_limit_bytes=VMEM_LIMIT_BYTES),
        name="ssm_step",
    )(u, h_re, h_im, abar, bstep, cstep)


def _attn_prompt_body(*refs, part_rows):
    per_group = 5
    ins = refs[:per_group * N_DIL]
    o_ref, m_ref, l_ref, acc_ref, bias_ref = refs[per_group * N_DIL:]
    part = pl.program_id(1)
    tq = KEYS_BACK
    row = lax.broadcasted_iota(jnp.int32, (tq, 2 * tq), 0)
    col = lax.broadcasted_iota(jnp.int32, (tq, 2 * tq), 1)
    back = row + tq - col
    in_window = (back >= 0) & (back <= KEYS_BACK)
    first_head = lax.broadcasted_iota(jnp.int32, (1, LANES), 1) < HEAD_DIM
    nt = (((1,), (1,)), ((), ()))

    order = tuple(reversed(range(N_DIL)))
    ones = jnp.ones((2 * tq, LANES), BF16)
    for g in order:
        q_ref, k_ref, kb_ref, v_ref, vb_ref = ins[per_group * g:per_group * (g + 1)]
        d = DIL_RATES[g]
        blocks_per_residue = part_rows // d // tq
        dist = (back * d).astype(F32)
        for h in range(HEADS):
            bias = jnp.where(in_window, -(_slope(g, h) * LOG2_E) * dist, -jnp.inf)
            bias_ref[h] = bias
            bias_ref[HEADS + h] = jnp.where(col >= tq, bias, -jnp.inf)

        def block(mi, _, g=g, d=d, q_ref=q_ref, k_ref=k_ref, kb_ref=kb_ref, v_ref=v_ref, vb_ref=vb_ref,
                  blocks_per_residue=blocks_per_residue):
            residue = mi // blocks_per_residue
            n = mi % blocks_per_residue
            cur = pl.multiple_of(n * tq, tq)
            prev = pl.multiple_of(jnp.maximum(n - 1, 0) * tq, tq)
            bias_at = jnp.where((n == 0) & (part == 0), HEADS, 0)
            token0 = residue + d * tq * n
            rows = pl.ds(pl.multiple_of(token0, tq), tq) if d == 1 else pl.ds(token0, tq, stride=d)
            pairs = range(HEADS // 2)
            first, final = g == order[0], g == order[-1]
            old = None if first else [(m_ref[pair, rows, :], l_ref[pair, rows, :], acc_ref[pair, rows, :])
                                      for pair in pairs]
            new = []
            for pair in pairs:
                lanes = slice(pair * LANES, (pair + 1) * LANES)
                qp = q_ref[residue, pl.ds(cur, tq), lanes]
                k_prev = jnp.where(n == 0, kb_ref[residue, :, lanes], k_ref[residue, pl.ds(prev, tq), lanes])
                v_prev = jnp.where(n == 0, vb_ref[residue, :, lanes], v_ref[residue, pl.ds(prev, tq), lanes])
                kp = jnp.concatenate([k_prev, k_ref[residue, pl.ds(cur, tq), lanes]], axis=0)
                vp = jnp.concatenate([v_prev, v_ref[residue, pl.ds(cur, tq), lanes]], axis=0)
                vp = jnp.concatenate([vp, ones], axis=1)
                stats = []
                for e in range(2):
                    qm = jnp.where(first_head if e == 0 else ~first_head, qp, jnp.zeros_like(qp))
                    s = lax.dot_general(qm, kp, nt, preferred_element_type=F32)
                    s = s + bias_ref[bias_at + 2 * pair + e]
                    m = jnp.max(s, axis=-1, keepdims=True)
                    p = jnp.exp2(s - m)
                    pv = jnp.dot(p.astype(BF16), vp, preferred_element_type=F32)
                    stats.append((m, pv[:, LANES:], pv[:, :LANES]))
                m_new, l_new, acc_new = (jnp.where(first_head, a, b) for a, b in zip(*stats))
                if not first:
                    m_old, l_old, acc_old = old[pair]
                    m_tot = jnp.maximum(m_old, m_new)
                    w_old = jnp.exp2(m_old - m_tot)
                    w_new = jnp.exp2(m_new - m_tot)
                    l_new = w_old * l_old + w_new * l_new
                    acc_new = w_old * acc_old + w_new * acc_new
                    m_new = m_tot
                new.append((m_new, l_new, acc_new))
            for pair, (m_new, l_new, acc_new) in zip(pairs, new):
                if not final:
                    m_ref[pair, rows, :] = m_new
                    l_ref[pair, rows, :] = l_new
                    acc_ref[pair, rows, :] = acc_new
                else:
                    o_ref[pair, rows, :] = acc_new / l_new
            return 0

        lax.fori_loop(0, part_rows // tq, block, 0, unroll=2)


def _attn_prompt(qkv, batch, seq):
    slabs = SLOT_WIDTH // LANES
    tq = KEYS_BACK
    part_rows = DIL_RATES[-1] * tq
    parts = seq // part_rows
    args, specs = [], []
    for g, d in enumerate(DIL_RATES):
        rows = part_rows // d
        cur = pl.BlockSpec((None, d, rows, SLOT_WIDTH), lambda b, p: (b, 0, p, 0))
        before = pl.BlockSpec((None, d, tq, SLOT_WIDTH),
                              functools.partial(lambda b, p, step: (b, 0, jnp.maximum(p * step - 1, 0), 0),
                                                step=rows // tq))
        q, k, v = qkv[g], qkv[N_DIL + g], qkv[2 * N_DIL + g]
        args += [q, k, k, v, v]
        specs += [cur, cur, before, cur, before]
    running = pltpu.VMEM((slabs, part_rows, LANES), F32)
    return pl.pallas_call(
        functools.partial(_attn_prompt_body, part_rows=part_rows),
        grid=(batch, parts),
        in_specs=specs,
        out_specs=pl.BlockSpec((slabs, part_rows, LANES), lambda b, p: (0, b * parts + p, 0)),
        out_shape=jax.ShapeDtypeStruct((slabs, batch * seq, LANES), F32),
        scratch_shapes=[running, running, running, pltpu.VMEM((2 * HEADS, tq, 2 * tq), F32)],
        compiler_params=_params("parallel", "arbitrary"),
        name="attn_prompt",
    )(*args)


def _as_column(row_vec):
    n = row_vec.shape[1]
    eye = lax.broadcasted_iota(jnp.int32, (n, n), 0) == lax.broadcasted_iota(jnp.int32, (n, n), 1)
    return jnp.sum(jnp.where(eye, row_vec, 0.0), axis=1, keepdims=True)


def _sample_window_group(g, q_row, k_row, v_row, c_ref, n_ref):
    head_row = lax.broadcasted_iota(jnp.int32, (SUBLANES, SLOT_WIDTH), 0)
    own_head = lax.broadcasted_iota(jnp.int32, (SUBLANES, SLOT_WIDTH), 1) // HEAD_DIM == head_row
    head_col = lax.broadcasted_iota(jnp.int32, (SUBLANES, 1), 0)
    w = DIL_WINDOWS[g]
    d = DIL_RATES[g]
    cols = slice(g * SLOT_WIDTH, (g + 1) * SLOT_WIDTH)
    q_g, k_new, v_new = q_row[:, cols], k_row[:, cols], v_row[:, cols]
    kt = c_ref[0].reshape(SLOT_WIDTH, w)
    vt = c_ref[1].reshape(SLOT_WIDTH, w)
    q_heads = jnp.where(own_head, q_g, 0.0)
    s = jnp.dot(q_heads.astype(BF16), kt.astype(BF16), preferred_element_type=F32)
    dist = w - lax.broadcasted_iota(jnp.int32, (1, w), 1)
    slope = functools.reduce(lambda acc, h: jnp.where(head_col == h, _slope(g, h), acc), range(HEADS), 0.0)
    s = jnp.where((dist & (d - 1)) == 0, s - slope * dist.astype(F32), -jnp.inf)
    s_new = jnp.sum(q_heads * k_new, axis=1, keepdims=True)
    m = jnp.maximum(jnp.max(s, axis=1, keepdims=True), s_new)
    p = jnp.exp(s - m)
    p_new = jnp.exp(s_new - m)
    den = jnp.sum(p, axis=1, keepdims=True) + p_new
    pv = lax.dot_general(p.astype(BF16), vt.astype(BF16), (((1,), (1,)), ((), ())),
                         preferred_element_type=F32)
    o_heads = (pv + p_new * v_new) / den
    last = lax.broadcasted_iota(jnp.int32, (SLOT_WIDTH, w), 1) == w - 1
    shape = (HEADS, HEAD_DIM, w)
    n_ref[0] = jnp.where(last, _as_column(k_new), pltpu.roll(kt, w - 1, 1)).reshape(shape)
    n_ref[1] = jnp.where(last, _as_column(v_new), pltpu.roll(vt, w - 1, 1)).reshape(shape)
    return (jnp.sum(jnp.where(own_head, o_heads, 0.0), axis=0, keepdims=True),
            jnp.sum(jnp.where(own_head, m + jnp.log(den), 0.0), axis=0, keepdims=True))


def _mix_groups(outs, lses):
    top = functools.reduce(jnp.maximum, lses)
    wts = [jnp.exp(l - top) for l in lses]
    return sum(w_g * o_g for w_g, o_g in zip(wts, outs)) / sum(wts)


def _ffn_windows_body(x_ref, nw_ref, *refs):
    w_refs = refs[:3]
    q_ref, k_ref, v_ref, c0_ref, c1_ref, c2_ref, o_ref, ob_ref, n0_ref, n1_ref, n2_ref = refs[3:]
    b = pl.program_id(0)
    rows = [r[pl.ds(b, 1), :] for r in (q_ref, k_ref, v_ref)]
    windows = ((c0_ref, n0_ref), (c1_ref, n1_ref), (c2_ref, n2_ref))
    results = {}

    def job(g):
        def run():
            results[g] = _sample_window_group(g, *rows, *windows[g])
        return run

    _ffn_body(x_ref, nw_ref, *w_refs, o_ref, side_jobs=[job(g) for g in reversed(range(N_DIL))])
    outs, lses = zip(*(results[g] for g in range(N_DIL)))
    ob_ref[...] = _mix_groups(outs, lses)


def _ffn_with_sample_windows(x, norm_w, weights, q, k, v, caches):
    t = x.shape[0]
    b = q.shape[0]
    assert t % b == 0 and (t // b) % SUBLANES == 0
    tm = t // b
    row = pl.BlockSpec((tm, D_MODEL), lambda i: (i, 0))
    full = _const_spec((b, ATTN_WIDTH))
    win = [pl.BlockSpec((None, 2, HEADS, HEAD_DIM, w), lambda i: (i, 0, 0, 0, 0)) for w in DIL_WINDOWS]
    outs = pl.pallas_call(
        _ffn_windows_body,
        grid=(b,),
        in_specs=[row, _const_spec((1, D_MODEL))] + _ffn_weight_specs() + [full, full, full] + win,
        out_specs=[row, pl.BlockSpec((None, 1, SLOT_WIDTH), lambda i: (i, 0, 0))] + win,
        out_shape=([jax.ShapeDtypeStruct((t, D_MODEL), F32), jax.ShapeDtypeStruct((b, 1, SLOT_WIDTH), F32)]
                   + [jax.ShapeDtypeStruct(c.shape, F32) for c in caches]),
        compiler_params=_params("parallel"),
        name="ffn_windows",
    )(x, norm_w, *weights, q, k, v, *caches)
    return outs[0], outs[1].reshape(b, SLOT_WIDTH), outs[2:]


def _mix_out_body(x_ref, y_ref, u_ref, d_ref, ob_ref,
                  ga_ref, gb_ref, wglu_ref, wpa_ref, wpb_ref, wout_ref, out_ref, *unfold):
    tm = x_ref.shape[0]
    if unfold:
        y_ref, folded_ref = unfold[0], y_ref
        groups_per_block = LANES // SSM_GROUP
    pieces = MIX_OUT_PIECES if unfold else 1
    rows = tm // pieces
    for piece in range(pieces):
        at = slice(piece * rows, (piece + 1) * rows)
        if unfold:
            chunks = rows // SSM_CHUNK
            chunk_at = slice(piece * chunks, (piece + 1) * chunks)
            for blk in range(LANE_BLOCKS):
                groups = [folded_ref[blk * (groups_per_block // 2) + q_i // 2, chunk_at,
                                     (q_i % 2) * LANES:(q_i % 2 + 1) * LANES] for q_i in range(groups_per_block)]
                for t, rows_t in enumerate(_transpose_granules(groups)):
                    y_ref[blk, pl.ds(piece * rows + t, chunks, stride=SSM_CHUNK), :] = rows_t
        y_raw = jnp.concatenate([y_ref[blk, at, :] for blk in range(LANE_BLOCKS)], axis=1)
        u = jnp.concatenate([u_ref[blk, at, :] for blk in range(LANE_BLOCKS)], axis=1)
        y = jax.nn.gelu(y_raw + d_ref[...] * u)
        yb = y.astype(BF16)
        y_a = y * jax.nn.sigmoid(jnp.dot(yb, wglu_ref[...], preferred_element_type=F32))
        branch_a = jnp.dot(y_a.astype(BF16), wpa_ref[...], preferred_element_type=F32)

        o_b = jnp.concatenate([ob_ref[s, at, :] for s in range(SLOT_WIDTH // LANES)], axis=1)
        branch_b = jnp.dot(o_b.astype(BF16), wpb_ref[...], preferred_element_type=F32)

        merged = ga_ref[at, :] * branch_a + gb_ref[at, :] * branch_b
        out_ref[at, :] = x_ref[at, :] + jnp.dot(merged.astype(BF16), wout_ref[...], preferred_element_type=F32)


def _mix_out(x, y, u, ssm_d, o_b, ga, gb, w_glu, w_pa, w_pb, w_out, tm, folded_y):
    t = x.shape[0]
    wide = pl.BlockSpec((tm, D_MODEL), lambda i: (i, 0))
    blocked = _lane_blocked_spec(tm, lambda i: i)
    y_spec = pl.BlockSpec((SSM_PAIRS, tm // SSM_CHUNK, MXU_DIM), lambda i: (0, i, 0)) if folded_y else blocked
    scratch = [pltpu.VMEM((LANE_BLOCKS, tm, LANES), F32)] if folded_y else []
    slot = pl.BlockSpec((SLOT_WIDTH // LANES, tm, LANES), lambda i: (0, i, 0))
    return pl.pallas_call(
        _mix_out_body,
        grid=(t // tm,),
        in_specs=[wide, y_spec, blocked, _const_spec((1, D_MODEL)), slot, wide, wide,
                  _const_spec((D_MODEL, D_MODEL)), _const_spec((D_MODEL, D_MODEL)),
                  _const_spec((SLOT_WIDTH, D_MODEL)), _const_spec((D_MODEL, D_MODEL))],
        out_specs=wide,
        out_shape=jax.ShapeDtypeStruct((t, D_MODEL), F32),
        scratch_shapes=scratch,
        compiler_params=_params("parallel"),
        name="mix_out",
    )(x, y, u, ssm_d, o_b, ga, gb, w_glu, w_pa, w_pb, w_out)


def _head_segment_ones():
    head = jnp.arange(MXU_DIM) // HEAD_DIM
    return (head[:, None] == head[None, :]).astype(BF16)


def kernel(x_prompt, x_sample, state_ssm_re, state_ssm_im, cache_kv_w128, cache_kv_w512, cache_kv_w2048, ffn1_norm, ffn1_w_gate, ffn1_w_up, ffn1_w_down, mix_norm, w_in, ssm_lambda_re, ssm_lambda_im, ssm_b_re, ssm_b_im, ssm_c_re, ssm_c_im, ssm_d, ssm_log_dt, w_glu, q_gain, k_gain, w_proj_a, w_proj_b, w_out, ffn2_norm, ffn2_w_gate, ffn2_w_up, ffn2_w_down):
    depth = ffn1_norm.shape[0]
    assert depth == 1, "single-layer step"
    batch, seq, _ = x_prompt.shape
    dec_batch, dec_seq, _ = x_sample.shape
    assert dec_seq == 1 and seq % (DIL_RATES[-1] * KEYS_BACK) == 0
    layer = 0
    bf = lambda w: w[layer].astype(BF16)
    vec = lambda w: w[layer][None]
    ffn1 = (vec(ffn1_norm), (bf(ffn1_w_gate), bf(ffn1_w_up), bf(ffn1_w_down)))
    mix_norm, ssm_d = vec(mix_norm), vec(ssm_d)
    q_gain_t = jnp.tile(vec(q_gain), (1, N_DIL * HEADS))
    k_gain_t = jnp.tile(vec(k_gain), (1, N_DIL * HEADS))
    seg_ones = _head_segment_ones()

    abar, a8, bstep, cstep, toep, win, wout = _ssm_prep(
        ssm_lambda_re[layer], ssm_lambda_im[layer], ssm_log_dt[layer],
        ssm_b_re[layer].transpose(0, 2, 1), ssm_b_im[layer].transpose(0, 2, 1),
        ssm_c_re[layer], ssm_c_im[layer])
    a8 = a8.reshape(2, SSM_PAIRS, LANES)
    abar = abar.reshape(2, SSM_PAIRS, LANES)

    def mix_in(x1, tm, seq=None):
        return _mix_in(x1, mix_norm, w_in_b, q_gain_t, k_gain_t, seg_ones, tm, seq)

    def mix_out(x1, y, u, o_b, ga, gb, tm, folded_y):
        return _mix_out(x1, y, u, ssm_d, o_b, ga, gb, w_glu_b, w_pa_b, w_pb_b, w_out_b, tm, folded_y)

    tm = 512
    tm_ffn = 1024
    xp = x_prompt.reshape(batch * seq, D_MODEL)
    later = [w[layer] for w in (ffn2_w_gate, ffn2_w_up, ffn2_w_down, w_in, w_glu, w_proj_a, w_proj_b, w_out)]
    x1, *later_b = _ffn(xp, *ffn1, tm_ffn, convert=later)
    ffn2 = (vec(ffn2_norm), tuple(later_b[:3]))
    w_in_b, w_glu_b, w_pa_b, w_pb_b, w_out_b = later_b[3:]

    xs = x_sample.reshape(dec_batch, D_MODEL)
    xs1 = _ffn(xs, *ffn1, dec_batch)
    us, qs, ks, vs, gas, gbs = mix_in(xs1, dec_batch)
    caches = [c[layer].transpose(0, 2, 3, 4, 1) for c in (cache_kv_w128, cache_kv_w512, cache_kv_w2048)]

    u, ga, gb, u_folded, *rest = mix_in(x1, tm, seq)
    p_windows, qkv = rest[:N_DIL], rest[N_DIL:]
    y, h_fin = _ssm_prompt(u_folded, toep, win, wout, a8, batch, seq, tile=2048)
    o_b = _attn_prompt(qkv, batch, seq)
    x2 = mix_out(x1, y, u, o_b, ga, gb, tm, folded_y=True)
    yp, o_bs, new_caches = _ffn_with_sample_windows(x2, *ffn2, qs, ks, vs, caches)
    yp = yp.reshape(batch, seq, D_MODEL)
    h_fin = h_fin.reshape(batch, 2, SSM_GROUPS, SSM_STATE)
    p_re = h_fin[None, :, 0]
    p_im = h_fin[None, :, 1]
    p_kv = [c.transpose(0, 4, 1, 2, 3)[None] for c in p_windows]

    width = SSM_GROUPS * SSM_STATE
    ys_ssm, s_re, s_im = _ssm_step(us, state_ssm_re[layer].reshape(dec_batch, width),
                                   state_ssm_im[layer].reshape(dec_batch, width),
                                   abar, bstep, cstep)
    o_bs = o_bs.reshape(dec_batch, SLOT_WIDTH // LANES, LANES).transpose(1, 0, 2)
    xs2 = mix_out(xs1, ys_ssm, us, o_bs, gas, gbs, dec_batch, folded_y=False)
    ys = _ffn(xs2, *ffn2, dec_batch)
    ys = ys.reshape(dec_batch, 1, D_MODEL)
    s_re = s_re.reshape(1, dec_batch, SSM_GROUPS, SSM_STATE)
    s_im = s_im.reshape(1, dec_batch, SSM_GROUPS, SSM_STATE)
    s_kv = [c.transpose(0, 4, 1, 2, 3)[None] for c in new_caches]

    return (yp, ys, p_re, p_im, p_kv[0], p_kv[1], p_kv[2], s_re, s_im, s_kv[0], s_kv[1], s_kv[2])
```

```python
import functools

import jax
import jax.numpy as jnp
from jax import lax
from jax.experimental import pallas as pl
from jax.experimental.pallas import tpu as pltpu

F32 = jnp.float32
BF16 = jnp.bfloat16

D_MODEL = 1024
SSM_GROUP = 16
SSM_GROUPS = 64
SSM_STATE = 64
SSM_PAIRS = SSM_GROUPS // 2
HEAD_DIM = 64
HEADS = 4
DIL_WINDOWS = (128, 512, 2048)
DIL_RATES = (1, 4, 16)
N_DIL = 3
KEYS_BACK = 128
ATTN_WIDTH = N_DIL * HEADS * HEAD_DIM
SLOT_WIDTH = HEADS * HEAD_DIM
D_FF = 2816
RMS_EPS = 1e-6
ALIBI_MAX_EXP = 8.0
LOG2_E = 1.4426950408889634
IN_SEGMENTS = (D_MODEL, ATTN_WIDTH, ATTN_WIDTH, ATTN_WIDTH, D_MODEL, D_MODEL)
IN_WIDTH = sum(IN_SEGMENTS)

LANES = 128
SUBLANES = 8
MXU_DIM = 256
VMEM_LIMIT_BYTES = 56 * 1024 * 1024

LANE_BLOCKS = D_MODEL // LANES

SSM_CHUNK = SUBLANES
SSM_ROW_PITCH = 40
FF_CHUNKS = ((0, 1024), (1024, 2048), (2048, 2816))
FF_CHUNKS_FINE = ((0, 512), (512, 1024), (1024, 1536), (1536, 2048), (2048, 2560), (2560, 2816))
MIX_OUT_PIECES = 2


def _slope(group, head):
    return 2.0 ** (-ALIBI_MAX_EXP * (group * HEADS + head + 1) / (N_DIL * HEADS))


def _const_spec(shape):
    zeros = (0,) * len(shape)
    return pl.BlockSpec(shape, lambda *_: zeros, pipeline_mode=pl.Buffered(1))


def _lane_blocked_spec(rows, row_block):
    return pl.BlockSpec((LANE_BLOCKS, rows, LANES), lambda *idx: (0, row_block(*idx), 0))


def _params(*semantics):
    return pltpu.CompilerParams(dimension_semantics=semantics, vmem_limit_bytes=VMEM_LIMIT_BYTES)


def _rms(x, w):
    return x * lax.rsqrt(jnp.mean(x * x, axis=-1, keepdims=True) + RMS_EPS) * w


def _split_bf16(x, terms):
    parts = []
    for _ in range(terms):
        p = x.astype(BF16)
        parts.append(p)
        x = x - p.astype(F32)
    return parts


def _ffn_body(x_ref, nw_ref, wg_ref, wu_ref, wd_ref, *rest, side_jobs=(), chunks=FF_CHUNKS):
    n_cast = (len(rest) - 1) // 2
    o_ref = rest[n_cast]
    for src_ref, dst_ref in zip(rest[:n_cast], rest[n_cast + 1:]):
        dst_ref[...] = src_ref[...].astype(BF16)
    x = x_ref[...]
    h = _rms(x, nw_ref[...]).astype(BF16)
    acc = jnp.zeros_like(x)
    side_jobs = list(side_jobs)
    for lo, hi in chunks:
        g = jnp.dot(h, wg_ref[:, lo:hi], preferred_element_type=F32)
        u = jnp.dot(h, wu_ref[:, lo:hi], preferred_element_type=F32)
        a = (jax.nn.silu(g) * u).astype(BF16)
        acc = acc + jnp.dot(a, wd_ref[lo:hi, :], preferred_element_type=F32)
        if side_jobs:
            side_jobs.pop(0)()
    for job in side_jobs:
        job()
    o_ref[...] = x + 0.5 * acc


def _ffn_weight_specs():
    return [_const_spec((D_MODEL, D_FF)), _const_spec((D_MODEL, D_FF)), _const_spec((D_FF, D_MODEL))]


def _ffn(x, norm_w, weights, tm, convert=()):
    t = x.shape[0]
    steps = t // tm
    row = pl.BlockSpec((tm, D_MODEL), lambda i: (i, 0))
    side = []
    for w in convert:
        rows = w.shape[0] // steps
        assert rows * steps == w.shape[0] and rows % (2 * SUBLANES) == 0
        side.append(pl.BlockSpec((rows, w.shape[1]), lambda i: (i, 0)))
    outs = pl.pallas_call(
        _ffn_body,
        grid=(steps,),
        in_specs=[row, _const_spec((1, D_MODEL))] + _ffn_weight_specs() + side,
        out_specs=[row] + side,
        out_shape=[jax.ShapeDtypeStruct((t, D_MODEL), F32)] + [jax.ShapeDtypeStruct(w.shape, BF16) for w in convert],
        compiler_params=_params("parallel"),
        name="ffn",
    )(x, norm_w, *weights, *convert)
    return outs if convert else outs[0]


def _head_norm(x, gain, seg_ones):
    parts = _split_bf16(x * x, 2)
    blocks = []
    for lo in range(0, x.shape[1], MXU_DIM):
        blocks.append(sum(jnp.dot(p[:, lo:lo + MXU_DIM], seg_ones, preferred_element_type=F32) for p in parts))
    ss = jnp.concatenate(blocks, axis=1)
    return x * lax.rsqrt(ss * (1.0 / HEAD_DIM) + RMS_EPS) * gain


def _transpose_granules(xs):
    xs = list(xs)
    n = len(xs)
    block = lax.broadcasted_iota(jnp.int32, (1, LANES), 1) // SSM_GROUP
    bit = n // 2
    while bit:
        upper = (block & bit) != 0
        shift = SSM_GROUP * bit
        for lo in range(n):
            if lo & bit:
                continue
            hi = lo + bit
            x_lo, x_hi = xs[lo], xs[hi]
            xs[lo] = jnp.where(upper, pltpu.roll(x_hi, shift, 1), x_lo)
            xs[hi] = jnp.where(upper, x_hi, pltpu.roll(x_lo, LANES - shift, 1))
        bit //= 2
    return xs


def _mix_in_body(x_ref, nw_ref, w_ref, qg_ref, kg_ref, seg_ref, u_ref, *rest, prompt):
    h = _rms(x_ref[...], nw_ref[...]).astype(BF16)

    edges = [0]
    for width in IN_SEGMENTS:
        edges.append(edges[-1] + width)

    def proj(i):
        return jnp.dot(h, w_ref[:, edges[i]:edges[i + 1]], preferred_element_type=F32)

    u = proj(0)
    for blk in range(LANE_BLOCKS):
        u_ref[blk] = u[:, blk * LANES:(blk + 1) * LANES]
    seg_ones = seg_ref[...]
    q = _head_norm(proj(1), qg_ref[...], seg_ones) * (HEAD_DIM ** -0.5)
    k = _head_norm(proj(2), kg_ref[...], seg_ones)
    v = proj(3)
    if not prompt:
        q_ref, k_ref, v_ref, ga_ref, gb_ref = rest
        q_ref[...] = q
        k_ref[...] = k
        v_ref[...] = v
    else:
        ga_ref, gb_ref, fold_ref = rest[:3]
        windows = rest[3:3 + N_DIL]
        by_residue = rest[3 + N_DIL:3 + N_DIL + 3 * N_DIL]
        stage_ref = rest[-1]
    ga_ref[...] = jax.nn.sigmoid(proj(4))
    gb_ref[...] = jax.nn.sigmoid(proj(5))
    if not prompt:
        return

    rows = x_ref.shape[0]
    groups_per_block = LANES // SSM_GROUP
    for blk in range(LANE_BLOCKS):
        tokens = [u_ref[blk, pl.ds(s, rows // SSM_CHUNK, stride=SSM_CHUNK), :] for s in range(SSM_CHUNK)]
        for q_i, folded in enumerate(_transpose_granules(tokens)):
            pair = blk * (groups_per_block // 2) + q_i // 2
            fold_ref[pair, :, (q_i % 2) * LANES:(q_i % 2 + 1) * LANES] = folded.astype(BF16)

    slabs = SLOT_WIDTH // LANES
    for g, win_ref in enumerate(windows):
        keep = win_ref.shape[-1]
        for t, x in enumerate((k, v)):
            for s in range(slabs):
                lo = g * SLOT_WIDTH + s * LANES
                xt = x[:, lo:lo + LANES].T[:, rows - keep:]
                win_ref[t, 2 * s] = xt[:HEAD_DIM]
                win_ref[t, 2 * s + 1] = xt[HEAD_DIM:]

    for ti, x in enumerate((q * LOG2_E, k, v)):
        outs = by_residue[ti * N_DIL:(ti + 1) * N_DIL]
        outs[0][0] = x[:, :SLOT_WIDTH].astype(BF16)
        for g in range(1, N_DIL):
            d = DIL_RATES[g]
            for s in range(slabs):
                lo = g * SLOT_WIDTH + s * LANES
                stage_ref[ti, (g - 1) * slabs + s] = x[:, lo:lo + LANES]
            for r in range(d):
                for s in range(slabs):
                    piece = stage_ref[ti, (g - 1) * slabs + s, pl.ds(r, rows // d, stride=d), :]
                    outs[g][r, :, s * LANES:(s + 1) * LANES] = piece.astype(BF16)


def _mix_in(x, norm_w, w_in, q_gain, k_gain, seg_ones, tm, seq=None):
    t = x.shape[0]
    prompt = seq is not None

    def row(width):
        return pl.BlockSpec((tm, width), lambda i: (i, 0))

    def rows_f32(width):
        return jax.ShapeDtypeStruct((t, width), F32)

    out_specs = [_lane_blocked_spec(tm, lambda i: i)]
    out_shape = [jax.ShapeDtypeStruct((LANE_BLOCKS, t, LANES), F32)]
    scratch = []
    if not prompt:
        out_specs += [row(ATTN_WIDTH)] * 3 + [row(D_MODEL)] * 2
        out_shape += [rows_f32(ATTN_WIDTH)] * 3 + [rows_f32(D_MODEL)] * 2
    else:
        tiles = seq // tm
        out_specs += [row(D_MODEL)] * 2
        out_shape += [rows_f32(D_MODEL)] * 2
        out_specs.append(pl.BlockSpec((SSM_PAIRS, tm // SSM_CHUNK, MXU_DIM), lambda i: (0, i, 0)))
        out_shape.append(jax.ShapeDtypeStruct((SSM_PAIRS, t // SSM_CHUNK, MXU_DIM), BF16))
        for w in DIL_WINDOWS:
            keep = min(w, seq)
            cols = min(keep, tm)
            assert keep % cols == 0 and tm % cols == 0
            skip = tiles - keep // cols
            out_specs.append(pl.BlockSpec(
                (None, 2, HEADS, HEAD_DIM, cols),
                functools.partial(lambda i, skip: (i // tiles, 0, 0, 0, jnp.maximum(i % tiles - skip, 0)), skip=skip)))
            out_shape.append(jax.ShapeDtypeStruct((t // seq, 2, HEADS, HEAD_DIM, keep), F32))
        for _ in range(3):
            for d in DIL_RATES:
                out_specs.append(pl.BlockSpec((None, d, tm // d, SLOT_WIDTH),
                                              lambda i: (i // tiles, 0, i % tiles, 0)))
                out_shape.append(jax.ShapeDtypeStruct((t // seq, d, seq // d, SLOT_WIDTH), BF16))
        scratch = [pltpu.VMEM((3, (N_DIL - 1) * SLOT_WIDTH // LANES, tm, LANES), F32)]
    return pl.pallas_call(
        functools.partial(_mix_in_body, prompt=prompt),
        grid=(t // tm,),
        in_specs=[row(D_MODEL), _const_spec((1, D_MODEL)), _const_spec((D_MODEL, IN_WIDTH)),
                  _const_spec((1, ATTN_WIDTH)), _const_spec((1, ATTN_WIDTH)), _const_spec((MXU_DIM, MXU_DIM))],
        out_specs=out_specs,
        out_shape=out_shape,
        scratch_shapes=scratch,
        compiler_params=_params("arbitrary"),
        name="mix_in",
    )(x, norm_w, w_in, q_gain, k_gain, seg_ones)


def _cmul(ar, ai, br, bi):
    return ar * br - ai * bi, ar * bi + ai * br


def _ssm_prep_body(lrow_re_ref, lrow_im_ref, lcol_re_ref, lcol_im_ref, ldt_ref, bre_ref, bim_ref,
                   ct_re_ref, ct_im_ref, abar_ref, a8_ref, bstep_ref, cstep_ref, toep_ref, win_ref, wout_ref):
    n = SSM_CHUNK
    dt = jnp.exp(ldt_ref[...])

    def discretise(lam_re, lam_im):
        lr = jnp.minimum(lam_re, -1e-4)
        mag = jnp.exp(lr * dt)
        return lr, lam_im, mag * jnp.cos(lam_im * dt), mag * jnp.sin(lam_im * dt)

    def powers(ar, ai):
        out = [(jnp.ones_like(ar), jnp.zeros_like(ai))]
        for _ in range(n):
            out.append(_cmul(*out[-1], ar, ai))
        return out

    lr, li, ar, ai = discretise(lrow_re_ref[...], lrow_im_ref[...])
    den = lr * lr + li * li
    fr = ((ar - 1.0) * lr + ai * li) / den
    fi = (ai * lr - (ar - 1.0) * li) / den
    bbr, bbi = _cmul(fr, fi, bre_ref[...], bim_ref[...])
    row_pow = powers(ar, ai)
    abar_ref[0], abar_ref[1] = ar, ai
    a8_ref[0], a8_ref[1] = row_pow[n]

    def pair_halves(x):
        x = x.reshape((x.shape[0] // 2, 2) + x.shape[1:])
        return x[:, 0], x[:, 1]

    bstep_ref[...] = jnp.zeros_like(bstep_ref)
    cstep_ref[...] = jnp.zeros_like(cstep_ref)
    for part, x in enumerate((bbr, bbi)):
        x_e, x_o = pair_halves(x)
        bstep_ref[:, :SSM_GROUP, (2 * part) * SSM_STATE:(2 * part + 1) * SSM_STATE] = x_e
        bstep_ref[:, SSM_GROUP:, (2 * part + 1) * SSM_STATE:(2 * part + 2) * SSM_STATE] = x_o
    for part, ct_ref in enumerate((ct_re_ref, ct_im_ref)):
        x_e, x_o = pair_halves(ct_ref[:, :, :SSM_GROUP])
        cstep_ref[part, :, :SSM_STATE, :SSM_GROUP] = x_e
        cstep_ref[part, :, SSM_STATE:, SSM_GROUP:] = x_o

    toep_ref[...] = jnp.zeros_like(toep_ref)
    win_ref[...] = jnp.zeros_like(win_ref)
    wout_ref[...] = jnp.zeros_like(wout_ref)
    half = n * SSM_GROUP

    for s in range(n):
        w_re, w_im = _cmul(*row_pow[n - 1 - s], bbr, bbi)
        rows_e = slice(s * SSM_GROUP, (s + 1) * SSM_GROUP)
        rows_o = slice(half + s * SSM_GROUP, half + (s + 1) * SSM_GROUP)
        for part, x in enumerate((w_re, w_im)):
            x_e, x_o = pair_halves(x.astype(BF16))
            win_ref[:, rows_e, (2 * part) * SSM_STATE:(2 * part + 1) * SSM_STATE] = x_e
            win_ref[:, rows_o, (2 * part + 1) * SSM_STATE:(2 * part + 2) * SSM_STATE] = x_o

    _, _, ar_c, ai_c = discretise(lcol_re_ref[...], lcol_im_ref[...])
    col_pow = powers(ar_c, ai_c)
    lane_t = lax.broadcasted_iota(jnp.int32, (1, 1, LANES), 2) // SSM_GROUP

    def spread(first):
        re = im = jnp.zeros((1, 1, LANES), F32)
        for t in range(n):
            re = jnp.where(lane_t == t, col_pow[first + t][0], re)
            im = jnp.where(lane_t == t, col_pow[first + t][1], im)
        return re, im

    ct = (ct_re_ref[...], ct_im_ref[...])
    m0_re, m0_im = _cmul(*ct, *spread(0))
    m1_re, m1_im = _cmul(*ct, *spread(1))

    for part, x in enumerate((m1_re, -m1_im)):
        x_e, x_o = pair_halves(x.astype(BF16))
        wout_ref[:, (2 * part) * SSM_STATE:(2 * part + 1) * SSM_STATE, :half] = x_e
        wout_ref[:, (2 * part + 1) * SSM_STATE:(2 * part + 2) * SSM_STATE, half:] = x_o

    nn = (((2,), (1,)), ((0,), (0,)))
    hp = lax.Precision.HIGHEST
    kern = (lax.dot_general(bbr, m0_re, nn, precision=hp, preferred_element_type=F32)
            - lax.dot_general(bbi, m0_im, nn, precision=hp, preferred_element_type=F32))
    lane = lax.broadcasted_iota(jnp.int32, (1, 1, LANES), 2)
    for s in range(n):
        shifted = kern if s == 0 else jnp.where(lane >= s * SSM_GROUP, pltpu.roll(kern, s * SSM_GROUP, 2), 0.0)
        x_e, x_o = pair_halves(shifted.astype(BF16))
        toep_ref[:, s * SSM_GROUP:(s + 1) * SSM_GROUP, :half] = x_e
        toep_ref[:, half + s * SSM_GROUP:half + (s + 1) * SSM_GROUP, half:] = x_o


def _ssm_prep(lam_re, lam_im, log_dt, b_re_t, b_im_t, c_re, c_im, groups_per_step=16):
    g, p, c = SSM_GROUPS, SSM_STATE, SSM_GROUP
    gb = groups_per_step

    def spec(*tail):
        return pl.BlockSpec((gb,) + tail, lambda i: (i,) + (0,) * len(tail))

    def stacked(*tail):
        return pl.BlockSpec((2, gb) + tail, lambda i: (0, i) + (0,) * len(tail))

    pair_spec = pl.BlockSpec((gb // 2, MXU_DIM, MXU_DIM), lambda i: (i, 0, 0))
    pair_shape = jax.ShapeDtypeStruct((SSM_PAIRS, MXU_DIM, MXU_DIM), BF16)
    ct_re = jnp.tile(c_re.transpose(0, 2, 1), (1, 1, SSM_CHUNK))
    ct_im = jnp.tile(c_im.transpose(0, 2, 1), (1, 1, SSM_CHUNK))
    return pl.pallas_call(
        _ssm_prep_body,
        grid=(g // gb,),
        in_specs=[spec(1, p), spec(1, p), spec(p, 1), spec(p, 1), spec(1, 1), spec(c, p), spec(c, p),
                  spec(p, LANES), spec(p, LANES)],
        out_specs=[stacked(1, p), stacked(1, p),
                   pl.BlockSpec((gb // 2, 2 * c, 4 * p), lambda i: (i, 0, 0)),
                   pl.BlockSpec((2, gb // 2, 2 * p, 2 * c), lambda i: (0, i, 0, 0)),
                   pair_spec, pair_spec, pair_spec],
        out_shape=[jax.ShapeDtypeStruct((2, g, 1, p), F32), jax.ShapeDtypeStruct((2, g, 1, p), F32),
                   jax.ShapeDtypeStruct((SSM_PAIRS, 2 * c, 4 * p), F32),
                   jax.ShapeDtypeStruct((2, SSM_PAIRS, 2 * p, 2 * c), F32),
                   pair_shape, pair_shape, pair_shape],
        compiler_params=_params("parallel"),
        name="ssm_prep",
    )(lam_re.reshape(g, 1, p), lam_im.reshape(g, 1, p), lam_re.reshape(g, p, 1), lam_im.reshape(g, p, 1),
      log_dt.reshape(g, 1, 1), b_re_t, b_im_t, ct_re, ct_im)


def _ssm_body(lhs_ref, toep_ref, win_ref, wout_ref, a8_ref, yfl_ref, hfin_ref, st_ref, carry_ref, *, rows):
    i = pl.program_id(1)

    @pl.when(i == 0)
    def _():
        carry_ref[...] = jnp.zeros_like(carry_ref)

    def state_in(r, _):
        b = jnp.dot(lhs_ref[r], win_ref[r], preferred_element_type=F32)
        st_ref[0, pl.ds(r, rows, stride=SSM_ROW_PITCH), :] = b[:, :LANES]
        st_ref[1, pl.ds(r, rows, stride=SSM_ROW_PITCH), :] = b[:, LANES:]
        return 0

    lax.fori_loop(0, SSM_PAIRS, state_in, 0, unroll=4)

    a_re = a8_ref[0]
    a_im = a8_ref[1]

    def step(j, h):
        h_re, h_im = h
        base = pl.multiple_of(j * SSM_ROW_PITCH, SUBLANES)
        n_re = a_re * h_re - a_im * h_im + st_ref[0, pl.ds(base, SSM_PAIRS), :]
        n_im = a_re * h_im + a_im * h_re + st_ref[1, pl.ds(base, SSM_PAIRS), :]
        st_ref[0, pl.ds(base, SSM_PAIRS), :] = h_re
        st_ref[1, pl.ds(base, SSM_PAIRS), :] = h_im
        return n_re, n_im

    h_re, h_im = lax.fori_loop(0, rows, step, (carry_ref[0], carry_ref[1]), unroll=4)
    carry_ref[0] = h_re
    carry_ref[1] = h_im
    hfin_ref[0] = h_re
    hfin_ref[1] = h_im

    def chunk_out(r, _):
        hcat = jnp.concatenate([st_ref[0, pl.ds(r, rows, stride=SSM_ROW_PITCH), :],
                                st_ref[1, pl.ds(r, rows, stride=SSM_ROW_PITCH), :]], axis=1).astype(BF16)
        yfl_ref[r] = (jnp.dot(lhs_ref[r], toep_ref[r], preferred_element_type=F32)
                      + jnp.dot(hcat, wout_ref[r], preferred_element_type=F32))
        return 0

    lax.fori_loop(0, SSM_PAIRS, chunk_out, 0, unroll=4)


def _ssm_prompt(lhs, toep, win, wout, a8, batch, seq, tile):
    rows = tile // SSM_CHUNK
    n_tiles = seq // tile
    tok = pl.BlockSpec((SSM_PAIRS, rows, MXU_DIM), lambda b, i: (0, b * n_tiles + i, 0))
    pair_w = _const_spec((SSM_PAIRS, MXU_DIM, MXU_DIM))
    return pl.pallas_call(
        functools.partial(_ssm_body, rows=rows),
        grid=(batch, n_tiles),
        in_specs=[tok, pair_w, pair_w, pair_w, _const_spec((2, SSM_PAIRS, LANES))],
        out_specs=[tok, pl.BlockSpec((None, 2, SSM_PAIRS, LANES), lambda b, i: (b, 0, 0, 0))],
        out_shape=[jax.ShapeDtypeStruct((SSM_PAIRS, batch * seq // SSM_CHUNK, MXU_DIM), F32),
                   jax.ShapeDtypeStruct((batch, 2, SSM_PAIRS, LANES), F32)],
        scratch_shapes=[pltpu.VMEM((2, rows * SSM_ROW_PITCH, LANES), F32),
                        pltpu.VMEM((2, SSM_PAIRS, LANES), F32)],
        compiler_params=_params("parallel", "arbitrary"),
        name="ssm_prompt",
    )(lhs, toep, win, wout, a8)


def _ssm_step_body(u_ref, hre_ref, him_ref, abar_ref, bstep_ref, cstep_ref, y_ref, ore_ref, oim_ref):
    hp = lax.Precision.HIGHEST
    pair_ch = 2 * SSM_GROUP
    pairs_per_block = LANES // pair_ch
    for r in range(SSM_PAIRS):
        blk = r // pairs_per_block
        ch = slice((r % pairs_per_block) * pair_ch, (r % pairs_per_block + 1) * pair_ch)
        st = slice(r * LANES, (r + 1) * LANES)
        bu = jnp.dot(u_ref[blk, :, ch], bstep_ref[r], precision=hp, preferred_element_type=F32)
        a_re = abar_ref[0, r:r + 1, :]
        a_im = abar_ref[1, r:r + 1, :]
        h_re = hre_ref[:, st]
        h_im = him_ref[:, st]
        n_re = a_re * h_re - a_im * h_im + bu[:, :LANES]
        n_im = a_re * h_im + a_im * h_re + bu[:, LANES:]
        ore_ref[:, st] = n_re
        oim_ref[:, st] = n_im
        y_ref[blk, :, ch] = (jnp.dot(n_re, cstep_ref[0, r], precision=hp, preferred_element_type=F32)
                             - jnp.dot(n_im, cstep_ref[1, r], precision=hp, preferred_element_type=F32))


def _ssm_step(u, h_re, h_im, abar, bstep, cstep):
    b = u.shape[1]
    width = SSM_GROUPS * SSM_STATE
    return pl.pallas_call(
        _ssm_step_body,
        out_shape=[jax.ShapeDtypeStruct((LANE_BLOCKS, b, LANES), F32),
                   jax.ShapeDtypeStruct((b, width), F32),
                   jax.ShapeDtypeStruct((b, width), F32)],
        compiler_params=pltpu.CompilerParams(vmem_limit_bytes=VMEM_LIMIT_BYTES),
        name="ssm_step",
    )(u, h_re, h_im, abar, bstep, cstep)


def _attn_prompt_body(*refs, part_rows):
    per_group = 5
    ins = refs[:per_group * N_DIL]
    o_ref, m_ref, l_ref, acc_ref, bias_ref = refs[per_group * N_DIL:]
    part = pl.program_id(1)
    tq = KEYS_BACK
    row = lax.broadcasted_iota(jnp.int32, (tq, 2 * tq), 0)
    col = lax.broadcasted_iota(jnp.int32, (tq, 2 * tq), 1)
    back = row + tq - col
    in_window = (back >= 0) & (back <= KEYS_BACK)
    first_head = lax.broadcasted_iota(jnp.int32, (1, LANES), 1) < HEAD_DIM
    nt = (((1,), (1,)), ((), ()))

    order = tuple(reversed(range(N_DIL)))
    ones = jnp.ones((2 * tq, LANES), BF16)
    for g in order:
        q_ref, k_ref, kb_ref, v_ref, vb_ref = ins[per_group * g:per_group * (g + 1)]
        d = DIL_RATES[g]
        blocks_per_residue = part_rows // d // tq
        dist = (back * d).astype(F32)
        for h in range(HEADS):
            bias = jnp.where(in_window, -(_slope(g, h) * LOG2_E) * dist, -jnp.inf)
            bias_ref[h] = bias
            bias_ref[HEADS + h] = jnp.where(col >= tq, bias, -jnp.inf)

        def block(mi, _, g=g, d=d, q_ref=q_ref, k_ref=k_ref, kb_ref=kb_ref, v_ref=v_ref, vb_ref=vb_ref,
                  blocks_per_residue=blocks_per_residue):
            residue = mi // blocks_per_residue
            n = mi % blocks_per_residue
            cur = pl.multiple_of(n * tq, tq)
            prev = pl.multiple_of(jnp.maximum(n - 1, 0) * tq, tq)
            bias_at = jnp.where((n == 0) & (part == 0), HEADS, 0)
            token0 = residue + d * tq * n
            rows = pl.ds(pl.multiple_of(token0, tq), tq) if d == 1 else pl.ds(token0, tq, stride=d)
            pairs = range(HEADS // 2)
            first, final = g == order[0], g == order[-1]
            old = None if first else [(m_ref[pair, rows, :], l_ref[pair, rows, :], acc_ref[pair, rows, :])
                                      for pair in pairs]
            new = []
            for pair in pairs:
                lanes = slice(pair * LANES, (pair + 1) * LANES)
                qp = q_ref[residue, pl.ds(cur, tq), lanes]
                k_prev = jnp.where(n == 0, kb_ref[residue, :, lanes], k_ref[residue, pl.ds(prev, tq), lanes])
                v_prev = jnp.where(n == 0, vb_ref[residue, :, lanes], v_ref[residue, pl.ds(prev, tq), lanes])
                kp = jnp.concatenate([k_prev, k_ref[residue, pl.ds(cur, tq), lanes]], axis=0)
                vp = jnp.concatenate([v_prev, v_ref[residue, pl.ds(cur, tq), lanes]], axis=0)
                vp = jnp.concatenate([vp, ones], axis=1)
                stats = []
                for e in range(2):
                    qm = jnp.where(first_head if e == 0 else ~first_head, qp, jnp.zeros_like(qp))
                    s = lax.dot_general(qm, kp, nt, preferred_element_type=F32)
                    s = s + bias_ref[bias_at + 2 * pair + e]
                    m = jnp.max(s, axis=-1, keepdims=True)
                    p = jnp.exp2(s - m)
                    pv = jnp.dot(p.astype(BF16), vp, preferred_element_type=F32)
                    stats.append((m, pv[:, LANES:], pv[:, :LANES]))
                m_new, l_new, acc_new = (jnp.where(first_head, a, b) for a, b in zip(*stats))
                if not first:
                    m_old, l_old, acc_old = old[pair]
                    m_tot = jnp.maximum(m_old, m_new)
                    w_old = jnp.exp2(m_old - m_tot)
                    w_new = jnp.exp2(m_new - m_tot)
                    l_new = w_old * l_old + w_new * l_new
                    acc_new = w_old * acc_old + w_new * acc_new
                    m_new = m_tot
                new.append((m_new, l_new, acc_new))
            for pair, (m_new, l_new, acc_new) in zip(pairs, new):
                if not final:
                    m_ref[pair, rows, :] = m_new
                    l_ref[pair, rows, :] = l_new
                    acc_ref[pair, rows, :] = acc_new
                else:
                    o_ref[pair, rows, :] = acc_new / l_new
            return 0

        lax.fori_loop(0, part_rows // tq, block, 0, unroll=2)


def _attn_prompt(qkv, batch, seq):
    slabs = SLOT_WIDTH // LANES
    tq = KEYS_BACK
    part_rows = DIL_RATES[-1] * tq
    parts = seq // part_rows
    args, specs = [], []
    for g, d in enumerate(DIL_RATES):
        rows = part_rows // d
        cur = pl.BlockSpec((None, d, rows, SLOT_WIDTH), lambda b, p: (b, 0, p, 0))
        before = pl.BlockSpec((None, d, tq, SLOT_WIDTH),
                              functools.partial(lambda b, p, step: (b, 0, jnp.maximum(p * step - 1, 0), 0),
                                                step=rows // tq))
        q, k, v = qkv[g], qkv[N_DIL + g], qkv[2 * N_DIL + g]
        args += [q, k, k, v, v]
        specs += [cur, cur, before, cur, before]
    running = pltpu.VMEM((slabs, part_rows, LANES), F32)
    return pl.pallas_call(
        functools.partial(_attn_prompt_body, part_rows=part_rows),
        grid=(batch, parts),
        in_specs=specs,
        out_specs=pl.BlockSpec((slabs, part_rows, LANES), lambda b, p: (0, b * parts + p, 0)),
        out_shape=jax.ShapeDtypeStruct((slabs, batch * seq, LANES), F32),
        scratch_shapes=[running, running, running, pltpu.VMEM((2 * HEADS, tq, 2 * tq), F32)],
        compiler_params=_params("parallel", "arbitrary"),
        name="attn_prompt",
    )(*args)


def _as_column(row_vec):
    n = row_vec.shape[1]
    eye = lax.broadcasted_iota(jnp.int32, (n, n), 0) == lax.broadcasted_iota(jnp.int32, (n, n), 1)
    return jnp.sum(jnp.where(eye, row_vec, 0.0), axis=1, keepdims=True)


def _sample_window_jobs(g, q_row, k_row, v_row, c_ref, n_ref, result):
    head_row = lax.broadcasted_iota(jnp.int32, (SUBLANES, SLOT_WIDTH), 0)
    own_head = lax.broadcasted_iota(jnp.int32, (SUBLANES, SLOT_WIDTH), 1) // HEAD_DIM == head_row
    head_col = lax.broadcasted_iota(jnp.int32, (SUBLANES, 1), 0)
    w = DIL_WINDOWS[g]
    d = DIL_RATES[g]
    cols = slice(g * SLOT_WIDTH, (g + 1) * SLOT_WIDTH)
    q_g, k_new, v_new = q_row[:, cols], k_row[:, cols], v_row[:, cols]
    last = lax.broadcasted_iota(jnp.int32, (SLOT_WIDTH, w), 1) == w - 1
    shape = (HEADS, HEAD_DIM, w)
    carry = {}

    def key_side():
        kt = c_ref[0].reshape(SLOT_WIDTH, w)
        q_heads = jnp.where(own_head, q_g, 0.0)
        s = jnp.dot(q_heads.astype(BF16), kt.astype(BF16), preferred_element_type=F32)
        dist = w - lax.broadcasted_iota(jnp.int32, (1, w), 1)
        slope = functools.reduce(lambda acc, h: jnp.where(head_col == h, _slope(g, h), acc), range(HEADS), 0.0)
        s = jnp.where((dist & (d - 1)) == 0, s - slope * dist.astype(F32), -jnp.inf)
        s_new = jnp.sum(q_heads * k_new, axis=1, keepdims=True)
        m = jnp.maximum(jnp.max(s, axis=1, keepdims=True), s_new)
        p = jnp.exp(s - m)
        p_new = jnp.exp(s_new - m)
        carry.update(m=m, p=p, p_new=p_new, den=jnp.sum(p, axis=1, keepdims=True) + p_new)
        n_ref[0] = jnp.where(last, _as_column(k_new), pltpu.roll(kt, w - 1, 1)).reshape(shape)

    def value_side():
        vt = c_ref[1].reshape(SLOT_WIDTH, w)
        pv = lax.dot_general(carry["p"].astype(BF16), vt.astype(BF16), (((1,), (1,)), ((), ())),
                             preferred_element_type=F32)
        o_heads = (pv + carry["p_new"] * v_new) / carry["den"]
        n_ref[1] = jnp.where(last, _as_column(v_new), pltpu.roll(vt, w - 1, 1)).reshape(shape)
        result[g] = (jnp.sum(jnp.where(own_head, o_heads, 0.0), axis=0, keepdims=True),
                     jnp.sum(jnp.where(own_head, carry["m"] + jnp.log(carry["den"]), 0.0), axis=0, keepdims=True))

    return [key_side, value_side]


def _mix_groups(outs, lses):
    top = functools.reduce(jnp.maximum, lses)
    wts = [jnp.exp(l - top) for l in lses]
    return sum(w_g * o_g for w_g, o_g in zip(wts, outs)) / sum(wts)


def _ffn_windows_body(x_ref, nw_ref, *refs):
    w_refs = refs[:3]
    q_ref, k_ref, v_ref, c0_ref, c1_ref, c2_ref, o_ref, ob_ref, n0_ref, n1_ref, n2_ref = refs[3:]
    b = pl.program_id(0)
    rows = [r[pl.ds(b, 1), :] for r in (q_ref, k_ref, v_ref)]
    windows = ((c0_ref, n0_ref), (c1_ref, n1_ref), (c2_ref, n2_ref))
    results = {}
    jobs = [job for g in reversed(range(N_DIL)) for job in _sample_window_jobs(g, *rows, *windows[g], results)]
    _ffn_body(x_ref, nw_ref, *w_refs, o_ref, side_jobs=jobs, chunks=FF_CHUNKS_FINE)
    outs, lses = zip(*(results[g] for g in range(N_DIL)))
    ob_ref[...] = _mix_groups(outs, lses)


def _ffn_with_sample_windows(x, norm_w, weights, q, k, v, caches):
    t = x.shape[0]
    b = q.shape[0]
    assert t % b == 0 and (t // b) % SUBLANES == 0
    tm = t // b
    row = pl.BlockSpec((tm, D_MODEL), lambda i: (i, 0))
    full = _const_spec((b, ATTN_WIDTH))
    win = [pl.BlockSpec((None, 2, HEADS, HEAD_DIM, w), lambda i: (i, 0, 0, 0, 0)) for w in DIL_WINDOWS]
    outs = pl.pallas_call(
        _ffn_windows_body,
        grid=(b,),
        in_specs=[row, _const_spec((1, D_MODEL))] + _ffn_weight_specs() + [full, full, full] + win,
        out_specs=[row, pl.BlockSpec((None, 1, SLOT_WIDTH), lambda i: (i, 0, 0))] + win,
        out_shape=([jax.ShapeDtypeStruct((t, D_MODEL), F32), jax.ShapeDtypeStruct((b, 1, SLOT_WIDTH), F32)]
                   + [jax.ShapeDtypeStruct(c.shape, F32) for c in caches]),
        compiler_params=_params("parallel"),
        name="ffn_windows",
    )(x, norm_w, *weights, q, k, v, *caches)
    return outs[0], outs[1].reshape(b, SLOT_WIDTH), outs[2:]


def _mix_out_body(x_ref, y_ref, u_ref, d_ref, ob_ref,
                  ga_ref, gb_ref, wglu_ref, wpa_ref, wpb_ref, wout_ref, out_ref, *unfold):
    tm = x_ref.shape[0]
    if unfold:
        y_ref, folded_ref = unfold[0], y_ref
        groups_per_block = LANES // SSM_GROUP
    pieces = MIX_OUT_PIECES if unfold else 1
    rows = tm // pieces
    for piece in range(pieces):
        at = slice(piece * rows, (piece + 1) * rows)
        if unfold:
            chunks = rows // SSM_CHUNK
            chunk_at = slice(piece * chunks, (piece + 1) * chunks)
            for blk in range(LANE_BLOCKS):
                groups = [folded_ref[blk * (groups_per_block // 2) + q_i // 2, chunk_at,
                                     (q_i % 2) * LANES:(q_i % 2 + 1) * LANES] for q_i in range(groups_per_block)]
                for t, rows_t in enumerate(_transpose_granules(groups)):
                    y_ref[blk, pl.ds(piece * rows + t, chunks, stride=SSM_CHUNK), :] = rows_t
        y_raw = jnp.concatenate([y_ref[blk, at, :] for blk in range(LANE_BLOCKS)], axis=1)
        u = jnp.concatenate([u_ref[blk, at, :] for blk in range(LANE_BLOCKS)], axis=1)
        y = jax.nn.gelu(y_raw + d_ref[...] * u)
        yb = y.astype(BF16)
        y_a = y * jax.nn.sigmoid(jnp.dot(yb, wglu_ref[...], preferred_element_type=F32))
        branch_a = jnp.dot(y_a.astype(BF16), wpa_ref[...], preferred_element_type=F32)

        o_b = jnp.concatenate([ob_ref[s, at, :] for s in range(SLOT_WIDTH // LANES)], axis=1)
        branch_b = jnp.dot(o_b.astype(BF16), wpb_ref[...], preferred_element_type=F32)

        merged = ga_ref[at, :] * branch_a + gb_ref[at, :] * branch_b
        out_ref[at, :] = x_ref[at, :] + jnp.dot(merged.astype(BF16), wout_ref[...], preferred_element_type=F32)


def _mix_out(x, y, u, ssm_d, o_b, ga, gb, w_glu, w_pa, w_pb, w_out, tm, folded_y):
    t = x.shape[0]
    wide = pl.BlockSpec((tm, D_MODEL), lambda i: (i, 0))
    blocked = _lane_blocked_spec(tm, lambda i: i)
    y_spec = pl.BlockSpec((SSM_PAIRS, tm // SSM_CHUNK, MXU_DIM), lambda i: (0, i, 0)) if folded_y else blocked
    scratch = [pltpu.VMEM((LANE_BLOCKS, tm, LANES), F32)] if folded_y else []
    slot = pl.BlockSpec((SLOT_WIDTH // LANES, tm, LANES), lambda i: (0, i, 0))
    return pl.pallas_call(
        _mix_out_body,
        grid=(t // tm,),
        in_specs=[wide, y_spec, blocked, _const_spec((1, D_MODEL)), slot, wide, wide,
                  _const_spec((D_MODEL, D_MODEL)), _const_spec((D_MODEL, D_MODEL)),
                  _const_spec((SLOT_WIDTH, D_MODEL)), _const_spec((D_MODEL, D_MODEL))],
        out_specs=wide,
        out_shape=jax.ShapeDtypeStruct((t, D_MODEL), F32),
        scratch_shapes=scratch,
        compiler_params=_params("parallel"),
        name="mix_out",
    )(x, y, u, ssm_d, o_b, ga, gb, w_glu, w_pa, w_pb, w_out)


def _head_segment_ones():
    head = jnp.arange(MXU_DIM) // HEAD_DIM
    return (head[:, None] == head[None, :]).astype(BF16)


def kernel(x_prompt, x_sample, state_ssm_re, state_ssm_im, cache_kv_w128, cache_kv_w512, cache_kv_w2048, ffn1_norm, ffn1_w_gate, ffn1_w_up, ffn1_w_down, mix_norm, w_in, ssm_lambda_re, ssm_lambda_im, ssm_b_re, ssm_b_im, ssm_c_re, ssm_c_im, ssm_d, ssm_log_dt, w_glu, q_gain, k_gain, w_proj_a, w_proj_b, w_out, ffn2_norm, ffn2_w_gate, ffn2_w_up, ffn2_w_down):
    depth = ffn1_norm.shape[0]
    assert depth == 1, "single-layer step"
    batch, seq, _ = x_prompt.shape
    dec_batch, dec_seq, _ = x_sample.shape
    assert dec_seq == 1 and seq % (DIL_RATES[-1] * KEYS_BACK) == 0
    layer = 0
    bf = lambda w: w[layer].astype(BF16)
    vec = lambda w: w[layer][None]
    ffn1 = (vec(ffn1_norm), (bf(ffn1_w_gate), bf(ffn1_w_up), bf(ffn1_w_down)))
    mix_norm, ssm_d = vec(mix_norm), vec(ssm_d)
    q_gain_t = jnp.tile(vec(q_gain), (1, N_DIL * HEADS))
    k_gain_t = jnp.tile(vec(k_gain), (1, N_DIL * HEADS))
    seg_ones = _head_segment_ones()

    abar, a8, bstep, cstep, toep, win, wout = _ssm_prep(
        ssm_lambda_re[layer], ssm_lambda_im[layer], ssm_log_dt[layer],
        ssm_b_re[layer].transpose(0, 2, 1), ssm_b_im[layer].transpose(0, 2, 1),
        ssm_c_re[layer], ssm_c_im[layer])
    a8 = a8.reshape(2, SSM_PAIRS, LANES)
    abar = abar.reshape(2, SSM_PAIRS, LANES)

    def mix_in(x1, tm, seq=None):
        return _mix_in(x1, mix_norm, w_in_b, q_gain_t, k_gain_t, seg_ones, tm, seq)

    def mix_out(x1, y, u, o_b, ga, gb, tm, folded_y):
        return _mix_out(x1, y, u, ssm_d, o_b, ga, gb, w_glu_b, w_pa_b, w_pb_b, w_out_b, tm, folded_y)

    tm = 512
    tm_ffn = 1024
    xp = x_prompt.reshape(batch * seq, D_MODEL)
    later = [w[layer] for w in (ffn2_w_gate, ffn2_w_up, ffn2_w_down, w_in, w_glu, w_proj_a, w_proj_b, w_out)]
    x1, *later_b = _ffn(xp, *ffn1, tm_ffn, convert=later)
    ffn2 = (vec(ffn2_norm), tuple(later_b[:3]))
    w_in_b, w_glu_b, w_pa_b, w_pb_b, w_out_b = later_b[3:]

    xs = x_sample.reshape(dec_batch, D_MODEL)
    xs1 = _ffn(xs, *ffn1, dec_batch)
    us, qs, ks, vs, gas, gbs = mix_in(xs1, dec_batch)
    caches = [c[layer].transpose(0, 2, 3, 4, 1) for c in (cache_kv_w128, cache_kv_w512, cache_kv_w2048)]

    u, ga, gb, u_folded, *rest = mix_in(x1, tm, seq)
    p_windows, qkv = rest[:N_DIL], rest[N_DIL:]
    y, h_fin = _ssm_prompt(u_folded, toep, win, wout, a8, batch, seq, tile=2048)
    o_b = _attn_prompt(qkv, batch, seq)
    x2 = mix_out(x1, y, u, o_b, ga, gb, tm, folded_y=True)
    yp, o_bs, new_caches = _ffn_with_sample_windows(x2, *ffn2, qs, ks, vs, caches)
    yp = yp.reshape(batch, seq, D_MODEL)
    h_fin = h_fin.reshape(batch, 2, SSM_GROUPS, SSM_STATE)
    p_re = h_fin[None, :, 0]
    p_im = h_fin[None, :, 1]
    p_kv = [c.transpose(0, 4, 1, 2, 3)[None] for c in p_windows]

    width = SSM_GROUPS * SSM_STATE
    ys_ssm, s_re, s_im = _ssm_step(us, state_ssm_re[layer].reshape(dec_batch, width),
                                   state_ssm_im[layer].reshape(dec_batch, width),
                                   abar, bstep, cstep)
    o_bs = o_bs.reshape(dec_batch, SLOT_WIDTH // LANES, LANES).transpose(1, 0, 2)
    xs2 = mix_out(xs1, ys_ssm, us, o_bs, gas, gbs, dec_batch, folded_y=False)
    ys = _ffn(xs2, *ffn2, dec_batch)
    ys = ys.reshape(dec_batch, 1, D_MODEL)
    s_re = s_re.reshape(1, dec_batch, SSM_GROUPS, SSM_STATE)
    s_im = s_im.reshape(1, dec_batch, SSM_GROUPS, SSM_STATE)
    s_kv = [c.transpose(0, 4, 1, 2, 3)[None] for c in new_caches]

    return (yp, ys, p_re, p_im, p_kv[0], p_kv[1], p_kv[2], s_re, s_im, s_kv[0], s_kv[1], s_kv[2])
```

```python
import functools

import jax
import jax.numpy as jnp
from jax import lax
from jax.experimental import pallas as pl
from jax.experimental.pallas import tpu as pltpu

F32 = jnp.float32
BF16 = jnp.bfloat16

D_MODEL = 1024
SSM_GROUP = 16
SSM_GROUPS = 64
SSM_STATE = 64
SSM_PAIRS = SSM_GROUPS // 2
HEAD_DIM = 64
HEADS = 4
DIL_WINDOWS = (128, 512, 2048)
DIL_RATES = (1, 4, 16)
N_DIL = 3
KEYS_BACK = 128
ATTN_WIDTH = N_DIL * HEADS * HEAD_DIM
SLOT_WIDTH = HEADS * HEAD_DIM
D_FF = 2816
RMS_EPS = 1e-6
ALIBI_MAX_EXP = 8.0
LOG2_E = 1.4426950408889634
IN_SEGMENTS = (D_MODEL, ATTN_WIDTH, ATTN_WIDTH, ATTN_WIDTH, D_MODEL, D_MODEL)
IN_WIDTH = sum(IN_SEGMENTS)

LANES = 128
SUBLANES = 8
MXU_DIM = 256
VMEM_LIMIT_BYTES = 56 * 1024 * 1024

LANE_BLOCKS = D_MODEL // LANES

SSM_CHUNK = SUBLANES
SSM_ROW_PITCH = 40
FF_CHUNKS = ((0, 1024), (1024, 2048), (2048, 2816))
FF_CHUNKS_FINE = ((0, 512), (512, 1024), (1024, 1536), (1536, 2048), (2048, 2560), (2560, 2816))
MIX_OUT_PIECES = 2


def _slope(group, head):
    return 2.0 ** (-ALIBI_MAX_EXP * (group * HEADS + head + 1) / (N_DIL * HEADS))


def _const_spec(shape):
    zeros = (0,) * len(shape)
    return pl.BlockSpec(shape, lambda *_: zeros, pipeline_mode=pl.Buffered(1))


def _lane_blocked_spec(rows, row_block):
    return pl.BlockSpec((LANE_BLOCKS, rows, LANES), lambda *idx: (0, row_block(*idx), 0))


def _params(*semantics):
    return pltpu.CompilerParams(dimension_semantics=semantics, vmem_limit_bytes=VMEM_LIMIT_BYTES)


def _rms(x, w):
    return x * lax.rsqrt(jnp.mean(x * x, axis=-1, keepdims=True) + RMS_EPS) * w


def _split_bf16(x, terms):
    parts = []
    for _ in range(terms):
        p = x.astype(BF16)
        parts.append(p)
        x = x - p.astype(F32)
    return parts


def _ffn_body(x_ref, nw_ref, wg_ref, wu_ref, wd_ref, *rest, side_jobs=(), chunks=FF_CHUNKS):
    n_cast = (len(rest) - 1) // 2
    o_ref = rest[n_cast]
    for src_ref, dst_ref in zip(rest[:n_cast], rest[n_cast + 1:]):
        dst_ref[...] = src_ref[...].astype(BF16)
    x = x_ref[...]
    h = _rms(x, nw_ref[...]).astype(BF16)
    acc = jnp.zeros_like(x)
    side_jobs = list(side_jobs)
    for lo, hi in chunks:
        g = jnp.dot(h, wg_ref[:, lo:hi], preferred_element_type=F32)
        u = jnp.dot(h, wu_ref[:, lo:hi], preferred_element_type=F32)
        a = (jax.nn.silu(g) * u).astype(BF16)
        acc = acc + jnp.dot(a, wd_ref[lo:hi, :], preferred_element_type=F32)
        if side_jobs:
            side_jobs.pop(0)()
    for job in side_jobs:
        job()
    o_ref[...] = x + 0.5 * acc


def _ffn_weight_specs():
    return [_const_spec((D_MODEL, D_FF)), _const_spec((D_MODEL, D_FF)), _const_spec((D_FF, D_MODEL))]


def _ffn(x, norm_w, weights, tm, convert=()):
    t = x.shape[0]
    steps = t // tm
    row = pl.BlockSpec((tm, D_MODEL), lambda i: (i, 0))
    side = []
    for w in convert:
        rows = w.shape[0] // steps
        assert rows * steps == w.shape[0] and rows % (2 * SUBLANES) == 0
        side.append(pl.BlockSpec((rows, w.shape[1]), lambda i: (i, 0)))
    outs = pl.pallas_call(
        _ffn_body,
        grid=(steps,),
        in_specs=[row, _const_spec((1, D_MODEL))] + _ffn_weight_specs() + side,
        out_specs=[row] + side,
        out_shape=[jax.ShapeDtypeStruct((t, D_MODEL), F32)] + [jax.ShapeDtypeStruct(w.shape, BF16) for w in convert],
        compiler_params=_params("parallel"),
        name="ffn",
    )(x, norm_w, *weights, *convert)
    return outs if convert else outs[0]


def _head_norm(x, gain, seg_ones):
    parts = _split_bf16(x * x, 2)
    blocks = []
    for lo in range(0, x.shape[1], MXU_DIM):
        blocks.append(sum(jnp.dot(p[:, lo:lo + MXU_DIM], seg_ones, preferred_element_type=F32) for p in parts))
    ss = jnp.concatenate(blocks, axis=1)
    return x * lax.rsqrt(ss * (1.0 / HEAD_DIM) + RMS_EPS) * gain


def _transpose_granules(xs):
    xs = list(xs)
    n = len(xs)
    block = lax.broadcasted_iota(jnp.int32, (1, LANES), 1) // SSM_GROUP
    bit = n // 2
    while bit:
        upper = (block & bit) != 0
        shift = SSM_GROUP * bit
        for lo in range(n):
            if lo & bit:
                continue
            hi = lo + bit
            x_lo, x_hi = xs[lo], xs[hi]
            xs[lo] = jnp.where(upper, pltpu.roll(x_hi, shift, 1), x_lo)
            xs[hi] = jnp.where(upper, x_hi, pltpu.roll(x_lo, LANES - shift, 1))
        bit //= 2
    return xs


def _mix_in_body(x_ref, nw_ref, w_ref, qg_ref, kg_ref, seg_ref, u_ref, *rest, prompt):
    h = _rms(x_ref[...], nw_ref[...]).astype(BF16)

    edges = [0]
    for width in IN_SEGMENTS:
        edges.append(edges[-1] + width)

    def proj(i):
        return jnp.dot(h, w_ref[:, edges[i]:edges[i + 1]], preferred_element_type=F32)

    u = proj(0)
    for blk in range(LANE_BLOCKS):
        u_ref[blk] = u[:, blk * LANES:(blk + 1) * LANES]
    seg_ones = seg_ref[...]
    q = _head_norm(proj(1), qg_ref[...], seg_ones) * (HEAD_DIM ** -0.5)
    k = _head_norm(proj(2), kg_ref[...], seg_ones)
    v = proj(3)
    if not prompt:
        q_ref, k_ref, v_ref, ga_ref, gb_ref = rest
        q_ref[...] = q
        k_ref[...] = k
        v_ref[...] = v
    else:
        ga_ref, gb_ref, fold_ref = rest[:3]
        windows = rest[3:3 + N_DIL]
        by_residue = rest[3 + N_DIL:3 + N_DIL + 3 * N_DIL]
        stage_ref = rest[-1]
    ga_ref[...] = jax.nn.sigmoid(proj(4))
    gb_ref[...] = jax.nn.sigmoid(proj(5))
    if not prompt:
        return

    rows = x_ref.shape[0]
    groups_per_block = LANES // SSM_GROUP
    for blk in range(LANE_BLOCKS):
        tokens = [u_ref[blk, pl.ds(s, rows // SSM_CHUNK, stride=SSM_CHUNK), :] for s in range(SSM_CHUNK)]
        for q_i, folded in enumerate(_transpose_granules(tokens)):
            pair = blk * (groups_per_block // 2) + q_i // 2
            fold_ref[pair, :, (q_i % 2) * LANES:(q_i % 2 + 1) * LANES] = folded.astype(BF16)

    slabs = SLOT_WIDTH // LANES
    for g, win_ref in enumerate(windows):
        keep = win_ref.shape[-1]
        for t, x in enumerate((k, v)):
            for s in range(slabs):
                lo = g * SLOT_WIDTH + s * LANES
                xt = x[:, lo:lo + LANES].T[:, rows - keep:]
                win_ref[t, 2 * s] = xt[:HEAD_DIM]
                win_ref[t, 2 * s + 1] = xt[HEAD_DIM:]

    for ti, x in enumerate((q * LOG2_E, k, v)):
        outs = by_residue[ti * N_DIL:(ti + 1) * N_DIL]
        outs[0][0] = x[:, :SLOT_WIDTH].astype(BF16)
        for g in range(1, N_DIL):
            d = DIL_RATES[g]
            for s in range(slabs):
                lo = g * SLOT_WIDTH + s * LANES
                stage_ref[ti, (g - 1) * slabs + s] = x[:, lo:lo + LANES]
            for r in range(d):
                for s in range(slabs):
                    piece = stage_ref[ti, (g - 1) * slabs + s, pl.ds(r, rows // d, stride=d), :]
                    outs[g][r, :, s * LANES:(s + 1) * LANES] = piece.astype(BF16)


def _mix_in(x, norm_w, w_in, q_gain, k_gain, seg_ones, tm, seq=None):
    t = x.shape[0]
    prompt = seq is not None

    def row(width):
        return pl.BlockSpec((tm, width), lambda i: (i, 0))

    def rows_f32(width):
        return jax.ShapeDtypeStruct((t, width), F32)

    out_specs = [_lane_blocked_spec(tm, lambda i: i)]
    out_shape = [jax.ShapeDtypeStruct((LANE_BLOCKS, t, LANES), F32)]
    scratch = []
    if not prompt:
        out_specs += [row(ATTN_WIDTH)] * 3 + [row(D_MODEL)] * 2
        out_shape += [rows_f32(ATTN_WIDTH)] * 3 + [rows_f32(D_MODEL)] * 2
    else:
        tiles = seq // tm
        out_specs += [row(D_MODEL)] * 2
        out_shape += [rows_f32(D_MODEL)] * 2
        out_specs.append(pl.BlockSpec((SSM_PAIRS, tm // SSM_CHUNK, MXU_DIM), lambda i: (0, i, 0)))
        out_shape.append(jax.ShapeDtypeStruct((SSM_PAIRS, t // SSM_CHUNK, MXU_DIM), BF16))
        for w in DIL_WINDOWS:
            keep = min(w, seq)
            cols = min(keep, tm)
            assert keep % cols == 0 and tm % cols == 0
            skip = tiles - keep // cols
            out_specs.append(pl.BlockSpec(
                (None, 2, HEADS, HEAD_DIM, cols),
                functools.partial(lambda i, skip: (i // tiles, 0, 0, 0, jnp.maximum(i % tiles - skip, 0)), skip=skip)))
            out_shape.append(jax.ShapeDtypeStruct((t // seq, 2, HEADS, HEAD_DIM, keep), F32))
        for _ in range(3):
            for d in DIL_RATES:
                out_specs.append(pl.BlockSpec((None, d, tm // d, SLOT_WIDTH),
                                              lambda i: (i // tiles, 0, i % tiles, 0)))
                out_shape.append(jax.ShapeDtypeStruct((t // seq, d, seq // d, SLOT_WIDTH), BF16))
        scratch = [pltpu.VMEM((3, (N_DIL - 1) * SLOT_WIDTH // LANES, tm, LANES), F32)]
    return pl.pallas_call(
        functools.partial(_mix_in_body, prompt=prompt),
        grid=(t // tm,),
        in_specs=[row(D_MODEL), _const_spec((1, D_MODEL)), _const_spec((D_MODEL, IN_WIDTH)),
                  _const_spec((1, ATTN_WIDTH)), _const_spec((1, ATTN_WIDTH)), _const_spec((MXU_DIM, MXU_DIM))],
        out_specs=out_specs,
        out_shape=out_shape,
        scratch_shapes=scratch,
        compiler_params=_params("arbitrary"),
        name="mix_in",
    )(x, norm_w, w_in, q_gain, k_gain, seg_ones)


def _cmul(ar, ai, br, bi):
    return ar * br - ai * bi, ar * bi + ai * br


def _ssm_prep_body(lrow_re_ref, lrow_im_ref, lcol_re_ref, lcol_im_ref, ldt_ref, bre_ref, bim_ref,
                   ct_re_ref, ct_im_ref, abar_ref, a8_ref, bstep_ref, cstep_ref, toep_ref, win_ref, wout_ref):
    n = SSM_CHUNK
    dt = jnp.exp(ldt_ref[...])

    def discretise(lam_re, lam_im):
        lr = jnp.minimum(lam_re, -1e-4)
        mag = jnp.exp(lr * dt)
        return lr, lam_im, mag * jnp.cos(lam_im * dt), mag * jnp.sin(lam_im * dt)

    def powers(ar, ai):
        out = [(jnp.ones_like(ar), jnp.zeros_like(ai))]
        for _ in range(n):
            out.append(_cmul(*out[-1], ar, ai))
        return out

    lr, li, ar, ai = discretise(lrow_re_ref[...], lrow_im_ref[...])
    den = lr * lr + li * li
    fr = ((ar - 1.0) * lr + ai * li) / den
    fi = (ai * lr - (ar - 1.0) * li) / den
    bbr, bbi = _cmul(fr, fi, bre_ref[...], bim_ref[...])
    row_pow = powers(ar, ai)
    abar_ref[0], abar_ref[1] = ar, ai
    a8_ref[0], a8_ref[1] = row_pow[n]

    def pair_halves(x):
        x = x.reshape((x.shape[0] // 2, 2) + x.shape[1:])
        return x[:, 0], x[:, 1]

    bstep_ref[...] = jnp.zeros_like(bstep_ref)
    cstep_ref[...] = jnp.zeros_like(cstep_ref)
    for part, x in enumerate((bbr, bbi)):
        x_e, x_o = pair_halves(x)
        bstep_ref[:, :SSM_GROUP, (2 * part) * SSM_STATE:(2 * part + 1) * SSM_STATE] = x_e
        bstep_ref[:, SSM_GROUP:, (2 * part + 1) * SSM_STATE:(2 * part + 2) * SSM_STATE] = x_o
    for part, ct_ref in enumerate((ct_re_ref, ct_im_ref)):
        x_e, x_o = pair_halves(ct_ref[:, :, :SSM_GROUP])
        cstep_ref[part, :, :SSM_STATE, :SSM_GROUP] = x_e
        cstep_ref[part, :, SSM_STATE:, SSM_GROUP:] = x_o

    toep_ref[...] = jnp.zeros_like(toep_ref)
    win_ref[...] = jnp.zeros_like(win_ref)
    wout_ref[...] = jnp.zeros_like(wout_ref)
    half = n * SSM_GROUP

    for s in range(n):
        w_re, w_im = _cmul(*row_pow[n - 1 - s], bbr, bbi)
        rows_e = slice(s * SSM_GROUP, (s + 1) * SSM_GROUP)
        rows_o = slice(half + s * SSM_GROUP, half + (s + 1) * SSM_GROUP)
        for part, x in enumerate((w_re, w_im)):
            x_e, x_o = pair_halves(x.astype(BF16))
            win_ref[:, rows_e, (2 * part) * SSM_STATE:(2 * part + 1) * SSM_STATE] = x_e
            win_ref[:, rows_o, (2 * part + 1) * SSM_STATE:(2 * part + 2) * SSM_STATE] = x_o

    _, _, ar_c, ai_c = discretise(lcol_re_ref[...], lcol_im_ref[...])
    col_pow = powers(ar_c, ai_c)
    lane_t = lax.broadcasted_iota(jnp.int32, (1, 1, LANES), 2) // SSM_GROUP

    def spread(first):
        re = im = jnp.zeros((1, 1, LANES), F32)
        for t in range(n):
            re = jnp.where(lane_t == t, col_pow[first + t][0], re)
            im = jnp.where(lane_t == t, col_pow[first + t][1], im)
        return re, im

    ct = (ct_re_ref[...], ct_im_ref[...])
    m0_re, m0_im = _cmul(*ct, *spread(0))
    m1_re, m1_im = _cmul(*ct, *spread(1))

    for part, x in enumerate((m1_re, -m1_im)):
        x_e, x_o = pair_halves(x.astype(BF16))
        wout_ref[:, (2 * part) * SSM_STATE:(2 * part + 1) * SSM_STATE, :half] = x_e
        wout_ref[:, (2 * part + 1) * SSM_STATE:(2 * part + 2) * SSM_STATE, half:] = x_o

    nn = (((2,), (1,)), ((0,), (0,)))
    hp = lax.Precision.HIGHEST
    kern = (lax.dot_general(bbr, m0_re, nn, precision=hp, preferred_element_type=F32)
            - lax.dot_general(bbi, m0_im, nn, precision=hp, preferred_element_type=F32))
    lane = lax.broadcasted_iota(jnp.int32, (1, 1, LANES), 2)
    for s in range(n):
        shifted = kern if s == 0 else jnp.where(lane >= s * SSM_GROUP, pltpu.roll(kern, s * SSM_GROUP, 2), 0.0)
        x_e, x_o = pair_halves(shifted.astype(BF16))
        toep_ref[:, s * SSM_GROUP:(s + 1) * SSM_GROUP, :half] = x_e
        toep_ref[:, half + s * SSM_GROUP:half + (s + 1) * SSM_GROUP, half:] = x_o


def _ssm_prep(lam_re, lam_im, log_dt, b_re_t, b_im_t, c_re, c_im, groups_per_step=16):
    g, p, c = SSM_GROUPS, SSM_STATE, SSM_GROUP
    gb = groups_per_step

    def spec(*tail):
        return pl.BlockSpec((gb,) + tail, lambda i: (i,) + (0,) * len(tail))

    def stacked(*tail):
        return pl.BlockSpec((2, gb) + tail, lambda i: (0, i) + (0,) * len(tail))

    pair_spec = pl.BlockSpec((gb // 2, MXU_DIM, MXU_DIM), lambda i: (i, 0, 0))
    pair_shape = jax.ShapeDtypeStruct((SSM_PAIRS, MXU_DIM, MXU_DIM), BF16)
    ct_re = jnp.tile(c_re.transpose(0, 2, 1), (1, 1, SSM_CHUNK))
    ct_im = jnp.tile(c_im.transpose(0, 2, 1), (1, 1, SSM_CHUNK))
    return pl.pallas_call(
        _ssm_prep_body,
        grid=(g // gb,),
        in_specs=[spec(1, p), spec(1, p), spec(p, 1), spec(p, 1), spec(1, 1), spec(c, p), spec(c, p),
                  spec(p, LANES), spec(p, LANES)],
        out_specs=[stacked(1, p), stacked(1, p),
                   pl.BlockSpec((gb // 2, 2 * c, 4 * p), lambda i: (i, 0, 0)),
                   pl.BlockSpec((2, gb // 2, 2 * p, 2 * c), lambda i: (0, i, 0, 0)),
                   pair_spec, pair_spec, pair_spec],
        out_shape=[jax.ShapeDtypeStruct((2, g, 1, p), F32), jax.ShapeDtypeStruct((2, g, 1, p), F32),
                   jax.ShapeDtypeStruct((SSM_PAIRS, 2 * c, 4 * p), F32),
                   jax.ShapeDtypeStruct((2, SSM_PAIRS, 2 * p, 2 * c), F32),
                   pair_shape, pair_shape, pair_shape],
        compiler_params=_params("parallel"),
        name="ssm_prep",
    )(lam_re.reshape(g, 1, p), lam_im.reshape(g, 1, p), lam_re.reshape(g, p, 1), lam_im.reshape(g, p, 1),
      log_dt.reshape(g, 1, 1), b_re_t, b_im_t, ct_re, ct_im)


def _ssm_body(lhs_ref, toep_ref, win_ref, wout_ref, a8_ref, yfl_ref, hfin_ref, st_ref, carry_ref, *, rows):
    i = pl.program_id(1)

    @pl.when(i == 0)
    def _():
        carry_ref[...] = jnp.zeros_like(carry_ref)

    def state_in(r, _):
        b = jnp.dot(lhs_ref[r], win_ref[r], preferred_element_type=F32)
        st_ref[0, pl.ds(r, rows, stride=SSM_ROW_PITCH), :] = b[:, :LANES]
        st_ref[1, pl.ds(r, rows, stride=SSM_ROW_PITCH), :] = b[:, LANES:]
        return 0

    lax.fori_loop(0, SSM_PAIRS, state_in, 0, unroll=4)

    a_re = a8_ref[0]
    a_im = a8_ref[1]

    def step(j, h):
        h_re, h_im = h
        base = pl.multiple_of(j * SSM_ROW_PITCH, SUBLANES)
        n_re = a_re * h_re - a_im * h_im + st_ref[0, pl.ds(base, SSM_PAIRS), :]
        n_im = a_re * h_im + a_im * h_re + st_ref[1, pl.ds(base, SSM_PAIRS), :]
        st_ref[0, pl.ds(base, SSM_PAIRS), :] = h_re
        st_ref[1, pl.ds(base, SSM_PAIRS), :] = h_im
        return n_re, n_im

    h_re, h_im = lax.fori_loop(0, rows, step, (carry_ref[0], carry_ref[1]), unroll=4)
    carry_ref[0] = h_re
    carry_ref[1] = h_im
    hfin_ref[0] = h_re
    hfin_ref[1] = h_im

    def chunk_out(r, _):
        hcat = jnp.concatenate([st_ref[0, pl.ds(r, rows, stride=SSM_ROW_PITCH), :],
                                st_ref[1, pl.ds(r, rows, stride=SSM_ROW_PITCH), :]], axis=1).astype(BF16)
        yfl_ref[r] = (jnp.dot(lhs_ref[r], toep_ref[r], preferred_element_type=F32)
                      + jnp.dot(hcat, wout_ref[r], preferred_element_type=F32))
        return 0

    lax.fori_loop(0, SSM_PAIRS, chunk_out, 0, unroll=4)


def _ssm_prompt(lhs, toep, win, wout, a8, batch, seq, tile):
    rows = tile // SSM_CHUNK
    n_tiles = seq // tile
    tok = pl.BlockSpec((SSM_PAIRS, rows, MXU_DIM), lambda b, i: (0, b * n_tiles + i, 0))
    pair_w = _const_spec((SSM_PAIRS, MXU_DIM, MXU_DIM))
    return pl.pallas_call(
        functools.partial(_ssm_body, rows=rows),
        grid=(batch, n_tiles),
        in_specs=[tok, pair_w, pair_w, pair_w, _const_spec((2, SSM_PAIRS, LANES))],
        out_specs=[tok, pl.BlockSpec((None, 2, SSM_PAIRS, LANES), lambda b, i: (b, 0, 0, 0))],
        out_shape=[jax.ShapeDtypeStruct((SSM_PAIRS, batch * seq // SSM_CHUNK, MXU_DIM), F32),
                   jax.ShapeDtypeStruct((batch, 2, SSM_PAIRS, LANES), F32)],
        scratch_shapes=[pltpu.VMEM((2, rows * SSM_ROW_PITCH, LANES), F32),
                        pltpu.VMEM((2, SSM_PAIRS, LANES), F32)],
        compiler_params=_params("parallel", "arbitrary"),
        name="ssm_prompt",
    )(lhs, toep, win, wout, a8)


def _ssm_step_body(u_ref, hre_ref, him_ref, abar_ref, bstep_ref, cstep_ref, y_ref, ore_ref, oim_ref):
    hp = lax.Precision.HIGHEST
    pair_ch = 2 * SSM_GROUP
    pairs_per_block = LANES // pair_ch
    for r in range(SSM_PAIRS):
        blk = r // pairs_per_block
        ch = slice((r % pairs_per_block) * pair_ch, (r % pairs_per_block + 1) * pair_ch)
        st = slice(r * LANES, (r + 1) * LANES)
        bu = jnp.dot(u_ref[blk, :, ch], bstep_ref[r], precision=hp, preferred_element_type=F32)
        a_re = abar_ref[0, r:r + 1, :]
        a_im = abar_ref[1, r:r + 1, :]
        h_re = hre_ref[:, st]
        h_im = him_ref[:, st]
        n_re = a_re * h_re - a_im * h_im + bu[:, :LANES]
        n_im = a_re * h_im + a_im * h_re + bu[:, LANES:]
        ore_ref[:, st] = n_re
        oim_ref[:, st] = n_im
        y_ref[blk, :, ch] = (jnp.dot(n_re, cstep_ref[0, r], precision=hp, preferred_element_type=F32)
                             - jnp.dot(n_im, cstep_ref[1, r], precision=hp, preferred_element_type=F32))


def _ssm_step(u, h_re, h_im, abar, bstep, cstep):
    b = u.shape[1]
    width = SSM_GROUPS * SSM_STATE
    return pl.pallas_call(
        _ssm_step_body,
        out_shape=[jax.ShapeDtypeStruct((LANE_BLOCKS, b, LANES), F32),
                   jax.ShapeDtypeStruct((b, width), F32),
                   jax.ShapeDtypeStruct((b, width), F32)],
        compiler_params=pltpu.CompilerParams(vmem_limit_bytes=VMEM_LIMIT_BYTES),
        name="ssm_step",
    )(u, h_re, h_im, abar, bstep, cstep)


def _attn_prompt_body(*refs, part_rows):
    per_group = 5
    ins = refs[:per_group * N_DIL]
    o_ref, m_ref, l_ref, acc_ref, bias_ref = refs[per_group * N_DIL:]
    part = pl.program_id(1)
    tq = KEYS_BACK
    row = lax.broadcasted_iota(jnp.int32, (tq, 2 * tq), 0)
    col = lax.broadcasted_iota(jnp.int32, (tq, 2 * tq), 1)
    back = row + tq - col
    in_window = (back >= 0) & (back <= KEYS_BACK)
    first_head = lax.broadcasted_iota(jnp.int32, (1, LANES), 1) < HEAD_DIM
    nt = (((1,), (1,)), ((), ()))

    order = tuple(reversed(range(N_DIL)))
    ones = jnp.ones((2 * tq, LANES), BF16)
    for g in order:
        q_ref, k_ref, kb_ref, v_ref, vb_ref = ins[per_group * g:per_group * (g + 1)]
        d = DIL_RATES[g]
        blocks_per_residue = part_rows // d // tq
        dist = (back * d).astype(F32)
        for h in range(HEADS):
            bias = jnp.where(in_window, -(_slope(g, h) * LOG2_E) * dist, -jnp.inf)
            bias_ref[h] = bias
            bias_ref[HEADS + h] = jnp.where(col >= tq, bias, -jnp.inf)

        for mi in range(part_rows // tq):
            residue = mi // blocks_per_residue
            n = mi % blocks_per_residue
            cur = n * tq
            prev = max(n - 1, 0) * tq
            bias_at = jnp.where(part == 0, HEADS, 0) if n == 0 else 0
            token0 = residue + d * tq * n
            rows = pl.ds(token0, tq) if d == 1 else pl.ds(token0, tq, stride=d)
            pairs = range(HEADS // 2)
            first, final = g == order[0], g == order[-1]
            old = None if first else [(m_ref[pair, rows, :], l_ref[pair, rows, :], acc_ref[pair, rows, :])
                                      for pair in pairs]
            new = []
            for pair in pairs:
                lanes = slice(pair * LANES, (pair + 1) * LANES)
                qp = q_ref[residue, pl.ds(cur, tq), lanes]
                k_prev = kb_ref[residue, :, lanes] if n == 0 else k_ref[residue, pl.ds(prev, tq), lanes]
                v_prev = vb_ref[residue, :, lanes] if n == 0 else v_ref[residue, pl.ds(prev, tq), lanes]
                kp = jnp.concatenate([k_prev, k_ref[residue, pl.ds(cur, tq), lanes]], axis=0)
                vp = jnp.concatenate([v_prev, v_ref[residue, pl.ds(cur, tq), lanes]], axis=0)
                vp = jnp.concatenate([vp, ones], axis=1)
                stats = []
                for e in range(2):
                    qm = jnp.where(first_head if e == 0 else ~first_head, qp, jnp.zeros_like(qp))
                    s = lax.dot_general(qm, kp, nt, preferred_element_type=F32)
                    s = s + bias_ref[bias_at + 2 * pair + e]
                    m = jnp.max(s, axis=-1, keepdims=True)
                    p = jnp.exp2(s - m)
                    pv = jnp.dot(p.astype(BF16), vp, preferred_element_type=F32)
                    stats.append((m, pv[:, LANES:], pv[:, :LANES]))
                m_new, l_new, acc_new = (jnp.where(first_head, a, b) for a, b in zip(*stats))
                if not first:
                    m_old, l_old, acc_old = old[pair]
                    m_tot = jnp.maximum(m_old, m_new)
                    w_old = jnp.exp2(m_old - m_tot)
                    w_new = jnp.exp2(m_new - m_tot)
                    l_new = w_old * l_old + w_new * l_new
                    acc_new = w_old * acc_old + w_new * acc_new
                    m_new = m_tot
                new.append((m_new, l_new, acc_new))
            for pair, (m_new, l_new, acc_new) in zip(pairs, new):
                if not final:
                    m_ref[pair, rows, :] = m_new
                    l_ref[pair, rows, :] = l_new
                    acc_ref[pair, rows, :] = acc_new
                else:
                    o_ref[pair, rows, :] = acc_new / l_new


def _attn_prompt(qkv, batch, seq):
    slabs = SLOT_WIDTH // LANES
    tq = KEYS_BACK
    part_rows = DIL_RATES[-1] * tq
    parts = seq // part_rows
    args, specs = [], []
    for g, d in enumerate(DIL_RATES):
        rows = part_rows // d
        cur = pl.BlockSpec((None, d, rows, SLOT_WIDTH), lambda b, p: (b, 0, p, 0))
        before = pl.BlockSpec((None, d, tq, SLOT_WIDTH),
                              functools.partial(lambda b, p, step: (b, 0, jnp.maximum(p * step - 1, 0), 0),
                                                step=rows // tq))
        q, k, v = qkv[g], qkv[N_DIL + g], qkv[2 * N_DIL + g]
        args += [q, k, k, v, v]
        specs += [cur, cur, before, cur, before]
    running = pltpu.VMEM((slabs, part_rows, LANES), F32)
    return pl.pallas_call(
        functools.partial(_attn_prompt_body, part_rows=part_rows),
        grid=(batch, parts),
        in_specs=specs,
        out_specs=pl.BlockSpec((slabs, part_rows, LANES), lambda b, p: (0, b * parts + p, 0)),
        out_shape=jax.ShapeDtypeStruct((slabs, batch * seq, LANES), F32),
        scratch_shapes=[running, running, running, pltpu.VMEM((2 * HEADS, tq, 2 * tq), F32)],
        compiler_params=_params("parallel", "arbitrary"),
        name="attn_prompt",
    )(*args)


def _as_column(row_vec):
    n = row_vec.shape[1]
    eye = lax.broadcasted_iota(jnp.int32, (n, n), 0) == lax.broadcasted_iota(jnp.int32, (n, n), 1)
    return jnp.sum(jnp.where(eye, row_vec, 0.0), axis=1, keepdims=True)


def _sample_window_jobs(g, q_row, k_row, v_row, c_ref, n_ref, result):
    head_row = lax.broadcasted_iota(jnp.int32, (SUBLANES, SLOT_WIDTH), 0)
    own_head = lax.broadcasted_iota(jnp.int32, (SUBLANES, SLOT_WIDTH), 1) // HEAD_DIM == head_row
    head_col = lax.broadcasted_iota(jnp.int32, (SUBLANES, 1), 0)
    w = DIL_WINDOWS[g]
    d = DIL_RATES[g]
    cols = slice(g * SLOT_WIDTH, (g + 1) * SLOT_WIDTH)
    q_g, k_new, v_new = q_row[:, cols], k_row[:, cols], v_row[:, cols]
    last = lax.broadcasted_iota(jnp.int32, (SLOT_WIDTH, w), 1) == w - 1
    shape = (HEADS, HEAD_DIM, w)
    carry = {}

    def key_side():
        kt = c_ref[0].reshape(SLOT_WIDTH, w)
        q_heads = jnp.where(own_head, q_g, 0.0)
        s = jnp.dot(q_heads.astype(BF16), kt.astype(BF16), preferred_element_type=F32)
        dist = w - lax.broadcasted_iota(jnp.int32, (1, w), 1)
        slope = functools.reduce(lambda acc, h: jnp.where(head_col == h, _slope(g, h), acc), range(HEADS), 0.0)
        s = jnp.where((dist & (d - 1)) == 0, s - slope * dist.astype(F32), -jnp.inf)
        s_new = jnp.sum(q_heads * k_new, axis=1, keepdims=True)
        m = jnp.maximum(jnp.max(s, axis=1, keepdims=True), s_new)
        p = jnp.exp(s - m)
        p_new = jnp.exp(s_new - m)
        carry.update(m=m, p=p, p_new=p_new, den=jnp.sum(p, axis=1, keepdims=True) + p_new)
        n_ref[0] = jnp.where(last, _as_column(k_new), pltpu.roll(kt, w - 1, 1)).reshape(shape)

    def value_side():
        vt = c_ref[1].reshape(SLOT_WIDTH, w)
        pv = lax.dot_general(carry["p"].astype(BF16), vt.astype(BF16), (((1,), (1,)), ((), ())),
                             preferred_element_type=F32)
        o_heads = (pv + carry["p_new"] * v_new) / carry["den"]
        n_ref[1] = jnp.where(last, _as_column(v_new), pltpu.roll(vt, w - 1, 1)).reshape(shape)
        result[g] = (jnp.sum(jnp.where(own_head, o_heads, 0.0), axis=0, keepdims=True),
                     jnp.sum(jnp.where(own_head, carry["m"] + jnp.log(carry["den"]), 0.0), axis=0, keepdims=True))

    return [key_side, value_side]


def _mix_groups(outs, lses):
    top = functools.reduce(jnp.maximum, lses)
    wts = [jnp.exp(l - top) for l in lses]
    return sum(w_g * o_g for w_g, o_g in zip(wts, outs)) / sum(wts)


def _ffn_windows_body(x_ref, nw_ref, *refs):
    w_refs = refs[:3]
    q_ref, k_ref, v_ref, c0_ref, c1_ref, c2_ref, o_ref, ob_ref, n0_ref, n1_ref, n2_ref = refs[3:]
    b = pl.program_id(0)
    rows = [r[pl.ds(b, 1), :] for r in (q_ref, k_ref, v_ref)]
    windows = ((c0_ref, n0_ref), (c1_ref, n1_ref), (c2_ref, n2_ref))
    results = {}
    jobs = [job for g in reversed(range(N_DIL)) for job in _sample_window_jobs(g, *rows, *windows[g], results)]
    _ffn_body(x_ref, nw_ref, *w_refs, o_ref, side_jobs=jobs, chunks=FF_CHUNKS_FINE)
    outs, lses = zip(*(results[g] for g in range(N_DIL)))
    ob_ref[...] = _mix_groups(outs, lses)


def _ffn_with_sample_windows(x, norm_w, weights, q, k, v, caches):
    t = x.shape[0]
    b = q.shape[0]
    assert t % b == 0 and (t // b) % SUBLANES == 0
    tm = t // b
    row = pl.BlockSpec((tm, D_MODEL), lambda i: (i, 0))
    full = _const_spec((b, ATTN_WIDTH))
    win = [pl.BlockSpec((None, 2, HEADS, HEAD_DIM, w), lambda i: (i, 0, 0, 0, 0)) for w in DIL_WINDOWS]
    outs = pl.pallas_call(
        _ffn_windows_body,
        grid=(b,),
        in_specs=[row, _const_spec((1, D_MODEL))] + _ffn_weight_specs() + [full, full, full] + win,
        out_specs=[row, pl.BlockSpec((None, 1, SLOT_WIDTH), lambda i: (i, 0, 0))] + win,
        out_shape=([jax.ShapeDtypeStruct((t, D_MODEL), F32), jax.ShapeDtypeStruct((b, 1, SLOT_WIDTH), F32)]
                   + [jax.ShapeDtypeStruct(c.shape, F32) for c in caches]),
        compiler_params=_params("parallel"),
        name="ffn_windows",
    )(x, norm_w, *weights, q, k, v, *caches)
    return outs[0], outs[1].reshape(b, SLOT_WIDTH), outs[2:]


def _mix_out_body(x_ref, y_ref, u_ref, d_ref, ob_ref,
                  ga_ref, gb_ref, wglu_ref, wpa_ref, wpb_ref, wout_ref, out_ref, *unfold):
    tm = x_ref.shape[0]
    if unfold:
        y_ref, folded_ref = unfold[0], y_ref
        groups_per_block = LANES // SSM_GROUP
    pieces = MIX_OUT_PIECES if unfold else 1
    rows = tm // pieces
    for piece in range(pieces):
        at = slice(piece * rows, (piece + 1) * rows)
        if unfold:
            chunks = rows // SSM_CHUNK
            chunk_at = slice(piece * chunks, (piece + 1) * chunks)
            for blk in range(LANE_BLOCKS):
                groups = [folded_ref[blk * (groups_per_block // 2) + q_i // 2, chunk_at,
                                     (q_i % 2) * LANES:(q_i % 2 + 1) * LANES] for q_i in range(groups_per_block)]
                for t, rows_t in enumerate(_transpose_granules(groups)):
                    y_ref[blk, pl.ds(piece * rows + t, chunks, stride=SSM_CHUNK), :] = rows_t
        y_raw = jnp.concatenate([y_ref[blk, at, :] for blk in range(LANE_BLOCKS)], axis=1)
        u = jnp.concatenate([u_ref[blk, at, :] for blk in range(LANE_BLOCKS)], axis=1)
        y = jax.nn.gelu(y_raw + d_ref[...] * u)
        yb = y.astype(BF16)
        y_a = y * jax.nn.sigmoid(jnp.dot(yb, wglu_ref[...], preferred_element_type=F32))
        branch_a = jnp.dot(y_a.astype(BF16), wpa_ref[...], preferred_element_type=F32)

        o_b = jnp.concatenate([ob_ref[s, at, :] for s in range(SLOT_WIDTH // LANES)], axis=1)
        branch_b = jnp.dot(o_b.astype(BF16), wpb_ref[...], preferred_element_type=F32)

        merged = ga_ref[at, :] * branch_a + gb_ref[at, :] * branch_b
        out_ref[at, :] = x_ref[at, :] + jnp.dot(merged.astype(BF16), wout_ref[...], preferred_element_type=F32)


def _mix_out(x, y, u, ssm_d, o_b, ga, gb, w_glu, w_pa, w_pb, w_out, tm, folded_y):
    t = x.shape[0]
    wide = pl.BlockSpec((tm, D_MODEL), lambda i: (i, 0))
    blocked = _lane_blocked_spec(tm, lambda i: i)
    y_spec = pl.BlockSpec((SSM_PAIRS, tm // SSM_CHUNK, MXU_DIM), lambda i: (0, i, 0)) if folded_y else blocked
    scratch = [pltpu.VMEM((LANE_BLOCKS, tm, LANES), F32)] if folded_y else []
    slot = pl.BlockSpec((SLOT_WIDTH // LANES, tm, LANES), lambda i: (0, i, 0))
    return pl.pallas_call(
        _mix_out_body,
        grid=(t // tm,),
        in_specs=[wide, y_spec, blocked, _const_spec((1, D_MODEL)), slot, wide, wide,
                  _const_spec((D_MODEL, D_MODEL)), _const_spec((D_MODEL, D_MODEL)),
                  _const_spec((SLOT_WIDTH, D_MODEL)), _const_spec((D_MODEL, D_MODEL))],
        out_specs=wide,
        out_shape=jax.ShapeDtypeStruct((t, D_MODEL), F32),
        scratch_shapes=scratch,
        compiler_params=_params("parallel"),
        name="mix_out",
    )(x, y, u, ssm_d, o_b, ga, gb, w_glu, w_pa, w_pb, w_out)


def _head_segment_ones():
    head = jnp.arange(MXU_DIM) // HEAD_DIM
    return (head[:, None] == head[None, :]).astype(BF16)


def kernel(x_prompt, x_sample, state_ssm_re, state_ssm_im, cache_kv_w128, cache_kv_w512, cache_kv_w2048, ffn1_norm, ffn1_w_gate, ffn1_w_up, ffn1_w_down, mix_norm, w_in, ssm_lambda_re, ssm_lambda_im, ssm_b_re, ssm_b_im, ssm_c_re, ssm_c_im, ssm_d, ssm_log_dt, w_glu, q_gain, k_gain, w_proj_a, w_proj_b, w_out, ffn2_norm, ffn2_w_gate, ffn2_w_up, ffn2_w_down):
    depth = ffn1_norm.shape[0]
    assert depth == 1, "single-layer step"
    batch, seq, _ = x_prompt.shape
    dec_batch, dec_seq, _ = x_sample.shape
    assert dec_seq == 1 and seq % (DIL_RATES[-1] * KEYS_BACK) == 0
    layer = 0
    bf = lambda w: w[layer].astype(BF16)
    vec = lambda w: w[layer][None]
    ffn1 = (vec(ffn1_norm), (bf(ffn1_w_gate), bf(ffn1_w_up), bf(ffn1_w_down)))
    mix_norm, ssm_d = vec(mix_norm), vec(ssm_d)
    q_gain_t = jnp.tile(vec(q_gain), (1, N_DIL * HEADS))
    k_gain_t = jnp.tile(vec(k_gain), (1, N_DIL * HEADS))
    seg_ones = _head_segment_ones()

    abar, a8, bstep, cstep, toep, win, wout = _ssm_prep(
        ssm_lambda_re[layer], ssm_lambda_im[layer], ssm_log_dt[layer],
        ssm_b_re[layer].transpose(0, 2, 1), ssm_b_im[layer].transpose(0, 2, 1),
        ssm_c_re[layer], ssm_c_im[layer])
    a8 = a8.reshape(2, SSM_PAIRS, LANES)
    abar = abar.reshape(2, SSM_PAIRS, LANES)

    def mix_in(x1, tm, seq=None):
        return _mix_in(x1, mix_norm, w_in_b, q_gain_t, k_gain_t, seg_ones, tm, seq)

    def mix_out(x1, y, u, o_b, ga, gb, tm, folded_y):
        return _mix_out(x1, y, u, ssm_d, o_b, ga, gb, w_glu_b, w_pa_b, w_pb_b, w_out_b, tm, folded_y)

    tm = 512
    tm_ffn = 1024
    xp = x_prompt.reshape(batch * seq, D_MODEL)
    later = [w[layer] for w in (ffn2_w_gate, ffn2_w_up, ffn2_w_down, w_in, w_glu, w_proj_a, w_proj_b, w_out)]
    x1, *later_b = _ffn(xp, *ffn1, tm_ffn, convert=later)
    ffn2 = (vec(ffn2_norm), tuple(later_b[:3]))
    w_in_b, w_glu_b, w_pa_b, w_pb_b, w_out_b = later_b[3:]

    xs = x_sample.reshape(dec_batch, D_MODEL)
    xs1 = _ffn(xs, *ffn1, dec_batch)
    us, qs, ks, vs, gas, gbs = mix_in(xs1, dec_batch)
    caches = [c[layer].transpose(0, 2, 3, 4, 1) for c in (cache_kv_w128, cache_kv_w512, cache_kv_w2048)]

    u, ga, gb, u_folded, *rest = mix_in(x1, tm, seq)
    p_windows, qkv = rest[:N_DIL], rest[N_DIL:]
    y, h_fin = _ssm_prompt(u_folded, toep, win, wout, a8, batch, seq, tile=2048)
    o_b = _attn_prompt(qkv, batch, seq)
    x2 = mix_out(x1, y, u, o_b, ga, gb, tm, folded_y=True)
    yp, o_bs, new_caches = _ffn_with_sample_windows(x2, *ffn2, qs, ks, vs, caches)
    yp = yp.reshape(batch, seq, D_MODEL)
    h_fin = h_fin.reshape(batch, 2, SSM_GROUPS, SSM_STATE)
    p_re = h_fin[None, :, 0]
    p_im = h_fin[None, :, 1]
    p_kv = [c.transpose(0, 4, 1, 2, 3)[None] for c in p_windows]

    width = SSM_GROUPS * SSM_STATE
    ys_ssm, s_re, s_im = _ssm_step(us, state_ssm_re[layer].reshape(dec_batch, width),
                                   state_ssm_im[layer].reshape(dec_batch, width),
                                   abar, bstep, cstep)
    o_bs = o_bs.reshape(dec_batch, SLOT_WIDTH // LANES, LANES).transpose(1, 0, 2)
    xs2 = mix_out(xs1, ys_ssm, us, o_bs, gas, gbs, dec_batch, folded_y=False)
    ys = _ffn(xs2, *ffn2, dec_batch)
    ys = ys.reshape(dec_batch, 1, D_MODEL)
    s_re = s_re.reshape(1, dec_batch, SSM_GROUPS, SSM_STATE)
    s_im = s_im.reshape(1, dec_batch, SSM_GROUPS, SSM_STATE)
    s_kv = [c.transpose(0, 4, 1, 2, 3)[None] for c in new_caches]

    return (yp, ys, p_re, p_im, p_kv[0], p_kv[1], p_kv[2], s_re, s_im, s_kv[0], s_kv[1], s_kv[2])
```

```python
import functools

import jax
import jax.numpy as jnp
from jax import lax
from jax.experimental import pallas as pl
from jax.experimental.pallas import tpu as pltpu

F32 = jnp.float32
BF16 = jnp.bfloat16

D_MODEL = 1024
SSM_GROUP = 16
SSM_GROUPS = 64
SSM_STATE = 64
SSM_PAIRS = SSM_GROUPS // 2
HEAD_DIM = 64
HEADS = 4
DIL_WINDOWS = (128, 512, 2048)
DIL_RATES = (1, 4, 16)
N_DIL = 3
KEYS_BACK = 128
ATTN_WIDTH = N_DIL * HEADS * HEAD_DIM
SLOT_WIDTH = HEADS * HEAD_DIM
D_FF = 2816
RMS_EPS = 1e-6
ALIBI_MAX_EXP = 8.0
LOG2_E = 1.4426950408889634
IN_SEGMENTS = (D_MODEL, ATTN_WIDTH, ATTN_WIDTH, ATTN_WIDTH, D_MODEL, D_MODEL)
IN_WIDTH = sum(IN_SEGMENTS)

LANES = 128
SUBLANES = 8
MXU_DIM = 256
VMEM_LIMIT_BYTES = 56 * 1024 * 1024

LANE_BLOCKS = D_MODEL // LANES

SSM_CHUNK = SUBLANES
SSM_ROW_PITCH = 40
FF_CHUNKS = ((0, 1024), (1024, 2048), (2048, 2816))
FF_CHUNKS_FINE = ((0, 512), (512, 1024), (1024, 1536), (1536, 2048), (2048, 2560), (2560, 2816))
MIX_OUT_PIECES = 2


def _slope(group, head):
    return 2.0 ** (-ALIBI_MAX_EXP * (group * HEADS + head + 1) / (N_DIL * HEADS))


def _const_spec(shape):
    zeros = (0,) * len(shape)
    return pl.BlockSpec(shape, lambda *_: zeros, pipeline_mode=pl.Buffered(1))


def _lane_blocked_spec(rows, row_block):
    return pl.BlockSpec((LANE_BLOCKS, rows, LANES), lambda *idx: (0, row_block(*idx), 0))


def _params(*semantics):
    return pltpu.CompilerParams(dimension_semantics=semantics, vmem_limit_bytes=VMEM_LIMIT_BYTES)


def _rms(x, w):
    return x * lax.rsqrt(jnp.mean(x * x, axis=-1, keepdims=True) + RMS_EPS) * w


def _split_bf16(x, terms):
    parts = []
    for _ in range(terms):
        p = x.astype(BF16)
        parts.append(p)
        x = x - p.astype(F32)
    return parts


def _ffn_body(x_ref, nw_ref, wg_ref, wu_ref, wd_ref, *rest, side_jobs=(), chunks=FF_CHUNKS):
    n_cast = (len(rest) - 1) // 2
    o_ref = rest[n_cast]
    for src_ref, dst_ref in zip(rest[:n_cast], rest[n_cast + 1:]):
        dst_ref[...] = src_ref[...].astype(BF16)
    x = x_ref[...]
    h = _rms(x, nw_ref[...]).astype(BF16)
    acc = jnp.zeros_like(x)
    side_jobs = list(side_jobs)
    for lo, hi in chunks:
        g = jnp.dot(h, wg_ref[:, lo:hi], preferred_element_type=F32)
        u = jnp.dot(h, wu_ref[:, lo:hi], preferred_element_type=F32)
        a = (jax.nn.silu(g) * u).astype(BF16)
        acc = acc + jnp.dot(a, wd_ref[lo:hi, :], preferred_element_type=F32)
        if side_jobs:
            side_jobs.pop(0)()
    for job in side_jobs:
        job()
    o_ref[...] = x + 0.5 * acc


def _ffn_weight_specs():
    return [_const_spec((D_MODEL, D_FF)), _const_spec((D_MODEL, D_FF)), _const_spec((D_FF, D_MODEL))]


def _row_block_specs(matrices, steps):
    specs = []
    for w in matrices:
        rows = w.shape[0] // steps
        assert rows * steps == w.shape[0] and rows % (2 * SUBLANES) == 0
        specs.append(pl.BlockSpec((rows, w.shape[1]), lambda i: (i, 0)))
    return specs


def _ffn(x, norm_w, weights, tm, convert=()):
    t = x.shape[0]
    steps = t // tm
    row = pl.BlockSpec((tm, D_MODEL), lambda i: (i, 0))
    side = _row_block_specs(convert, steps)
    outs = pl.pallas_call(
        _ffn_body,
        grid=(steps,),
        in_specs=[row, _const_spec((1, D_MODEL))] + _ffn_weight_specs() + side,
        out_specs=[row] + side,
        out_shape=[jax.ShapeDtypeStruct((t, D_MODEL), F32)] + [jax.ShapeDtypeStruct(w.shape, BF16) for w in convert],
        compiler_params=_params("parallel"),
        name="ffn",
    )(x, norm_w, *weights, *convert)
    return outs if convert else outs[0]


def _head_norm(x, gain, seg_ones):
    parts = _split_bf16(x * x, 2)
    blocks = []
    for lo in range(0, x.shape[1], MXU_DIM):
        blocks.append(sum(jnp.dot(p[:, lo:lo + MXU_DIM], seg_ones, preferred_element_type=F32) for p in parts))
    ss = jnp.concatenate(blocks, axis=1)
    return x * lax.rsqrt(ss * (1.0 / HEAD_DIM) + RMS_EPS) * gain


def _transpose_granules(xs):
    xs = list(xs)
    n = len(xs)
    block = lax.broadcasted_iota(jnp.int32, (1, LANES), 1) // SSM_GROUP
    bit = n // 2
    while bit:
        upper = (block & bit) != 0
        shift = SSM_GROUP * bit
        for lo in range(n):
            if lo & bit:
                continue
            hi = lo + bit
            x_lo, x_hi = xs[lo], xs[hi]
            xs[lo] = jnp.where(upper, pltpu.roll(x_hi, shift, 1), x_lo)
            xs[hi] = jnp.where(upper, x_hi, pltpu.roll(x_lo, LANES - shift, 1))
        bit //= 2
    return xs


def _mix_in_body(x_ref, nw_ref, w_ref, qg_ref, kg_ref, seg_ref, u_ref, *rest, prompt):
    h = _rms(x_ref[...], nw_ref[...]).astype(BF16)

    edges = [0]
    for width in IN_SEGMENTS:
        edges.append(edges[-1] + width)

    def proj(i):
        return jnp.dot(h, w_ref[:, edges[i]:edges[i + 1]], preferred_element_type=F32)

    u = proj(0)
    for blk in range(LANE_BLOCKS):
        u_ref[blk] = u[:, blk * LANES:(blk + 1) * LANES]
    seg_ones = seg_ref[...]
    q = _head_norm(proj(1), qg_ref[...], seg_ones) * (HEAD_DIM ** -0.5)
    k = _head_norm(proj(2), kg_ref[...], seg_ones)
    v = proj(3)
    if not prompt:
        q_ref, k_ref, v_ref, ga_ref, gb_ref = rest
        q_ref[...] = q
        k_ref[...] = k
        v_ref[...] = v
    else:
        ga_ref, gb_ref, fold_ref = rest[:3]
        windows = rest[3:3 + N_DIL]
        by_residue = rest[3 + N_DIL:3 + N_DIL + 3 * N_DIL]
        stage_ref = rest[-1]
    ga_ref[...] = jax.nn.sigmoid(proj(4))
    gb_ref[...] = jax.nn.sigmoid(proj(5))
    if not prompt:
        return

    rows = x_ref.shape[0]
    groups_per_block = LANES // SSM_GROUP
    for blk in range(LANE_BLOCKS):
        tokens = [u_ref[blk, pl.ds(s, rows // SSM_CHUNK, stride=SSM_CHUNK), :] for s in range(SSM_CHUNK)]
        for q_i, folded in enumerate(_transpose_granules(tokens)):
            pair = blk * (groups_per_block // 2) + q_i // 2
            fold_ref[pair, :, (q_i % 2) * LANES:(q_i % 2 + 1) * LANES] = folded.astype(BF16)

    slabs = SLOT_WIDTH // LANES
    for g, win_ref in enumerate(windows):
        keep = win_ref.shape[-1]
        for t, x in enumerate((k, v)):
            for s in range(slabs):
                lo = g * SLOT_WIDTH + s * LANES
                xt = x[:, lo:lo + LANES].T[:, rows - keep:]
                win_ref[t, 2 * s] = xt[:HEAD_DIM]
                win_ref[t, 2 * s + 1] = xt[HEAD_DIM:]

    for ti, x in enumerate((q * LOG2_E, k, v)):
        outs = by_residue[ti * N_DIL:(ti + 1) * N_DIL]
        outs[0][0] = x[:, :SLOT_WIDTH].astype(BF16)
        for g in range(1, N_DIL):
            d = DIL_RATES[g]
            for s in range(slabs):
                lo = g * SLOT_WIDTH + s * LANES
                stage_ref[ti, (g - 1) * slabs + s] = x[:, lo:lo + LANES]
            for r in range(d):
                for s in range(slabs):
                    piece = stage_ref[ti, (g - 1) * slabs + s, pl.ds(r, rows // d, stride=d), :]
                    outs[g][r, :, s * LANES:(s + 1) * LANES] = piece.astype(BF16)


def _mix_in(x, norm_w, w_in, q_gain, k_gain, seg_ones, tm, seq=None):
    t = x.shape[0]
    prompt = seq is not None

    def row(width):
        return pl.BlockSpec((tm, width), lambda i: (i, 0))

    def rows_f32(width):
        return jax.ShapeDtypeStruct((t, width), F32)

    out_specs = [_lane_blocked_spec(tm, lambda i: i)]
    out_shape = [jax.ShapeDtypeStruct((LANE_BLOCKS, t, LANES), F32)]
    scratch = []
    if not prompt:
        out_specs += [row(ATTN_WIDTH)] * 3 + [row(D_MODEL)] * 2
        out_shape += [rows_f32(ATTN_WIDTH)] * 3 + [rows_f32(D_MODEL)] * 2
    else:
        tiles = seq // tm
        out_specs += [row(D_MODEL)] * 2
        out_shape += [rows_f32(D_MODEL)] * 2
        out_specs.append(pl.BlockSpec((SSM_PAIRS, tm // SSM_CHUNK, MXU_DIM), lambda i: (0, i, 0)))
        out_shape.append(jax.ShapeDtypeStruct((SSM_PAIRS, t // SSM_CHUNK, MXU_DIM), BF16))
        for w in DIL_WINDOWS:
            keep = min(w, seq)
            cols = min(keep, tm)
            assert keep % cols == 0 and tm % cols == 0
            skip = tiles - keep // cols
            out_specs.append(pl.BlockSpec(
                (None, 2, HEADS, HEAD_DIM, cols),
                functools.partial(lambda i, skip: (i // tiles, 0, 0, 0, jnp.maximum(i % tiles - skip, 0)), skip=skip)))
            out_shape.append(jax.ShapeDtypeStruct((t // seq, 2, HEADS, HEAD_DIM, keep), F32))
        for _ in range(3):
            for d in DIL_RATES:
                out_specs.append(pl.BlockSpec((None, d, tm // d, SLOT_WIDTH),
                                              lambda i: (i // tiles, 0, i % tiles, 0)))
                out_shape.append(jax.ShapeDtypeStruct((t // seq, d, seq // d, SLOT_WIDTH), BF16))
        scratch = [pltpu.VMEM((3, (N_DIL - 1) * SLOT_WIDTH // LANES, tm, LANES), F32)]
    return pl.pallas_call(
        functools.partial(_mix_in_body, prompt=prompt),
        grid=(t // tm,),
        in_specs=[row(D_MODEL), _const_spec((1, D_MODEL)), _const_spec((D_MODEL, IN_WIDTH)),
                  _const_spec((1, ATTN_WIDTH)), _const_spec((1, ATTN_WIDTH)), _const_spec((MXU_DIM, MXU_DIM))],
        out_specs=out_specs,
        out_shape=out_shape,
        scratch_shapes=scratch,
        compiler_params=_params("arbitrary"),
        name="mix_in",
    )(x, norm_w, w_in, q_gain, k_gain, seg_ones)


def _cmul(ar, ai, br, bi):
    return ar * br - ai * bi, ar * bi + ai * br


def _ssm_prep_body(lrow_re_ref, lrow_im_ref, lcol_re_ref, lcol_im_ref, ldt_ref, bre_ref, bim_ref,
                   ct_re_ref, ct_im_ref, *rest):
    n_cast = (len(rest) - 7) // 2
    abar_ref, a8_ref, bstep_ref, cstep_ref, toep_ref, win_ref, wout_ref = rest[n_cast:n_cast + 7]
    for src_ref, dst_ref in zip(rest[:n_cast], rest[n_cast + 7:]):
        dst_ref[...] = src_ref[...].astype(BF16)
    n = SSM_CHUNK
    dt = jnp.exp(ldt_ref[...])

    def discretise(lam_re, lam_im):
        lr = jnp.minimum(lam_re, -1e-4)
        mag = jnp.exp(lr * dt)
        return lr, lam_im, mag * jnp.cos(lam_im * dt), mag * jnp.sin(lam_im * dt)

    def powers(ar, ai):
        out = [(jnp.ones_like(ar), jnp.zeros_like(ai))]
        for _ in range(n):
            out.append(_cmul(*out[-1], ar, ai))
        return out

    lr, li, ar, ai = discretise(lrow_re_ref[...], lrow_im_ref[...])
    den = lr * lr + li * li
    fr = ((ar - 1.0) * lr + ai * li) / den
    fi = (ai * lr - (ar - 1.0) * li) / den
    bbr, bbi = _cmul(fr, fi, bre_ref[...], bim_ref[...])
    row_pow = powers(ar, ai)
    abar_ref[0], abar_ref[1] = ar, ai
    a8_ref[0], a8_ref[1] = row_pow[n]

    def pair_halves(x):
        x = x.reshape((x.shape[0] // 2, 2) + x.shape[1:])
        return x[:, 0], x[:, 1]

    bstep_ref[...] = jnp.zeros_like(bstep_ref)
    cstep_ref[...] = jnp.zeros_like(cstep_ref)
    for part, x in enumerate((bbr, bbi)):
        x_e, x_o = pair_halves(x)
        bstep_ref[:, :SSM_GROUP, (2 * part) * SSM_STATE:(2 * part + 1) * SSM_STATE] = x_e
        bstep_ref[:, SSM_GROUP:, (2 * part + 1) * SSM_STATE:(2 * part + 2) * SSM_STATE] = x_o
    for part, ct_ref in enumerate((ct_re_ref, ct_im_ref)):
        x_e, x_o = pair_halves(ct_ref[:, :, :SSM_GROUP])
        cstep_ref[part, :, :SSM_STATE, :SSM_GROUP] = x_e
        cstep_ref[part, :, SSM_STATE:, SSM_GROUP:] = x_o

    toep_ref[...] = jnp.zeros_like(toep_ref)
    win_ref[...] = jnp.zeros_like(win_ref)
    wout_ref[...] = jnp.zeros_like(wout_ref)
    half = n * SSM_GROUP

    for s in range(n):
        w_re, w_im = _cmul(*row_pow[n - 1 - s], bbr, bbi)
        rows_e = slice(s * SSM_GROUP, (s + 1) * SSM_GROUP)
        rows_o = slice(half + s * SSM_GROUP, half + (s + 1) * SSM_GROUP)
        for part, x in enumerate((w_re, w_im)):
            x_e, x_o = pair_halves(x.astype(BF16))
            win_ref[:, rows_e, (2 * part) * SSM_STATE:(2 * part + 1) * SSM_STATE] = x_e
            win_ref[:, rows_o, (2 * part + 1) * SSM_STATE:(2 * part + 2) * SSM_STATE] = x_o

    _, _, ar_c, ai_c = discretise(lcol_re_ref[...], lcol_im_ref[...])
    col_pow = powers(ar_c, ai_c)
    lane_t = lax.broadcasted_iota(jnp.int32, (1, 1, LANES), 2) // SSM_GROUP

    def spread(first):
        re = im = jnp.zeros((1, 1, LANES), F32)
        for t in range(n):
            re = jnp.where(lane_t == t, col_pow[first + t][0], re)
            im = jnp.where(lane_t == t, col_pow[first + t][1], im)
        return re, im

    ct = (ct_re_ref[...], ct_im_ref[...])
    m0_re, m0_im = _cmul(*ct, *spread(0))
    m1_re, m1_im = _cmul(*ct, *spread(1))

    for part, x in enumerate((m1_re, -m1_im)):
        x_e, x_o = pair_halves(x.astype(BF16))
        wout_ref[:, (2 * part) * SSM_STATE:(2 * part + 1) * SSM_STATE, :half] = x_e
        wout_ref[:, (2 * part + 1) * SSM_STATE:(2 * part + 2) * SSM_STATE, half:] = x_o

    nn = (((2,), (1,)), ((0,), (0,)))
    hp = lax.Precision.HIGHEST
    kern = (lax.dot_general(bbr, m0_re, nn, precision=hp, preferred_element_type=F32)
            - lax.dot_general(bbi, m0_im, nn, precision=hp, preferred_element_type=F32))
    lane = lax.broadcasted_iota(jnp.int32, (1, 1, LANES), 2)
    for s in range(n):
        shifted = kern if s == 0 else jnp.where(lane >= s * SSM_GROUP, pltpu.roll(kern, s * SSM_GROUP, 2), 0.0)
        x_e, x_o = pair_halves(shifted.astype(BF16))
        toep_ref[:, s * SSM_GROUP:(s + 1) * SSM_GROUP, :half] = x_e
        toep_ref[:, half + s * SSM_GROUP:half + (s + 1) * SSM_GROUP, half:] = x_o


def _ssm_prep(lam_re, lam_im, log_dt, b_re_t, b_im_t, c_re, c_im, convert=(), groups_per_step=16):
    g, p, c = SSM_GROUPS, SSM_STATE, SSM_GROUP
    gb = groups_per_step
    side = _row_block_specs(convert, g // gb)

    def spec(*tail):
        return pl.BlockSpec((gb,) + tail, lambda i: (i,) + (0,) * len(tail))

    def stacked(*tail):
        return pl.BlockSpec((2, gb) + tail, lambda i: (0, i) + (0,) * len(tail))

    pair_spec = pl.BlockSpec((gb // 2, MXU_DIM, MXU_DIM), lambda i: (i, 0, 0))
    pair_shape = jax.ShapeDtypeStruct((SSM_PAIRS, MXU_DIM, MXU_DIM), BF16)
    ct_re = jnp.tile(c_re.transpose(0, 2, 1), (1, 1, SSM_CHUNK))
    ct_im = jnp.tile(c_im.transpose(0, 2, 1), (1, 1, SSM_CHUNK))
    return pl.pallas_call(
        _ssm_prep_body,
        grid=(g // gb,),
        in_specs=[spec(1, p), spec(1, p), spec(p, 1), spec(p, 1), spec(1, 1), spec(c, p), spec(c, p),
                  spec(p, LANES), spec(p, LANES)] + side,
        out_specs=[stacked(1, p), stacked(1, p),
                   pl.BlockSpec((gb // 2, 2 * c, 4 * p), lambda i: (i, 0, 0)),
                   pl.BlockSpec((2, gb // 2, 2 * p, 2 * c), lambda i: (0, i, 0, 0)),
                   pair_spec, pair_spec, pair_spec] + side,
        out_shape=[jax.ShapeDtypeStruct((2, g, 1, p), F32), jax.ShapeDtypeStruct((2, g, 1, p), F32),
                   jax.ShapeDtypeStruct((SSM_PAIRS, 2 * c, 4 * p), F32),
                   jax.ShapeDtypeStruct((2, SSM_PAIRS, 2 * p, 2 * c), F32),
                   pair_shape, pair_shape, pair_shape] + [jax.ShapeDtypeStruct(w.shape, BF16) for w in convert],
        compiler_params=_params("parallel"),
        name="ssm_prep",
    )(lam_re.reshape(g, 1, p), lam_im.reshape(g, 1, p), lam_re.reshape(g, p, 1), lam_im.reshape(g, p, 1),
      log_dt.reshape(g, 1, 1), b_re_t, b_im_t, ct_re, ct_im, *convert)


def _ssm_body(lhs_ref, toep_ref, win_ref, wout_ref, a8_ref, yfl_ref, hfin_ref, st_ref, carry_ref, *, rows):
    i = pl.program_id(1)

    @pl.when(i == 0)
    def _():
        carry_ref[...] = jnp.zeros_like(carry_ref)

    for r in range(SSM_PAIRS):
        b = jnp.dot(lhs_ref[r], win_ref[r], preferred_element_type=F32)
        st_ref[0, pl.ds(r, rows, stride=SSM_ROW_PITCH), :] = b[:, :LANES]
        st_ref[1, pl.ds(r, rows, stride=SSM_ROW_PITCH), :] = b[:, LANES:]

    a_re = a8_ref[0]
    a_im = a8_ref[1]

    def step(j, h):
        h_re, h_im = h
        base = pl.multiple_of(j * SSM_ROW_PITCH, SUBLANES)
        n_re = a_re * h_re - a_im * h_im + st_ref[0, pl.ds(base, SSM_PAIRS), :]
        n_im = a_re * h_im + a_im * h_re + st_ref[1, pl.ds(base, SSM_PAIRS), :]
        st_ref[0, pl.ds(base, SSM_PAIRS), :] = h_re
        st_ref[1, pl.ds(base, SSM_PAIRS), :] = h_im
        return n_re, n_im

    h_re, h_im = lax.fori_loop(0, rows, step, (carry_ref[0], carry_ref[1]), unroll=4)
    carry_ref[0] = h_re
    carry_ref[1] = h_im
    hfin_ref[0] = h_re
    hfin_ref[1] = h_im

    for r in range(SSM_PAIRS):
        hcat = jnp.concatenate([st_ref[0, pl.ds(r, rows, stride=SSM_ROW_PITCH), :],
                                st_ref[1, pl.ds(r, rows, stride=SSM_ROW_PITCH), :]], axis=1).astype(BF16)
        yfl_ref[r] = (jnp.dot(lhs_ref[r], toep_ref[r], preferred_element_type=F32)
                      + jnp.dot(hcat, wout_ref[r], preferred_element_type=F32))


def _ssm_prompt(lhs, toep, win, wout, a8, batch, seq, tile):
    rows = tile // SSM_CHUNK
    n_tiles = seq // tile
    tok = pl.BlockSpec((SSM_PAIRS, rows, MXU_DIM), lambda b, i: (0, b * n_tiles + i, 0))
    pair_w = _const_spec((SSM_PAIRS, MXU_DIM, MXU_DIM))
    return pl.pallas_call(
        functools.partial(_ssm_body, rows=rows),
        grid=(batch, n_tiles),
        in_specs=[tok, pair_w, pair_w, pair_w, _const_spec((2, SSM_PAIRS, LANES))],
        out_specs=[tok, pl.BlockSpec((None, 2, SSM_PAIRS, LANES), lambda b, i: (b, 0, 0, 0))],
        out_shape=[jax.ShapeDtypeStruct((SSM_PAIRS, batch * seq // SSM_CHUNK, MXU_DIM), F32),
                   jax.ShapeDtypeStruct((batch, 2, SSM_PAIRS, LANES), F32)],
        scratch_shapes=[pltpu.VMEM((2, rows * SSM_ROW_PITCH, LANES), F32),
                        pltpu.VMEM((2, SSM_PAIRS, LANES), F32)],
        compiler_params=_params("parallel", "arbitrary"),
        name="ssm_prompt",
    )(lhs, toep, win, wout, a8)


def _ssm_step_body(u_ref, hre_ref, him_ref, abar_ref, bstep_ref, cstep_ref, y_ref, ore_ref, oim_ref):
    hp = lax.Precision.HIGHEST
    pair_ch = 2 * SSM_GROUP
    pairs_per_block = LANES // pair_ch
    for r in range(SSM_PAIRS):
        blk = r // pairs_per_block
        ch = slice((r % pairs_per_block) * pair_ch, (r % pairs_per_block + 1) * pair_ch)
        st = slice(r * LANES, (r + 1) * LANES)
        bu = jnp.dot(u_ref[blk, :, ch], bstep_ref[r], precision=hp, preferred_element_type=F32)
        a_re = abar_ref[0, r:r + 1, :]
        a_im = abar_ref[1, r:r + 1, :]
        h_re = hre_ref[:, st]
        h_im = him_ref[:, st]
        n_re = a_re * h_re - a_im * h_im + bu[:, :LANES]
        n_im = a_re * h_im + a_im * h_re + bu[:, LANES:]
        ore_ref[:, st] = n_re
        oim_ref[:, st] = n_im
        y_ref[blk, :, ch] = (jnp.dot(n_re, cstep_ref[0, r], precision=hp, preferred_element_type=F32)
                             - jnp.dot(n_im, cstep_ref[1, r], precision=hp, preferred_element_type=F32))


def _ssm_step(u, h_re, h_im, abar, bstep, cstep):
    b = u.shape[1]
    width = SSM_GROUPS * SSM_STATE
    return pl.pallas_call(
        _ssm_step_body,
        out_shape=[jax.ShapeDtypeStruct((LANE_BLOCKS, b, LANES), F32),
                   jax.ShapeDtypeStruct((b, width), F32),
                   jax.ShapeDtypeStruct((b, width), F32)],
        compiler_params=pltpu.CompilerParams(vmem_limit_bytes=VMEM_LIMIT_BYTES),
        name="ssm_step",
    )(u, h_re, h_im, abar, bstep, cstep)


def _attn_prompt_body(*refs, part_rows):
    per_group = 5
    ins = refs[:per_group * N_DIL]
    o_ref, m_ref, l_ref, acc_ref, bias_ref = refs[per_group * N_DIL:]
    part = pl.program_id(1)
    tq = KEYS_BACK
    row = lax.broadcasted_iota(jnp.int32, (tq, 2 * tq), 0)
    col = lax.broadcasted_iota(jnp.int32, (tq, 2 * tq), 1)
    back = row + tq - col
    in_window = (back >= 0) & (back <= KEYS_BACK)
    first_head = lax.broadcasted_iota(jnp.int32, (1, LANES), 1) < HEAD_DIM
    nt = (((1,), (1,)), ((), ()))

    order = tuple(reversed(range(N_DIL)))
    ones = jnp.ones((2 * tq, LANES), BF16)
    for g in order:
        q_ref, k_ref, kb_ref, v_ref, vb_ref = ins[per_group * g:per_group * (g + 1)]
        d = DIL_RATES[g]
        blocks_per_residue = part_rows // d // tq
        dist = (back * d).astype(F32)
        for h in range(HEADS):
            bias = jnp.where(in_window, -(_slope(g, h) * LOG2_E) * dist, -jnp.inf)
            bias_ref[h] = bias
            bias_ref[HEADS + h] = jnp.where(col >= tq, bias, -jnp.inf)

        for mi in range(part_rows // tq):
            residue = mi // blocks_per_residue
            n = mi % blocks_per_residue
            cur = n * tq
            prev = max(n - 1, 0) * tq
            bias_at = jnp.where(part == 0, HEADS, 0) if n == 0 else 0
            token0 = residue + d * tq * n
            rows = pl.ds(token0, tq) if d == 1 else pl.ds(token0, tq, stride=d)
            pairs = range(HEADS // 2)
            first, final = g == order[0], g == order[-1]
            old = None if first else [(m_ref[pair, rows, :], l_ref[pair, rows, :], acc_ref[pair, rows, :])
                                      for pair in pairs]
            new = []
            for pair in pairs:
                lanes = slice(pair * LANES, (pair + 1) * LANES)
                qp = q_ref[residue, pl.ds(cur, tq), lanes]
                k_prev = kb_ref[residue, :, lanes] if n == 0 else k_ref[residue, pl.ds(prev, tq), lanes]
                v_prev = vb_ref[residue, :, lanes] if n == 0 else v_ref[residue, pl.ds(prev, tq), lanes]
                kp = jnp.concatenate([k_prev, k_ref[residue, pl.ds(cur, tq), lanes]], axis=0)
                vp = jnp.concatenate([v_prev, v_ref[residue, pl.ds(cur, tq), lanes]], axis=0)
                vp = jnp.concatenate([vp, ones], axis=1)
                stats = []
                for e in range(2):
                    qm = jnp.where(first_head if e == 0 else ~first_head, qp, jnp.zeros_like(qp))
                    s = lax.dot_general(qm, kp, nt, preferred_element_type=F32)
                    s = s + bias_ref[bias_at + 2 * pair + e]
                    m = jnp.max(s, axis=-1, keepdims=True)
                    p = jnp.exp2(s - m)
                    pv = jnp.dot(p.astype(BF16), vp, preferred_element_type=F32)
                    stats.append((m, pv[:, LANES:], pv[:, :LANES]))
                m_new, l_new, acc_new = (jnp.where(first_head, a, b) for a, b in zip(*stats))
                if not first:
                    m_old, l_old, acc_old = old[pair]
                    m_tot = jnp.maximum(m_old, m_new)
                    w_old = jnp.exp2(m_old - m_tot)
                    w_new = jnp.exp2(m_new - m_tot)
                    l_new = w_old * l_old + w_new * l_new
                    acc_new = w_old * acc_old + w_new * acc_new
                    m_new = m_tot
                new.append((m_new, l_new, acc_new))
            for pair, (m_new, l_new, acc_new) in zip(pairs, new):
                if not final:
                    m_ref[pair, rows, :] = m_new
                    l_ref[pair, rows, :] = l_new
                    acc_ref[pair, rows, :] = acc_new
                else:
                    o_ref[pair, rows, :] = acc_new / l_new


def _attn_prompt(qkv, batch, seq):
    slabs = SLOT_WIDTH // LANES
    tq = KEYS_BACK
    part_rows = DIL_RATES[-1] * tq
    parts = seq // part_rows
    args, specs = [], []
    for g, d in enumerate(DIL_RATES):
        rows = part_rows // d
        cur = pl.BlockSpec((None, d, rows, SLOT_WIDTH), lambda b, p: (b, 0, p, 0))
        before = pl.BlockSpec((None, d, tq, SLOT_WIDTH),
                              functools.partial(lambda b, p, step: (b, 0, jnp.maximum(p * step - 1, 0), 0),
                                                step=rows // tq))
        q, k, v = qkv[g], qkv[N_DIL + g], qkv[2 * N_DIL + g]
        args += [q, k, k, v, v]
        specs += [cur, cur, before, cur, before]
    running = pltpu.VMEM((slabs, part_rows, LANES), F32)
    return pl.pallas_call(
        functools.partial(_attn_prompt_body, part_rows=part_rows),
        grid=(batch, parts),
        in_specs=specs,
        out_specs=pl.BlockSpec((slabs, part_rows, LANES), lambda b, p: (0, b * parts + p, 0)),
        out_shape=jax.ShapeDtypeStruct((slabs, batch * seq, LANES), F32),
        scratch_shapes=[running, running, running, pltpu.VMEM((2 * HEADS, tq, 2 * tq), F32)],
        compiler_params=_params("parallel", "arbitrary"),
        name="attn_prompt",
    )(*args)


def _as_column(row_vec):
    n = row_vec.shape[1]
    eye = lax.broadcasted_iota(jnp.int32, (n, n), 0) == lax.broadcasted_iota(jnp.int32, (n, n), 1)
    return jnp.sum(jnp.where(eye, row_vec, 0.0), axis=1, keepdims=True)


def _sample_window_jobs(g, q_row, k_row, v_row, c_ref, n_ref, result):
    head_row = lax.broadcasted_iota(jnp.int32, (SUBLANES, SLOT_WIDTH), 0)
    own_head = lax.broadcasted_iota(jnp.int32, (SUBLANES, SLOT_WIDTH), 1) // HEAD_DIM == head_row
    head_col = lax.broadcasted_iota(jnp.int32, (SUBLANES, 1), 0)
    w = DIL_WINDOWS[g]
    d = DIL_RATES[g]
    cols = slice(g * SLOT_WIDTH, (g + 1) * SLOT_WIDTH)
    q_g, k_new, v_new = q_row[:, cols], k_row[:, cols], v_row[:, cols]
    last = lax.broadcasted_iota(jnp.int32, (SLOT_WIDTH, w), 1) == w - 1
    shape = (HEADS, HEAD_DIM, w)
    carry = {}

    def key_side():
        kt = c_ref[0].reshape(SLOT_WIDTH, w)
        q_heads = jnp.where(own_head, q_g, 0.0)
        s = jnp.dot(q_heads.astype(BF16), kt.astype(BF16), preferred_element_type=F32)
        dist = w - lax.broadcasted_iota(jnp.int32, (1, w), 1)
        slope = functools.reduce(lambda acc, h: jnp.where(head_col == h, _slope(g, h), acc), range(HEADS), 0.0)
        s = jnp.where((dist & (d - 1)) == 0, s - slope * dist.astype(F32), -jnp.inf)
        s_new = jnp.sum(q_heads * k_new, axis=1, keepdims=True)
        m = jnp.maximum(jnp.max(s, axis=1, keepdims=True), s_new)
        p = jnp.exp(s - m)
        p_new = jnp.exp(s_new - m)
        carry.update(m=m, p=p, p_new=p_new, den=jnp.sum(p, axis=1, keepdims=True) + p_new)
        n_ref[0] = jnp.where(last, _as_column(k_new), pltpu.roll(kt, w - 1, 1)).reshape(shape)

    def value_side():
        vt = c_ref[1].reshape(SLOT_WIDTH, w)
        pv = lax.dot_general(carry["p"].astype(BF16), vt.astype(BF16), (((1,), (1,)), ((), ())),
                             preferred_element_type=F32)
        o_heads = (pv + carry["p_new"] * v_new) / carry["den"]
        n_ref[1] = jnp.where(last, _as_column(v_new), pltpu.roll(vt, w - 1, 1)).reshape(shape)
        result[g] = (jnp.sum(jnp.where(own_head, o_heads, 0.0), axis=0, keepdims=True),
                     jnp.sum(jnp.where(own_head, carry["m"] + jnp.log(carry["den"]), 0.0), axis=0, keepdims=True))

    return [key_side, value_side]


def _mix_groups(outs, lses):
    top = functools.reduce(jnp.maximum, lses)
    wts = [jnp.exp(l - top) for l in lses]
    return sum(w_g * o_g for w_g, o_g in zip(wts, outs)) / sum(wts)


def _ffn_windows_body(x_ref, nw_ref, *refs):
    w_refs = refs[:3]
    q_ref, k_ref, v_ref, c0_ref, c1_ref, c2_ref, o_ref, ob_ref, n0_ref, n1_ref, n2_ref = refs[3:]
    b = pl.program_id(0)
    rows = [r[pl.ds(b, 1), :] for r in (q_ref, k_ref, v_ref)]
    windows = ((c0_ref, n0_ref), (c1_ref, n1_ref), (c2_ref, n2_ref))
    results = {}
    jobs = [job for g in reversed(range(N_DIL)) for job in _sample_window_jobs(g, *rows, *windows[g], results)]
    _ffn_body(x_ref, nw_ref, *w_refs, o_ref, side_jobs=jobs, chunks=FF_CHUNKS_FINE)
    outs, lses = zip(*(results[g] for g in range(N_DIL)))
    ob_ref[...] = _mix_groups(outs, lses)


def _ffn_with_sample_windows(x, norm_w, weights, q, k, v, caches):
    t = x.shape[0]
    b = q.shape[0]
    assert t % b == 0 and (t // b) % SUBLANES == 0
    tm = t // b
    row = pl.BlockSpec((tm, D_MODEL), lambda i: (i, 0))
    full = _const_spec((b, ATTN_WIDTH))
    win = [pl.BlockSpec((None, 2, HEADS, HEAD_DIM, w), lambda i: (i, 0, 0, 0, 0)) for w in DIL_WINDOWS]
    outs = pl.pallas_call(
        _ffn_windows_body,
        grid=(b,),
        in_specs=[row, _const_spec((1, D_MODEL))] + _ffn_weight_specs() + [full, full, full] + win,
        out_specs=[row, pl.BlockSpec((None, 1, SLOT_WIDTH), lambda i: (i, 0, 0))] + win,
        out_shape=([jax.ShapeDtypeStruct((t, D_MODEL), F32), jax.ShapeDtypeStruct((b, 1, SLOT_WIDTH), F32)]
                   + [jax.ShapeDtypeStruct(c.shape, F32) for c in caches]),
        compiler_params=_params("parallel"),
        name="ffn_windows",
    )(x, norm_w, *weights, q, k, v, *caches)
    return outs[0], outs[1].reshape(b, SLOT_WIDTH), outs[2:]


def _mix_out_body(x_ref, y_ref, u_ref, d_ref, ob_ref,
                  ga_ref, gb_ref, wglu_ref, wpa_ref, wpb_ref, wout_ref, out_ref, *unfold):
    tm = x_ref.shape[0]
    if unfold:
        y_ref, folded_ref = unfold[0], y_ref
        groups_per_block = LANES // SSM_GROUP
    pieces = MIX_OUT_PIECES if unfold else 1
    rows = tm // pieces
    for piece in range(pieces):
        at = slice(piece * rows, (piece + 1) * rows)
        if unfold:
            chunks = rows // SSM_CHUNK
            chunk_at = slice(piece * chunks, (piece + 1) * chunks)
            for blk in range(LANE_BLOCKS):
                groups = [folded_ref[blk * (groups_per_block // 2) + q_i // 2, chunk_at,
                                     (q_i % 2) * LANES:(q_i % 2 + 1) * LANES] for q_i in range(groups_per_block)]
                for t, rows_t in enumerate(_transpose_granules(groups)):
                    y_ref[blk, pl.ds(piece * rows + t, chunks, stride=SSM_CHUNK), :] = rows_t
        y_raw = jnp.concatenate([y_ref[blk, at, :] for blk in range(LANE_BLOCKS)], axis=1)
        u = jnp.concatenate([u_ref[blk, at, :] for blk in range(LANE_BLOCKS)], axis=1)
        y = jax.nn.gelu(y_raw + d_ref[...] * u)
        yb = y.astype(BF16)
        y_a = y * jax.nn.sigmoid(jnp.dot(yb, wglu_ref[...], preferred_element_type=F32))
        branch_a = jnp.dot(y_a.astype(BF16), wpa_ref[...], preferred_element_type=F32)

        o_b = jnp.concatenate([ob_ref[s, at, :] for s in range(SLOT_WIDTH // LANES)], axis=1)
        branch_b = jnp.dot(o_b.astype(BF16), wpb_ref[...], preferred_element_type=F32)

        merged = ga_ref[at, :] * branch_a + gb_ref[at, :] * branch_b
        out_ref[at, :] = x_ref[at, :] + jnp.dot(merged.astype(BF16), wout_ref[...], preferred_element_type=F32)


def _mix_out(x, y, u, ssm_d, o_b, ga, gb, w_glu, w_pa, w_pb, w_out, tm, folded_y):
    t = x.shape[0]
    wide = pl.BlockSpec((tm, D_MODEL), lambda i: (i, 0))
    blocked = _lane_blocked_spec(tm, lambda i: i)
    y_spec = pl.BlockSpec((SSM_PAIRS, tm // SSM_CHUNK, MXU_DIM), lambda i: (0, i, 0)) if folded_y else blocked
    scratch = [pltpu.VMEM((LANE_BLOCKS, tm, LANES), F32)] if folded_y else []
    slot = pl.BlockSpec((SLOT_WIDTH // LANES, tm, LANES), lambda i: (0, i, 0))
    return pl.pallas_call(
        _mix_out_body,
        grid=(t // tm,),
        in_specs=[wide, y_spec, blocked, _const_spec((1, D_MODEL)), slot, wide, wide,
                  _const_spec((D_MODEL, D_MODEL)), _const_spec((D_MODEL, D_MODEL)),
                  _const_spec((SLOT_WIDTH, D_MODEL)), _const_spec((D_MODEL, D_MODEL))],
        out_specs=wide,
        out_shape=jax.ShapeDtypeStruct((t, D_MODEL), F32),
        scratch_shapes=scratch,
        compiler_params=_params("parallel"),
        name="mix_out",
    )(x, y, u, ssm_d, o_b, ga, gb, w_glu, w_pa, w_pb, w_out)


def _head_segment_ones():
    head = jnp.arange(MXU_DIM) // HEAD_DIM
    return (head[:, None] == head[None, :]).astype(BF16)


def kernel(x_prompt, x_sample, state_ssm_re, state_ssm_im, cache_kv_w128, cache_kv_w512, cache_kv_w2048, ffn1_norm, ffn1_w_gate, ffn1_w_up, ffn1_w_down, mix_norm, w_in, ssm_lambda_re, ssm_lambda_im, ssm_b_re, ssm_b_im, ssm_c_re, ssm_c_im, ssm_d, ssm_log_dt, w_glu, q_gain, k_gain, w_proj_a, w_proj_b, w_out, ffn2_norm, ffn2_w_gate, ffn2_w_up, ffn2_w_down):
    depth = ffn1_norm.shape[0]
    assert depth == 1, "single-layer step"
    batch, seq, _ = x_prompt.shape
    dec_batch, dec_seq, _ = x_sample.shape
    assert dec_seq == 1 and seq % (DIL_RATES[-1] * KEYS_BACK) == 0
    layer = 0
    vec = lambda w: w[layer][None]
    mix_norm, ssm_d = vec(mix_norm), vec(ssm_d)
    q_gain_t = jnp.tile(vec(q_gain), (1, N_DIL * HEADS))
    k_gain_t = jnp.tile(vec(k_gain), (1, N_DIL * HEADS))
    seg_ones = _head_segment_ones()

    abar, a8, bstep, cstep, toep, win, wout, *ffn1_w = _ssm_prep(
        ssm_lambda_re[layer], ssm_lambda_im[layer], ssm_log_dt[layer],
        ssm_b_re[layer].transpose(0, 2, 1), ssm_b_im[layer].transpose(0, 2, 1),
        ssm_c_re[layer], ssm_c_im[layer],
        convert=[w[layer] for w in (ffn1_w_gate, ffn1_w_up, ffn1_w_down)])
    ffn1 = (vec(ffn1_norm), tuple(ffn1_w))
    a8 = a8.reshape(2, SSM_PAIRS, LANES)
    abar = abar.reshape(2, SSM_PAIRS, LANES)

    def mix_in(x1, tm, seq=None):
        return _mix_in(x1, mix_norm, w_in_b, q_gain_t, k_gain_t, seg_ones, tm, seq)

    def mix_out(x1, y, u, o_b, ga, gb, tm, folded_y):
        return _mix_out(x1, y, u, ssm_d, o_b, ga, gb, w_glu_b, w_pa_b, w_pb_b, w_out_b, tm, folded_y)

    tm = 512
    tm_ffn = 1024
    xp = x_prompt.reshape(batch * seq, D_MODEL)
    later = [w[layer] for w in (ffn2_w_gate, ffn2_w_up, ffn2_w_down, w_in, w_glu, w_proj_a, w_proj_b, w_out)]
    x1, *later_b = _ffn(xp, *ffn1, tm_ffn, convert=later)
    ffn2 = (vec(ffn2_norm), tuple(later_b[:3]))
    w_in_b, w_glu_b, w_pa_b, w_pb_b, w_out_b = later_b[3:]

    xs = x_sample.reshape(dec_batch, D_MODEL)
    xs1 = _ffn(xs, *ffn1, dec_batch)
    us, qs, ks, vs, gas, gbs = mix_in(xs1, dec_batch)
    caches = [c[layer].transpose(0, 2, 3, 4, 1) for c in (cache_kv_w128, cache_kv_w512, cache_kv_w2048)]

    u, ga, gb, u_folded, *rest = mix_in(x1, tm, seq)
    p_windows, qkv = rest[:N_DIL], rest[N_DIL:]
    y, h_fin = _ssm_prompt(u_folded, toep, win, wout, a8, batch, seq, tile=2048)
    o_b = _attn_prompt(qkv, batch, seq)
    x2 = mix_out(x1, y, u, o_b, ga, gb, tm, folded_y=True)
    yp, o_bs, new_caches = _ffn_with_sample_windows(x2, *ffn2, qs, ks, vs, caches)
    yp = yp.reshape(batch, seq, D_MODEL)
    h_fin = h_fin.reshape(batch, 2, SSM_GROUPS, SSM_STATE)
    p_re = h_fin[None, :, 0]
    p_im = h_fin[None, :, 1]
    p_kv = [c.transpose(0, 4, 1, 2, 3)[None] for c in p_windows]

    width = SSM_GROUPS * SSM_STATE
    ys_ssm, s_re, s_im = _ssm_step(us, state_ssm_re[layer].reshape(dec_batch, width),
                                   state_ssm_im[layer].reshape(dec_batch, width),
                                   abar, bstep, cstep)
    o_bs = o_bs.reshape(dec_batch, SLOT_WIDTH // LANES, LANES).transpose(1, 0, 2)
    xs2 = mix_out(xs1, ys_ssm, us, o_bs, gas, gbs, dec_batch, folded_y=False)
    ys = _ffn(xs2, *ffn2, dec_batch)
    ys = ys.reshape(dec_batch, 1, D_MODEL)
    s_re = s_re.reshape(1, dec_batch, SSM_GROUPS, SSM_STATE)
    s_im = s_im.reshape(1, dec_batch, SSM_GROUPS, SSM_STATE)
    s_kv = [c.transpose(0, 4, 1, 2, 3)[None] for c in new_caches]

    return (yp, ys, p_re, p_im, p_kv[0], p_kv[1], p_kv[2], s_re, s_im, s_kv[0], s_kv[1], s_kv[2])
```

```python
import functools

import jax
import jax.numpy as jnp
from jax import lax
from jax.experimental import pallas as pl
from jax.experimental.pallas import tpu as pltpu

F32 = jnp.float32
BF16 = jnp.bfloat16

D_MODEL = 1024
SSM_GROUP = 16
SSM_GROUPS = 64
SSM_STATE = 64
SSM_PAIRS = SSM_GROUPS // 2
HEAD_DIM = 64
HEADS = 4
DIL_WINDOWS = (128, 512, 2048)
DIL_RATES = (1, 4, 16)
N_DIL = 3
KEYS_BACK = 128
ATTN_WIDTH = N_DIL * HEADS * HEAD_DIM
SLOT_WIDTH = HEADS * HEAD_DIM
D_FF = 2816
RMS_EPS = 1e-6
ALIBI_MAX_EXP = 8.0
LOG2_E = 1.4426950408889634
IN_SEGMENTS = (D_MODEL, ATTN_WIDTH, ATTN_WIDTH, ATTN_WIDTH, D_MODEL, D_MODEL)
IN_WIDTH = sum(IN_SEGMENTS)

LANES = 128
SUBLANES = 8
MXU_DIM = 256
VMEM_LIMIT_BYTES = 56 * 1024 * 1024

LANE_BLOCKS = D_MODEL // LANES

SSM_CHUNK = SUBLANES
SSM_ROW_PITCH = 40
FF_CHUNKS = ((0, 1024), (1024, 2048), (2048, 2816))
FF_CHUNKS_FINE = ((0, 512), (512, 1024), (1024, 1536), (1536, 2048), (2048, 2560), (2560, 2816))
MIX_OUT_PIECES = 2


def _slope(group, head):
    return 2.0 ** (-ALIBI_MAX_EXP * (group * HEADS + head + 1) / (N_DIL * HEADS))


def _const_spec(shape):
    zeros = (0,) * len(shape)
    return pl.BlockSpec(shape, lambda *_: zeros, pipeline_mode=pl.Buffered(1))


def _lane_blocked_spec(rows, row_block):
    return pl.BlockSpec((LANE_BLOCKS, rows, LANES), lambda *idx: (0, row_block(*idx), 0))


def _params(*semantics):
    return pltpu.CompilerParams(dimension_semantics=semantics, vmem_limit_bytes=VMEM_LIMIT_BYTES)


def _rms(x, w):
    return x * lax.rsqrt(jnp.mean(x * x, axis=-1, keepdims=True) + RMS_EPS) * w


def _split_bf16(x, terms):
    parts = []
    for _ in range(terms):
        p = x.astype(BF16)
        parts.append(p)
        x = x - p.astype(F32)
    return parts


def _ffn_body(x_ref, nw_ref, wg_ref, wu_ref, wd_ref, *rest, side_jobs=(), chunks=FF_CHUNKS):
    n_cast = (len(rest) - 1) // 2
    o_ref = rest[n_cast]
    for src_ref, dst_ref in zip(rest[:n_cast], rest[n_cast + 1:]):
        dst_ref[...] = src_ref[...].astype(BF16)
    x = x_ref[...]
    h = _rms(x, nw_ref[...]).astype(BF16)
    acc = jnp.zeros_like(x)
    side_jobs = list(side_jobs)
    for lo, hi in chunks:
        g = jnp.dot(h, wg_ref[:, lo:hi], preferred_element_type=F32)
        u = jnp.dot(h, wu_ref[:, lo:hi], preferred_element_type=F32)
        a = (jax.nn.silu(g) * u).astype(BF16)
        acc = acc + jnp.dot(a, wd_ref[lo:hi, :], preferred_element_type=F32)
        if side_jobs:
            side_jobs.pop(0)()
    for job in side_jobs:
        job()
    o_ref[...] = x + 0.5 * acc


def _ffn_weight_specs():
    return [_const_spec((D_MODEL, D_FF)), _const_spec((D_MODEL, D_FF)), _const_spec((D_FF, D_MODEL))]


def _row_block_specs(matrices, steps):
    specs = []
    for w in matrices:
        rows = w.shape[0] // steps
        assert rows * steps == w.shape[0] and rows % (2 * SUBLANES) == 0
        specs.append(pl.BlockSpec((rows, w.shape[1]), lambda i: (i, 0)))
    return specs


def _ffn(x, norm_w, weights, tm, convert=()):
    t = x.shape[0]
    steps = t // tm
    row = pl.BlockSpec((tm, D_MODEL), lambda i: (i, 0))
    side = _row_block_specs(convert, steps)
    outs = pl.pallas_call(
        _ffn_body,
        grid=(steps,),
        in_specs=[row, _const_spec((1, D_MODEL))] + _ffn_weight_specs() + side,
        out_specs=[row] + side,
        out_shape=[jax.ShapeDtypeStruct((t, D_MODEL), F32)] + [jax.ShapeDtypeStruct(w.shape, BF16) for w in convert],
        compiler_params=_params("parallel"),
        name="ffn",
    )(x, norm_w, *weights, *convert)
    return outs if convert else outs[0]


def _head_norm(x, gain, seg_ones):
    parts = _split_bf16(x * x, 2)
    blocks = []
    for lo in range(0, x.shape[1], MXU_DIM):
        blocks.append(sum(jnp.dot(p[:, lo:lo + MXU_DIM], seg_ones, preferred_element_type=F32) for p in parts))
    ss = jnp.concatenate(blocks, axis=1)
    return x * lax.rsqrt(ss * (1.0 / HEAD_DIM) + RMS_EPS) * gain


def _transpose_granules(xs):
    xs = list(xs)
    n = len(xs)
    block = lax.broadcasted_iota(jnp.int32, (1, LANES), 1) // SSM_GROUP
    bit = n // 2
    while bit:
        upper = (block & bit) != 0
        shift = SSM_GROUP * bit
        for lo in range(n):
            if lo & bit:
                continue
            hi = lo + bit
            x_lo, x_hi = xs[lo], xs[hi]
            xs[lo] = jnp.where(upper, pltpu.roll(x_hi, shift, 1), x_lo)
            xs[hi] = jnp.where(upper, x_hi, pltpu.roll(x_lo, LANES - shift, 1))
        bit //= 2
    return xs


def _mix_in_body(x_ref, nw_ref, w_ref, qg_ref, kg_ref, seg_ref, u_ref, *rest, prompt):
    h = _rms(x_ref[...], nw_ref[...]).astype(BF16)

    edges = [0]
    for width in IN_SEGMENTS:
        edges.append(edges[-1] + width)

    def proj(i):
        return jnp.dot(h, w_ref[:, edges[i]:edges[i + 1]], preferred_element_type=F32)

    u = proj(0)
    for blk in range(LANE_BLOCKS):
        u_ref[blk] = u[:, blk * LANES:(blk + 1) * LANES]
    seg_ones = seg_ref[...]
    q = _head_norm(proj(1), qg_ref[...], seg_ones) * (HEAD_DIM ** -0.5)
    k = _head_norm(proj(2), kg_ref[...], seg_ones)
    v = proj(3)
    if not prompt:
        q_ref, k_ref, v_ref, ga_ref, gb_ref = rest
        q_ref[...] = q
        k_ref[...] = k
        v_ref[...] = v
    else:
        ga_ref, gb_ref, fold_ref = rest[:3]
        windows = rest[3:3 + N_DIL]
        by_residue = rest[3 + N_DIL:3 + N_DIL + 3 * N_DIL]
        stage_ref = rest[-1]
    ga_ref[...] = jax.nn.sigmoid(proj(4))
    gb_ref[...] = jax.nn.sigmoid(proj(5))
    if not prompt:
        return

    rows = x_ref.shape[0]
    groups_per_block = LANES // SSM_GROUP
    for blk in range(LANE_BLOCKS):
        tokens = [u_ref[blk, pl.ds(s, rows // SSM_CHUNK, stride=SSM_CHUNK), :] for s in range(SSM_CHUNK)]
        for q_i, folded in enumerate(_transpose_granules(tokens)):
            pair = blk * (groups_per_block // 2) + q_i // 2
            fold_ref[pair, :, (q_i % 2) * LANES:(q_i % 2 + 1) * LANES] = folded.astype(BF16)

    slabs = SLOT_WIDTH // LANES
    for g, win_ref in enumerate(windows):
        keep = win_ref.shape[-1]
        for t, x in enumerate((k, v)):
            for s in range(slabs):
                lo = g * SLOT_WIDTH + s * LANES
                xt = x[:, lo:lo + LANES].T[:, rows - keep:]
                win_ref[t, 2 * s] = xt[:HEAD_DIM]
                win_ref[t, 2 * s + 1] = xt[HEAD_DIM:]

    for ti, x in enumerate((q * LOG2_E, k, v)):
        outs = by_residue[ti * N_DIL:(ti + 1) * N_DIL]
        outs[0][0] = x[:, :SLOT_WIDTH].astype(BF16)
        for g in range(1, N_DIL):
            d = DIL_RATES[g]
            for s in range(slabs):
                lo = g * SLOT_WIDTH + s * LANES
                stage_ref[ti, (g - 1) * slabs + s] = x[:, lo:lo + LANES]
            for r in range(d):
                for s in range(slabs):
                    piece = stage_ref[ti, (g - 1) * slabs + s, pl.ds(r, rows // d, stride=d), :]
                    outs[g][r, :, s * LANES:(s + 1) * LANES] = piece.astype(BF16)


def _mix_in(x, norm_w, w_in, q_gain, k_gain, seg_ones, tm, seq=None):
    t = x.shape[0]
    prompt = seq is not None

    def row(width):
        return pl.BlockSpec((tm, width), lambda i: (i, 0))

    def rows_f32(width):
        return jax.ShapeDtypeStruct((t, width), F32)

    out_specs = [_lane_blocked_spec(tm, lambda i: i)]
    out_shape = [jax.ShapeDtypeStruct((LANE_BLOCKS, t, LANES), F32)]
    scratch = []
    if not prompt:
        out_specs += [row(ATTN_WIDTH)] * 3 + [row(D_MODEL)] * 2
        out_shape += [rows_f32(ATTN_WIDTH)] * 3 + [rows_f32(D_MODEL)] * 2
    else:
        tiles = seq // tm
        out_specs += [row(D_MODEL)] * 2
        out_shape += [rows_f32(D_MODEL)] * 2
        out_specs.append(pl.BlockSpec((SSM_PAIRS, tm // SSM_CHUNK, MXU_DIM), lambda i: (0, i, 0)))
        out_shape.append(jax.ShapeDtypeStruct((SSM_PAIRS, t // SSM_CHUNK, MXU_DIM), BF16))
        for w in DIL_WINDOWS:
            keep = min(w, seq)
            cols = min(keep, tm)
            assert keep % cols == 0 and tm % cols == 0
            skip = tiles - keep // cols
            out_specs.append(pl.BlockSpec(
                (None, 2, HEADS, HEAD_DIM, cols),
                functools.partial(lambda i, skip: (i // tiles, 0, 0, 0, jnp.maximum(i % tiles - skip, 0)), skip=skip)))
            out_shape.append(jax.ShapeDtypeStruct((t // seq, 2, HEADS, HEAD_DIM, keep), F32))
        for _ in range(3):
            for d in DIL_RATES:
                out_specs.append(pl.BlockSpec((None, d, tm // d, SLOT_WIDTH),
                                              lambda i: (i // tiles, 0, i % tiles, 0)))
                out_shape.append(jax.ShapeDtypeStruct((t // seq, d, seq // d, SLOT_WIDTH), BF16))
        scratch = [pltpu.VMEM((3, (N_DIL - 1) * SLOT_WIDTH // LANES, tm, LANES), F32)]
    return pl.pallas_call(
        functools.partial(_mix_in_body, prompt=prompt),
        grid=(t // tm,),
        in_specs=[row(D_MODEL), _const_spec((1, D_MODEL)), _const_spec((D_MODEL, IN_WIDTH)),
                  _const_spec((1, ATTN_WIDTH)), _const_spec((1, ATTN_WIDTH)), _const_spec((MXU_DIM, MXU_DIM))],
        out_specs=out_specs,
        out_shape=out_shape,
        scratch_shapes=scratch,
        compiler_params=_params("arbitrary"),
        name="mix_in",
    )(x, norm_w, w_in, q_gain, k_gain, seg_ones)


def _cmul(ar, ai, br, bi):
    return ar * br - ai * bi, ar * bi + ai * br


def _ssm_prep_body(lrow_re_ref, lrow_im_ref, ldt_ref, bre_ref, bim_ref, ct_re_ref, ct_im_ref, *rest):
    n_cast = (len(rest) - 7) // 2
    abar_ref, a8_ref, bstep_ref, cstep_ref, toep_ref, win_ref, wout_ref = rest[n_cast:n_cast + 7]
    for src_ref, dst_ref in zip(rest[:n_cast], rest[n_cast + 7:]):
        dst_ref[...] = src_ref[...].astype(BF16)
    n = SSM_CHUNK
    dt = jnp.exp(ldt_ref[...])

    def discretise(lam_re, lam_im):
        lr = jnp.minimum(lam_re, -1e-4)
        mag = jnp.exp(lr * dt)
        return lr, lam_im, mag * jnp.cos(lam_im * dt), mag * jnp.sin(lam_im * dt)

    def powers(ar, ai):
        out = [(jnp.ones_like(ar), jnp.zeros_like(ai))]
        for _ in range(n):
            out.append(_cmul(*out[-1], ar, ai))
        return out

    lr, li, ar, ai = discretise(lrow_re_ref[...], lrow_im_ref[...])
    den = lr * lr + li * li
    fr = ((ar - 1.0) * lr + ai * li) / den
    fi = (ai * lr - (ar - 1.0) * li) / den
    bbr, bbi = _cmul(fr, fi, bre_ref[...], bim_ref[...])
    row_pow = powers(ar, ai)
    abar_ref[0], abar_ref[1] = ar, ai
    a8_ref[0], a8_ref[1] = row_pow[n]

    def pair_halves(x):
        x = x.reshape((x.shape[0] // 2, 2) + x.shape[1:])
        return x[:, 0], x[:, 1]

    bstep_ref[...] = jnp.zeros_like(bstep_ref)
    cstep_ref[...] = jnp.zeros_like(cstep_ref)
    for part, x in enumerate((bbr, bbi)):
        x_e, x_o = pair_halves(x)
        bstep_ref[:, :SSM_GROUP, (2 * part) * SSM_STATE:(2 * part + 1) * SSM_STATE] = x_e
        bstep_ref[:, SSM_GROUP:, (2 * part + 1) * SSM_STATE:(2 * part + 2) * SSM_STATE] = x_o
    for part, ct_ref in enumerate((ct_re_ref, ct_im_ref)):
        x_e, x_o = pair_halves(ct_ref[:, :, :SSM_GROUP])
        cstep_ref[part, :, :SSM_STATE, :SSM_GROUP] = x_e
        cstep_ref[part, :, SSM_STATE:, SSM_GROUP:] = x_o

    toep_ref[...] = jnp.zeros_like(toep_ref)
    win_ref[...] = jnp.zeros_like(win_ref)
    wout_ref[...] = jnp.zeros_like(wout_ref)
    half = n * SSM_GROUP

    for s in range(n):
        w_re, w_im = _cmul(*row_pow[n - 1 - s], bbr, bbi)
        rows_e = slice(s * SSM_GROUP, (s + 1) * SSM_GROUP)
        rows_o = slice(half + s * SSM_GROUP, half + (s + 1) * SSM_GROUP)
        for part, x in enumerate((w_re, w_im)):
            x_e, x_o = pair_halves(x.astype(BF16))
            win_ref[:, rows_e, (2 * part) * SSM_STATE:(2 * part + 1) * SSM_STATE] = x_e
            win_ref[:, rows_o, (2 * part + 1) * SSM_STATE:(2 * part + 2) * SSM_STATE] = x_o

    eye = (lax.broadcasted_iota(jnp.int32, (1, SSM_STATE, SSM_STATE), 1)
           == lax.broadcasted_iota(jnp.int32, (1, SSM_STATE, SSM_STATE), 2))

    def on_sublanes(x):
        return jnp.sum(jnp.where(eye, x, 0.0), axis=2, keepdims=True)

    col_pow = powers(on_sublanes(ar), on_sublanes(ai))
    lane_t = lax.broadcasted_iota(jnp.int32, (1, 1, LANES), 2) // SSM_GROUP

    def spread(first):
        re = im = jnp.zeros((1, 1, LANES), F32)
        for t in range(n):
            re = jnp.where(lane_t == t, col_pow[first + t][0], re)
            im = jnp.where(lane_t == t, col_pow[first + t][1], im)
        return re, im

    ct = (ct_re_ref[...], ct_im_ref[...])
    m0_re, m0_im = _cmul(*ct, *spread(0))
    m1_re, m1_im = _cmul(*ct, *spread(1))

    for part, x in enumerate((m1_re, -m1_im)):
        x_e, x_o = pair_halves(x.astype(BF16))
        wout_ref[:, (2 * part) * SSM_STATE:(2 * part + 1) * SSM_STATE, :half] = x_e
        wout_ref[:, (2 * part + 1) * SSM_STATE:(2 * part + 2) * SSM_STATE, half:] = x_o

    nn = (((2,), (1,)), ((0,), (0,)))
    hp = lax.Precision.HIGHEST
    kern = (lax.dot_general(bbr, m0_re, nn, precision=hp, preferred_element_type=F32)
            - lax.dot_general(bbi, m0_im, nn, precision=hp, preferred_element_type=F32))
    lane = lax.broadcasted_iota(jnp.int32, (1, 1, LANES), 2)
    for s in range(n):
        shifted = kern if s == 0 else jnp.where(lane >= s * SSM_GROUP, pltpu.roll(kern, s * SSM_GROUP, 2), 0.0)
        x_e, x_o = pair_halves(shifted.astype(BF16))
        toep_ref[:, s * SSM_GROUP:(s + 1) * SSM_GROUP, :half] = x_e
        toep_ref[:, half + s * SSM_GROUP:half + (s + 1) * SSM_GROUP, half:] = x_o


def _ssm_prep(lam_re, lam_im, log_dt, b_re_t, b_im_t, c_re, c_im, convert=(), groups_per_step=16):
    g, p, c = SSM_GROUPS, SSM_STATE, SSM_GROUP
    gb = groups_per_step
    side = _row_block_specs(convert, g // gb)

    def spec(*tail):
        return pl.BlockSpec((gb,) + tail, lambda i: (i,) + (0,) * len(tail))

    def stacked(*tail):
        return pl.BlockSpec((2, gb) + tail, lambda i: (0, i) + (0,) * len(tail))

    pair_spec = pl.BlockSpec((gb // 2, MXU_DIM, MXU_DIM), lambda i: (i, 0, 0))
    pair_shape = jax.ShapeDtypeStruct((SSM_PAIRS, MXU_DIM, MXU_DIM), BF16)
    ct_re = jnp.tile(c_re.transpose(0, 2, 1), (1, 1, SSM_CHUNK))
    ct_im = jnp.tile(c_im.transpose(0, 2, 1), (1, 1, SSM_CHUNK))
    return pl.pallas_call(
        _ssm_prep_body,
        grid=(g // gb,),
        in_specs=[spec(1, p), spec(1, p), spec(1, 1), spec(c, p), spec(c, p),
                  spec(p, LANES), spec(p, LANES)] + side,
        out_specs=[stacked(1, p), stacked(1, p),
                   pl.BlockSpec((gb // 2, 2 * c, 4 * p), lambda i: (i, 0, 0)),
                   pl.BlockSpec((2, gb // 2, 2 * p, 2 * c), lambda i: (0, i, 0, 0)),
                   pair_spec, pair_spec, pair_spec] + side,
        out_shape=[jax.ShapeDtypeStruct((2, g, 1, p), F32), jax.ShapeDtypeStruct((2, g, 1, p), F32),
                   jax.ShapeDtypeStruct((SSM_PAIRS, 2 * c, 4 * p), F32),
                   jax.ShapeDtypeStruct((2, SSM_PAIRS, 2 * p, 2 * c), F32),
                   pair_shape, pair_shape, pair_shape] + [jax.ShapeDtypeStruct(w.shape, BF16) for w in convert],
        compiler_params=_params("parallel"),
        name="ssm_prep",
    )(lam_re.reshape(g, 1, p), lam_im.reshape(g, 1, p), log_dt.reshape(g, 1, 1), b_re_t, b_im_t, ct_re, ct_im,
      *convert)


def _ssm_body(lhs_ref, toep_ref, win_ref, wout_ref, a8_ref, yfl_ref, hfin_ref, st_ref, carry_ref, *, rows):
    i = pl.program_id(1)

    @pl.when(i == 0)
    def _():
        carry_ref[...] = jnp.zeros_like(carry_ref)

    for r in range(SSM_PAIRS):
        b = jnp.dot(lhs_ref[r], win_ref[r], preferred_element_type=F32)
        st_ref[0, pl.ds(r, rows, stride=SSM_ROW_PITCH), :] = b[:, :LANES]
        st_ref[1, pl.ds(r, rows, stride=SSM_ROW_PITCH), :] = b[:, LANES:]

    a_re = a8_ref[0]
    a_im = a8_ref[1]

    def step(j, h):
        h_re, h_im = h
        base = pl.multiple_of(j * SSM_ROW_PITCH, SUBLANES)
        n_re = a_re * h_re - a_im * h_im + st_ref[0, pl.ds(base, SSM_PAIRS), :]
        n_im = a_re * h_im + a_im * h_re + st_ref[1, pl.ds(base, SSM_PAIRS), :]
        st_ref[0, pl.ds(base, SSM_PAIRS), :] = h_re
        st_ref[1, pl.ds(base, SSM_PAIRS), :] = h_im
        return n_re, n_im

    h_re, h_im = lax.fori_loop(0, rows, step, (carry_ref[0], carry_ref[1]), unroll=4)
    carry_ref[0] = h_re
    carry_ref[1] = h_im
    hfin_ref[0] = h_re
    hfin_ref[1] = h_im

    for r in range(SSM_PAIRS):
        hcat = jnp.concatenate([st_ref[0, pl.ds(r, rows, stride=SSM_ROW_PITCH), :],
                                st_ref[1, pl.ds(r, rows, stride=SSM_ROW_PITCH), :]], axis=1).astype(BF16)
        yfl_ref[r] = (jnp.dot(lhs_ref[r], toep_ref[r], preferred_element_type=F32)
                      + jnp.dot(hcat, wout_ref[r], preferred_element_type=F32))


def _ssm_prompt(lhs, toep, win, wout, a8, batch, seq, tile):
    rows = tile // SSM_CHUNK
    n_tiles = seq // tile
    tok = pl.BlockSpec((SSM_PAIRS, rows, MXU_DIM), lambda b, i: (0, b * n_tiles + i, 0))
    pair_w = _const_spec((SSM_PAIRS, MXU_DIM, MXU_DIM))
    return pl.pallas_call(
        functools.partial(_ssm_body, rows=rows),
        grid=(batch, n_tiles),
        in_specs=[tok, pair_w, pair_w, pair_w, _const_spec((2, SSM_PAIRS, LANES))],
        out_specs=[tok, pl.BlockSpec((None, 2, SSM_PAIRS, LANES), lambda b, i: (b, 0, 0, 0))],
        out_shape=[jax.ShapeDtypeStruct((SSM_PAIRS, batch * seq // SSM_CHUNK, MXU_DIM), F32),
                   jax.ShapeDtypeStruct((batch, 2, SSM_PAIRS, LANES), F32)],
        scratch_shapes=[pltpu.VMEM((2, rows * SSM_ROW_PITCH, LANES), F32),
                        pltpu.VMEM((2, SSM_PAIRS, LANES), F32)],
        compiler_params=_params("parallel", "arbitrary"),
        name="ssm_prompt",
    )(lhs, toep, win, wout, a8)


def _ssm_step_body(u_ref, hre_ref, him_ref, abar_ref, bstep_ref, cstep_ref, y_ref, ore_ref, oim_ref):
    hp = lax.Precision.HIGHEST
    pair_ch = 2 * SSM_GROUP
    pairs_per_block = LANES // pair_ch
    for r in range(SSM_PAIRS):
        blk = r // pairs_per_block
        ch = slice((r % pairs_per_block) * pair_ch, (r % pairs_per_block + 1) * pair_ch)
        st = slice(r * LANES, (r + 1) * LANES)
        bu = jnp.dot(u_ref[blk, :, ch], bstep_ref[r], precision=hp, preferred_element_type=F32)
        a_re = abar_ref[0, r:r + 1, :]
        a_im = abar_ref[1, r:r + 1, :]
        h_re = hre_ref[:, st]
        h_im = him_ref[:, st]
        n_re = a_re * h_re - a_im * h_im + bu[:, :LANES]
        n_im = a_re * h_im + a_im * h_re + bu[:, LANES:]
        ore_ref[:, st] = n_re
        oim_ref[:, st] = n_im
        y_ref[blk, :, ch] = (jnp.dot(n_re, cstep_ref[0, r], precision=hp, preferred_element_type=F32)
                             - jnp.dot(n_im, cstep_ref[1, r], precision=hp, preferred_element_type=F32))


def _ssm_step(u, h_re, h_im, abar, bstep, cstep):
    b = u.shape[1]
    width = SSM_GROUPS * SSM_STATE
    return pl.pallas_call(
        _ssm_step_body,
        out_shape=[jax.ShapeDtypeStruct((LANE_BLOCKS, b, LANES), F32),
                   jax.ShapeDtypeStruct((b, width), F32),
                   jax.ShapeDtypeStruct((b, width), F32)],
        compiler_params=pltpu.CompilerParams(vmem_limit_bytes=VMEM_LIMIT_BYTES),
        name="ssm_step",
    )(u, h_re, h_im, abar, bstep, cstep)


def _attn_prompt_body(*refs, part_rows):
    per_group = 5
    ins = refs[:per_group * N_DIL]
    o_ref, m_ref, l_ref, acc_ref, bias_ref = refs[per_group * N_DIL:]
    part = pl.program_id(1)
    tq = KEYS_BACK
    row = lax.broadcasted_iota(jnp.int32, (tq, 2 * tq), 0)
    col = lax.broadcasted_iota(jnp.int32, (tq, 2 * tq), 1)
    back = row + tq - col
    in_window = (back >= 0) & (back <= KEYS_BACK)
    first_head = lax.broadcasted_iota(jnp.int32, (1, LANES), 1) < HEAD_DIM
    nt = (((1,), (1,)), ((), ()))

    order = tuple(reversed(range(N_DIL)))
    ones = jnp.ones((2 * tq, LANES), BF16)
    for g in order:
        q_ref, k_ref, kb_ref, v_ref, vb_ref = ins[per_group * g:per_group * (g + 1)]
        d = DIL_RATES[g]
        blocks_per_residue = part_rows // d // tq
        dist = (back * d).astype(F32)
        for h in range(HEADS):
            bias = jnp.where(in_window, -(_slope(g, h) * LOG2_E) * dist, -jnp.inf)
            bias_ref[h] = bias
            bias_ref[HEADS + h] = jnp.where(col >= tq, bias, -jnp.inf)

        for mi in range(part_rows // tq):
            residue = mi // blocks_per_residue
            n = mi % blocks_per_residue
            cur = n * tq
            prev = max(n - 1, 0) * tq
            bias_at = jnp.where(part == 0, HEADS, 0) if n == 0 else 0
            token0 = residue + d * tq * n
            rows = pl.ds(token0, tq) if d == 1 else pl.ds(token0, tq, stride=d)
            pairs = range(HEADS // 2)
            first, final = g == order[0], g == order[-1]
            old = None if first else [(m_ref[pair, rows, :], l_ref[pair, rows, :], acc_ref[pair, rows, :])
                                      for pair in pairs]
            new = []
            for pair in pairs:
                lanes = slice(pair * LANES, (pair + 1) * LANES)
                qp = q_ref[residue, pl.ds(cur, tq), lanes]
                k_prev = kb_ref[residue, :, lanes] if n == 0 else k_ref[residue, pl.ds(prev, tq), lanes]
                v_prev = vb_ref[residue, :, lanes] if n == 0 else v_ref[residue, pl.ds(prev, tq), lanes]
                kp = jnp.concatenate([k_prev, k_ref[residue, pl.ds(cur, tq), lanes]], axis=0)
                vp = jnp.concatenate([v_prev, v_ref[residue, pl.ds(cur, tq), lanes]], axis=0)
                vp = jnp.concatenate([vp, ones], axis=1)
                stats = []
                for e in range(2):
                    qm = jnp.where(first_head if e == 0 else ~first_head, qp, jnp.zeros_like(qp))
                    s = lax.dot_general(qm, kp, nt, preferred_element_type=F32)
                    s = s + bias_ref[bias_at + 2 * pair + e]
                    m = jnp.max(s, axis=-1, keepdims=True)
                    p = jnp.exp2(s - m)
                    pv = jnp.dot(p.astype(BF16), vp, preferred_element_type=F32)
                    stats.append((m, pv[:, LANES:], pv[:, :LANES]))
                m_new, l_new, acc_new = (jnp.where(first_head, a, b) for a, b in zip(*stats))
                if not first:
                    m_old, l_old, acc_old = old[pair]
                    m_tot = jnp.maximum(m_old, m_new)
                    w_old = jnp.exp2(m_old - m_tot)
                    w_new = jnp.exp2(m_new - m_tot)
                    l_new = w_old * l_old + w_new * l_new
                    acc_new = w_old * acc_old + w_new * acc_new
                    m_new = m_tot
                new.append((m_new, l_new, acc_new))
            for pair, (m_new, l_new, acc_new) in zip(pairs, new):
                if not final:
                    m_ref[pair, rows, :] = m_new
                    l_ref[pair, rows, :] = l_new
                    acc_ref[pair, rows, :] = acc_new
                else:
                    o_ref[pair, rows, :] = acc_new / l_new


def _attn_prompt(qkv, batch, seq):
    slabs = SLOT_WIDTH // LANES
    tq = KEYS_BACK
    part_rows = DIL_RATES[-1] * tq
    parts = seq // part_rows
    args, specs = [], []
    for g, d in enumerate(DIL_RATES):
        rows = part_rows // d
        cur = pl.BlockSpec((None, d, rows, SLOT_WIDTH), lambda b, p: (b, 0, p, 0))
        before = pl.BlockSpec((None, d, tq, SLOT_WIDTH),
                              functools.partial(lambda b, p, step: (b, 0, jnp.maximum(p * step - 1, 0), 0),
                                                step=rows // tq))
        q, k, v = qkv[g], qkv[N_DIL + g], qkv[2 * N_DIL + g]
        args += [q, k, k, v, v]
        specs += [cur, cur, before, cur, before]
    running = pltpu.VMEM((slabs, part_rows, LANES), F32)
    return pl.pallas_call(
        functools.partial(_attn_prompt_body, part_rows=part_rows),
        grid=(batch, parts),
        in_specs=specs,
        out_specs=pl.BlockSpec((slabs, part_rows, LANES), lambda b, p: (0, b * parts + p, 0)),
        out_shape=jax.ShapeDtypeStruct((slabs, batch * seq, LANES), F32),
        scratch_shapes=[running, running, running, pltpu.VMEM((2 * HEADS, tq, 2 * tq), F32)],
        compiler_params=_params("parallel", "arbitrary"),
        name="attn_prompt",
    )(*args)


def _as_column(row_vec):
    n = row_vec.shape[1]
    eye = lax.broadcasted_iota(jnp.int32, (n, n), 0) == lax.broadcasted_iota(jnp.int32, (n, n), 1)
    return jnp.sum(jnp.where(eye, row_vec, 0.0), axis=1, keepdims=True)


def _sample_window_jobs(g, q_row, k_row, v_row, c_ref, n_ref, result):
    head_row = lax.broadcasted_iota(jnp.int32, (SUBLANES, SLOT_WIDTH), 0)
    own_head = lax.broadcasted_iota(jnp.int32, (SUBLANES, SLOT_WIDTH), 1) // HEAD_DIM == head_row
    head_col = lax.broadcasted_iota(jnp.int32, (SUBLANES, 1), 0)
    w = DIL_WINDOWS[g]
    d = DIL_RATES[g]
    cols = slice(g * SLOT_WIDTH, (g + 1) * SLOT_WIDTH)
    q_g, k_new, v_new = q_row[:, cols], k_row[:, cols], v_row[:, cols]
    last = lax.broadcasted_iota(jnp.int32, (SLOT_WIDTH, w), 1) == w - 1
    shape = (HEADS, HEAD_DIM, w)
    carry = {}

    def key_side():
        kt = c_ref[0].reshape(SLOT_WIDTH, w)
        q_heads = jnp.where(own_head, q_g, 0.0)
        s = jnp.dot(q_heads.astype(BF16), kt.astype(BF16), preferred_element_type=F32)
        dist = w - lax.broadcasted_iota(jnp.int32, (1, w), 1)
        slope = functools.reduce(lambda acc, h: jnp.where(head_col == h, _slope(g, h), acc), range(HEADS), 0.0)
        s = jnp.where((dist & (d - 1)) == 0, s - slope * dist.astype(F32), -jnp.inf)
        s_new = jnp.sum(q_heads * k_new, axis=1, keepdims=True)
        m = jnp.maximum(jnp.max(s, axis=1, keepdims=True), s_new)
        p = jnp.exp(s - m)
        p_new = jnp.exp(s_new - m)
        carry.update(m=m, p=p, p_new=p_new, den=jnp.sum(p, axis=1, keepdims=True) + p_new)
        n_ref[0] = jnp.where(last, _as_column(k_new), pltpu.roll(kt, w - 1, 1)).reshape(shape)

    def value_side():
        vt = c_ref[1].reshape(SLOT_WIDTH, w)
        pv = lax.dot_general(carry["p"].astype(BF16), vt.astype(BF16), (((1,), (1,)), ((), ())),
                             preferred_element_type=F32)
        o_heads = (pv + carry["p_new"] * v_new) / carry["den"]
        n_ref[1] = jnp.where(last, _as_column(v_new), pltpu.roll(vt, w - 1, 1)).reshape(shape)
        result[g] = (jnp.sum(jnp.where(own_head, o_heads, 0.0), axis=0, keepdims=True),
                     jnp.sum(jnp.where(own_head, carry["m"] + jnp.log(carry["den"]), 0.0), axis=0, keepdims=True))

    return [key_side, value_side]


def _mix_groups(outs, lses):
    top = functools.reduce(jnp.maximum, lses)
    wts = [jnp.exp(l - top) for l in lses]
    return sum(w_g * o_g for w_g, o_g in zip(wts, outs)) / sum(wts)


def _ffn_windows_body(x_ref, nw_ref, *refs):
    w_refs = refs[:3]
    q_ref, k_ref, v_ref, c0_ref, c1_ref, c2_ref, o_ref, ob_ref, n0_ref, n1_ref, n2_ref = refs[3:]
    b = pl.program_id(0)
    rows = [r[pl.ds(b, 1), :] for r in (q_ref, k_ref, v_ref)]
    windows = ((c0_ref, n0_ref), (c1_ref, n1_ref), (c2_ref, n2_ref))
    results = {}
    jobs = [job for g in reversed(range(N_DIL)) for job in _sample_window_jobs(g, *rows, *windows[g], results)]
    _ffn_body(x_ref, nw_ref, *w_refs, o_ref, side_jobs=jobs, chunks=FF_CHUNKS_FINE)
    outs, lses = zip(*(results[g] for g in range(N_DIL)))
    ob_ref[...] = _mix_groups(outs, lses)


def _ffn_with_sample_windows(x, norm_w, weights, q, k, v, caches):
    t = x.shape[0]
    b = q.shape[0]
    assert t % b == 0 and (t // b) % SUBLANES == 0
    tm = t // b
    row = pl.BlockSpec((tm, D_MODEL), lambda i: (i, 0))
    full = _const_spec((b, ATTN_WIDTH))
    win = [pl.BlockSpec((None, 2, HEADS, HEAD_DIM, w), lambda i: (i, 0, 0, 0, 0)) for w in DIL_WINDOWS]
    outs = pl.pallas_call(
        _ffn_windows_body,
        grid=(b,),
        in_specs=[row, _const_spec((1, D_MODEL))] + _ffn_weight_specs() + [full, full, full] + win,
        out_specs=[row, pl.BlockSpec((None, 1, SLOT_WIDTH), lambda i: (i, 0, 0))] + win,
        out_shape=([jax.ShapeDtypeStruct((t, D_MODEL), F32), jax.ShapeDtypeStruct((b, 1, SLOT_WIDTH), F32)]
                   + [jax.ShapeDtypeStruct(c.shape, F32) for c in caches]),
        compiler_params=_params("parallel"),
        name="ffn_windows",
    )(x, norm_w, *weights, q, k, v, *caches)
    return outs[0], outs[1].reshape(b, SLOT_WIDTH), outs[2:]


def _mix_out_body(x_ref, y_ref, u_ref, d_ref, ob_ref,
                  ga_ref, gb_ref, wglu_ref, wpa_ref, wpb_ref, wout_ref, out_ref, *unfold):
    tm = x_ref.shape[0]
    if unfold:
        y_ref, folded_ref = unfold[0], y_ref
        groups_per_block = LANES // SSM_GROUP
    pieces = MIX_OUT_PIECES if unfold else 1
    rows = tm // pieces
    for piece in range(pieces):
        at = slice(piece * rows, (piece + 1) * rows)
        if unfold:
            chunks = rows // SSM_CHUNK
            chunk_at = slice(piece * chunks, (piece + 1) * chunks)
            for blk in range(LANE_BLOCKS):
                groups = [folded_ref[blk * (groups_per_block // 2) + q_i // 2, chunk_at,
                                     (q_i % 2) * LANES:(q_i % 2 + 1) * LANES] for q_i in range(groups_per_block)]
                for t, rows_t in enumerate(_transpose_granules(groups)):
                    y_ref[blk, pl.ds(piece * rows + t, chunks, stride=SSM_CHUNK), :] = rows_t
        y_raw = jnp.concatenate([y_ref[blk, at, :] for blk in range(LANE_BLOCKS)], axis=1)
        u = jnp.concatenate([u_ref[blk, at, :] for blk in range(LANE_BLOCKS)], axis=1)
        y = jax.nn.gelu(y_raw + d_ref[...] * u)
        yb = y.astype(BF16)
        y_a = y * jax.nn.sigmoid(jnp.dot(yb, wglu_ref[...], preferred_element_type=F32))
        branch_a = jnp.dot(y_a.astype(BF16), wpa_ref[...], preferred_element_type=F32)

        o_b = jnp.concatenate([ob_ref[s, at, :] for s in range(SLOT_WIDTH // LANES)], axis=1)
        branch_b = jnp.dot(o_b.astype(BF16), wpb_ref[...], preferred_element_type=F32)

        merged = ga_ref[at, :] * branch_a + gb_ref[at, :] * branch_b
        out_ref[at, :] = x_ref[at, :] + jnp.dot(merged.astype(BF16), wout_ref[...], preferred_element_type=F32)


def _mix_out(x, y, u, ssm_d, o_b, ga, gb, w_glu, w_pa, w_pb, w_out, tm, folded_y):
    t = x.shape[0]
    wide = pl.BlockSpec((tm, D_MODEL), lambda i: (i, 0))
    blocked = _lane_blocked_spec(tm, lambda i: i)
    y_spec = pl.BlockSpec((SSM_PAIRS, tm // SSM_CHUNK, MXU_DIM), lambda i: (0, i, 0)) if folded_y else blocked
    scratch = [pltpu.VMEM((LANE_BLOCKS, tm, LANES), F32)] if folded_y else []
    slot = pl.BlockSpec((SLOT_WIDTH // LANES, tm, LANES), lambda i: (0, i, 0))
    return pl.pallas_call(
        _mix_out_body,
        grid=(t // tm,),
        in_specs=[wide, y_spec, blocked, _const_spec((1, D_MODEL)), slot, wide, wide,
                  _const_spec((D_MODEL, D_MODEL)), _const_spec((D_MODEL, D_MODEL)),
                  _const_spec((SLOT_WIDTH, D_MODEL)), _const_spec((D_MODEL, D_MODEL))],
        out_specs=wide,
        out_shape=jax.ShapeDtypeStruct((t, D_MODEL), F32),
        scratch_shapes=scratch,
        compiler_params=_params("parallel"),
        name="mix_out",
    )(x, y, u, ssm_d, o_b, ga, gb, w_glu, w_pa, w_pb, w_out)


def _head_segment_ones():
    head = jnp.arange(MXU_DIM) // HEAD_DIM
    return (head[:, None] == head[None, :]).astype(BF16)


def kernel(x_prompt, x_sample, state_ssm_re, state_ssm_im, cache_kv_w128, cache_kv_w512, cache_kv_w2048, ffn1_norm, ffn1_w_gate, ffn1_w_up, ffn1_w_down, mix_norm, w_in, ssm_lambda_re, ssm_lambda_im, ssm_b_re, ssm_b_im, ssm_c_re, ssm_c_im, ssm_d, ssm_log_dt, w_glu, q_gain, k_gain, w_proj_a, w_proj_b, w_out, ffn2_norm, ffn2_w_gate, ffn2_w_up, ffn2_w_down):
    depth = ffn1_norm.shape[0]
    assert depth == 1, "single-layer step"
    batch, seq, _ = x_prompt.shape
    dec_batch, dec_seq, _ = x_sample.shape
    assert dec_seq == 1 and seq % (DIL_RATES[-1] * KEYS_BACK) == 0
    layer = 0
    vec = lambda w: w[layer][None]
    mix_norm, ssm_d = vec(mix_norm), vec(ssm_d)
    q_gain_t = jnp.tile(vec(q_gain), (1, N_DIL * HEADS))
    k_gain_t = jnp.tile(vec(k_gain), (1, N_DIL * HEADS))
    seg_ones = _head_segment_ones()

    abar, a8, bstep, cstep, toep, win, wout, *ffn1_w = _ssm_prep(
        ssm_lambda_re[layer], ssm_lambda_im[layer], ssm_log_dt[layer],
        ssm_b_re[layer].transpose(0, 2, 1), ssm_b_im[layer].transpose(0, 2, 1),
        ssm_c_re[layer], ssm_c_im[layer],
        convert=[w[layer] for w in (ffn1_w_gate, ffn1_w_up, ffn1_w_down)])
    ffn1 = (vec(ffn1_norm), tuple(ffn1_w))
    a8 = a8.reshape(2, SSM_PAIRS, LANES)
    abar = abar.reshape(2, SSM_PAIRS, LANES)

    def mix_in(x1, tm, seq=None):
        return _mix_in(x1, mix_norm, w_in_b, q_gain_t, k_gain_t, seg_ones, tm, seq)

    def mix_out(x1, y, u, o_b, ga, gb, tm, folded_y):
        return _mix_out(x1, y, u, ssm_d, o_b, ga, gb, w_glu_b, w_pa_b, w_pb_b, w_out_b, tm, folded_y)

    tm = 512
    tm_ffn = 1024
    xp = x_prompt.reshape(batch * seq, D_MODEL)
    later = [w[layer] for w in (ffn2_w_gate, ffn2_w_up, ffn2_w_down, w_in, w_glu, w_proj_a, w_proj_b, w_out)]
    x1, *later_b = _ffn(xp, *ffn1, tm_ffn, convert=later)
    ffn2 = (vec(ffn2_norm), tuple(later_b[:3]))
    w_in_b, w_glu_b, w_pa_b, w_pb_b, w_out_b = later_b[3:]

    xs = x_sample.reshape(dec_batch, D_MODEL)
    xs1 = _ffn(xs, *ffn1, dec_batch)
    us, qs, ks, vs, gas, gbs = mix_in(xs1, dec_batch)
    caches = [c[layer].transpose(0, 2, 3, 4, 1) for c in (cache_kv_w128, cache_kv_w512, cache_kv_w2048)]

    u, ga, gb, u_folded, *rest = mix_in(x1, tm, seq)
    p_windows, qkv = rest[:N_DIL], rest[N_DIL:]
    y, h_fin = _ssm_prompt(u_folded, toep, win, wout, a8, batch, seq, tile=2048)
    o_b = _attn_prompt(qkv, batch, seq)
    x2 = mix_out(x1, y, u, o_b, ga, gb, tm, folded_y=True)
    yp, o_bs, new_caches = _ffn_with_sample_windows(x2, *ffn2, qs, ks, vs, caches)
    yp = yp.reshape(batch, seq, D_MODEL)
    h_fin = h_fin.reshape(batch, 2, SSM_GROUPS, SSM_STATE)
    p_re = h_fin[None, :, 0]
    p_im = h_fin[None, :, 1]
    p_kv = [c.transpose(0, 4, 1, 2, 3)[None] for c in p_windows]

    width = SSM_GROUPS * SSM_STATE
    ys_ssm, s_re, s_im = _ssm_step(us, state_ssm_re[layer].reshape(dec_batch, width),
                                   state_ssm_im[layer].reshape(dec_batch, width),
                                   abar, bstep, cstep)
    o_bs = o_bs.reshape(dec_batch, SLOT_WIDTH // LANES, LANES).transpose(1, 0, 2)
    xs2 = mix_out(xs1, ys_ssm, us, o_bs, gas, gbs, dec_batch, folded_y=False)
    ys = _ffn(xs2, *ffn2, dec_batch)
    ys = ys.reshape(dec_batch, 1, D_MODEL)
    s_re = s_re.reshape(1, dec_batch, SSM_GROUPS, SSM_STATE)
    s_im = s_im.reshape(1, dec_batch, SSM_GROUPS, SSM_STATE)
    s_kv = [c.transpose(0, 4, 1, 2, 3)[None] for c in new_caches]

    return (yp, ys, p_re, p_im, p_kv[0], p_kv[1], p_kv[2], s_re, s_im, s_kv[0], s_kv[1], s_kv[2])
```
